```python
import jax, jax.numpy as jnp
from jax import lax
import numpy as np

D_MODEL = 1024
BATCH = 1
SEQ = 16384
DEPTH = 2
DEC_BATCH = 8
DEC_SEQ = 16
PAST_LEN = 2048

CHUNK = 64
QBLOCK = 128
SGU_LEN = 128
G_SGU = 4
W_SGU = 384
CG_SGU = W_SGU // G_SGU
HEAD_DIM = 64
H_FOX = 6
H_SB = 4
W_FOX = H_FOX * HEAD_DIM
W_SB = H_SB * HEAD_DIM
P_IN = 2 * W_SGU + 3 * W_FOX + H_FOX + 3 * W_SB
D_FF = -(-8 * D_MODEL // (3 * 256)) * 256
EPS = 1e-6

kernel_name = 'hybrid_streaming_encoder_step'


def rmsnorm(x, g):
    xf = x.astype(jnp.float32)
    y = xf * lax.rsqrt(jnp.mean(xf * xf, axis=-1, keepdims=True) + EPS)
    return y.astype(x.dtype) * g


def fox_attend(q, k, v, fq, fk, q_pos, k_pos):
    logits = jnp.einsum('bqhd,bkhd->bhqk', q, k).astype(jnp.float32) * (HEAD_DIM ** -0.5)
    logits = logits + jnp.swapaxes(fq, 1, 2)[..., :, None] - jnp.swapaxes(fk, 1, 2)[..., None, :]
    mask = k_pos[None, :] <= q_pos[:, None]
    p = jax.nn.softmax(jnp.where(mask, logits, -jnp.inf), axis=-1)
    return jnp.einsum('bhqk,bkhd->bqhd', p.astype(v.dtype), v)


def sb_attend(q, k, v, q_pos, k_pos):
    z = jnp.einsum('bqhd,bkhd->bhqk', q, k).astype(jnp.float32) * (HEAD_DIM ** -0.5)
    mask = k_pos[None, :] < q_pos[:, None]
    sp = jax.nn.softplus(z)
    sp_m = jnp.where(mask, sp, 0.0)
    later = lax.cumsum(sp_m, axis=3, reverse=True) - sp_m
    a = jnp.where(mask, jnp.exp(z - sp - later), 0.0)
    return jnp.einsum('bhqk,bkhd->bqhd', a.astype(v.dtype), v)


def sweep_query_blocks(fn, q_side, q_pos):
    b, s = q_side[0].shape[:2]
    nb = s // QBLOCK
    blocks = tuple(jnp.moveaxis(a.reshape(b, nb, QBLOCK, *a.shape[2:]), 1, 0) for a in q_side)
    out = lax.map(lambda blk: fn(*blk[0], blk[1]), (blocks, q_pos.reshape(nb, QBLOCK)))
    return jnp.moveaxis(out, 0, 1).reshape(b, s, *out.shape[3:])


def layer(x, c, cache, w_ada, b_ada, g_mix, g_ffn, w_in, g_sgu_v, w_sgu, b_sgu, b_fgt, g_q, g_k,
          w_br_sgu, w_br_fox, w_br_sb, w_gate, b_gate, w_out, w_ffn_in, w_ffn_out):
    b, n, _ = x.shape
    mod = jax.nn.silu(c) @ w_ada + b_ada
    sh1, sc1, gt1, sh2, sc2, gt2 = [m[:, None, :] for m in jnp.split(mod, 6, axis=-1)]
    h = rmsnorm(x, g_mix) * (1 + sc1) + sh1

    offs = np.cumsum([W_SGU, W_SGU, W_FOX, W_FOX, W_FOX, H_FOX, W_SB, W_SB]).tolist()
    u, v_s, q_f, k_f, v_f, f_logit, q_b, k_b, v_b = jnp.split(h @ w_in, offs, axis=-1)

    v_s = rmsnorm(v_s.reshape(b, n, G_SGU, CG_SGU), g_sgu_v)
    blk = jnp.arange(SGU_LEN) // CHUNK
    w_m = jnp.where(blk[None, :] <= blk[:, None], w_sgu, 0.0)
    L = min(n, SGU_LEN)
    vc = v_s.reshape(b, n // L, L, G_SGU, CG_SGU)
    spat = jnp.einsum('gij,bcjgd->bcigd', w_m[:, :L, :L], vc) \
        + jnp.swapaxes(b_sgu[:, :L], 0, 1)[None, None, :, :, None]
    y_sgu = u * spat.reshape(b, n, W_SGU)

    q_f = rmsnorm(q_f.reshape(b, n, H_FOX, HEAD_DIM), g_q)
    k_f = rmsnorm(k_f.reshape(b, n, H_FOX, HEAD_DIM), g_k)
    v_f = v_f.reshape(b, n, H_FOX, HEAD_DIM)
    logf = jax.nn.log_sigmoid(f_logit + b_fgt)

    q_b = q_b.reshape(b, n, H_SB, HEAD_DIM)
    k_b = k_b.reshape(b, n, H_SB, HEAD_DIM)
    v_b = v_b.reshape(b, n, H_SB, HEAD_DIM)

    if cache is None:
        pos = jnp.arange(n)
        F = jnp.cumsum(logf.astype(jnp.float32), axis=1)
        o_fox = sweep_query_blocks(lambda qb, fb, pb: fox_attend(qb, k_f, v_f, fb, F, pb, pos), (q_f, F), pos)
        o_sb = sweep_query_blocks(lambda qb, pb: sb_attend(qb, k_b, v_b, pb, pos), (q_b,), pos)
    else:
        ck_f, cv_f, clogf, ck_b, cv_b = cache
        past = ck_f.shape[1]
        k_pos = jnp.arange(past + n)
        q_pos = past + jnp.arange(n)
        F = jnp.cumsum(jnp.concatenate([clogf, logf], axis=1).astype(jnp.float32), axis=1)
        o_fox = fox_attend(q_f, jnp.concatenate([ck_f, k_f], axis=1), jnp.concatenate([cv_f, v_f], axis=1),
                           F[:, past:], F, q_pos, k_pos)
        o_sb = sb_attend(q_b, jnp.concatenate([ck_b, k_b], axis=1), jnp.concatenate([cv_b, v_b], axis=1),
                         q_pos, k_pos)

    g_a, g_b, g_c = jnp.split(jax.nn.sigmoid(h @ w_gate + b_gate), 3, axis=-1)
    merged = g_a * (y_sgu @ w_br_sgu) \
        + g_b * (o_fox.reshape(b, n, W_FOX) @ w_br_fox) \
        + g_c * (o_sb.reshape(b, n, W_SB) @ w_br_sb)
    x = x + gt1 * (merged @ w_out)

    h2 = rmsnorm(x, g_ffn) * (1 + sc2) + sh2
    a, gl = jnp.split(h2 @ w_ffn_in, 2, axis=-1)
    x = x + gt2 * ((jax.nn.silu(a) * gl) @ w_ffn_out)
    return x, (k_f, v_f, logf, k_b, v_b, v_s.reshape(b, n, W_SGU))


def setup_inputs(seed: int = 0) -> dict:
    key = jax.random.key(seed)
    ks = jax.random.split(key, 32)

    def nrm(k, shape, scale):
        return jax.random.normal(k, shape, jnp.float32) * scale

    D = D_MODEL
    return {
        'x_prompt': nrm(ks[0], (BATCH, SEQ, D), 1.0),
        'x_sample': nrm(ks[1], (DEC_BATCH, DEC_SEQ, D), 1.0),
        'c_prompt': nrm(ks[2], (BATCH, D), 1.0),
        'c_sample': nrm(ks[3], (DEC_BATCH, D), 1.0),
        'cache_fox_k': nrm(ks[4], (DEPTH, DEC_BATCH, PAST_LEN, H_FOX, HEAD_DIM), 1.0),
        'cache_fox_v': nrm(ks[5], (DEPTH, DEC_BATCH, PAST_LEN, H_FOX, HEAD_DIM), 1.0),
        'cache_fox_logf': jax.nn.log_sigmoid(2.5 + nrm(ks[6], (DEPTH, DEC_BATCH, PAST_LEN, H_FOX), 0.5)),
        'cache_sb_k': nrm(ks[7], (DEPTH, DEC_BATCH, PAST_LEN, H_SB, HEAD_DIM), 1.0),
        'cache_sb_v': nrm(ks[8], (DEPTH, DEC_BATCH, PAST_LEN, H_SB, HEAD_DIM), 1.0),
        'w_ada': nrm(ks[9], (DEPTH, D, 6 * D), 0.5 * D ** -0.5),
        'b_ada': nrm(ks[10], (DEPTH, 6 * D), 0.02),
        'g_mix': 1.0 + nrm(ks[11], (DEPTH, D), 0.02),
        'g_ffn': 1.0 + nrm(ks[12], (DEPTH, D), 0.02),
        'w_in': nrm(ks[13], (DEPTH, D, P_IN), D ** -0.5),
        'g_sgu_v': 1.0 + nrm(ks[14], (DEPTH, G_SGU, CG_SGU), 0.02),
        'w_sgu': nrm(ks[15], (DEPTH, G_SGU, SGU_LEN, SGU_LEN), SGU_LEN ** -0.5),
        'b_sgu': 1.0 + nrm(ks[16], (DEPTH, G_SGU, SGU_LEN), 0.1),
        'b_fgt': 2.5 + nrm(ks[17], (DEPTH, H_FOX), 0.5),
        'g_q': 1.0 + nrm(ks[18], (DEPTH, HEAD_DIM), 0.02),
        'g_k': 1.0 + nrm(ks[19], (DEPTH, HEAD_DIM), 0.02),
        'w_br_sgu': nrm(ks[20], (DEPTH, W_SGU, D), W_SGU ** -0.5),
        'w_br_fox': nrm(ks[21], (DEPTH, W_FOX, D), W_FOX ** -0.5),
        'w_br_sb': nrm(ks[22], (DEPTH, W_SB, D), W_SB ** -0.5),
        'w_gate': nrm(ks[23], (DEPTH, D, 3 * D), D ** -0.5),
        'b_gate': nrm(ks[24], (DEPTH, 3 * D), 0.02),
        'w_out': nrm(ks[25], (DEPTH, D, D), D ** -0.5),
        'w_ffn_in': nrm(ks[26], (DEPTH, D, 2 * D_FF), D ** -0.5),
        'w_ffn_out': nrm(ks[27], (DEPTH, D_FF, D), D_FF ** -0.5),
    }


def reference(x_prompt, x_sample, c_prompt, c_sample, cache_fox_k, cache_fox_v, cache_fox_logf,
              cache_sb_k, cache_sb_v, w_ada, b_ada, g_mix, g_ffn, w_in, g_sgu_v, w_sgu, b_sgu, b_fgt,
              g_q, g_k, w_br_sgu, w_br_fox, w_br_sb, w_gate, b_gate, w_out, w_ffn_in, w_ffn_out):
    xp, xs = x_prompt, x_sample
    st_p, st_s = [], []
    for l in range(DEPTH):
        params = (w_ada[l], b_ada[l], g_mix[l], g_ffn[l], w_in[l], g_sgu_v[l], w_sgu[l], b_sgu[l],
                  b_fgt[l], g_q[l], g_k[l], w_br_sgu[l], w_br_fox[l], w_br_sb[l], w_gate[l], b_gate[l],
                  w_out[l], w_ffn_in[l], w_ffn_out[l])
        xp, sp = layer(xp, c_prompt, None, *params)
        xs, ss = layer(xs, c_sample,
                       (cache_fox_k[l], cache_fox_v[l], cache_fox_logf[l], cache_sb_k[l], cache_sb_v[l]),
                       *params)
        st_p.append(sp)
        st_s.append(ss)

    def stack(states, i):
        return jnp.stack([s[i] for s in states], axis=0)

    return (xp, xs,
            stack(st_p, 0), stack(st_p, 1), stack(st_p, 2), stack(st_p, 3), stack(st_p, 4),
            stack(st_s, 0), stack(st_s, 1), stack(st_s, 2), stack(st_s, 3), stack(st_s, 4), stack(st_s, 5))
```

```python
import functools

import numpy as np
import jax
import jax.numpy as jnp
from jax import lax
from jax.experimental import pallas as pl
from jax.experimental.pallas import tpu as pltpu

F32 = jnp.float32
BF16 = jnp.bfloat16

EPS = 1e-6
HEAD_DIM = 64
LANES = 128
CHUNK = 64
SGU_LEN = 128
G_SGU = 4
NEG_BIG = -1e30

PRUNE_LOG = 30.0

VMEM_LIMIT = 56 * 1024 * 1024


def _cparams(sem):
    return pltpu.CompilerParams(dimension_semantics=sem, vmem_limit_bytes=VMEM_LIMIT)


def _const_spec(shape):
    nd = len(shape)
    return pl.BlockSpec(shape, lambda *_: (0,) * nd, pipeline_mode=pl.Buffered(1))


def _dot(a, b):
    return jnp.dot(a, b, preferred_element_type=F32)


def _dot_nt(a, b):
    return lax.dot_general(a, b, (((1,), (1,)), ((), ())), preferred_element_type=F32)


def _split3(x):
    h = x.astype(BF16)
    r = x - h.astype(F32)
    m = r.astype(BF16)
    l = (r - m.astype(F32)).astype(BF16)
    return h, m, l


def _split2(x):
    h = x.astype(BF16)
    l = (x - h.astype(F32)).astype(BF16)
    return h, l


def _mod_kernel(c_ref, w_ref, b_ref, o_ref):
    c = c_ref[...]
    s = c * (1.0 / (1.0 + jnp.exp(-c)))
    o_ref[0] = _dot(s.astype(BF16), w_ref[0].astype(BF16)) + b_ref[0]


def _modulation(c_all, w_ada, b_ada):
    depth, d, n6 = w_ada.shape
    rows = c_all.shape[0]
    tn = 1024
    return pl.pallas_call(
        _mod_kernel,
        grid=(depth, n6 // tn),
        in_specs=[
            pl.BlockSpec((rows, d), lambda l, j: (0, 0)),
            pl.BlockSpec((1, d, tn), lambda l, j: (l, 0, j)),
            pl.BlockSpec((1, 1, tn), lambda l, j: (l, 0, j)),
        ],
        out_specs=pl.BlockSpec((1, rows, tn), lambda l, j: (l, 0, j)),
        out_shape=jax.ShapeDtypeStruct((depth, rows, n6), F32),
        compiler_params=_cparams(("arbitrary", "arbitrary")),
        name="adaln_mod",
    )(c_all, w_ada, b_ada.reshape(depth, 1, n6))


def _modulated_norm(x, g, sc, sh):
    ms = jnp.mean(x * x, axis=-1, keepdims=True)
    return (x * lax.rsqrt(ms + EPS)) * g * (1.0 + sc) + sh


def _group_rms(t, ind, inv_size, g):
    ss = _dot((t * t).astype(BF16), ind)
    return t * lax.rsqrt(ss * inv_size + EPS) * g


def _log_sigmoid(x):
    return jnp.minimum(x, 0.0) - jnp.log(1.0 + jnp.exp(-jnp.abs(x)))


def _inproj_kernel(x_ref, sc_ref, sh_ref, gmix_ref, w_ref, gsgu_ref, gq_ref, gk_ref, bf_ref,
                   ind96_ref, ind64_ref, msgu_ref, bsgu_ref,
                   ysgu_ref, qf_ref, kf16_ref, vf16_ref, qb_ref, kb16_ref, vb16_ref,
                   kf32_ref, vf32_ref, kb32_ref, vb32_ref, logft_ref, *maybe_sguv_ref,
                   tm, w_sgu, w_fox, w_sb, period):
    x = x_ref[...]
    h = _modulated_norm(x, gmix_ref[...], sc_ref[...], sh_ref[...])
    p = _dot(h.astype(BF16), w_ref[...])

    o = 0
    u = p[:, o:o + w_sgu]; o += w_sgu
    vs = p[:, o:o + w_sgu]; o += w_sgu
    qf = p[:, o:o + w_fox]; o += w_fox
    kf = p[:, o:o + w_fox]; o += w_fox
    vf = p[:, o:o + w_fox]; o += w_fox
    qb = p[:, o:o + w_sb]; o += w_sb
    kb = p[:, o:o + w_sb]; o += w_sb
    vb = p[:, o:o + w_sb]; o += w_sb
    fl = p[:, o:o + LANES]

    scale = HEAD_DIM ** -0.5
    ind64 = ind64_ref[...]
    qfn = _group_rms(qf, ind64, 1.0 / HEAD_DIM, gq_ref[...])
    kfn = _group_rms(kf, ind64, 1.0 / HEAD_DIM, gk_ref[...])
    qf_ref[...] = (qfn * scale).astype(BF16)
    kf32_ref[...] = kfn
    kf16_ref[...] = kfn.astype(BF16)
    vf32_ref[...] = vf
    vf16_ref[...] = vf.astype(BF16)
    qb_ref[...] = (qb * scale).astype(BF16)
    kb32_ref[...] = kb
    kb16_ref[...] = kb.astype(BF16)
    vb32_ref[...] = vb
    vb16_ref[...] = vb.astype(BF16)

    lf = _log_sigmoid(fl + bf_ref[...])
    logft_ref[...] = lf.T[0:8, :]

    cg = w_sgu // G_SGU
    vsn = _group_rms(vs, ind96_ref[...], 1.0 / cg, gsgu_ref[...])
    if maybe_sguv_ref:
        maybe_sguv_ref[0][...] = vsn
    r = lax.broadcasted_iota(jnp.int32, (SGU_LEN, G_SGU * SGU_LEN), 0)
    c = lax.broadcasted_iota(jnp.int32, (SGU_LEN, G_SGU * SGU_LEN), 1) % SGU_LEN
    keep = (r // period == c // period) & ((c % period) // CHUNK <= (r % period) // CHUNK)
    mix = jnp.where(keep, msgu_ref[...], 0.0).astype(BF16)
    lane_group = lax.broadcasted_iota(jnp.int32, (SGU_LEN, w_sgu), 1) // cg
    vsb = vsn.astype(BF16)
    spat = []
    for ci in range(tm // SGU_LEN):
        vc = vsb[ci * SGU_LEN:(ci + 1) * SGU_LEN]
        stacked = jnp.concatenate(
            [jnp.where(lane_group == g, vc, jnp.zeros_like(vc)) for g in range(G_SGU)], axis=0)
        spat.append(_dot(mix, stacked) + bsgu_ref[...])
    spat = jnp.concatenate(spat, axis=0) if len(spat) > 1 else spat[0]
    ysgu_ref[...] = (u * spat).astype(BF16)


def _inproj(x, sc, sh, gmix, w, gsgu, gq, gk, bf, ind96, ind64, msgu, bsgu, *, tm, period,
            emit_sguv):
    n, d = x.shape
    w_sgu, w_fox = gsgu.shape[1], gq.shape[1]
    w_sb = (w.shape[1] - LANES - 2 * w_sgu - 3 * w_fox) // 3
    mod_rows = sc.shape[0]
    if mod_rows == 1:
        mod_spec = pl.BlockSpec((1, d), lambda i: (0, 0))
    else:
        mod_spec = pl.BlockSpec((tm, d), lambda i: (i, 0))
    row = lambda width: pl.BlockSpec((tm, width), lambda i: (i, 0))
    out_specs = [row(w_sgu), row(w_fox), row(w_fox), row(w_fox), row(w_sb), row(w_sb), row(w_sb),
                 row(w_fox), row(w_fox), row(w_sb), row(w_sb),
                 pl.BlockSpec((8, tm), lambda i: (0, i))]
    sds = jax.ShapeDtypeStruct
    out_shape = [sds((n, w_sgu), BF16), sds((n, w_fox), BF16), sds((n, w_fox), BF16),
                 sds((n, w_fox), BF16), sds((n, w_sb), BF16), sds((n, w_sb), BF16),
                 sds((n, w_sb), BF16),
                 sds((n, w_fox), F32), sds((n, w_fox), F32), sds((n, w_sb), F32),
                 sds((n, w_sb), F32), sds((8, n), F32)]
    if emit_sguv:
        out_specs.append(row(w_sgu))
        out_shape.append(sds((n, w_sgu), F32))
    kern = functools.partial(_inproj_kernel, tm=tm, w_sgu=w_sgu, w_fox=w_fox, w_sb=w_sb,
                             period=period)
    return pl.pallas_call(
        kern,
        grid=(n // tm,),
        in_specs=[row(d), mod_spec, mod_spec, _const_spec(gmix.shape), _const_spec(w.shape),
                  _const_spec(gsgu.shape), _const_spec(gq.shape), _const_spec(gk.shape),
                  _const_spec(bf.shape), _const_spec(ind96.shape), _const_spec(ind64.shape),
                  _const_spec(msgu.shape), _const_spec(bsgu.shape)],
        out_specs=out_specs,
        out_shape=out_shape,
        compiler_params=_cparams(("arbitrary",)),
        name="inproj",
    )(x, sc, sh, gmix, w, gsgu, gq, gk, bf, ind96, ind64, msgu, bsgu)


def _seq_cumsum(x, nc, reverse_exclusive):
    rows = x.shape[0]
    a = lax.broadcasted_iota(jnp.int32, (LANES, LANES), 0)
    b = lax.broadcasted_iota(jnp.int32, (LANES, LANES), 1)
    tri = (a > b) if reverse_exclusive else (a <= b)
    tri = jnp.where(tri, 1.0, 0.0).astype(BF16)
    ones = jnp.ones((LANES, LANES), BF16)
    xh, xm, xl = _split3(x)
    within = _dot(xh, tri) + _dot(xm, tri) + _dot(xl, tri)
    tot = _dot(xh, ones) + _dot(xm, ones) + _dot(xl, ones)
    ra = lax.broadcasted_iota(jnp.int32, (rows, rows), 0)
    rb = lax.broadcasted_iota(jnp.int32, (rows, rows), 1)
    other = (rb > ra) if reverse_exclusive else (rb < ra)
    blk = jnp.where((ra // nc == rb // nc) & other, 1.0, 0.0).astype(BF16)
    th, tm_, tl = _split3(tot)
    return within + _dot(blk, th) + _dot(blk, tm_) + _dot(blk, tl)


def _prompt_cumsum_kernel(lf_ref, gq_ref, gk_ref, f_ref, jlo_ref, *, nc):
    f = _seq_cumsum(lf_ref[...], nc, reverse_exclusive=False)
    f_ref[...] = f
    bound = 1.01 * (HEAD_DIM ** 0.5) * jnp.max(jnp.abs(gq_ref[...]), axis=1, keepdims=True) \
        * jnp.max(jnp.abs(gk_ref[...]), axis=1, keepdims=True)
    cut = -(2.0 * bound + PRUNE_LOG)
    for h in range(8):
        e = f[h * nc:(h + 1) * nc, :]
        f_end = e[:, LANES - 1:LANES]
        f_start = e.T[0:1, :]
        skip = jnp.where(f_start - f_end < cut, 1.0, 0.0)
        jlo_ref[h:h + 1, :] = jnp.sum(skip, axis=0, keepdims=True).astype(jnp.int32)


def _prompt_cumsum(logft, gq, gk):
    n = logft.shape[1]
    nc = n // LANES
    f, jlo = pl.pallas_call(
        functools.partial(_prompt_cumsum_kernel, nc=nc),
        out_shape=[jax.ShapeDtypeStruct((8 * nc, LANES), F32),
                   jax.ShapeDtypeStruct((8, nc), jnp.int32)],
        compiler_params=pltpu.CompilerParams(vmem_limit_bytes=VMEM_LIMIT),
        name="logf_cumsum",
    )(logft.reshape(8 * nc, LANES), gq, gk)
    return f.reshape(8, n), jlo


def _sample_cumsum_kernel(clf_ref, lf_ref, suf_ref, cum_ref, *, nc, dec_seq):
    suf_ref[...] = _seq_cumsum(clf_ref[...], nc, reverse_exclusive=True)
    a = lax.broadcasted_iota(jnp.int32, (LANES, LANES), 0)
    b = lax.broadcasted_iota(jnp.int32, (LANES, LANES), 1)
    tri = jnp.where((a // dec_seq == b // dec_seq) & (a <= b), 1.0, 0.0).astype(BF16)
    xh, xm, xl = _split3(lf_ref[...])
    cum_ref[...] = _dot(xh, tri) + _dot(xm, tri) + _dot(xl, tri)


def _sample_cumsum(clogf_t, logft, dec_seq):
    rows, past = clogf_t.shape
    nc = past // LANES
    suf, cum = pl.pallas_call(
        functools.partial(_sample_cumsum_kernel, nc=nc, dec_seq=dec_seq),
        out_shape=[jax.ShapeDtypeStruct((rows * nc, LANES), F32),
                   jax.ShapeDtypeStruct(logft.shape, F32)],
        compiler_params=pltpu.CompilerParams(vmem_limit_bytes=VMEM_LIMIT),
        name="sample_logf_cumsum",
    )(clogf_t.reshape(rows * nc, LANES), logft)
    return suf.reshape(rows, past), cum


def _fox_kernel(jlo_ref, q_ref, k_ref, v_ref, f_ref, o_ref, *, tq):
    p = pl.program_id(0)
    i = pl.program_id(1)
    tk = tq
    q = q_ref[...]
    lane = lax.broadcasted_iota(jnp.int32, (tq, LANES), 1)
    qs = [jnp.where(lane < HEAD_DIM, q, jnp.zeros_like(q)),
          jnp.where(lane >= HEAD_DIM, q, jnp.zeros_like(q))]
    row = lax.broadcasted_iota(jnp.int32, (tq, tk), 0)
    col = lax.broadcasted_iota(jnp.int32, (tq, tk), 1)
    q0 = pl.multiple_of(i * tq, tq)
    fref = [f_ref[0, hh:hh + 1, pl.ds(q0, LANES)][:, 0:1] for hh in range(2)]

    def step(j, carry, masked):
        k0 = pl.multiple_of(j * tk, tk)
        k = k_ref[pl.ds(k0, tk), :]
        v = v_ref[pl.ds(k0, tk), :]
        out = []
        for hh in range(2):
            m, l, acc = carry[3 * hh:3 * hh + 3]
            s = _dot_nt(qs[hh], k) + (fref[hh] - f_ref[0, hh:hh + 1, pl.ds(k0, tk)])
            if masked:
                s = jnp.where(col <= row, s, NEG_BIG)
            mn = jnp.maximum(m, jnp.max(s, axis=1, keepdims=True))
            alpha = jnp.exp(m - mn)
            pe = jnp.exp(s - mn)
            l = alpha * l + jnp.sum(pe, axis=1, keepdims=True)
            acc = alpha * acc + _dot(pe.astype(BF16), v)
            out += [mn, l, acc]
        return tuple(out)

    init = []
    for _ in range(2):
        init += [jnp.full((tq, 1), NEG_BIG, F32), jnp.zeros((tq, 1), F32),
                 jnp.zeros((tq, LANES), F32)]
    cpb = tq // LANES
    jlo = jnp.minimum(jlo_ref[2 * p, i * cpb], jlo_ref[2 * p + 1, i * cpb]) // cpb
    carry = lax.fori_loop(jlo, i, lambda j, c: step(j, c, False), tuple(init))
    m0, l0, a0, m1, l1, a1 = step(i, carry, True)
    o_ref[...] = jnp.where(lane < HEAD_DIM, a0 / l0, a1 / l1).astype(BF16)


def _fox_prompt(jlo, q, k, v, f, *, tq):
    n, w = q.shape
    pairs = w // LANES
    f3 = f.reshape(4, 2, n)
    grid_spec = pltpu.PrefetchScalarGridSpec(
        num_scalar_prefetch=1,
        grid=(pairs, n // tq),
        in_specs=[
            pl.BlockSpec((tq, LANES), lambda p, i, jl: (i, p)),
            pl.BlockSpec((n, LANES), lambda p, i, jl: (0, p)),
            pl.BlockSpec((n, LANES), lambda p, i, jl: (0, p)),
            pl.BlockSpec((1, 2, n), lambda p, i, jl: (p, 0, 0)),
        ],
        out_specs=pl.BlockSpec((tq, LANES), lambda p, i, jl: (i, p)),
    )
    return pl.pallas_call(
        functools.partial(_fox_kernel, tq=tq),
        grid_spec=grid_spec,
        out_shape=jax.ShapeDtypeStruct((n, w), BF16),
        compiler_params=_cparams(("arbitrary", "arbitrary")),
        name="fox_prompt",
    )(jlo, q, k, v, f3)


def _sb_block(z, carry, v, tri, mask):
    lg = jnp.log(1.0 + jnp.exp(-jnp.abs(z)))
    sp = jnp.maximum(z, 0.0) + lg
    if mask is not None:
        sp = jnp.where(mask, sp, 0.0)
    hi, lo = _split2(sp)
    later = _dot(hi, tri) + _dot(lo, tri)
    a = jnp.exp((jnp.minimum(z, 0.0) - lg) - later - carry)
    if mask is not None:
        a = jnp.where(mask, a, 0.0)
    return _dot(a.astype(BF16), v), carry + jnp.sum(sp, axis=1, keepdims=True)


def _suffix_matrix(tk):
    a = lax.broadcasted_iota(jnp.int32, (tk, tk), 0)
    b = lax.broadcasted_iota(jnp.int32, (tk, tk), 1)
    return jnp.where(a > b, 1.0, 0.0).astype(BF16)


def _sb_kernel(q_ref, k_ref, v_ref, o_ref, *, tq):
    i = pl.program_id(1)
    tk = tq
    q = q_ref[...]
    lane = lax.broadcasted_iota(jnp.int32, (tq, LANES), 1)
    qs = [jnp.where(lane < HEAD_DIM, q, jnp.zeros_like(q)),
          jnp.where(lane >= HEAD_DIM, q, jnp.zeros_like(q))]
    row = lax.broadcasted_iota(jnp.int32, (tq, tk), 0)
    col = lax.broadcasted_iota(jnp.int32, (tq, tk), 1)
    tri = _suffix_matrix(tk)

    def step(j, carry, mask):
        k0 = pl.multiple_of(j * tk, tk)
        k = k_ref[pl.ds(k0, tk), :]
        v = v_ref[pl.ds(k0, tk), :]
        out = []
        for hh in range(2):
            cr, acc = carry[2 * hh:2 * hh + 2]
            pv, cr = _sb_block(_dot_nt(qs[hh], k), cr, v, tri, mask)
            out += [cr, acc + pv]
        return tuple(out)

    zero = (jnp.zeros((tq, 1), F32), jnp.zeros((tq, LANES), F32))
    carry = step(i, zero + zero, col < row)

    def cond(state):
        j, c0, _, c1, _ = state
        live = jnp.minimum(jnp.min(c0), jnp.min(c1)) < PRUNE_LOG
        return (j >= 0) & live

    def body(state):
        j = state[0]
        return (j - 1,) + step(j, state[1:], None)

    _, _, a0, _, a1 = lax.while_loop(cond, body, (i - 1,) + carry)
    o_ref[...] = jnp.where(lane < HEAD_DIM, a0, a1).astype(BF16)


def _sb_prompt(q, k, v, *, tq):
    n, w = q.shape
    pairs = w // LANES
    return pl.pallas_call(
        functools.partial(_sb_kernel, tq=tq),
        grid=(pairs, n // tq),
        in_specs=[
            pl.BlockSpec((tq, LANES), lambda p, i: (i, p)),
            pl.BlockSpec((n, LANES), lambda p, i: (0, p)),
            pl.BlockSpec((n, LANES), lambda p, i: (0, p)),
        ],
        out_specs=pl.BlockSpec((tq, LANES), lambda p, i: (i, p)),
        out_shape=jax.ShapeDtypeStruct((n, w), BF16),
        compiler_params=_cparams(("arbitrary", "arbitrary")),
        name="sb_prompt",
    )(q, k, v)


def _head_rows(x, heads):
    lane_head = lax.broadcasted_iota(jnp.int32, x.shape, 1) // HEAD_DIM
    return jnp.concatenate(
        [jnp.where(lane_head == h, x, jnp.zeros_like(x)) for h in range(heads)], axis=0)


def _fold_heads(o, heads, s):
    lane_head = lax.broadcasted_iota(jnp.int32, (s, o.shape[1]), 1) // HEAD_DIM
    out = jnp.zeros((s, o.shape[1]), F32)
    for h in range(heads):
        out = jnp.where(lane_head == h, o[h * s:(h + 1) * s], out)
    return out


def _sample_attn_kernel(qf_ref, kf_ref, vf_ref, ck_ref, cv_ref, suf_ref, cum_ref,
                        qb_ref, kb_ref, vb_ref, cbk_ref, cbv_ref, of_ref, ob_ref,
                        *, s, h_fox, h_sb, past):
    qa = _head_rows(qf_ref[...], h_fox)
    rows = h_fox * s
    ck = ck_ref[0].astype(BF16)
    cv = cv_ref[0].astype(BF16)
    suf = suf_ref[0]
    cum = cum_ref[0]
    bias_c = jnp.concatenate(
        [jnp.broadcast_to(suf[h:h + 1, :], (s, past)) for h in range(h_fox)], axis=0)
    bias_n = jnp.concatenate(
        [jnp.broadcast_to(-cum[h:h + 1, :], (s, s)) for h in range(h_fox)], axis=0)
    lc = _dot_nt(qa, ck) + bias_c
    ln = _dot_nt(qa, kf_ref[...]) + bias_n
    r_pos = lax.broadcasted_iota(jnp.int32, (rows, s), 0) % s
    k_pos = lax.broadcasted_iota(jnp.int32, (rows, s), 1)
    ln = jnp.where(k_pos <= r_pos, ln, NEG_BIG)
    m = jnp.maximum(jnp.max(lc, axis=1, keepdims=True), jnp.max(ln, axis=1, keepdims=True))
    pc = jnp.exp(lc - m)
    pn = jnp.exp(ln - m)
    den = jnp.sum(pc, axis=1, keepdims=True) + jnp.sum(pn, axis=1, keepdims=True)
    o = (_dot(pc.astype(BF16), cv) + _dot(pn.astype(BF16), vf_ref[...])) / den
    of_ref[...] = _fold_heads(o, h_fox, s).astype(BF16)

    qb = _head_rows(qb_ref[...], h_sb)
    rows_b = h_sb * s
    rb = lax.broadcasted_iota(jnp.int32, (rows_b, s), 0) % s
    cb = lax.broadcasted_iota(jnp.int32, (rows_b, s), 1)
    acc, carry = _sb_block(_dot_nt(qb, kb_ref[...]), jnp.zeros((rows_b, 1), F32), vb_ref[...],
                           _suffix_matrix(s), cb < rb)
    tri = _suffix_matrix(LANES)

    def body(t, state):
        acc, carry = state
        k0 = pl.multiple_of(past - (t + 1) * LANES, LANES)
        k = cbk_ref[0, pl.ds(k0, LANES), :].astype(BF16)
        v = cbv_ref[0, pl.ds(k0, LANES), :].astype(BF16)
        pv, carry = _sb_block(_dot_nt(qb, k), carry, v, tri, None)
        return acc + pv, carry

    acc, _ = lax.fori_loop(0, past // LANES, body, (acc, carry))
    ob_ref[...] = _fold_heads(acc, h_sb, s).astype(BF16)


def _sample_attn(qf, kf, vf, ck, cv, suf, cum, qb, kb, vb, cbk, cbv, *, batch, s):
    w_fox, w_sb = qf.shape[1], qb.shape[1]
    past = ck.shape[1]
    h_fox, h_sb = w_fox // HEAD_DIM, w_sb // HEAD_DIM
    new = lambda w: pl.BlockSpec((s, w), lambda b: (b, 0))
    cache = lambda w: pl.BlockSpec((1, past, w), lambda b: (b, 0, 0))
    kern = functools.partial(_sample_attn_kernel, s=s, h_fox=h_fox, h_sb=h_sb, past=past)
    return pl.pallas_call(
        kern,
        grid=(batch,),
        in_specs=[new(w_fox), new(w_fox), new(w_fox), cache(w_fox), cache(w_fox),
                  pl.BlockSpec((1, 8, past), lambda b: (b, 0, 0)),
                  pl.BlockSpec((1, 8, s), lambda b: (b, 0, 0)),
                  new(w_sb), new(w_sb), new(w_sb), cache(w_sb), cache(w_sb)],
        out_specs=[new(w_fox), new(w_sb)],
        out_shape=[jax.ShapeDtypeStruct((batch * s, w_fox), BF16),
                   jax.ShapeDtypeStruct((batch * s, w_sb), BF16)],
        compiler_params=_cparams(("arbitrary",)),
        name="sample_attn",
    )(qf, kf, vf, ck, cv, suf, cum, qb, kb, vb, cbk, cbv)


def _merge_kernel(x_ref, sc_ref, sh_ref, gt_ref, gmix_ref, ysgu_ref, ofox_ref, osb_ref,
                  wg_ref, bg_ref, wbs_ref, wbf_ref, wbb_ref, wo_ref, o_ref):
    x = x_ref[...]
    d = x.shape[1]
    h = _modulated_norm(x, gmix_ref[...], sc_ref[...], sh_ref[...])
    gates = _dot(h.astype(BF16), wg_ref[...]) + bg_ref[...]
    gates = 1.0 / (1.0 + jnp.exp(-gates))
    merged = gates[:, 0:d] * _dot(ysgu_ref[...], wbs_ref[...]) \
        + gates[:, d:2 * d] * _dot(ofox_ref[...], wbf_ref[...]) \
        + gates[:, 2 * d:3 * d] * _dot(osb_ref[...], wbb_ref[...])
    o_ref[...] = x + gt_ref[...] * _dot(merged.astype(BF16), wo_ref[...])


def _merge(x, sc, sh, gt, gmix, ysgu, ofox, osb, wg, bg, wbs, wbf, wbb, wo, *, tm):
    n, d = x.shape
    if sc.shape[0] == 1:
        mod_spec = pl.BlockSpec((1, d), lambda i: (0, 0))
    else:
        mod_spec = pl.BlockSpec((tm, d), lambda i: (i, 0))
    row = lambda width: pl.BlockSpec((tm, width), lambda i: (i, 0))
    return pl.pallas_call(
        _merge_kernel,
        grid=(n // tm,),
        in_specs=[row(d), mod_spec, mod_spec, mod_spec, _const_spec(gmix.shape),
                  row(ysgu.shape[1]), row(ofox.shape[1]), row(osb.shape[1]),
                  _const_spec(wg.shape), _const_spec(bg.shape), _const_spec(wbs.shape),
                  _const_spec(wbf.shape), _const_spec(wbb.shape), _const_spec(wo.shape)],
        out_specs=row(d),
        out_shape=jax.ShapeDtypeStruct((n, d), F32),
        compiler_params=_cparams(("arbitrary",)),
        name="merge",
    )(x, sc, sh, gt, gmix, ysgu, ofox, osb, wg, bg, wbs, wbf, wbb, wo)


def _ffn_kernel(x_ref, sc_ref, sh_ref, gt_ref, g_ref, wi_ref, wo_ref, o_ref, *, d_ff):
    x = x_ref[...]
    h = _modulated_norm(x, g_ref[...], sc_ref[...], sh_ref[...])
    ag = _dot(h.astype(BF16), wi_ref[...])
    a = ag[:, 0:d_ff]
    act = a * (1.0 / (1.0 + jnp.exp(-a))) * ag[:, d_ff:2 * d_ff]
    o_ref[...] = x + gt_ref[...] * _dot(act.astype(BF16), wo_ref[...])


def _ffn(x, sc, sh, gt, g, wi, wo, *, tm):
    n, d = x.shape
    d_ff = wo.shape[0]
    if sc.shape[0] == 1:
        mod_spec = pl.BlockSpec((1, d), lambda i: (0, 0))
    else:
        mod_spec = pl.BlockSpec((tm, d), lambda i: (i, 0))
    row = pl.BlockSpec((tm, d), lambda i: (i, 0))
    return pl.pallas_call(
        functools.partial(_ffn_kernel, d_ff=d_ff),
        grid=(n // tm,),
        in_specs=[row, mod_spec, mod_spec, mod_spec, _const_spec(g.shape), _const_spec(wi.shape),
                  _const_spec(wo.shape)],
        out_specs=row,
        out_shape=jax.ShapeDtypeStruct((n, d), F32),
        compiler_params=_cparams(("arbitrary",)),
        name="ffn",
    )(x, sc, sh, gt, g, wi, wo)


def _indicator(width, group):
    idx = np.arange(width) // group
    return jnp.asarray(idx[:, None] == idx[None, :], dtype=BF16)


def kernel(x_prompt, x_sample, c_prompt, c_sample, cache_fox_k, cache_fox_v, cache_fox_logf,
           cache_sb_k, cache_sb_v, w_ada, b_ada, g_mix, g_ffn, w_in, g_sgu_v, w_sgu, b_sgu, b_fgt,
           g_q, g_k, w_br_sgu, w_br_fox, w_br_sb, w_gate, b_gate, w_out, w_ffn_in, w_ffn_out):
    batch, seq, d = x_prompt.shape
    dec_batch, dec_seq, _ = x_sample.shape
    depth = w_ada.shape[0]
    past = cache_fox_k.shape[2]
    h_fox, h_sb = cache_fox_k.shape[3], cache_sb_k.shape[3]
    w_fox, w_sb = h_fox * HEAD_DIM, h_sb * HEAD_DIM
    g_sgu, cg = g_sgu_v.shape[1], g_sgu_v.shape[2]
    w_sgu_ = g_sgu * cg
    assert batch == 1 and g_sgu == G_SGU and w_sgu.shape[2] == SGU_LEN
    n_dec = dec_batch * dec_seq

    n_c = batch + dec_batch
    c_rows = -(-n_c // 8) * 8
    c_all = jnp.zeros((c_rows, d), F32).at[:n_c].set(jnp.concatenate([c_prompt, c_sample], axis=0))
    mod = _modulation(c_all, w_ada, b_ada)

    offs = np.cumsum([0, w_sgu_, w_sgu_, w_fox, w_fox, w_fox, h_fox, w_sb, w_sb, w_sb]).tolist()
    f_cols = jnp.zeros((depth, d, LANES), F32).at[:, :, :h_fox].set(w_in[:, :, offs[5]:offs[6]])
    w_main = jnp.concatenate([w_in[:, :, :offs[5]], w_in[:, :, offs[6]:], f_cols],
                             axis=2).astype(BF16)
    bf_pad = jnp.zeros((depth, 1, LANES), F32).at[:, 0, :h_fox].set(b_fgt)
    ind96, ind64 = _indicator(w_sgu_, cg), _indicator(w_fox, HEAD_DIM)
    gq_t = jnp.tile(g_q, (1, h_fox)).reshape(depth, 1, w_fox)
    gk_t = jnp.tile(g_k, (1, h_fox)).reshape(depth, 1, w_fox)
    gsgu = g_sgu_v.reshape(depth, 1, w_sgu_)
    msgu_p = jnp.transpose(w_sgu, (0, 2, 1, 3)).reshape(depth, SGU_LEN, g_sgu * SGU_LEN)
    reps = SGU_LEN // dec_seq
    w_small = jnp.tile(w_sgu[:, :, :dec_seq, :dec_seq], (1, 1, reps, reps))
    msgu_s = jnp.transpose(w_small, (0, 2, 1, 3)).reshape(depth, SGU_LEN, g_sgu * SGU_LEN)
    bsgu_p = jnp.repeat(jnp.transpose(b_sgu, (0, 2, 1)), cg, axis=2)
    bsgu_s = jnp.tile(bsgu_p[:, :dec_seq], (1, reps, 1))
    wg, wbs, wbf, wbb = (w.astype(BF16) for w in (w_gate, w_br_sgu, w_br_fox, w_br_sb))
    wo, wfi, wfo = (w.astype(BF16) for w in (w_out, w_ffn_in, w_ffn_out))

    xp = x_prompt.reshape(seq, d)
    xs = x_sample.reshape(n_dec, d)
    tm_p = min(512, seq)
    tm_f = min(256, seq)
    tq = min(256, seq)
    st_p, st_s = [], []
    for l in range(depth):
        mp = [mod[l, 0:1, k * d:(k + 1) * d] for k in range(6)]
        ms = [jnp.repeat(mod[l, batch:batch + dec_batch, k * d:(k + 1) * d], dec_seq, axis=0)
              for k in range(6)]
        gmix, gffn = g_mix[l].reshape(1, d), g_ffn[l].reshape(1, d)
        shared = (gmix, w_main[l], gsgu[l], gq_t[l], gk_t[l], bf_pad[l], ind96, ind64)

        (ysgu, qf, kf16, vf16, qb, kb16, vb16, kf32, vf32, kb32, vb32, logft) = _inproj(
            xp, mp[1], mp[0], *shared, msgu_p[l], bsgu_p[l], tm=tm_p, period=SGU_LEN,
            emit_sguv=False)
        f_cum, jlo = _prompt_cumsum(logft, g_q[l].reshape(1, HEAD_DIM), g_k[l].reshape(1, HEAD_DIM))
        ofox = _fox_prompt(jlo, qf, kf16, vf16, f_cum, tq=tq)
        osb = _sb_prompt(qb, kb16, vb16, tq=min(128, seq))
        x1 = _merge(xp, mp[1], mp[0], mp[2], gmix, ysgu, ofox, osb, wg[l],
                    b_gate[l].reshape(1, 3 * d), wbs[l], wbf[l], wbb[l], wo[l], tm=tm_p)
        xp = _ffn(x1, mp[4], mp[3], mp[5], gffn, wfi[l], wfo[l], tm=tm_f)
        st_p.append((kf32.reshape(batch, seq, h_fox, HEAD_DIM),
                     vf32.reshape(batch, seq, h_fox, HEAD_DIM),
                     logft[:h_fox].T.reshape(batch, seq, h_fox),
                     kb32.reshape(batch, seq, h_sb, HEAD_DIM),
                     vb32.reshape(batch, seq, h_sb, HEAD_DIM)))

        (ysgu, qf, kf16, vf16, qb, kb16, vb16, kf32, vf32, kb32, vb32, logft, sguv) = _inproj(
            xs, ms[1], ms[0], *shared, msgu_s[l], bsgu_s[l], tm=n_dec, period=dec_seq,
            emit_sguv=True)
        clf = jnp.zeros((dec_batch, 8, past), F32).at[:, :h_fox].set(
            jnp.transpose(cache_fox_logf[l], (0, 2, 1)))
        suf, cum = _sample_cumsum(clf.reshape(dec_batch * 8, past), logft, dec_seq)
        cum_b = jnp.transpose(cum.reshape(8, dec_batch, dec_seq), (1, 0, 2))
        ofox, osb = _sample_attn(
            qf, kf16, vf16, cache_fox_k[l].reshape(dec_batch, past, w_fox),
            cache_fox_v[l].reshape(dec_batch, past, w_fox), suf.reshape(dec_batch, 8, past), cum_b,
            qb, kb16, vb16, cache_sb_k[l].reshape(dec_batch, past, w_sb),
            cache_sb_v[l].reshape(dec_batch, past, w_sb), batch=dec_batch, s=dec_seq)
        x1 = _merge(xs, ms[1], ms[0], ms[2], gmix, ysgu, ofox, osb, wg[l],
                    b_gate[l].reshape(1, 3 * d), wbs[l], wbf[l], wbb[l], wo[l], tm=n_dec)
        xs = _ffn(x1, ms[4], ms[3], ms[5], gffn, wfi[l], wfo[l], tm=n_dec)
        st_s.append((kf32.reshape(dec_batch, dec_seq, h_fox, HEAD_DIM),
                     vf32.reshape(dec_batch, dec_seq, h_fox, HEAD_DIM),
                     logft[:h_fox].T.reshape(dec_batch, dec_seq, h_fox),
                     kb32.reshape(dec_batch, dec_seq, h_sb, HEAD_DIM),
                     vb32.reshape(dec_batch, dec_seq, h_sb, HEAD_DIM),
                     sguv.reshape(dec_batch, dec_seq, w_sgu_)))

    def stack(states, idx):
        return jnp.stack([s[idx] for s in states], axis=0)

    return (xp.reshape(batch, seq, d), xs.reshape(dec_batch, dec_seq, d),
            stack(st_p, 0), stack(st_p, 1), stack(st_p, 2), stack(st_p, 3), stack(st_p, 4),
            stack(st_s, 0), stack(st_s, 1), stack(st_s, 2), stack(st_s, 3), stack(st_s, 4),
            stack(st_s, 5))
```

```python
import functools

import numpy as np
import jax
import jax.numpy as jnp
from jax import lax
from jax.experimental import pallas as pl
from jax.experimental.pallas import tpu as pltpu

F32 = jnp.float32
BF16 = jnp.bfloat16

EPS = 1e-6
HEAD_DIM = 64
LANES = 128
CHUNK = 64
SGU_LEN = 128
G_SGU = 4
NEG_BIG = -1e30

PRUNE_LOG = 30.0

VMEM_LIMIT = 56 * 1024 * 1024


def _cparams(sem):
    return pltpu.CompilerParams(dimension_semantics=sem, vmem_limit_bytes=VMEM_LIMIT)


def _const_spec(shape):
    nd = len(shape)
    return pl.BlockSpec(shape, lambda *_: (0,) * nd, pipeline_mode=pl.Buffered(1))


def _dot(a, b):
    return jnp.dot(a, b, preferred_element_type=F32)


def _dot_nt(a, b):
    return lax.dot_general(a, b, (((1,), (1,)), ((), ())), preferred_element_type=F32)


def _split3(x):
    h = x.astype(BF16)
    r = x - h.astype(F32)
    m = r.astype(BF16)
    l = (r - m.astype(F32)).astype(BF16)
    return h, m, l


def _split2(x):
    h = x.astype(BF16)
    l = (x - h.astype(F32)).astype(BF16)
    return h, l


def _mod_kernel(c_ref, w_ref, b_ref, o_ref):
    c = c_ref[...]
    s = c * (1.0 / (1.0 + jnp.exp(-c)))
    o_ref[0] = _dot(s.astype(BF16), w_ref[0].astype(BF16)) + b_ref[0]


def _modulation(c_all, w_ada, b_ada):
    depth, d, n6 = w_ada.shape
    rows = c_all.shape[0]
    tn = 1024
    return pl.pallas_call(
        _mod_kernel,
        grid=(depth, n6 // tn),
        in_specs=[
            pl.BlockSpec((rows, d), lambda l, j: (0, 0)),
            pl.BlockSpec((1, d, tn), lambda l, j: (l, 0, j)),
            pl.BlockSpec((1, 1, tn), lambda l, j: (l, 0, j)),
        ],
        out_specs=pl.BlockSpec((1, rows, tn), lambda l, j: (l, 0, j)),
        out_shape=jax.ShapeDtypeStruct((depth, rows, n6), F32),
        compiler_params=_cparams(("arbitrary", "arbitrary")),
        name="adaln_mod",
    )(c_all, w_ada, b_ada.reshape(depth, 1, n6))


def _modulated_norm(x, g, sc, sh):
    ms = jnp.mean(x * x, axis=-1, keepdims=True)
    return (x * lax.rsqrt(ms + EPS)) * g * (1.0 + sc) + sh


def _group_rms(t, ind, inv_size, g):
    ss = _dot((t * t).astype(BF16), ind)
    return t * lax.rsqrt(ss * inv_size + EPS) * g


def _log_sigmoid(x):
    return jnp.minimum(x, 0.0) - jnp.log(1.0 + jnp.exp(-jnp.abs(x)))


def _inproj_kernel(x_ref, sc_ref, sh_ref, gmix_ref, w_ref, gsgu_ref, gq_ref, gk_ref, bf_ref,
                   ind96_ref, ind64_ref, msgu_ref, bsgu_ref,
                   ysgu_ref, qf_ref, kf16_ref, vf16_ref, qb_ref, kb16_ref, vb16_ref,
                   kf32_ref, vf32_ref, kb32_ref, vb32_ref, logft_ref, *maybe_sguv_ref,
                   tm, w_sgu, w_fox, w_sb, period):
    x = x_ref[...]
    h = _modulated_norm(x, gmix_ref[...], sc_ref[...], sh_ref[...])
    p = _dot(h.astype(BF16), w_ref[...])

    o = 0
    u = p[:, o:o + w_sgu]; o += w_sgu
    vs = p[:, o:o + w_sgu]; o += w_sgu
    qf = p[:, o:o + w_fox]; o += w_fox
    kf = p[:, o:o + w_fox]; o += w_fox
    vf = p[:, o:o + w_fox]; o += w_fox
    qb = p[:, o:o + w_sb]; o += w_sb
    kb = p[:, o:o + w_sb]; o += w_sb
    vb = p[:, o:o + w_sb]; o += w_sb
    fl = p[:, o:o + LANES]

    scale = HEAD_DIM ** -0.5
    ind64 = ind64_ref[...]
    qfn = _group_rms(qf, ind64, 1.0 / HEAD_DIM, gq_ref[...])
    kfn = _group_rms(kf, ind64, 1.0 / HEAD_DIM, gk_ref[...])
    qf_ref[...] = (qfn * scale).astype(BF16)
    kf32_ref[...] = kfn
    kf16_ref[...] = kfn.astype(BF16)
    vf32_ref[...] = vf
    vf16_ref[...] = vf.astype(BF16)
    qb_ref[...] = (qb * scale).astype(BF16)
    kb32_ref[...] = kb
    kb16_ref[...] = kb.astype(BF16)
    vb32_ref[...] = vb
    vb16_ref[...] = vb.astype(BF16)

    lf = _log_sigmoid(fl + bf_ref[...])
    logft_ref[...] = lf.T[0:8, :]

    cg = w_sgu // G_SGU
    vsn = _group_rms(vs, ind96_ref[...], 1.0 / cg, gsgu_ref[...])
    if maybe_sguv_ref:
        maybe_sguv_ref[0][...] = vsn
    r = lax.broadcasted_iota(jnp.int32, (SGU_LEN, G_SGU * SGU_LEN), 0)
    c = lax.broadcasted_iota(jnp.int32, (SGU_LEN, G_SGU * SGU_LEN), 1) % SGU_LEN
    keep = (r // period == c // period) & ((c % period) // CHUNK <= (r % period) // CHUNK)
    mix = jnp.where(keep, msgu_ref[...], 0.0).astype(BF16)
    lane_group = lax.broadcasted_iota(jnp.int32, (SGU_LEN, w_sgu), 1) // cg
    vsb = vsn.astype(BF16)
    spat = []
    for ci in range(tm // SGU_LEN):
        vc = vsb[ci * SGU_LEN:(ci + 1) * SGU_LEN]
        stacked = jnp.concatenate(
            [jnp.where(lane_group == g, vc, jnp.zeros_like(vc)) for g in range(G_SGU)], axis=0)
        spat.append(_dot(mix, stacked) + bsgu_ref[...])
    spat = jnp.concatenate(spat, axis=0) if len(spat) > 1 else spat[0]
    ysgu_ref[...] = (u * spat).astype(BF16)


def _inproj(x, sc, sh, gmix, w, gsgu, gq, gk, bf, ind96, ind64, msgu, bsgu, *, tm, period,
            emit_sguv):
    n, d = x.shape
    w_sgu, w_fox = gsgu.shape[1], gq.shape[1]
    w_sb = (w.shape[1] - LANES - 2 * w_sgu - 3 * w_fox) // 3
    mod_rows = sc.shape[0]
    if mod_rows == 1:
        mod_spec = pl.BlockSpec((1, d), lambda i: (0, 0))
    else:
        mod_spec = pl.BlockSpec((tm, d), lambda i: (i, 0))
    row = lambda width: pl.BlockSpec((tm, width), lambda i: (i, 0))
    out_specs = [row(w_sgu), row(w_fox), row(w_fox), row(w_fox), row(w_sb), row(w_sb), row(w_sb),
                 row(w_fox), row(w_fox), row(w_sb), row(w_sb),
                 pl.BlockSpec((8, tm), lambda i: (0, i))]
    sds = jax.ShapeDtypeStruct
    out_shape = [sds((n, w_sgu), BF16), sds((n, w_fox), BF16), sds((n, w_fox), BF16),
                 sds((n, w_fox), BF16), sds((n, w_sb), BF16), sds((n, w_sb), BF16),
                 sds((n, w_sb), BF16),
                 sds((n, w_fox), F32), sds((n, w_fox), F32), sds((n, w_sb), F32),
                 sds((n, w_sb), F32), sds((8, n), F32)]
    if emit_sguv:
        out_specs.append(row(w_sgu))
        out_shape.append(sds((n, w_sgu), F32))
    kern = functools.partial(_inproj_kernel, tm=tm, w_sgu=w_sgu, w_fox=w_fox, w_sb=w_sb,
                             period=period)
    return pl.pallas_call(
        kern,
        grid=(n // tm,),
        in_specs=[row(d), mod_spec, mod_spec, _const_spec(gmix.shape), _const_spec(w.shape),
                  _const_spec(gsgu.shape), _const_spec(gq.shape), _const_spec(gk.shape),
                  _const_spec(bf.shape), _const_spec(ind96.shape), _const_spec(ind64.shape),
                  _const_spec(msgu.shape), _const_spec(bsgu.shape)],
        out_specs=out_specs,
        out_shape=out_shape,
        compiler_params=_cparams(("arbitrary",)),
        name="inproj",
    )(x, sc, sh, gmix, w, gsgu, gq, gk, bf, ind96, ind64, msgu, bsgu)


def _seq_cumsum(x, nc, reverse_exclusive):
    rows = x.shape[0]
    a = lax.broadcasted_iota(jnp.int32, (LANES, LANES), 0)
    b = lax.broadcasted_iota(jnp.int32, (LANES, LANES), 1)
    tri = (a > b) if reverse_exclusive else (a <= b)
    tri = jnp.where(tri, 1.0, 0.0).astype(BF16)
    ones = jnp.ones((LANES, LANES), BF16)
    xh, xm, xl = _split3(x)
    within = _dot(xh, tri) + _dot(xm, tri) + _dot(xl, tri)
    tot = _dot(xh, ones) + _dot(xm, ones) + _dot(xl, ones)
    ra = lax.broadcasted_iota(jnp.int32, (rows, rows), 0)
    rb = lax.broadcasted_iota(jnp.int32, (rows, rows), 1)
    other = (rb > ra) if reverse_exclusive else (rb < ra)
    blk = jnp.where((ra // nc == rb // nc) & other, 1.0, 0.0).astype(BF16)
    th, tm_, tl = _split3(tot)
    return within + _dot(blk, th) + _dot(blk, tm_) + _dot(blk, tl)


def _prompt_cumsum_kernel(lf_ref, gq_ref, gk_ref, f_ref, jlo_ref, *, nc):
    f = _seq_cumsum(lf_ref[...], nc, reverse_exclusive=False)
    f_ref[...] = f
    bound = 1.01 * (HEAD_DIM ** 0.5) * jnp.max(jnp.abs(gq_ref[...]), axis=1, keepdims=True) \
        * jnp.max(jnp.abs(gk_ref[...]), axis=1, keepdims=True)
    cut = -(2.0 * bound + PRUNE_LOG)
    for h in range(8):
        e = f[h * nc:(h + 1) * nc, :]
        f_end = e[:, LANES - 1:LANES]
        f_start = e.T[0:1, :]
        skip = jnp.where(f_start - f_end < cut, 1.0, 0.0)
        jlo_ref[h:h + 1, :] = jnp.sum(skip, axis=0, keepdims=True).astype(jnp.int32)


def _prompt_cumsum(logft, gq, gk):
    n = logft.shape[1]
    nc = n // LANES
    f, jlo = pl.pallas_call(
        functools.partial(_prompt_cumsum_kernel, nc=nc),
        out_shape=[jax.ShapeDtypeStruct((8 * nc, LANES), F32),
                   jax.ShapeDtypeStruct((8, nc), jnp.int32)],
        compiler_params=pltpu.CompilerParams(vmem_limit_bytes=VMEM_LIMIT),
        name="logf_cumsum",
    )(logft.reshape(8 * nc, LANES), gq, gk)
    return f.reshape(8, n), jlo


def _sample_cumsum_kernel(clf_ref, lf_ref, suf_ref, cum_ref, *, nc, dec_seq):
    suf_ref[...] = _seq_cumsum(clf_ref[...], nc, reverse_exclusive=True)
    a = lax.broadcasted_iota(jnp.int32, (LANES, LANES), 0)
    b = lax.broadcasted_iota(jnp.int32, (LANES, LANES), 1)
    tri = jnp.where((a // dec_seq == b // dec_seq) & (a <= b), 1.0, 0.0).astype(BF16)
    xh, xm, xl = _split3(lf_ref[...])
    cum_ref[...] = _dot(xh, tri) + _dot(xm, tri) + _dot(xl, tri)


def _sample_cumsum(clogf_t, logft, dec_seq):
    rows, past = clogf_t.shape
    nc = past // LANES
    suf, cum = pl.pallas_call(
        functools.partial(_sample_cumsum_kernel, nc=nc, dec_seq=dec_seq),
        out_shape=[jax.ShapeDtypeStruct((rows * nc, LANES), F32),
                   jax.ShapeDtypeStruct(logft.shape, F32)],
        compiler_params=pltpu.CompilerParams(vmem_limit_bytes=VMEM_LIMIT),
        name="sample_logf_cumsum",
    )(clogf_t.reshape(rows * nc, LANES), logft)
    return suf.reshape(rows, past), cum


def _head_queries(q_ref, heads, tq):
    lane = lax.broadcasted_iota(jnp.int32, (tq, LANES), 1)
    qs = []
    for h in range(heads):
        qp = q_ref[:, (h // 2) * LANES:(h // 2 + 1) * LANES]
        keep = (lane < HEAD_DIM) if h % 2 == 0 else (lane >= HEAD_DIM)
        qs.append(jnp.where(keep, qp, jnp.zeros_like(qp)))
    return qs


def _fox_kernel(jlo_ref, q_ref, k_ref, v_ref, f_ref, o_ref, *, tq, heads):
    i = pl.program_id(0)
    tk = tq
    qs = _head_queries(q_ref, heads, tq)
    lane = lax.broadcasted_iota(jnp.int32, (tq, LANES), 1)
    row = lax.broadcasted_iota(jnp.int32, (tq, tk), 0)
    col = lax.broadcasted_iota(jnp.int32, (tq, tk), 1)
    q0 = pl.multiple_of(i * tq, tq)
    fref = [f_ref[h:h + 1, pl.ds(q0, LANES)][:, 0:1] for h in range(heads)]

    def step(j, carry, masked):
        k0 = pl.multiple_of(j * tk, tk)
        out = []
        for p in range(heads // 2):
            k = k_ref[pl.ds(k0, tk), p * LANES:(p + 1) * LANES]
            v = v_ref[pl.ds(k0, tk), p * LANES:(p + 1) * LANES]
            for h in (2 * p, 2 * p + 1):
                m, l, acc = carry[3 * h:3 * h + 3]
                s = _dot_nt(qs[h], k) + (fref[h] - f_ref[h:h + 1, pl.ds(k0, tk)])
                if masked:
                    s = jnp.where(col <= row, s, NEG_BIG)
                mn = jnp.maximum(m, jnp.max(s, axis=1, keepdims=True))
                alpha = jnp.exp(m - mn)
                pe = jnp.exp(s - mn)
                l = alpha * l + jnp.sum(pe, axis=1, keepdims=True)
                acc = alpha * acc + _dot(pe.astype(BF16), v)
                out += [mn, l, acc]
        return tuple(out)

    init = []
    for _ in range(heads):
        init += [jnp.full((tq, 1), NEG_BIG, F32), jnp.zeros((tq, 1), F32),
                 jnp.zeros((tq, LANES), F32)]
    cpb = tq // LANES
    jlo = jlo_ref[0, i * cpb]
    for h in range(1, heads):
        jlo = jnp.minimum(jlo, jlo_ref[h, i * cpb])
    carry = lax.fori_loop(jlo // cpb, i, lambda j, c: step(j, c, False), tuple(init))
    res = step(i, carry, True)
    for p in range(heads // 2):
        _, l0, a0, _, l1, a1 = res[6 * p:6 * p + 6]
        o_ref[:, p * LANES:(p + 1) * LANES] = jnp.where(
            lane < HEAD_DIM, a0 / l0, a1 / l1).astype(BF16)


def _fox_prompt(jlo, q, k, v, f, *, tq):
    n, w = q.shape
    grid_spec = pltpu.PrefetchScalarGridSpec(
        num_scalar_prefetch=1,
        grid=(n // tq,),
        in_specs=[
            pl.BlockSpec((tq, w), lambda i, jl: (i, 0)),
            pl.BlockSpec((n, w), lambda i, jl: (0, 0), pipeline_mode=pl.Buffered(1)),
            pl.BlockSpec((n, w), lambda i, jl: (0, 0), pipeline_mode=pl.Buffered(1)),
            pl.BlockSpec((8, n), lambda i, jl: (0, 0), pipeline_mode=pl.Buffered(1)),
        ],
        out_specs=pl.BlockSpec((tq, w), lambda i, jl: (i, 0)),
    )
    return pl.pallas_call(
        functools.partial(_fox_kernel, tq=tq, heads=w // HEAD_DIM),
        grid_spec=grid_spec,
        out_shape=jax.ShapeDtypeStruct((n, w), BF16),
        compiler_params=_cparams(("arbitrary",)),
        name="fox_prompt",
    )(jlo, q, k, v, f)


def _sb_block(z, carry, v, tri, mask):
    lg = jnp.log(1.0 + jnp.exp(-jnp.abs(z)))
    sp = jnp.maximum(z, 0.0) + lg
    if mask is not None:
        sp = jnp.where(mask, sp, 0.0)
    hi, lo = _split2(sp)
    later = _dot(hi, tri) + _dot(lo, tri)
    a = jnp.exp((jnp.minimum(z, 0.0) - lg) - later - carry)
    if mask is not None:
        a = jnp.where(mask, a, 0.0)
    return _dot(a.astype(BF16), v), carry + jnp.sum(sp, axis=1, keepdims=True)


def _suffix_matrix(tk):
    a = lax.broadcasted_iota(jnp.int32, (tk, tk), 0)
    b = lax.broadcasted_iota(jnp.int32, (tk, tk), 1)
    return jnp.where(a > b, 1.0, 0.0).astype(BF16)


def _sb_kernel(q_ref, k_ref, v_ref, o_ref, *, tq, heads):
    i = pl.program_id(0)
    tk = tq
    qs = _head_queries(q_ref, heads, tq)
    lane = lax.broadcasted_iota(jnp.int32, (tq, LANES), 1)
    row = lax.broadcasted_iota(jnp.int32, (tq, tk), 0)
    col = lax.broadcasted_iota(jnp.int32, (tq, tk), 1)
    tri = _suffix_matrix(tk)

    def step(j, carry, mask):
        k0 = pl.multiple_of(j * tk, tk)
        out = []
        for p in range(heads // 2):
            k = k_ref[pl.ds(k0, tk), p * LANES:(p + 1) * LANES]
            v = v_ref[pl.ds(k0, tk), p * LANES:(p + 1) * LANES]
            for h in (2 * p, 2 * p + 1):
                cr, acc = carry[2 * h:2 * h + 2]
                pv, cr = _sb_block(_dot_nt(qs[h], k), cr, v, tri, mask)
                out += [cr, acc + pv]
        return tuple(out)

    zero = (jnp.zeros((tq, 1), F32), jnp.zeros((tq, LANES), F32))
    carry = step(i, zero * heads, col < row)
    carry = step(jnp.maximum(i - 1, 0), carry, jnp.broadcast_to(i > 0, (tq, tk)))

    def cond(state):
        live = jnp.min(state[1])
        for h in range(1, heads):
            live = jnp.minimum(live, jnp.min(state[1 + 2 * h]))
        return (state[0] >= 0) & (live < PRUNE_LOG)

    def body(state):
        j = state[0]
        return (j - 1,) + step(j, state[1:], None)

    res = lax.while_loop(cond, body, (i - 2,) + carry)[1:]
    for p in range(heads // 2):
        o_ref[:, p * LANES:(p + 1) * LANES] = jnp.where(
            lane < HEAD_DIM, res[4 * p + 1], res[4 * p + 3]).astype(BF16)


def _sb_prompt(q, k, v, *, tq):
    n, w = q.shape
    return pl.pallas_call(
        functools.partial(_sb_kernel, tq=tq, heads=w // HEAD_DIM),
        grid=(n // tq,),
        in_specs=[
            pl.BlockSpec((tq, w), lambda i: (i, 0)),
            pl.BlockSpec((n, w), lambda i: (0, 0), pipeline_mode=pl.Buffered(1)),
            pl.BlockSpec((n, w), lambda i: (0, 0), pipeline_mode=pl.Buffered(1)),
        ],
        out_specs=pl.BlockSpec((tq, w), lambda i: (i, 0)),
        out_shape=jax.ShapeDtypeStruct((n, w), BF16),
        compiler_params=_cparams(("arbitrary",)),
        name="sb_prompt",
    )(q, k, v)


def _head_rows(x, heads):
    lane_head = lax.broadcasted_iota(jnp.int32, x.shape, 1) // HEAD_DIM
    return jnp.concatenate(
        [jnp.where(lane_head == h, x, jnp.zeros_like(x)) for h in range(heads)], axis=0)


def _fold_heads(o, heads, s):
    lane_head = lax.broadcasted_iota(jnp.int32, (s, o.shape[1]), 1) // HEAD_DIM
    out = jnp.zeros((s, o.shape[1]), F32)
    for h in range(heads):
        out = jnp.where(lane_head == h, o[h * s:(h + 1) * s], out)
    return out


def _sample_attn_kernel(qf_ref, kf_ref, vf_ref, ck_ref, cv_ref, suf_ref, cum_ref,
                        qb_ref, kb_ref, vb_ref, cbk_ref, cbv_ref, of_ref, ob_ref,
                        *, s, h_fox, h_sb, past):
    qa = _head_rows(qf_ref[...], h_fox)
    rows = h_fox * s
    ck = ck_ref[0].astype(BF16)
    cv = cv_ref[0].astype(BF16)
    suf = suf_ref[0]
    cum = cum_ref[0]
    bias_c = jnp.concatenate(
        [jnp.broadcast_to(suf[h:h + 1, :], (s, past)) for h in range(h_fox)], axis=0)
    bias_n = jnp.concatenate(
        [jnp.broadcast_to(-cum[h:h + 1, :], (s, s)) for h in range(h_fox)], axis=0)
    lc = _dot_nt(qa, ck) + bias_c
    ln = _dot_nt(qa, kf_ref[...]) + bias_n
    r_pos = lax.broadcasted_iota(jnp.int32, (rows, s), 0) % s
    k_pos = lax.broadcasted_iota(jnp.int32, (rows, s), 1)
    ln = jnp.where(k_pos <= r_pos, ln, NEG_BIG)
    m = jnp.maximum(jnp.max(lc, axis=1, keepdims=True), jnp.max(ln, axis=1, keepdims=True))
    pc = jnp.exp(lc - m)
    pn = jnp.exp(ln - m)
    den = jnp.sum(pc, axis=1, keepdims=True) + jnp.sum(pn, axis=1, keepdims=True)
    o = (_dot(pc.astype(BF16), cv) + _dot(pn.astype(BF16), vf_ref[...])) / den
    of_ref[...] = _fold_heads(o, h_fox, s).astype(BF16)

    qb = _head_rows(qb_ref[...], h_sb)
    rows_b = h_sb * s
    rb = lax.broadcasted_iota(jnp.int32, (rows_b, s), 0) % s
    cb = lax.broadcasted_iota(jnp.int32, (rows_b, s), 1)
    acc, carry = _sb_block(_dot_nt(qb, kb_ref[...]), jnp.zeros((rows_b, 1), F32), vb_ref[...],
                           _suffix_matrix(s), cb < rb)
    tri = _suffix_matrix(LANES)

    def body(t, state):
        acc, carry = state
        k0 = pl.multiple_of(past - (t + 1) * LANES, LANES)
        k = cbk_ref[0, pl.ds(k0, LANES), :].astype(BF16)
        v = cbv_ref[0, pl.ds(k0, LANES), :].astype(BF16)
        pv, carry = _sb_block(_dot_nt(qb, k), carry, v, tri, None)
        return acc + pv, carry

    acc, _ = lax.fori_loop(0, past // LANES, body, (acc, carry))
    ob_ref[...] = _fold_heads(acc, h_sb, s).astype(BF16)


def _sample_attn(qf, kf, vf, ck, cv, suf, cum, qb, kb, vb, cbk, cbv, *, batch, s):
    w_fox, w_sb = qf.shape[1], qb.shape[1]
    past = ck.shape[1]
    h_fox, h_sb = w_fox // HEAD_DIM, w_sb // HEAD_DIM
    new = lambda w: pl.BlockSpec((s, w), lambda b: (b, 0))
    cache = lambda w: pl.BlockSpec((1, past, w), lambda b: (b, 0, 0))
    kern = functools.partial(_sample_attn_kernel, s=s, h_fox=h_fox, h_sb=h_sb, past=past)
    return pl.pallas_call(
        kern,
        grid=(batch,),
        in_specs=[new(w_fox), new(w_fox), new(w_fox), cache(w_fox), cache(w_fox),
                  pl.BlockSpec((1, 8, past), lambda b: (b, 0, 0)),
                  pl.BlockSpec((1, 8, s), lambda b: (b, 0, 0)),
                  new(w_sb), new(w_sb), new(w_sb), cache(w_sb), cache(w_sb)],
        out_specs=[new(w_fox), new(w_sb)],
        out_shape=[jax.ShapeDtypeStruct((batch * s, w_fox), BF16),
                   jax.ShapeDtypeStruct((batch * s, w_sb), BF16)],
        compiler_params=_cparams(("arbitrary",)),
        name="sample_attn",
    )(qf, kf, vf, ck, cv, suf, cum, qb, kb, vb, cbk, cbv)


def _merge_kernel(x_ref, sc_ref, sh_ref, gt_ref, gmix_ref, ysgu_ref, ofox_ref, osb_ref,
                  wg_ref, bg_ref, wbs_ref, wbf_ref, wbb_ref, wo_ref, o_ref):
    x = x_ref[...]
    d = x.shape[1]
    h = _modulated_norm(x, gmix_ref[...], sc_ref[...], sh_ref[...])
    gates = _dot(h.astype(BF16), wg_ref[...]) + bg_ref[...]
    gates = 1.0 / (1.0 + jnp.exp(-gates))
    merged = gates[:, 0:d] * _dot(ysgu_ref[...], wbs_ref[...]) \
        + gates[:, d:2 * d] * _dot(ofox_ref[...], wbf_ref[...]) \
        + gates[:, 2 * d:3 * d] * _dot(osb_ref[...], wbb_ref[...])
    o_ref[...] = x + gt_ref[...] * _dot(merged.astype(BF16), wo_ref[...])


def _merge(x, sc, sh, gt, gmix, ysgu, ofox, osb, wg, bg, wbs, wbf, wbb, wo, *, tm):
    n, d = x.shape
    if sc.shape[0] == 1:
        mod_spec = pl.BlockSpec((1, d), lambda i: (0, 0))
    else:
        mod_spec = pl.BlockSpec((tm, d), lambda i: (i, 0))
    row = lambda width: pl.BlockSpec((tm, width), lambda i: (i, 0))
    return pl.pallas_call(
        _merge_kernel,
        grid=(n // tm,),
        in_specs=[row(d), mod_spec, mod_spec, mod_spec, _const_spec(gmix.shape),
                  row(ysgu.shape[1]), row(ofox.shape[1]), row(osb.shape[1]),
                  _const_spec(wg.shape), _const_spec(bg.shape), _const_spec(wbs.shape),
                  _const_spec(wbf.shape), _const_spec(wbb.shape), _const_spec(wo.shape)],
        out_specs=row(d),
        out_shape=jax.ShapeDtypeStruct((n, d), F32),
        compiler_params=_cparams(("arbitrary",)),
        name="merge",
    )(x, sc, sh, gt, gmix, ysgu, ofox, osb, wg, bg, wbs, wbf, wbb, wo)


def _ffn_kernel(x_ref, sc_ref, sh_ref, gt_ref, g_ref, wi_ref, wo_ref, o_ref, *, d_ff):
    x = x_ref[...]
    h = _modulated_norm(x, g_ref[...], sc_ref[...], sh_ref[...])
    ag = _dot(h.astype(BF16), wi_ref[...])
    a = ag[:, 0:d_ff]
    act = a * (1.0 / (1.0 + jnp.exp(-a))) * ag[:, d_ff:2 * d_ff]
    o_ref[...] = x + gt_ref[...] * _dot(act.astype(BF16), wo_ref[...])


def _ffn(x, sc, sh, gt, g, wi, wo, *, tm):
    n, d = x.shape
    d_ff = wo.shape[0]
    if sc.shape[0] == 1:
        mod_spec = pl.BlockSpec((1, d), lambda i: (0, 0))
    else:
        mod_spec = pl.BlockSpec((tm, d), lambda i: (i, 0))
    row = pl.BlockSpec((tm, d), lambda i: (i, 0))
    return pl.pallas_call(
        functools.partial(_ffn_kernel, d_ff=d_ff),
        grid=(n // tm,),
        in_specs=[row, mod_spec, mod_spec, mod_spec, _const_spec(g.shape), _const_spec(wi.shape),
                  _const_spec(wo.shape)],
        out_specs=row,
        out_shape=jax.ShapeDtypeStruct((n, d), F32),
        compiler_params=_cparams(("arbitrary",)),
        name="ffn",
    )(x, sc, sh, gt, g, wi, wo)


def _indicator(width, group):
    idx = np.arange(width) // group
    return jnp.asarray(idx[:, None] == idx[None, :], dtype=BF16)


def kernel(x_prompt, x_sample, c_prompt, c_sample, cache_fox_k, cache_fox_v, cache_fox_logf,
           cache_sb_k, cache_sb_v, w_ada, b_ada, g_mix, g_ffn, w_in, g_sgu_v, w_sgu, b_sgu, b_fgt,
           g_q, g_k, w_br_sgu, w_br_fox, w_br_sb, w_gate, b_gate, w_out, w_ffn_in, w_ffn_out):
    batch, seq, d = x_prompt.shape
    dec_batch, dec_seq, _ = x_sample.shape
    depth = w_ada.shape[0]
    past = cache_fox_k.shape[2]
    h_fox, h_sb = cache_fox_k.shape[3], cache_sb_k.shape[3]
    w_fox, w_sb = h_fox * HEAD_DIM, h_sb * HEAD_DIM
    g_sgu, cg = g_sgu_v.shape[1], g_sgu_v.shape[2]
    w_sgu_ = g_sgu * cg
    assert batch == 1 and g_sgu == G_SGU and w_sgu.shape[2] == SGU_LEN
    n_dec = dec_batch * dec_seq

    n_c = batch + dec_batch
    c_rows = -(-n_c // 8) * 8
    c_all = jnp.zeros((c_rows, d), F32).at[:n_c].set(jnp.concatenate([c_prompt, c_sample], axis=0))
    mod = _modulation(c_all, w_ada, b_ada)

    offs = np.cumsum([0, w_sgu_, w_sgu_, w_fox, w_fox, w_fox, h_fox, w_sb, w_sb, w_sb]).tolist()
    f_cols = jnp.zeros((depth, d, LANES), F32).at[:, :, :h_fox].set(w_in[:, :, offs[5]:offs[6]])
    w_main = jnp.concatenate([w_in[:, :, :offs[5]], w_in[:, :, offs[6]:], f_cols],
                             axis=2).astype(BF16)
    bf_pad = jnp.zeros((depth, 1, LANES), F32).at[:, 0, :h_fox].set(b_fgt)
    ind96, ind64 = _indicator(w_sgu_, cg), _indicator(w_fox, HEAD_DIM)
    gq_t = jnp.tile(g_q, (1, h_fox)).reshape(depth, 1, w_fox)
    gk_t = jnp.tile(g_k, (1, h_fox)).reshape(depth, 1, w_fox)
    gsgu = g_sgu_v.reshape(depth, 1, w_sgu_)
    msgu_p = jnp.transpose(w_sgu, (0, 2, 1, 3)).reshape(depth, SGU_LEN, g_sgu * SGU_LEN)
    reps = SGU_LEN // dec_seq
    w_small = jnp.tile(w_sgu[:, :, :dec_seq, :dec_seq], (1, 1, reps, reps))
    msgu_s = jnp.transpose(w_small, (0, 2, 1, 3)).reshape(depth, SGU_LEN, g_sgu * SGU_LEN)
    bsgu_p = jnp.repeat(jnp.transpose(b_sgu, (0, 2, 1)), cg, axis=2)
    bsgu_s = jnp.tile(bsgu_p[:, :dec_seq], (1, reps, 1))
    wg, wbs, wbf, wbb = (w.astype(BF16) for w in (w_gate, w_br_sgu, w_br_fox, w_br_sb))
    wo, wfi, wfo = (w.astype(BF16) for w in (w_out, w_ffn_in, w_ffn_out))

    xp = x_prompt.reshape(seq, d)
    xs = x_sample.reshape(n_dec, d)
    tm_p = min(512, seq)
    tm_f = min(256, seq)
    tq = min(256, seq)
    st_p, st_s = [], []
    for l in range(depth):
        mp = [mod[l, 0:1, k * d:(k + 1) * d] for k in range(6)]
        ms = [jnp.repeat(mod[l, batch:batch + dec_batch, k * d:(k + 1) * d], dec_seq, axis=0)
              for k in range(6)]
        gmix, gffn = g_mix[l].reshape(1, d), g_ffn[l].reshape(1, d)
        shared = (gmix, w_main[l], gsgu[l], gq_t[l], gk_t[l], bf_pad[l], ind96, ind64)

        (ysgu, qf, kf16, vf16, qb, kb16, vb16, kf32, vf32, kb32, vb32, logft) = _inproj(
            xp, mp[1], mp[0], *shared, msgu_p[l], bsgu_p[l], tm=tm_p, period=SGU_LEN,
            emit_sguv=False)
        f_cum, jlo = _prompt_cumsum(logft, g_q[l].reshape(1, HEAD_DIM), g_k[l].reshape(1, HEAD_DIM))
        ofox = _fox_prompt(jlo, qf, kf16, vf16, f_cum, tq=tq)
        osb = _sb_prompt(qb, kb16, vb16, tq=min(128, seq))
        x1 = _merge(xp, mp[1], mp[0], mp[2], gmix, ysgu, ofox, osb, wg[l],
                    b_gate[l].reshape(1, 3 * d), wbs[l], wbf[l], wbb[l], wo[l], tm=tm_p)
        xp = _ffn(x1, mp[4], mp[3], mp[5], gffn, wfi[l], wfo[l], tm=tm_f)
        st_p.append((kf32.reshape(batch, seq, h_fox, HEAD_DIM),
                     vf32.reshape(batch, seq, h_fox, HEAD_DIM),
                     logft[:h_fox].T.reshape(batch, seq, h_fox),
                     kb32.reshape(batch, seq, h_sb, HEAD_DIM),
                     vb32.reshape(batch, seq, h_sb, HEAD_DIM)))

        (ysgu, qf, kf16, vf16, qb, kb16, vb16, kf32, vf32, kb32, vb32, logft, sguv) = _inproj(
            xs, ms[1], ms[0], *shared, msgu_s[l], bsgu_s[l], tm=n_dec, period=dec_seq,
            emit_sguv=True)
        clf = jnp.zeros((dec_batch, 8, past), F32).at[:, :h_fox].set(
            jnp.transpose(cache_fox_logf[l], (0, 2, 1)))
        suf, cum = _sample_cumsum(clf.reshape(dec_batch * 8, past), logft, dec_seq)
        cum_b = jnp.transpose(cum.reshape(8, dec_batch, dec_seq), (1, 0, 2))
        ofox, osb = _sample_attn(
            qf, kf16, vf16, cache_fox_k[l].reshape(dec_batch, past, w_fox),
            cache_fox_v[l].reshape(dec_batch, past, w_fox), suf.reshape(dec_batch, 8, past), cum_b,
            qb, kb16, vb16, cache_sb_k[l].reshape(dec_batch, past, w_sb),
            cache_sb_v[l].reshape(dec_batch, past, w_sb), batch=dec_batch, s=dec_seq)
        x1 = _merge(xs, ms[1], ms[0], ms[2], gmix, ysgu, ofox, osb, wg[l],
                    b_gate[l].reshape(1, 3 * d), wbs[l], wbf[l], wbb[l], wo[l], tm=n_dec)
        xs = _ffn(x1, ms[4], ms[3], ms[5], gffn, wfi[l], wfo[l], tm=n_dec)
        st_s.append((kf32.reshape(dec_batch, dec_seq, h_fox, HEAD_DIM),
                     vf32.reshape(dec_batch, dec_seq, h_fox, HEAD_DIM),
                     logft[:h_fox].T.reshape(dec_batch, dec_seq, h_fox),
                     kb32.reshape(dec_batch, dec_seq, h_sb, HEAD_DIM),
                     vb32.reshape(dec_batch, dec_seq, h_sb, HEAD_DIM),
                     sguv.reshape(dec_batch, dec_seq, w_sgu_)))

    def stack(states, idx):
        return jnp.stack([s[idx] for s in states], axis=0)

    return (xp.reshape(batch, seq, d), xs.reshape(dec_batch, dec_seq, d),
            stack(st_p, 0), stack(st_p, 1), stack(st_p, 2), stack(st_p, 3), stack(st_p, 4),
            stack(st_s, 0), stack(st_s, 1), stack(st_s, 2), stack(st_s, 3), stack(st_s, 4),
            stack(st_s, 5))
```

```python
import functools

import numpy as np
import jax
import jax.numpy as jnp
from jax import lax
from jax.experimental import pallas as pl
from jax.experimental.pallas import tpu as pltpu

F32 = jnp.float32
BF16 = jnp.bfloat16

EPS = 1e-6
HEAD_DIM = 64
LANES = 128
CHUNK = 64
SGU_LEN = 128
G_SGU = 4
NEG_BIG = -1e30

PRUNE_LOG = 30.0

VMEM_LIMIT = 56 * 1024 * 1024


def _cparams(sem):
    return pltpu.CompilerParams(dimension_semantics=sem, vmem_limit_bytes=VMEM_LIMIT)


def _const_spec(shape):
    nd = len(shape)
    return pl.BlockSpec(shape, lambda *_: (0,) * nd, pipeline_mode=pl.Buffered(1))


def _dot(a, b):
    return jnp.dot(a, b, preferred_element_type=F32)


def _dot_nt(a, b):
    return lax.dot_general(a, b, (((1,), (1,)), ((), ())), preferred_element_type=F32)


def _split3(x):
    h = x.astype(BF16)
    r = x - h.astype(F32)
    m = r.astype(BF16)
    l = (r - m.astype(F32)).astype(BF16)
    return h, m, l


def _split2(x):
    h = x.astype(BF16)
    l = (x - h.astype(F32)).astype(BF16)
    return h, l


def _mod_kernel(c_ref, w_ref, b_ref, o_ref):
    c = c_ref[...]
    s = c * (1.0 / (1.0 + jnp.exp(-c)))
    o_ref[0] = _dot(s.astype(BF16), w_ref[0].astype(BF16)) + b_ref[0]


def _modulation(c_all, w_ada, b_ada):
    depth, d, n6 = w_ada.shape
    rows = c_all.shape[0]
    tn = 1024
    return pl.pallas_call(
        _mod_kernel,
        grid=(depth, n6 // tn),
        in_specs=[
            pl.BlockSpec((rows, d), lambda l, j: (0, 0)),
            pl.BlockSpec((1, d, tn), lambda l, j: (l, 0, j)),
            pl.BlockSpec((1, 1, tn), lambda l, j: (l, 0, j)),
        ],
        out_specs=pl.BlockSpec((1, rows, tn), lambda l, j: (l, 0, j)),
        out_shape=jax.ShapeDtypeStruct((depth, rows, n6), F32),
        compiler_params=_cparams(("arbitrary", "arbitrary")),
        name="adaln_mod",
    )(c_all, w_ada, b_ada.reshape(depth, 1, n6))


def _modulated_norm(x, g, sc, sh):
    ms = jnp.mean(x * x, axis=-1, keepdims=True)
    return (x * lax.rsqrt(ms + EPS)) * g * (1.0 + sc) + sh


def _group_rms(t, ind, inv_size, g):
    ss = _dot((t * t).astype(BF16), ind)
    return t * lax.rsqrt(ss * inv_size + EPS) * g


def _log_sigmoid(x):
    return jnp.minimum(x, 0.0) - jnp.log(1.0 + jnp.exp(-jnp.abs(x)))


def _inproj_kernel(x_ref, sc_ref, sh_ref, gmix_ref, w_ref, gsgu_ref, gq_ref, gk_ref, bf_ref,
                   ind96_ref, ind64_ref, msgu_ref, bsgu_ref,
                   ysgu_ref, qf_ref, kf16_ref, vf16_ref, qb_ref, kb16_ref, vb16_ref,
                   kf32_ref, vf32_ref, kb32_ref, vb32_ref, logft_ref, *maybe_sguv_ref,
                   tm, w_sgu, w_fox, w_sb, period):
    x = x_ref[...]
    h = _modulated_norm(x, gmix_ref[...], sc_ref[...], sh_ref[...])
    p = _dot(h.astype(BF16), w_ref[...])

    o = 0
    u = p[:, o:o + w_sgu]; o += w_sgu
    vs = p[:, o:o + w_sgu]; o += w_sgu
    qf = p[:, o:o + w_fox]; o += w_fox
    kf = p[:, o:o + w_fox]; o += w_fox
    vf = p[:, o:o + w_fox]; o += w_fox
    qb = p[:, o:o + w_sb]; o += w_sb
    kb = p[:, o:o + w_sb]; o += w_sb
    vb = p[:, o:o + w_sb]; o += w_sb
    fl = p[:, o:o + LANES]

    scale = HEAD_DIM ** -0.5
    ind64 = ind64_ref[...]
    qfn = _group_rms(qf, ind64, 1.0 / HEAD_DIM, gq_ref[...])
    kfn = _group_rms(kf, ind64, 1.0 / HEAD_DIM, gk_ref[...])
    qf_ref[...] = (qfn * scale).astype(BF16)
    kf32_ref[...] = kfn
    kf16_ref[...] = kfn.astype(BF16)
    vf32_ref[...] = vf
    vf16_ref[...] = vf.astype(BF16)
    qb_ref[...] = (qb * scale).astype(BF16)
    kb32_ref[...] = kb
    kb16_ref[...] = kb.astype(BF16)
    vb32_ref[...] = vb
    vb16_ref[...] = vb.astype(BF16)

    lf = _log_sigmoid(fl + bf_ref[...])
    logft_ref[...] = lf.T[0:8, :]

    cg = w_sgu // G_SGU
    vsn = _group_rms(vs, ind96_ref[...], 1.0 / cg, gsgu_ref[...])
    if maybe_sguv_ref:
        maybe_sguv_ref[0][...] = vsn
    r = lax.broadcasted_iota(jnp.int32, (SGU_LEN, G_SGU * SGU_LEN), 0)
    c = lax.broadcasted_iota(jnp.int32, (SGU_LEN, G_SGU * SGU_LEN), 1) % SGU_LEN
    keep = (r // period == c // period) & ((c % period) // CHUNK <= (r % period) // CHUNK)
    mix = jnp.where(keep, msgu_ref[...], 0.0).astype(BF16)
    lane_group = lax.broadcasted_iota(jnp.int32, (SGU_LEN, w_sgu), 1) // cg
    vsb = vsn.astype(BF16)
    spat = []
    for ci in range(tm // SGU_LEN):
        vc = vsb[ci * SGU_LEN:(ci + 1) * SGU_LEN]
        stacked = jnp.concatenate(
            [jnp.where(lane_group == g, vc, jnp.zeros_like(vc)) for g in range(G_SGU)], axis=0)
        spat.append(_dot(mix, stacked) + bsgu_ref[...])
    spat = jnp.concatenate(spat, axis=0) if len(spat) > 1 else spat[0]
    ysgu_ref[...] = (u * spat).astype(BF16)


def _inproj(x, sc, sh, gmix, w, gsgu, gq, gk, bf, ind96, ind64, msgu, bsgu, *, tm, period,
            emit_sguv):
    n, d = x.shape
    w_sgu, w_fox = gsgu.shape[1], gq.shape[1]
    w_sb = (w.shape[1] - LANES - 2 * w_sgu - 3 * w_fox) // 3
    mod_rows = sc.shape[0]
    if mod_rows == 1:
        mod_spec = pl.BlockSpec((1, d), lambda i: (0, 0))
    else:
        mod_spec = pl.BlockSpec((tm, d), lambda i: (i, 0))
    row = lambda width: pl.BlockSpec((tm, width), lambda i: (i, 0))
    out_specs = [row(w_sgu), row(w_fox), row(w_fox), row(w_fox), row(w_sb), row(w_sb), row(w_sb),
                 row(w_fox), row(w_fox), row(w_sb), row(w_sb),
                 pl.BlockSpec((8, tm), lambda i: (0, i))]
    sds = jax.ShapeDtypeStruct
    out_shape = [sds((n, w_sgu), BF16), sds((n, w_fox), BF16), sds((n, w_fox), BF16),
                 sds((n, w_fox), BF16), sds((n, w_sb), BF16), sds((n, w_sb), BF16),
                 sds((n, w_sb), BF16),
                 sds((n, w_fox), F32), sds((n, w_fox), F32), sds((n, w_sb), F32),
                 sds((n, w_sb), F32), sds((8, n), F32)]
    if emit_sguv:
        out_specs.append(row(w_sgu))
        out_shape.append(sds((n, w_sgu), F32))
    kern = functools.partial(_inproj_kernel, tm=tm, w_sgu=w_sgu, w_fox=w_fox, w_sb=w_sb,
                             period=period)
    return pl.pallas_call(
        kern,
        grid=(n // tm,),
        in_specs=[row(d), mod_spec, mod_spec, _const_spec(gmix.shape), _const_spec(w.shape),
                  _const_spec(gsgu.shape), _const_spec(gq.shape), _const_spec(gk.shape),
                  _const_spec(bf.shape), _const_spec(ind96.shape), _const_spec(ind64.shape),
                  _const_spec(msgu.shape), _const_spec(bsgu.shape)],
        out_specs=out_specs,
        out_shape=out_shape,
        compiler_params=_cparams(("arbitrary",)),
        name="inproj",
    )(x, sc, sh, gmix, w, gsgu, gq, gk, bf, ind96, ind64, msgu, bsgu)


def _seq_cumsum(x, nc, reverse_exclusive):
    rows = x.shape[0]
    a = lax.broadcasted_iota(jnp.int32, (LANES, LANES), 0)
    b = lax.broadcasted_iota(jnp.int32, (LANES, LANES), 1)
    tri = (a > b) if reverse_exclusive else (a <= b)
    tri = jnp.where(tri, 1.0, 0.0).astype(BF16)
    ones = jnp.ones((LANES, LANES), BF16)
    xh, xm, xl = _split3(x)
    within = _dot(xh, tri) + _dot(xm, tri) + _dot(xl, tri)
    tot = _dot(xh, ones) + _dot(xm, ones) + _dot(xl, ones)
    ra = lax.broadcasted_iota(jnp.int32, (rows, rows), 0)
    rb = lax.broadcasted_iota(jnp.int32, (rows, rows), 1)
    other = (rb > ra) if reverse_exclusive else (rb < ra)
    blk = jnp.where((ra // nc == rb // nc) & other, 1.0, 0.0).astype(BF16)
    th, tm_, tl = _split3(tot)
    return within + _dot(blk, th) + _dot(blk, tm_) + _dot(blk, tl)


def _prompt_cumsum_kernel(lf_ref, gq_ref, gk_ref, f_ref, jlo_ref, *, nc):
    f = _seq_cumsum(lf_ref[...], nc, reverse_exclusive=False)
    f_ref[...] = f
    bound = 1.01 * (HEAD_DIM ** 0.5) * jnp.max(jnp.abs(gq_ref[...]), axis=1, keepdims=True) \
        * jnp.max(jnp.abs(gk_ref[...]), axis=1, keepdims=True)
    cut = -(2.0 * bound + PRUNE_LOG)
    for h in range(8):
        e = f[h * nc:(h + 1) * nc, :]
        f_end = e[:, LANES - 1:LANES]
        f_start = e.T[0:1, :]
        skip = jnp.where(f_start - f_end < cut, 1.0, 0.0)
        jlo_ref[h:h + 1, :] = jnp.sum(skip, axis=0, keepdims=True).astype(jnp.int32)


def _prompt_cumsum(logft, gq, gk):
    n = logft.shape[1]
    nc = n // LANES
    f, jlo = pl.pallas_call(
        functools.partial(_prompt_cumsum_kernel, nc=nc),
        out_shape=[jax.ShapeDtypeStruct((8 * nc, LANES), F32),
                   jax.ShapeDtypeStruct((8, nc), jnp.int32)],
        compiler_params=pltpu.CompilerParams(vmem_limit_bytes=VMEM_LIMIT),
        name="logf_cumsum",
    )(logft.reshape(8 * nc, LANES), gq, gk)
    return f.reshape(8, n), jlo


def _sample_cumsum_kernel(clf_ref, lf_ref, suf_ref, cum_ref, *, nc, dec_seq):
    suf_ref[...] = _seq_cumsum(clf_ref[...], nc, reverse_exclusive=True)
    a = lax.broadcasted_iota(jnp.int32, (LANES, LANES), 0)
    b = lax.broadcasted_iota(jnp.int32, (LANES, LANES), 1)
    tri = jnp.where((a // dec_seq == b // dec_seq) & (a <= b), 1.0, 0.0).astype(BF16)
    xh, xm, xl = _split3(lf_ref[...])
    cum_ref[...] = _dot(xh, tri) + _dot(xm, tri) + _dot(xl, tri)


def _sample_cumsum(clogf_t, logft, dec_seq):
    rows, past = clogf_t.shape
    nc = past // LANES
    suf, cum = pl.pallas_call(
        functools.partial(_sample_cumsum_kernel, nc=nc, dec_seq=dec_seq),
        out_shape=[jax.ShapeDtypeStruct((rows * nc, LANES), F32),
                   jax.ShapeDtypeStruct(logft.shape, F32)],
        compiler_params=pltpu.CompilerParams(vmem_limit_bytes=VMEM_LIMIT),
        name="sample_logf_cumsum",
    )(clogf_t.reshape(rows * nc, LANES), logft)
    return suf.reshape(rows, past), cum


def _head_queries(q_ref, heads, tq):
    lane = lax.broadcasted_iota(jnp.int32, (tq, LANES), 1)
    qs = []
    for h in range(heads):
        qp = q_ref[:, (h // 2) * LANES:(h // 2 + 1) * LANES]
        keep = (lane < HEAD_DIM) if h % 2 == 0 else (lane >= HEAD_DIM)
        qs.append(jnp.where(keep, qp, jnp.zeros_like(qp)))
    return qs


def _fox_kernel(jlo_ref, q_ref, k_ref, v_ref, f_ref, o_ref, *, tq, heads):
    i = pl.program_id(0)
    tk = tq
    qs = _head_queries(q_ref, heads, tq)
    lane = lax.broadcasted_iota(jnp.int32, (tq, LANES), 1)
    row = lax.broadcasted_iota(jnp.int32, (tq, tk), 0)
    col = lax.broadcasted_iota(jnp.int32, (tq, tk), 1)
    q0 = pl.multiple_of(i * tq, tq)
    fref = [f_ref[h:h + 1, pl.ds(q0, LANES)][:, 0:1] for h in range(heads)]

    def step(j, carry, masked):
        k0 = pl.multiple_of(j * tk, tk)
        out = []
        for p in range(heads // 2):
            k = k_ref[pl.ds(k0, tk), p * LANES:(p + 1) * LANES]
            v = v_ref[pl.ds(k0, tk), p * LANES:(p + 1) * LANES]
            for h in (2 * p, 2 * p + 1):
                m, l, acc = carry[3 * h:3 * h + 3]
                s = _dot_nt(qs[h], k) + (fref[h] - f_ref[h:h + 1, pl.ds(k0, tk)])
                if masked:
                    s = jnp.where(col <= row, s, NEG_BIG)
                mn = jnp.maximum(m, jnp.max(s, axis=1, keepdims=True))
                alpha = jnp.exp(m - mn)
                pe = jnp.exp(s - mn)
                l = alpha * l + jnp.sum(pe, axis=1, keepdims=True)
                acc = alpha * acc + _dot(pe.astype(BF16), v)
                out += [mn, l, acc]
        return tuple(out)

    init = []
    for _ in range(heads):
        init += [jnp.full((tq, 1), NEG_BIG, F32), jnp.zeros((tq, 1), F32),
                 jnp.zeros((tq, LANES), F32)]
    cpb = tq // LANES
    jlo = jlo_ref[0, i * cpb]
    for h in range(1, heads):
        jlo = jnp.minimum(jlo, jlo_ref[h, i * cpb])
    carry = lax.fori_loop(jlo // cpb, i, lambda j, c: step(j, c, False), tuple(init))
    res = step(i, carry, True)
    for p in range(heads // 2):
        _, l0, a0, _, l1, a1 = res[6 * p:6 * p + 6]
        o_ref[:, p * LANES:(p + 1) * LANES] = jnp.where(
            lane < HEAD_DIM, a0 / l0, a1 / l1).astype(BF16)


def _fox_prompt(jlo, q, k, v, f, *, tq):
    n, w = q.shape
    heads = w // HEAD_DIM
    grid_spec = pltpu.PrefetchScalarGridSpec(
        num_scalar_prefetch=1,
        grid=(n // tq,),
        in_specs=[
            pl.BlockSpec((tq, w), lambda i, jl: (i, 0)),
            pl.BlockSpec((n, w), lambda i, jl: (0, 0), pipeline_mode=pl.Buffered(1)),
            pl.BlockSpec((n, w), lambda i, jl: (0, 0), pipeline_mode=pl.Buffered(1)),
            pl.BlockSpec((8, n), lambda i, jl: (0, 0), pipeline_mode=pl.Buffered(1)),
        ],
        out_specs=pl.BlockSpec((tq, w), lambda i, jl: (i, 0)),
    )
    return pl.pallas_call(
        functools.partial(_fox_kernel, tq=tq, heads=heads),
        grid_spec=grid_spec,
        out_shape=jax.ShapeDtypeStruct((n, w), BF16),
        compiler_params=_cparams(("arbitrary",)),
        name="fox_prompt",
    )(jlo, q, k, v, f)


def _sb_block(z, carry, v, tri, mask, v_feature_major=False):
    lg = jnp.log(1.0 + jnp.exp(-jnp.abs(z)))
    sp = jnp.maximum(z, 0.0) + lg
    if mask is not None:
        sp = jnp.where(mask, sp, 0.0)
    hi, lo = _split2(sp)
    later = _dot(hi, tri) + _dot(lo, tri)
    a = jnp.exp((jnp.minimum(z, 0.0) - lg) - later - carry)
    if mask is not None:
        a = jnp.where(mask, a, 0.0)
    pv = _dot_nt(a.astype(BF16), v) if v_feature_major else _dot(a.astype(BF16), v)
    return pv, carry + jnp.sum(sp, axis=1, keepdims=True)


def _suffix_matrix(tk):
    a = lax.broadcasted_iota(jnp.int32, (tk, tk), 0)
    b = lax.broadcasted_iota(jnp.int32, (tk, tk), 1)
    return jnp.where(a > b, 1.0, 0.0).astype(BF16)


def _sb_kernel(q_ref, k_ref, v_ref, o_ref, *, tq, heads):
    i = pl.program_id(0)
    tk = tq
    qs = _head_queries(q_ref, heads, tq)
    lane = lax.broadcasted_iota(jnp.int32, (tq, LANES), 1)
    row = lax.broadcasted_iota(jnp.int32, (tq, tk), 0)
    col = lax.broadcasted_iota(jnp.int32, (tq, tk), 1)
    tri = _suffix_matrix(tk)

    def step(j, carry, mask):
        k0 = pl.multiple_of(j * tk, tk)
        out = []
        for p in range(heads // 2):
            k = k_ref[pl.ds(k0, tk), p * LANES:(p + 1) * LANES]
            v = v_ref[pl.ds(k0, tk), p * LANES:(p + 1) * LANES]
            for h in (2 * p, 2 * p + 1):
                cr, acc = carry[2 * h:2 * h + 2]
                pv, cr = _sb_block(_dot_nt(qs[h], k), cr, v, tri, mask)
                out += [cr, acc + pv]
        return tuple(out)

    zero = (jnp.zeros((tq, 1), F32), jnp.zeros((tq, LANES), F32))
    carry = step(i, zero * heads, col < row)
    carry = step(jnp.maximum(i - 1, 0), carry, jnp.broadcast_to(i > 0, (tq, tk)))

    def cond(state):
        live = jnp.min(state[1])
        for h in range(1, heads):
            live = jnp.minimum(live, jnp.min(state[1 + 2 * h]))
        return (state[0] >= 0) & (live < PRUNE_LOG)

    def body(state):
        j = state[0]
        return (j - 1,) + step(j, state[1:], None)

    res = lax.while_loop(cond, body, (i - 2,) + carry)[1:]
    for p in range(heads // 2):
        o_ref[:, p * LANES:(p + 1) * LANES] = jnp.where(
            lane < HEAD_DIM, res[4 * p + 1], res[4 * p + 3]).astype(BF16)


def _sb_prompt(q, k, v, *, tq):
    n, w = q.shape
    return pl.pallas_call(
        functools.partial(_sb_kernel, tq=tq, heads=w // HEAD_DIM),
        grid=(n // tq,),
        in_specs=[
            pl.BlockSpec((tq, w), lambda i: (i, 0)),
            pl.BlockSpec((n, w), lambda i: (0, 0), pipeline_mode=pl.Buffered(1)),
            pl.BlockSpec((n, w), lambda i: (0, 0), pipeline_mode=pl.Buffered(1)),
        ],
        out_specs=pl.BlockSpec((tq, w), lambda i: (i, 0)),
        out_shape=jax.ShapeDtypeStruct((n, w), BF16),
        compiler_params=_cparams(("arbitrary",)),
        name="sb_prompt",
    )(q, k, v)


def _head_rows(x, heads):
    lane_head = lax.broadcasted_iota(jnp.int32, x.shape, 1) // HEAD_DIM
    return jnp.concatenate(
        [jnp.where(lane_head == h, x, jnp.zeros_like(x)) for h in range(heads)], axis=0)


def _fold_heads(o, heads, s):
    lane_head = lax.broadcasted_iota(jnp.int32, (s, o.shape[1]), 1) // HEAD_DIM
    out = jnp.zeros((s, o.shape[1]), F32)
    for h in range(heads):
        out = jnp.where(lane_head == h, o[h * s:(h + 1) * s], out)
    return out


def _sample_attn_kernel(qf_ref, kf_ref, vf_ref, ck_ref, cv_ref, suf_ref, cum_ref,
                        qb_ref, kb_ref, vb_ref, cbk_ref, cbv_ref, of_ref, ob_ref,
                        *, s, h_fox, h_sb, past):
    qa = _head_rows(qf_ref[...], h_fox)
    rows = h_fox * s
    ck = ck_ref[0, 0].astype(BF16)
    cv = cv_ref[0, 0].astype(BF16)
    suf = suf_ref[0]
    cum = cum_ref[0]
    bias_c = jnp.concatenate(
        [jnp.broadcast_to(suf[h:h + 1, :], (s, past)) for h in range(h_fox)], axis=0)
    bias_n = jnp.concatenate(
        [jnp.broadcast_to(-cum[h:h + 1, :], (s, s)) for h in range(h_fox)], axis=0)
    lc = _dot(qa, ck) + bias_c
    ln = _dot_nt(qa, kf_ref[...]) + bias_n
    r_pos = lax.broadcasted_iota(jnp.int32, (rows, s), 0) % s
    k_pos = lax.broadcasted_iota(jnp.int32, (rows, s), 1)
    ln = jnp.where(k_pos <= r_pos, ln, NEG_BIG)
    m = jnp.maximum(jnp.max(lc, axis=1, keepdims=True), jnp.max(ln, axis=1, keepdims=True))
    pc = jnp.exp(lc - m)
    pn = jnp.exp(ln - m)
    den = jnp.sum(pc, axis=1, keepdims=True) + jnp.sum(pn, axis=1, keepdims=True)
    o = (_dot_nt(pc.astype(BF16), cv) + _dot(pn.astype(BF16), vf_ref[...])) / den
    of_ref[...] = _fold_heads(o, h_fox, s).astype(BF16)

    qb = _head_rows(qb_ref[...], h_sb)
    rows_b = h_sb * s
    rb = lax.broadcasted_iota(jnp.int32, (rows_b, s), 0) % s
    cb = lax.broadcasted_iota(jnp.int32, (rows_b, s), 1)
    acc, carry = _sb_block(_dot_nt(qb, kb_ref[...]), jnp.zeros((rows_b, 1), F32), vb_ref[...],
                           _suffix_matrix(s), cb < rb)
    tri = _suffix_matrix(LANES)

    def body(t, state):
        acc, carry = state
        k0 = pl.multiple_of(past - (t + 1) * LANES, LANES)
        k = cbk_ref[0, 0, :, pl.ds(k0, LANES)].astype(BF16)
        v = cbv_ref[0, 0, :, pl.ds(k0, LANES)].astype(BF16)
        pv, carry = _sb_block(_dot(qb, k), carry, v, tri, None, v_feature_major=True)
        return acc + pv, carry

    acc, _ = lax.fori_loop(0, past // LANES, body, (acc, carry))
    ob_ref[...] = _fold_heads(acc, h_sb, s).astype(BF16)


def _sample_attn(qf, kf, vf, ck, cv, suf, cum, qb, kb, vb, cbk, cbv, *, layer, batch, s):
    w_fox, w_sb = qf.shape[1], qb.shape[1]
    past = ck.shape[3]
    h_fox, h_sb = w_fox // HEAD_DIM, w_sb // HEAD_DIM
    new = lambda w: pl.BlockSpec((s, w), lambda b: (b, 0))
    cache = lambda w: pl.BlockSpec((1, 1, w, past), lambda b: (layer, b, 0, 0))
    kern = functools.partial(_sample_attn_kernel, s=s, h_fox=h_fox, h_sb=h_sb, past=past)
    return pl.pallas_call(
        kern,
        grid=(batch,),
        in_specs=[new(w_fox), new(w_fox), new(w_fox), cache(w_fox), cache(w_fox),
                  pl.BlockSpec((1, 8, past), lambda b: (b, 0, 0)),
                  pl.BlockSpec((1, 8, s), lambda b: (b, 0, 0)),
                  new(w_sb), new(w_sb), new(w_sb), cache(w_sb), cache(w_sb)],
        out_specs=[new(w_fox), new(w_sb)],
        out_shape=[jax.ShapeDtypeStruct((batch * s, w_fox), BF16),
                   jax.ShapeDtypeStruct((batch * s, w_sb), BF16)],
        compiler_params=_cparams(("arbitrary",)),
        name="sample_attn",
    )(qf, kf, vf, ck, cv, suf, cum, qb, kb, vb, cbk, cbv)


def _merge_kernel(x_ref, sc_ref, sh_ref, gt_ref, gmix_ref, ysgu_ref, ofox_ref, osb_ref,
                  wg_ref, bg_ref, wbs_ref, wbf_ref, wbb_ref, wo_ref, o_ref):
    x = x_ref[...]
    d = x.shape[1]
    h = _modulated_norm(x, gmix_ref[...], sc_ref[...], sh_ref[...])
    gates = _dot(h.astype(BF16), wg_ref[...]) + bg_ref[...]
    gates = 1.0 / (1.0 + jnp.exp(-gates))
    merged = gates[:, 0:d] * _dot(ysgu_ref[...], wbs_ref[...]) \
        + gates[:, d:2 * d] * _dot(ofox_ref[...], wbf_ref[...]) \
        + gates[:, 2 * d:3 * d] * _dot(osb_ref[...], wbb_ref[...])
    o_ref[...] = x + gt_ref[...] * _dot(merged.astype(BF16), wo_ref[...])


def _merge(x, sc, sh, gt, gmix, ysgu, ofox, osb, wg, bg, wbs, wbf, wbb, wo, *, tm):
    n, d = x.shape
    if sc.shape[0] == 1:
        mod_spec = pl.BlockSpec((1, d), lambda i: (0, 0))
    else:
        mod_spec = pl.BlockSpec((tm, d), lambda i: (i, 0))
    row = lambda width: pl.BlockSpec((tm, width), lambda i: (i, 0))
    return pl.pallas_call(
        _merge_kernel,
        grid=(n // tm,),
        in_specs=[row(d), mod_spec, mod_spec, mod_spec, _const_spec(gmix.shape),
                  row(ysgu.shape[1]), row(ofox.shape[1]), row(osb.shape[1]),
                  _const_spec(wg.shape), _const_spec(bg.shape), _const_spec(wbs.shape),
                  _const_spec(wbf.shape), _const_spec(wbb.shape), _const_spec(wo.shape)],
        out_specs=row(d),
        out_shape=jax.ShapeDtypeStruct((n, d), F32),
        compiler_params=_cparams(("arbitrary",)),
        name="merge",
    )(x, sc, sh, gt, gmix, ysgu, ofox, osb, wg, bg, wbs, wbf, wbb, wo)


def _ffn_kernel(x_ref, sc_ref, sh_ref, gt_ref, g_ref, wi_ref, wo_ref, o_ref, *, d_ff):
    x = x_ref[...]
    h = _modulated_norm(x, g_ref[...], sc_ref[...], sh_ref[...])
    ag = _dot(h.astype(BF16), wi_ref[...])
    a = ag[:, 0:d_ff]
    act = a * (1.0 / (1.0 + jnp.exp(-a))) * ag[:, d_ff:2 * d_ff]
    o_ref[...] = x + gt_ref[...] * _dot(act.astype(BF16), wo_ref[...])


def _ffn(x, sc, sh, gt, g, wi, wo, *, tm):
    n, d = x.shape
    d_ff = wo.shape[0]
    if sc.shape[0] == 1:
        mod_spec = pl.BlockSpec((1, d), lambda i: (0, 0))
    else:
        mod_spec = pl.BlockSpec((tm, d), lambda i: (i, 0))
    row = pl.BlockSpec((tm, d), lambda i: (i, 0))
    return pl.pallas_call(
        functools.partial(_ffn_kernel, d_ff=d_ff),
        grid=(n // tm,),
        in_specs=[row, mod_spec, mod_spec, mod_spec, _const_spec(g.shape), _const_spec(wi.shape),
                  _const_spec(wo.shape)],
        out_specs=row,
        out_shape=jax.ShapeDtypeStruct((n, d), F32),
        compiler_params=_cparams(("arbitrary",)),
        name="ffn",
    )(x, sc, sh, gt, g, wi, wo)


def _indicator(width, group):
    idx = np.arange(width) // group
    return jnp.asarray(idx[:, None] == idx[None, :], dtype=BF16)


def kernel(x_prompt, x_sample, c_prompt, c_sample, cache_fox_k, cache_fox_v, cache_fox_logf,
           cache_sb_k, cache_sb_v, w_ada, b_ada, g_mix, g_ffn, w_in, g_sgu_v, w_sgu, b_sgu, b_fgt,
           g_q, g_k, w_br_sgu, w_br_fox, w_br_sb, w_gate, b_gate, w_out, w_ffn_in, w_ffn_out):
    batch, seq, d = x_prompt.shape
    dec_batch, dec_seq, _ = x_sample.shape
    depth = w_ada.shape[0]
    past = cache_fox_k.shape[2]
    h_fox, h_sb = cache_fox_k.shape[3], cache_sb_k.shape[3]
    w_fox, w_sb = h_fox * HEAD_DIM, h_sb * HEAD_DIM
    g_sgu, cg = g_sgu_v.shape[1], g_sgu_v.shape[2]
    w_sgu_ = g_sgu * cg
    assert batch == 1 and g_sgu == G_SGU and w_sgu.shape[2] == SGU_LEN
    n_dec = dec_batch * dec_seq

    n_c = batch + dec_batch
    c_rows = -(-n_c // 8) * 8
    c_all = jnp.zeros((c_rows, d), F32).at[:n_c].set(jnp.concatenate([c_prompt, c_sample], axis=0))
    mod = _modulation(c_all, w_ada, b_ada)

    offs = np.cumsum([0, w_sgu_, w_sgu_, w_fox, w_fox, w_fox, h_fox, w_sb, w_sb, w_sb]).tolist()
    f_cols = jnp.zeros((depth, d, LANES), F32).at[:, :, :h_fox].set(w_in[:, :, offs[5]:offs[6]])
    w_main = jnp.concatenate([w_in[:, :, :offs[5]], w_in[:, :, offs[6]:], f_cols],
                             axis=2).astype(BF16)
    bf_pad = jnp.zeros((depth, 1, LANES), F32).at[:, 0, :h_fox].set(b_fgt)
    ind96, ind64 = _indicator(w_sgu_, cg), _indicator(w_fox, HEAD_DIM)
    gq_t = jnp.tile(g_q, (1, h_fox)).reshape(depth, 1, w_fox)
    gk_t = jnp.tile(g_k, (1, h_fox)).reshape(depth, 1, w_fox)
    gsgu = g_sgu_v.reshape(depth, 1, w_sgu_)
    msgu_p = jnp.transpose(w_sgu, (0, 2, 1, 3)).reshape(depth, SGU_LEN, g_sgu * SGU_LEN)
    reps = SGU_LEN // dec_seq
    w_small = jnp.tile(w_sgu[:, :, :dec_seq, :dec_seq], (1, 1, reps, reps))
    msgu_s = jnp.transpose(w_small, (0, 2, 1, 3)).reshape(depth, SGU_LEN, g_sgu * SGU_LEN)
    bsgu_p = jnp.repeat(jnp.transpose(b_sgu, (0, 2, 1)), cg, axis=2)
    bsgu_s = jnp.tile(bsgu_p[:, :dec_seq], (1, reps, 1))
    wg, wbs, wbf, wbb = (w.astype(BF16) for w in (w_gate, w_br_sgu, w_br_fox, w_br_sb))
    wo, wfi, wfo = (w.astype(BF16) for w in (w_out, w_ffn_in, w_ffn_out))

    to_fm = lambda c: jnp.transpose(c, (0, 1, 3, 4, 2)).reshape(
        depth, dec_batch, c.shape[3] * HEAD_DIM, past)
    ck_t, cv_t, cbk_t, cbv_t = (to_fm(c) for c in (cache_fox_k, cache_fox_v, cache_sb_k, cache_sb_v))

    xp = x_prompt.reshape(seq, d)
    xs = x_sample.reshape(n_dec, d)
    tm_p = min(512, seq)
    tm_f = min(256, seq)
    tq = min(256, seq)
    st_p, st_s = [], []
    for l in range(depth):
        mp = [mod[l, 0:1, k * d:(k + 1) * d] for k in range(6)]
        ms = [jnp.repeat(mod[l, batch:batch + dec_batch, k * d:(k + 1) * d], dec_seq, axis=0)
              for k in range(6)]
        gmix, gffn = g_mix[l].reshape(1, d), g_ffn[l].reshape(1, d)
        shared = (gmix, w_main[l], gsgu[l], gq_t[l], gk_t[l], bf_pad[l], ind96, ind64)

        (ysgu, qf, kf16, vf16, qb, kb16, vb16, kf32, vf32, kb32, vb32, logft) = _inproj(
            xp, mp[1], mp[0], *shared, msgu_p[l], bsgu_p[l], tm=tm_p, period=SGU_LEN,
            emit_sguv=False)
        f_cum, jlo = _prompt_cumsum(logft, g_q[l].reshape(1, HEAD_DIM), g_k[l].reshape(1, HEAD_DIM))
        ofox = _fox_prompt(jlo, qf, kf16, vf16, f_cum, tq=tq)
        osb = _sb_prompt(qb, kb16, vb16, tq=min(128, seq))
        x1 = _merge(xp, mp[1], mp[0], mp[2], gmix, ysgu, ofox, osb, wg[l],
                    b_gate[l].reshape(1, 3 * d), wbs[l], wbf[l], wbb[l], wo[l], tm=tm_p)
        xp = _ffn(x1, mp[4], mp[3], mp[5], gffn, wfi[l], wfo[l], tm=tm_f)
        st_p.append((kf32.reshape(batch, seq, h_fox, HEAD_DIM),
                     vf32.reshape(batch, seq, h_fox, HEAD_DIM),
                     logft[:h_fox].T.reshape(batch, seq, h_fox),
                     kb32.reshape(batch, seq, h_sb, HEAD_DIM),
                     vb32.reshape(batch, seq, h_sb, HEAD_DIM)))

        (ysgu, qf, kf16, vf16, qb, kb16, vb16, kf32, vf32, kb32, vb32, logft, sguv) = _inproj(
            xs, ms[1], ms[0], *shared, msgu_s[l], bsgu_s[l], tm=n_dec, period=dec_seq,
            emit_sguv=True)
        clf = jnp.zeros((dec_batch, 8, past), F32).at[:, :h_fox].set(
            jnp.transpose(cache_fox_logf[l], (0, 2, 1)))
        suf, cum = _sample_cumsum(clf.reshape(dec_batch * 8, past), logft, dec_seq)
        cum_b = jnp.transpose(cum.reshape(8, dec_batch, dec_seq), (1, 0, 2))
        ofox, osb = _sample_attn(
            qf, kf16, vf16, ck_t, cv_t, suf.reshape(dec_batch, 8, past), cum_b,
            qb, kb16, vb16, cbk_t, cbv_t, layer=l, batch=dec_batch, s=dec_seq)
        x1 = _merge(xs, ms[1], ms[0], ms[2], gmix, ysgu, ofox, osb, wg[l],
                    b_gate[l].reshape(1, 3 * d), wbs[l], wbf[l], wbb[l], wo[l], tm=n_dec)
        xs = _ffn(x1, ms[4], ms[3], ms[5], gffn, wfi[l], wfo[l], tm=n_dec)
        st_s.append((kf32.reshape(dec_batch, dec_seq, h_fox, HEAD_DIM),
                     vf32.reshape(dec_batch, dec_seq, h_fox, HEAD_DIM),
                     logft[:h_fox].T.reshape(dec_batch, dec_seq, h_fox),
                     kb32.reshape(dec_batch, dec_seq, h_sb, HEAD_DIM),
                     vb32.reshape(dec_batch, dec_seq, h_sb, HEAD_DIM),
                     sguv.reshape(dec_batch, dec_seq, w_sgu_)))

    def stack(states, idx):
        return jnp.stack([s[idx] for s in states], axis=0)

    return (xp.reshape(batch, seq, d), xs.reshape(dec_batch, dec_seq, d),
            stack(st_p, 0), stack(st_p, 1), stack(st_p, 2), stack(st_p, 3), stack(st_p, 4),
            stack(st_s, 0), stack(st_s, 1), stack(st_s, 2), stack(st_s, 3), stack(st_s, 4),
            stack(st_s, 5))
```

```python
import functools

import numpy as np
import jax
import jax.numpy as jnp
from jax import lax
from jax.experimental import pallas as pl
from jax.experimental.pallas import tpu as pltpu

F32 = jnp.float32
BF16 = jnp.bfloat16

EPS = 1e-6
HEAD_DIM = 64
LANES = 128
CHUNK = 64
SGU_LEN = 128
G_SGU = 4
NEG_BIG = -1e30

PRUNE_LOG = 30.0

VMEM_LIMIT = 56 * 1024 * 1024


def _cparams(sem):
    return pltpu.CompilerParams(dimension_semantics=sem, vmem_limit_bytes=VMEM_LIMIT)


def _const_spec(shape):
    nd = len(shape)
    return pl.BlockSpec(shape, lambda *_: (0,) * nd, pipeline_mode=pl.Buffered(1))


def _dot(a, b):
    return jnp.dot(a, b, preferred_element_type=F32)


def _dot_nt(a, b):
    return lax.dot_general(a, b, (((1,), (1,)), ((), ())), preferred_element_type=F32)


def _split3(x):
    h = x.astype(BF16)
    r = x - h.astype(F32)
    m = r.astype(BF16)
    l = (r - m.astype(F32)).astype(BF16)
    return h, m, l


def _split2(x):
    h = x.astype(BF16)
    l = (x - h.astype(F32)).astype(BF16)
    return h, l


def _mod_kernel(c_ref, w_ref, b_ref, o_ref):
    c = c_ref[...]
    s = c * (1.0 / (1.0 + jnp.exp(-c)))
    o_ref[0] = _dot(s.astype(BF16), w_ref[0].astype(BF16)) + b_ref[0]


def _modulation(c_all, w_ada, b_ada):
    depth, d, n6 = w_ada.shape
    rows = c_all.shape[0]
    tn = 1024
    return pl.pallas_call(
        _mod_kernel,
        grid=(depth, n6 // tn),
        in_specs=[
            pl.BlockSpec((rows, d), lambda l, j: (0, 0)),
            pl.BlockSpec((1, d, tn), lambda l, j: (l, 0, j)),
            pl.BlockSpec((1, 1, tn), lambda l, j: (l, 0, j)),
        ],
        out_specs=pl.BlockSpec((1, rows, tn), lambda l, j: (l, 0, j)),
        out_shape=jax.ShapeDtypeStruct((depth, rows, n6), F32),
        compiler_params=_cparams(("arbitrary", "arbitrary")),
        name="adaln_mod",
    )(c_all, w_ada, b_ada.reshape(depth, 1, n6))


def _modulated_norm(x, g, sc, sh):
    ms = jnp.mean(x * x, axis=-1, keepdims=True)
    return (x * lax.rsqrt(ms + EPS)) * g * (1.0 + sc) + sh


def _group_rms(t, ind, inv_size, g):
    ss = _dot((t * t).astype(BF16), ind)
    return t * lax.rsqrt(ss * inv_size + EPS) * g


def _log_sigmoid(x):
    return jnp.minimum(x, 0.0) - jnp.log(1.0 + jnp.exp(-jnp.abs(x)))


def _augmented_keys(kfn, f_cum, heads):
    lane = lax.broadcasted_iota(jnp.int32, f_cum.shape, 1)
    hi, mid, lo = (t.astype(F32) for t in _split3(-f_cum))
    aug_even = pltpu.roll(hi, 64, 1) + pltpu.roll(mid, 72, 1) + pltpu.roll(lo, 80, 1)
    aug_odd = hi + pltpu.roll(mid, 8, 1) + pltpu.roll(lo, 16, 1)
    blocks = []
    for h in range(heads):
        kp = kfn[:, (h // 2) * LANES:(h // 2 + 1) * LANES]
        if h % 2 == 0:
            blocks.append(jnp.where(lane < HEAD_DIM, kp, aug_even))
        else:
            blocks.append(jnp.where(lane >= HEAD_DIM, kp, aug_odd))
    return jnp.concatenate(blocks, axis=1).astype(BF16)


def _inproj_kernel(x_ref, sc_ref, sh_ref, gmix_ref, w_ref, gsgu_ref, gq_ref, gk_ref, bf_ref,
                   ind96_ref, ind64_ref, msgu_ref, bsgu_ref, *rest,
                   tm, w_sgu, w_fox, w_sb, period, sweep):
    if sweep:
        (tril_ref, ysgu_ref, kaug_ref, qft_ref, vft_ref, qbt_ref, kb16_ref, vbt_ref,
         kf32_ref, vf32_ref, kb32_ref, vb32_ref, logft_ref, fcumt_ref, carry_ref) = rest
    else:
        (ysgu_ref, qf_ref, kf16_ref, vf16_ref, qb_ref, kb16_ref, vb16_ref,
         kf32_ref, vf32_ref, kb32_ref, vb32_ref, logft_ref, sguv_ref) = rest
    x = x_ref[...]
    h = _modulated_norm(x, gmix_ref[...], sc_ref[...], sh_ref[...])
    p = _dot(h.astype(BF16), w_ref[...])

    o = 0
    u = p[:, o:o + w_sgu]; o += w_sgu
    vs = p[:, o:o + w_sgu]; o += w_sgu
    qf = p[:, o:o + w_fox]; o += w_fox
    kf = p[:, o:o + w_fox]; o += w_fox
    vf = p[:, o:o + w_fox]; o += w_fox
    qb = p[:, o:o + w_sb]; o += w_sb
    kb = p[:, o:o + w_sb]; o += w_sb
    vb = p[:, o:o + w_sb]; o += w_sb
    fl = p[:, o:o + LANES]

    scale = HEAD_DIM ** -0.5
    ind64 = ind64_ref[...]
    qfn = _group_rms(qf, ind64, 1.0 / HEAD_DIM, gq_ref[...])
    kfn = _group_rms(kf, ind64, 1.0 / HEAD_DIM, gk_ref[...])
    kf32_ref[...] = kfn
    vf32_ref[...] = vf
    kb32_ref[...] = kb
    vb32_ref[...] = vb
    kb16_ref[...] = kb.astype(BF16)
    lf = _log_sigmoid(fl + bf_ref[...])
    logft_ref[...] = lf.T[0:8, :]
    if sweep:
        qft_ref[...] = (qfn * scale).T.astype(BF16)
        vft_ref[...] = vf.T.astype(BF16)
        qbt_ref[...] = (qb * scale).T.astype(BF16)
        vbt_ref[...] = vb.T.astype(BF16)
        @pl.when(pl.program_id(0) == 0)
        def _():
            carry_ref[...] = jnp.zeros(carry_ref.shape, F32)
        lane = lax.broadcasted_iota(jnp.int32, lf.shape, 1)
        lfh, lfm, lfl = _split3(jnp.where(lane < 8, lf, 0.0))
        tril = tril_ref[...]
        f_cum = _dot(tril, lfh) + _dot(tril, lfm) + _dot(tril, lfl) + carry_ref[0:1, :]
        carry_ref[...] = jnp.broadcast_to(f_cum[tm - 1:tm, :], carry_ref.shape)
        fcumt_ref[...] = f_cum.T[0:8, :]
        kaug_ref[...] = _augmented_keys(kfn, f_cum, w_fox // HEAD_DIM)
    else:
        qf_ref[...] = (qfn * scale).astype(BF16)
        kf16_ref[...] = kfn.astype(BF16)
        vf16_ref[...] = vf.astype(BF16)
        qb_ref[...] = (qb * scale).astype(BF16)
        vb16_ref[...] = vb.astype(BF16)

    cg = w_sgu // G_SGU
    vsn = _group_rms(vs, ind96_ref[...], 1.0 / cg, gsgu_ref[...])
    if not sweep:
        sguv_ref[...] = vsn
    r = lax.broadcasted_iota(jnp.int32, (SGU_LEN, G_SGU * SGU_LEN), 0)
    c = lax.broadcasted_iota(jnp.int32, (SGU_LEN, G_SGU * SGU_LEN), 1) % SGU_LEN
    keep = (r // period == c // period) & ((c % period) // CHUNK <= (r % period) // CHUNK)
    mix = jnp.where(keep, msgu_ref[...], 0.0).astype(BF16)
    lane_group = lax.broadcasted_iota(jnp.int32, (SGU_LEN, w_sgu), 1) // cg
    vsb = vsn.astype(BF16)
    spat = []
    for ci in range(tm // SGU_LEN):
        vc = vsb[ci * SGU_LEN:(ci + 1) * SGU_LEN]
        stacked = jnp.concatenate(
            [jnp.where(lane_group == g, vc, jnp.zeros_like(vc)) for g in range(G_SGU)], axis=0)
        spat.append(_dot(mix, stacked) + bsgu_ref[...])
    spat = jnp.concatenate(spat, axis=0) if len(spat) > 1 else spat[0]
    ysgu_ref[...] = (u * spat).astype(BF16)


def _inproj(x, sc, sh, gmix, w, gsgu, gq, gk, bf, ind96, ind64, msgu, bsgu, *, tm, period, sweep):
    n, d = x.shape
    w_sgu, w_fox = gsgu.shape[1], gq.shape[1]
    w_sb = (w.shape[1] - LANES - 2 * w_sgu - 3 * w_fox) // 3
    mod_rows = sc.shape[0]
    if mod_rows == 1:
        mod_spec = pl.BlockSpec((1, d), lambda i: (0, 0))
    else:
        mod_spec = pl.BlockSpec((tm, d), lambda i: (i, 0))
    row = lambda width: pl.BlockSpec((tm, width), lambda i: (i, 0))
    col = lambda height: pl.BlockSpec((height, tm), lambda i: (0, i))
    sds = jax.ShapeDtypeStruct
    states_specs = [row(w_fox), row(w_fox), row(w_sb), row(w_sb), col(8)]
    states_shape = [sds((n, w_fox), F32), sds((n, w_fox), F32), sds((n, w_sb), F32),
                    sds((n, w_sb), F32), sds((8, n), F32)]
    operands = [x, sc, sh, gmix, w, gsgu, gq, gk, bf, ind96, ind64, msgu, bsgu]
    in_specs = [row(d), mod_spec, mod_spec] + [_const_spec(a.shape) for a in operands[3:]]
    scratch = []
    if sweep:
        heads = w_fox // HEAD_DIM
        a = np.arange(tm)
        operands.append(jnp.asarray(a[None, :] <= a[:, None], dtype=BF16))
        in_specs.append(_const_spec((tm, tm)))
        out_specs = [row(w_sgu), row(heads * LANES), col(w_fox), col(w_fox), col(w_sb), row(w_sb),
                     col(w_sb)] + states_specs + [col(8)]
        out_shape = [sds((n, w_sgu), BF16), sds((n, heads * LANES), BF16), sds((w_fox, n), BF16),
                     sds((w_fox, n), BF16), sds((w_sb, n), BF16), sds((n, w_sb), BF16),
                     sds((w_sb, n), BF16)] + states_shape + [sds((8, n), F32)]
        scratch = [pltpu.VMEM((8, LANES), F32)]
    else:
        out_specs = [row(w_sgu), row(w_fox), row(w_fox), row(w_fox), row(w_sb), row(w_sb),
                     row(w_sb)] + states_specs + [row(w_sgu)]
        out_shape = [sds((n, w_sgu), BF16), sds((n, w_fox), BF16), sds((n, w_fox), BF16),
                     sds((n, w_fox), BF16), sds((n, w_sb), BF16), sds((n, w_sb), BF16),
                     sds((n, w_sb), BF16)] + states_shape + [sds((n, w_sgu), F32)]
    kern = functools.partial(_inproj_kernel, tm=tm, w_sgu=w_sgu, w_fox=w_fox, w_sb=w_sb,
                             period=period, sweep=sweep)
    return pl.pallas_call(
        kern,
        grid=(n // tm,),
        in_specs=in_specs,
        out_specs=out_specs,
        out_shape=out_shape,
        scratch_shapes=scratch,
        compiler_params=_cparams(("arbitrary",)),
        name="inproj",
    )(*operands)


def _seq_cumsum(x, nc, reverse_exclusive):
    rows = x.shape[0]
    a = lax.broadcasted_iota(jnp.int32, (LANES, LANES), 0)
    b = lax.broadcasted_iota(jnp.int32, (LANES, LANES), 1)
    tri = (a > b) if reverse_exclusive else (a <= b)
    tri = jnp.where(tri, 1.0, 0.0).astype(BF16)
    ones = jnp.ones((LANES, LANES), BF16)
    xh, xm, xl = _split3(x)
    within = _dot(xh, tri) + _dot(xm, tri) + _dot(xl, tri)
    tot = _dot(xh, ones) + _dot(xm, ones) + _dot(xl, ones)
    ra = lax.broadcasted_iota(jnp.int32, (rows, rows), 0)
    rb = lax.broadcasted_iota(jnp.int32, (rows, rows), 1)
    other = (rb > ra) if reverse_exclusive else (rb < ra)
    blk = jnp.where((ra // nc == rb // nc) & other, 1.0, 0.0).astype(BF16)
    th, tm_, tl = _split3(tot)
    return within + _dot(blk, th) + _dot(blk, tm_) + _dot(blk, tl)


def _sample_cumsum_kernel(clf_ref, lf_ref, suf_ref, cum_ref, *, nc, dec_seq):
    suf_ref[...] = _seq_cumsum(clf_ref[...], nc, reverse_exclusive=True)
    a = lax.broadcasted_iota(jnp.int32, (LANES, LANES), 0)
    b = lax.broadcasted_iota(jnp.int32, (LANES, LANES), 1)
    tri = jnp.where((a // dec_seq == b // dec_seq) & (a <= b), 1.0, 0.0).astype(BF16)
    xh, xm, xl = _split3(lf_ref[...])
    cum_ref[...] = _dot(xh, tri) + _dot(xm, tri) + _dot(xl, tri)


def _sample_cumsum(clogf_t, logft, dec_seq):
    rows, past = clogf_t.shape
    nc = past // LANES
    suf, cum = pl.pallas_call(
        functools.partial(_sample_cumsum_kernel, nc=nc, dec_seq=dec_seq),
        out_shape=[jax.ShapeDtypeStruct((rows * nc, LANES), F32),
                   jax.ShapeDtypeStruct(logft.shape, F32)],
        compiler_params=pltpu.CompilerParams(vmem_limit_bytes=VMEM_LIMIT),
        name="sample_logf_cumsum",
    )(clogf_t.reshape(rows * nc, LANES), logft)
    return suf.reshape(rows, past), cum


def _head_query_blocks(qt_ref, heads, tq, ones_rows):
    rowi = lax.broadcasted_iota(jnp.int32, (LANES, tq), 0)
    out = []
    for h in range(heads):
        qp = qt_ref[(h // 2) * LANES:(h // 2 + 1) * LANES, :]
        own = (rowi < HEAD_DIM) if h % 2 == 0 else (rowi >= HEAD_DIM)
        fill = jnp.zeros((LANES, tq), F32)
        for r in ones_rows(h):
            fill = jnp.where(rowi == r, 1.0, fill)
        out.append(jnp.where(own, qp, fill.astype(BF16)))
    return out


def _bias_rows(h):
    base = HEAD_DIM if h % 2 == 0 else 0
    return [base + h, base + 8 + h, base + 16 + h]


def _fox_kernel(fs_ref, fe_ref, cut_ref, qt_ref, ka_ref, vt_ref, o_ref, *, tq, heads):
    i = pl.program_id(0)
    tk = tq
    cpb = tq // LANES
    qa = _head_query_blocks(qt_ref, heads, tq, _bias_rows)
    rowi = lax.broadcasted_iota(jnp.int32, (LANES, tq), 0)
    krow = lax.broadcasted_iota(jnp.int32, (tk, tq), 0)
    qcol = lax.broadcasted_iota(jnp.int32, (tk, tq), 1)

    def step(j, carry, masked):
        k0 = pl.multiple_of(j * tk, tk)
        logits = [_dot(ka_ref[pl.ds(k0, tk), h * LANES:(h + 1) * LANES], qa[h])
                  for h in range(heads)]
        out = []
        for h in range(heads):
            vt = vt_ref[(h // 2) * LANES:(h // 2 + 1) * LANES, pl.ds(k0, tk)]
            m, l, acc = carry[3 * h:3 * h + 3]
            s = logits[h]
            if masked:
                s = jnp.where(krow <= qcol, s, NEG_BIG)
            mn = jnp.maximum(m, jnp.max(s, axis=0, keepdims=True))
            alpha = jnp.exp(m - mn)
            pe = jnp.exp(s - mn)
            l = alpha * l + jnp.sum(pe, axis=0, keepdims=True)
            acc = alpha * acc + _dot(vt, pe.astype(BF16))
            out += [mn, l, acc]
        return tuple(out)

    init = []
    for _ in range(heads):
        init += [jnp.full((1, tq), NEG_BIG, F32), jnp.zeros((1, tq), F32),
                 jnp.zeros((LANES, tq), F32)]
    carry = step(i, tuple(init), True)

    def cond(state):
        j = state[0]
        jc = jnp.maximum(j, 0) * cpb + (cpb - 1)
        live = fs_ref[0, i * cpb] - fe_ref[0, jc] >= cut_ref[0]
        for h in range(1, heads):
            live = live | (fs_ref[h, i * cpb] - fe_ref[h, jc] >= cut_ref[0])
        return (j >= 0) & live

    def body(state):
        j = state[0]
        return (j - 1,) + step(j, state[1:], False)

    res = lax.while_loop(cond, body, (i - 1,) + carry)[1:]
    for p in range(heads // 2):
        _, l0, a0, _, l1, a1 = res[6 * p:6 * p + 6]
        ot = jnp.where(rowi < HEAD_DIM, a0 / l0, a1 / l1)
        o_ref[:, p * LANES:(p + 1) * LANES] = ot.T.astype(BF16)


def _fox_prompt(fs, fe, cut, qt, kaug, vt, *, tq):
    w, n = qt.shape
    heads = w // HEAD_DIM
    grid_spec = pltpu.PrefetchScalarGridSpec(
        num_scalar_prefetch=3,
        grid=(n // tq,),
        in_specs=[
            pl.BlockSpec((w, tq), lambda i, *_: (0, i)),
            pl.BlockSpec(kaug.shape, lambda i, *_: (0, 0), pipeline_mode=pl.Buffered(1)),
            pl.BlockSpec((w, n), lambda i, *_: (0, 0), pipeline_mode=pl.Buffered(1)),
        ],
        out_specs=pl.BlockSpec((tq, w), lambda i, *_: (i, 0)),
    )
    return pl.pallas_call(
        functools.partial(_fox_kernel, tq=tq, heads=heads),
        grid_spec=grid_spec,
        out_shape=jax.ShapeDtypeStruct((n, w), BF16),
        compiler_params=_cparams(("arbitrary",)),
        name="fox_prompt",
    )(fs, fe, cut, qt, kaug, vt)


def _sb_kernel(qt_ref, k_ref, vt_ref, o_ref, *, tq, heads):
    i = pl.program_id(0)
    tk = tq
    qm = _head_query_blocks(qt_ref, heads, tq, lambda h: [])
    rowi = lax.broadcasted_iota(jnp.int32, (LANES, tq), 0)
    krow = lax.broadcasted_iota(jnp.int32, (tk, tq), 0)
    qcol = lax.broadcasted_iota(jnp.int32, (tk, tq), 1)
    a = lax.broadcasted_iota(jnp.int32, (tk, 2 * tk), 0)
    b = lax.broadcasted_iota(jnp.int32, (tk, 2 * tk), 1) % tk
    tri2 = jnp.where(b > a, 1.0, 0.0).astype(BF16)

    def sweep(blocks, carry):
        k0s = [pl.multiple_of(j * tk, tk) for j, _ in blocks]
        z = [[_dot(k_ref[pl.ds(k0, tk), (h // 2) * LANES:(h // 2 + 1) * LANES], qm[h])
              for h in range(heads)] for k0 in k0s]
        logsig, later, mass = {}, {}, {}
        for bi, (_, mask) in enumerate(blocks):
            for h in range(heads):
                zz = z[bi][h]
                lg = jnp.log(1.0 + jnp.exp(-jnp.abs(zz)))
                sp = jnp.maximum(zz, 0.0) + lg
                if mask is not None:
                    sp = jnp.where(mask, sp, 0.0)
                hi, lo = _split2(sp)
                later[bi, h] = _dot(tri2, jnp.concatenate([hi, lo], axis=0))
                logsig[bi, h] = jnp.minimum(zz, 0.0) - lg
                mass[bi, h] = jnp.sum(sp, axis=0, keepdims=True)
        out = []
        for h in range(heads):
            cr, acc = carry[2 * h:2 * h + 2]
            for bi, (_, mask) in enumerate(blocks):
                w = jnp.exp(logsig[bi, h] - later[bi, h] - cr)
                if mask is not None:
                    w = jnp.where(mask, w, 0.0)
                vt = vt_ref[(h // 2) * LANES:(h // 2 + 1) * LANES, pl.ds(k0s[bi], tk)]
                acc = acc + _dot(vt, w.astype(BF16))
                cr = cr + mass[bi, h]
            out += [cr, acc]
        return tuple(out)

    zero = (jnp.zeros((1, tq), F32), jnp.zeros((LANES, tq), F32))
    carry = sweep([(i, krow < qcol), (jnp.maximum(i - 1, 0), jnp.broadcast_to(i > 0, (tk, tq)))],
                  zero * heads)

    def cond(state):
        live = jnp.min(state[1])
        for h in range(1, heads):
            live = jnp.minimum(live, jnp.min(state[1 + 2 * h]))
        return (state[0] >= 0) & (live < PRUNE_LOG)

    def body(state):
        j = state[0]
        return (j - 1,) + sweep([(j, None)], state[1:])

    res = lax.while_loop(cond, body, (i - 2,) + carry)[1:]
    for p in range(heads // 2):
        ot = jnp.where(rowi < HEAD_DIM, res[4 * p + 1], res[4 * p + 3])
        o_ref[:, p * LANES:(p + 1) * LANES] = ot.T.astype(BF16)


def _sb_prompt(qt, k, vt, *, tq):
    w, n = qt.shape
    return pl.pallas_call(
        functools.partial(_sb_kernel, tq=tq, heads=w // HEAD_DIM),
        grid=(n // tq,),
        in_specs=[
            pl.BlockSpec((w, tq), lambda i: (0, i)),
            pl.BlockSpec((n, w), lambda i: (0, 0), pipeline_mode=pl.Buffered(1)),
            pl.BlockSpec((w, n), lambda i: (0, 0), pipeline_mode=pl.Buffered(1)),
        ],
        out_specs=pl.BlockSpec((tq, w), lambda i: (i, 0)),
        out_shape=jax.ShapeDtypeStruct((n, w), BF16),
        compiler_params=_cparams(("arbitrary",)),
        name="sb_prompt",
    )(qt, k, vt)


def _sb_block(z, carry, v, tri, mask, v_feature_major=False):
    lg = jnp.log(1.0 + jnp.exp(-jnp.abs(z)))
    sp = jnp.maximum(z, 0.0) + lg
    if mask is not None:
        sp = jnp.where(mask, sp, 0.0)
    hi, lo = _split2(sp)
    later = _dot(hi, tri) + _dot(lo, tri)
    a = jnp.exp((jnp.minimum(z, 0.0) - lg) - later - carry)
    if mask is not None:
        a = jnp.where(mask, a, 0.0)
    pv = _dot_nt(a.astype(BF16), v) if v_feature_major else _dot(a.astype(BF16), v)
    return pv, carry + jnp.sum(sp, axis=1, keepdims=True)


def _suffix_matrix(tk):
    a = lax.broadcasted_iota(jnp.int32, (tk, tk), 0)
    b = lax.broadcasted_iota(jnp.int32, (tk, tk), 1)
    return jnp.where(a > b, 1.0, 0.0).astype(BF16)


def _head_rows(x, heads):
    lane_head = lax.broadcasted_iota(jnp.int32, x.shape, 1) // HEAD_DIM
    return jnp.concatenate(
        [jnp.where(lane_head == h, x, jnp.zeros_like(x)) for h in range(heads)], axis=0)


def _fold_heads(o, heads, s):
    lane_head = lax.broadcasted_iota(jnp.int32, (s, o.shape[1]), 1) // HEAD_DIM
    out = jnp.zeros((s, o.shape[1]), F32)
    for h in range(heads):
        out = jnp.where(lane_head == h, o[h * s:(h + 1) * s], out)
    return out


def _sample_attn_kernel(qf_ref, kf_ref, vf_ref, ck_ref, cv_ref, suf_ref, cum_ref,
                        qb_ref, kb_ref, vb_ref, cbk_ref, cbv_ref, of_ref, ob_ref,
                        *, s, h_fox, h_sb, past):
    qa = _head_rows(qf_ref[...], h_fox)
    rows = h_fox * s
    ck = ck_ref[0, 0].astype(BF16)
    cv = cv_ref[0, 0].astype(BF16)
    suf = suf_ref[0]
    cum = cum_ref[0]
    bias_c = jnp.concatenate(
        [jnp.broadcast_to(suf[h:h + 1, :], (s, past)) for h in range(h_fox)], axis=0)
    bias_n = jnp.concatenate(
        [jnp.broadcast_to(-cum[h:h + 1, :], (s, s)) for h in range(h_fox)], axis=0)
    lc = _dot(qa, ck) + bias_c
    ln = _dot_nt(qa, kf_ref[...]) + bias_n
    r_pos = lax.broadcasted_iota(jnp.int32, (rows, s), 0) % s
    k_pos = lax.broadcasted_iota(jnp.int32, (rows, s), 1)
    ln = jnp.where(k_pos <= r_pos, ln, NEG_BIG)
    m = jnp.maximum(jnp.max(lc, axis=1, keepdims=True), jnp.max(ln, axis=1, keepdims=True))
    pc = jnp.exp(lc - m)
    pn = jnp.exp(ln - m)
    den = jnp.sum(pc, axis=1, keepdims=True) + jnp.sum(pn, axis=1, keepdims=True)
    o = (_dot_nt(pc.astype(BF16), cv) + _dot(pn.astype(BF16), vf_ref[...])) / den
    of_ref[...] = _fold_heads(o, h_fox, s).astype(BF16)

    qb = _head_rows(qb_ref[...], h_sb)
    rows_b = h_sb * s
    rb = lax.broadcasted_iota(jnp.int32, (rows_b, s), 0) % s
    cb = lax.broadcasted_iota(jnp.int32, (rows_b, s), 1)
    acc, carry = _sb_block(_dot_nt(qb, kb_ref[...]), jnp.zeros((rows_b, 1), F32), vb_ref[...],
                           _suffix_matrix(s), cb < rb)
    tri = _suffix_matrix(LANES)

    def body(t, state):
        acc, carry = state
        k0 = pl.multiple_of(past - (t + 1) * LANES, LANES)
        k = cbk_ref[0, 0, :, pl.ds(k0, LANES)].astype(BF16)
        v = cbv_ref[0, 0, :, pl.ds(k0, LANES)].astype(BF16)
        pv, carry = _sb_block(_dot(qb, k), carry, v, tri, None, v_feature_major=True)
        return acc + pv, carry

    acc, _ = lax.fori_loop(0, past // LANES, body, (acc, carry))
    ob_ref[...] = _fold_heads(acc, h_sb, s).astype(BF16)


def _sample_attn(qf, kf, vf, ck, cv, suf, cum, qb, kb, vb, cbk, cbv, *, layer, batch, s):
    w_fox, w_sb = qf.shape[1], qb.shape[1]
    past = ck.shape[3]
    h_fox, h_sb = w_fox // HEAD_DIM, w_sb // HEAD_DIM
    new = lambda w: pl.BlockSpec((s, w), lambda b: (b, 0))
    cache = lambda w: pl.BlockSpec((1, 1, w, past), lambda b: (layer, b, 0, 0))
    kern = functools.partial(_sample_attn_kernel, s=s, h_fox=h_fox, h_sb=h_sb, past=past)
    return pl.pallas_call(
        kern,
        grid=(batch,),
        in_specs=[new(w_fox), new(w_fox), new(w_fox), cache(w_fox), cache(w_fox),
                  pl.BlockSpec((1, 8, past), lambda b: (b, 0, 0)),
                  pl.BlockSpec((1, 8, s), lambda b: (b, 0, 0)),
                  new(w_sb), new(w_sb), new(w_sb), cache(w_sb), cache(w_sb)],
        out_specs=[new(w_fox), new(w_sb)],
        out_shape=[jax.ShapeDtypeStruct((batch * s, w_fox), BF16),
                   jax.ShapeDtypeStruct((batch * s, w_sb), BF16)],
        compiler_params=_cparams(("arbitrary",)),
        name="sample_attn",
    )(qf, kf, vf, ck, cv, suf, cum, qb, kb, vb, cbk, cbv)


def _merge_kernel(x_ref, sc_ref, sh_ref, gt_ref, gmix_ref, ysgu_ref, ofox_ref, osb_ref,
                  wg_ref, bg_ref, wbs_ref, wbf_ref, wbb_ref, wo_ref, o_ref):
    x = x_ref[...]
    d = x.shape[1]
    h = _modulated_norm(x, gmix_ref[...], sc_ref[...], sh_ref[...])
    gates = _dot(h.astype(BF16), wg_ref[...]) + bg_ref[...]
    gates = 1.0 / (1.0 + jnp.exp(-gates))
    merged = gates[:, 0:d] * _dot(ysgu_ref[...], wbs_ref[...]) \
        + gates[:, d:2 * d] * _dot(ofox_ref[...], wbf_ref[...]) \
        + gates[:, 2 * d:3 * d] * _dot(osb_ref[...], wbb_ref[...])
    o_ref[...] = x + gt_ref[...] * _dot(merged.astype(BF16), wo_ref[...])


def _merge(x, sc, sh, gt, gmix, ysgu, ofox, osb, wg, bg, wbs, wbf, wbb, wo, *, tm):
    n, d = x.shape
    if sc.shape[0] == 1:
        mod_spec = pl.BlockSpec((1, d), lambda i: (0, 0))
    else:
        mod_spec = pl.BlockSpec((tm, d), lambda i: (i, 0))
    row = lambda width: pl.BlockSpec((tm, width), lambda i: (i, 0))
    return pl.pallas_call(
        _merge_kernel,
        grid=(n // tm,),
        in_specs=[row(d), mod_spec, mod_spec, mod_spec, _const_spec(gmix.shape),
                  row(ysgu.shape[1]), row(ofox.shape[1]), row(osb.shape[1]),
                  _const_spec(wg.shape), _const_spec(bg.shape), _const_spec(wbs.shape),
                  _const_spec(wbf.shape), _const_spec(wbb.shape), _const_spec(wo.shape)],
        out_specs=row(d),
        out_shape=jax.ShapeDtypeStruct((n, d), F32),
        compiler_params=_cparams(("arbitrary",)),
        name="merge",
    )(x, sc, sh, gt, gmix, ysgu, ofox, osb, wg, bg, wbs, wbf, wbb, wo)


def _ffn_kernel(x_ref, sc_ref, sh_ref, gt_ref, g_ref, wi_ref, wo_ref, o_ref, *, d_ff):
    x = x_ref[...]
    h = _modulated_norm(x, g_ref[...], sc_ref[...], sh_ref[...])
    ag = _dot(h.astype(BF16), wi_ref[...])
    a = ag[:, 0:d_ff]
    act = a * (1.0 / (1.0 + jnp.exp(-a))) * ag[:, d_ff:2 * d_ff]
    o_ref[...] = x + gt_ref[...] * _dot(act.astype(BF16), wo_ref[...])


def _ffn(x, sc, sh, gt, g, wi, wo, *, tm):
    n, d = x.shape
    d_ff = wo.shape[0]
    if sc.shape[0] == 1:
        mod_spec = pl.BlockSpec((1, d), lambda i: (0, 0))
    else:
        mod_spec = pl.BlockSpec((tm, d), lambda i: (i, 0))
    row = pl.BlockSpec((tm, d), lambda i: (i, 0))
    return pl.pallas_call(
        functools.partial(_ffn_kernel, d_ff=d_ff),
        grid=(n // tm,),
        in_specs=[row, mod_spec, mod_spec, mod_spec, _const_spec(g.shape), _const_spec(wi.shape),
                  _const_spec(wo.shape)],
        out_specs=row,
        out_shape=jax.ShapeDtypeStruct((n, d), F32),
        compiler_params=_cparams(("arbitrary",)),
        name="ffn",
    )(x, sc, sh, gt, g, wi, wo)


def _indicator(width, group):
    idx = np.arange(width) // group
    return jnp.asarray(idx[:, None] == idx[None, :], dtype=BF16)


def kernel(x_prompt, x_sample, c_prompt, c_sample, cache_fox_k, cache_fox_v, cache_fox_logf,
           cache_sb_k, cache_sb_v, w_ada, b_ada, g_mix, g_ffn, w_in, g_sgu_v, w_sgu, b_sgu, b_fgt,
           g_q, g_k, w_br_sgu, w_br_fox, w_br_sb, w_gate, b_gate, w_out, w_ffn_in, w_ffn_out):
    batch, seq, d = x_prompt.shape
    dec_batch, dec_seq, _ = x_sample.shape
    depth = w_ada.shape[0]
    past = cache_fox_k.shape[2]
    h_fox, h_sb = cache_fox_k.shape[3], cache_sb_k.shape[3]
    w_fox, w_sb = h_fox * HEAD_DIM, h_sb * HEAD_DIM
    g_sgu, cg = g_sgu_v.shape[1], g_sgu_v.shape[2]
    w_sgu_ = g_sgu * cg
    assert batch == 1 and g_sgu == G_SGU and w_sgu.shape[2] == SGU_LEN
    n_dec = dec_batch * dec_seq

    n_c = batch + dec_batch
    c_rows = -(-n_c // 8) * 8
    c_all = jnp.zeros((c_rows, d), F32).at[:n_c].set(jnp.concatenate([c_prompt, c_sample], axis=0))
    mod = _modulation(c_all, w_ada, b_ada)

    offs = np.cumsum([0, w_sgu_, w_sgu_, w_fox, w_fox, w_fox, h_fox, w_sb, w_sb, w_sb]).tolist()
    f_cols = jnp.zeros((depth, d, LANES), F32).at[:, :, :h_fox].set(w_in[:, :, offs[5]:offs[6]])
    w_main = jnp.concatenate([w_in[:, :, :offs[5]], w_in[:, :, offs[6]:], f_cols],
                             axis=2).astype(BF16)
    bf_pad = jnp.zeros((depth, 1, LANES), F32).at[:, 0, :h_fox].set(b_fgt)
    ind96, ind64 = _indicator(w_sgu_, cg), _indicator(w_fox, HEAD_DIM)
    gq_t = jnp.tile(g_q, (1, h_fox)).reshape(depth, 1, w_fox)
    gk_t = jnp.tile(g_k, (1, h_fox)).reshape(depth, 1, w_fox)
    gsgu = g_sgu_v.reshape(depth, 1, w_sgu_)
    msgu_p = jnp.transpose(w_sgu, (0, 2, 1, 3)).reshape(depth, SGU_LEN, g_sgu * SGU_LEN)
    reps = SGU_LEN // dec_seq
    w_small = jnp.tile(w_sgu[:, :, :dec_seq, :dec_seq], (1, 1, reps, reps))
    msgu_s = jnp.transpose(w_small, (0, 2, 1, 3)).reshape(depth, SGU_LEN, g_sgu * SGU_LEN)
    bsgu_p = jnp.repeat(jnp.transpose(b_sgu, (0, 2, 1)), cg, axis=2)
    bsgu_s = jnp.tile(bsgu_p[:, :dec_seq], (1, reps, 1))
    wg, wbs, wbf, wbb = (w.astype(BF16) for w in (w_gate, w_br_sgu, w_br_fox, w_br_sb))
    wo, wfi, wfo = (w.astype(BF16) for w in (w_out, w_ffn_in, w_ffn_out))

    to_fm = lambda c: jnp.transpose(c, (0, 1, 3, 4, 2)).reshape(
        depth, dec_batch, c.shape[3] * HEAD_DIM, past)
    ck_t, cv_t, cbk_t, cbv_t = (to_fm(c) for c in (cache_fox_k, cache_fox_v, cache_sb_k, cache_sb_v))

    xp = x_prompt.reshape(seq, d)
    xs = x_sample.reshape(n_dec, d)
    tm_p = min(512, seq)
    tm_f = min(256, seq)
    tq = min(256, seq)
    st_p, st_s = [], []
    for l in range(depth):
        mp = [mod[l, 0:1, k * d:(k + 1) * d] for k in range(6)]
        ms = [jnp.repeat(mod[l, batch:batch + dec_batch, k * d:(k + 1) * d], dec_seq, axis=0)
              for k in range(6)]
        gmix, gffn = g_mix[l].reshape(1, d), g_ffn[l].reshape(1, d)
        shared = (gmix, w_main[l], gsgu[l], gq_t[l], gk_t[l], bf_pad[l], ind96, ind64)

        (ysgu, kaug, qft, vft, qbt, kb16, vbt, kf32, vf32, kb32, vb32, logft, fcumt) = _inproj(
            xp, mp[1], mp[0], *shared, msgu_p[l], bsgu_p[l], tm=tm_p, period=SGU_LEN, sweep=True)
        bound = 1.01 * HEAD_DIM ** 0.5 * jnp.max(jnp.abs(g_q[l])) * jnp.max(jnp.abs(g_k[l]))
        cut = (-(2.0 * bound + PRUNE_LOG)).reshape(1)
        ofox = _fox_prompt(fcumt[:, 0::LANES], fcumt[:, LANES - 1::LANES], cut, qft, kaug, vft, tq=tq)
        osb = _sb_prompt(qbt, kb16, vbt, tq=min(128, seq))
        x1 = _merge(xp, mp[1], mp[0], mp[2], gmix, ysgu, ofox, osb, wg[l],
                    b_gate[l].reshape(1, 3 * d), wbs[l], wbf[l], wbb[l], wo[l], tm=tm_p)
        xp = _ffn(x1, mp[4], mp[3], mp[5], gffn, wfi[l], wfo[l], tm=tm_f)
        st_p.append((kf32.reshape(batch, seq, h_fox, HEAD_DIM),
                     vf32.reshape(batch, seq, h_fox, HEAD_DIM),
                     logft[:h_fox].T.reshape(batch, seq, h_fox),
                     kb32.reshape(batch, seq, h_sb, HEAD_DIM),
                     vb32.reshape(batch, seq, h_sb, HEAD_DIM)))

        (ysgu, qf, kf16, vf16, qb, kb16, vb16, kf32, vf32, kb32, vb32, logft, sguv) = _inproj(
            xs, ms[1], ms[0], *shared, msgu_s[l], bsgu_s[l], tm=n_dec, period=dec_seq, sweep=False)
        clf = jnp.zeros((dec_batch, 8, past), F32).at[:, :h_fox].set(
            jnp.transpose(cache_fox_logf[l], (0, 2, 1)))
        suf, cum = _sample_cumsum(clf.reshape(dec_batch * 8, past), logft, dec_seq)
        cum_b = jnp.transpose(cum.reshape(8, dec_batch, dec_seq), (1, 0, 2))
        ofox, osb = _sample_attn(
            qf, kf16, vf16, ck_t, cv_t, suf.reshape(dec_batch, 8, past), cum_b,
            qb, kb16, vb16, cbk_t, cbv_t, layer=l, batch=dec_batch, s=dec_seq)
        x1 = _merge(xs, ms[1], ms[0], ms[2], gmix, ysgu, ofox, osb, wg[l],
                    b_gate[l].reshape(1, 3 * d), wbs[l], wbf[l], wbb[l], wo[l], tm=n_dec)
        xs = _ffn(x1, ms[4], ms[3], ms[5], gffn, wfi[l], wfo[l], tm=n_dec)
        st_s.append((kf32.reshape(dec_batch, dec_seq, h_fox, HEAD_DIM),
                     vf32.reshape(dec_batch, dec_seq, h_fox, HEAD_DIM),
                     logft[:h_fox].T.reshape(dec_batch, dec_seq, h_fox),
                     kb32.reshape(dec_batch, dec_seq, h_sb, HEAD_DIM),
                     vb32.reshape(dec_batch, dec_seq, h_sb, HEAD_DIM),
                     sguv.reshape(dec_batch, dec_seq, w_sgu_)))

    def stack(states, idx):
        return jnp.stack([s[idx] for s in states], axis=0)

    return (xp.reshape(batch, seq, d), xs.reshape(dec_batch, dec_seq, d),
            stack(st_p, 0), stack(st_p, 1), stack(st_p, 2), stack(st_p, 3), stack(st_p, 4),
            stack(st_s, 0), stack(st_s, 1), stack(st_s, 2), stack(st_s, 3), stack(st_s, 4),
            stack(st_s, 5))
```

```python
import functools

import numpy as np
import jax
import jax.numpy as jnp
from jax import lax
from jax.experimental import pallas as pl
from jax.experimental.pallas import tpu as pltpu

F32 = jnp.float32
BF16 = jnp.bfloat16

EPS = 1e-6
HEAD_DIM = 64
LANES = 128
CHUNK = 64
SGU_LEN = 128
G_SGU = 4
NEG_BIG = -1e30

PRUNE_LOG = 30.0

VMEM_LIMIT = 56 * 1024 * 1024


def _cparams(sem):
    return pltpu.CompilerParams(dimension_semantics=sem, vmem_limit_bytes=VMEM_LIMIT)


def _const_spec(shape):
    nd = len(shape)
    return pl.BlockSpec(shape, lambda *_: (0,) * nd, pipeline_mode=pl.Buffered(1))


def _layer_spec(arr, layer):
    nd = arr.ndim
    return pl.BlockSpec((1,) + arr.shape[1:], lambda *_: (layer,) + (0,) * (nd - 1),
                        pipeline_mode=pl.Buffered(1))


def _mod_spec(mod, layer, k, tm):
    d = mod.shape[2] // 6
    if mod.shape[1] == 1:
        return pl.BlockSpec((1, 1, d), lambda i: (layer, 0, k))
    return pl.BlockSpec((1, tm, d), lambda i: (layer, i, k))


def _dot(a, b):
    return jnp.dot(a, b, preferred_element_type=F32)


def _dot_nt(a, b):
    return lax.dot_general(a, b, (((1,), (1,)), ((), ())), preferred_element_type=F32)


def _split3(x):
    h = x.astype(BF16)
    r = x - h.astype(F32)
    m = r.astype(BF16)
    l = (r - m.astype(F32)).astype(BF16)
    return h, m, l


def _split2(x):
    h = x.astype(BF16)
    l = (x - h.astype(F32)).astype(BF16)
    return h, l


def _mod_kernel(c_ref, w_ref, b_ref, o_ref):
    c = c_ref[...]
    s = c * (1.0 / (1.0 + jnp.exp(-c)))
    o_ref[0] = _dot(s.astype(BF16), w_ref[0].astype(BF16)) + b_ref[0]


def _modulation(c_all, w_ada, b_ada):
    depth, d, n6 = w_ada.shape
    rows = c_all.shape[0]
    tn = 1024
    return pl.pallas_call(
        _mod_kernel,
        grid=(depth, n6 // tn),
        in_specs=[
            pl.BlockSpec((rows, d), lambda l, j: (0, 0)),
            pl.BlockSpec((1, d, tn), lambda l, j: (l, 0, j)),
            pl.BlockSpec((1, 1, tn), lambda l, j: (l, 0, j)),
        ],
        out_specs=pl.BlockSpec((1, rows, tn), lambda l, j: (l, 0, j)),
        out_shape=jax.ShapeDtypeStruct((depth, rows, n6), F32),
        compiler_params=_cparams(("arbitrary", "arbitrary")),
        name="adaln_mod",
    )(c_all, w_ada, b_ada.reshape(depth, 1, n6))


def _modulated_norm(x, g, sc, sh):
    ms = jnp.mean(x * x, axis=-1, keepdims=True)
    return (x * lax.rsqrt(ms + EPS)) * g * (1.0 + sc) + sh


def _group_rms(t, ind, inv_size, g):
    ss = _dot((t * t).astype(BF16), ind)
    return t * lax.rsqrt(ss * inv_size + EPS) * g


def _log_sigmoid(x):
    return jnp.minimum(x, 0.0) - jnp.log(1.0 + jnp.exp(-jnp.abs(x)))


def _augmented_keys(kfn, f_cum, heads):
    lane = lax.broadcasted_iota(jnp.int32, f_cum.shape, 1)
    hi, mid, lo = (t.astype(F32) for t in _split3(-f_cum))
    aug_even = pltpu.roll(hi, 64, 1) + pltpu.roll(mid, 72, 1) + pltpu.roll(lo, 80, 1)
    aug_odd = hi + pltpu.roll(mid, 8, 1) + pltpu.roll(lo, 16, 1)
    blocks = []
    for h in range(heads):
        kp = kfn[:, (h // 2) * LANES:(h // 2 + 1) * LANES]
        if h % 2 == 0:
            blocks.append(jnp.where(lane < HEAD_DIM, kp, aug_even))
        else:
            blocks.append(jnp.where(lane >= HEAD_DIM, kp, aug_odd))
    return jnp.concatenate(blocks, axis=1).astype(BF16)


def _inproj_kernel(x_ref, sc_ref, sh_ref, gmix_ref, w_ref, gsgu_ref, gq_ref, gk_ref, bf_ref,
                   ind96_ref, ind64_ref, msgu_ref, bsgu_ref, *rest,
                   tm, w_sgu, w_fox, w_sb, period, sweep):
    if sweep:
        (tril_ref, ysgu_ref, kaug_ref, qft_ref, vft_ref, qbt_ref, kb16_ref, vbt_ref,
         kf32_ref, vf32_ref, kb32_ref, vb32_ref, logft_ref, fcumt_ref, carry_ref) = rest
    else:
        (ysgu_ref, qf_ref, kf16_ref, vf16_ref, qb_ref, kb16_ref, vb16_ref,
         kf32_ref, vf32_ref, kb32_ref, vb32_ref, logft_ref, sguv_ref) = rest
    x = x_ref[...]
    h = _modulated_norm(x, gmix_ref[0], sc_ref[0], sh_ref[0])
    p = _dot(h.astype(BF16), w_ref[0])

    o = 0
    u = p[:, o:o + w_sgu]; o += w_sgu
    vs = p[:, o:o + w_sgu]; o += w_sgu
    qf = p[:, o:o + w_fox]; o += w_fox
    kf = p[:, o:o + w_fox]; o += w_fox
    vf = p[:, o:o + w_fox]; o += w_fox
    qb = p[:, o:o + w_sb]; o += w_sb
    kb = p[:, o:o + w_sb]; o += w_sb
    vb = p[:, o:o + w_sb]; o += w_sb
    fl = p[:, o:o + LANES]

    scale = HEAD_DIM ** -0.5
    ind64 = ind64_ref[...]
    qfn = _group_rms(qf, ind64, 1.0 / HEAD_DIM, gq_ref[0])
    kfn = _group_rms(kf, ind64, 1.0 / HEAD_DIM, gk_ref[0])
    kf32_ref[...] = kfn
    vf32_ref[...] = vf
    kb32_ref[...] = kb
    vb32_ref[...] = vb
    kb16_ref[...] = kb.astype(BF16)
    lf = _log_sigmoid(fl + bf_ref[0])
    logft_ref[...] = lf.T[0:8, :]
    if sweep:
        qft_ref[...] = (qfn * scale).T.astype(BF16)
        vft_ref[...] = vf.T.astype(BF16)
        qbt_ref[...] = (qb * scale).T.astype(BF16)
        vbt_ref[...] = vb.T.astype(BF16)
        @pl.when(pl.program_id(0) == 0)
        def _():
            carry_ref[...] = jnp.zeros(carry_ref.shape, F32)
        lane = lax.broadcasted_iota(jnp.int32, lf.shape, 1)
        lfh, lfm, lfl = _split3(jnp.where(lane < 8, lf, 0.0))
        tril = tril_ref[...]
        f_cum = _dot(tril, lfh) + _dot(tril, lfm) + _dot(tril, lfl) + carry_ref[0:1, :]
        carry_ref[...] = jnp.broadcast_to(f_cum[tm - 1:tm, :], carry_ref.shape)
        fcumt_ref[...] = f_cum.T[0:8, :]
        kaug_ref[...] = _augmented_keys(kfn, f_cum, w_fox // HEAD_DIM)
    else:
        qf_ref[...] = (qfn * scale).astype(BF16)
        kf16_ref[...] = kfn.astype(BF16)
        vf16_ref[...] = vf.astype(BF16)
        qb_ref[...] = (qb * scale).astype(BF16)
        vb16_ref[...] = vb.astype(BF16)

    cg = w_sgu // G_SGU
    vsn = _group_rms(vs, ind96_ref[...], 1.0 / cg, gsgu_ref[0])
    if not sweep:
        sguv_ref[...] = vsn
    r = lax.broadcasted_iota(jnp.int32, (SGU_LEN, G_SGU * SGU_LEN), 0)
    c = lax.broadcasted_iota(jnp.int32, (SGU_LEN, G_SGU * SGU_LEN), 1) % SGU_LEN
    keep = (r // period == c // period) & ((c % period) // CHUNK <= (r % period) // CHUNK)
    mix = jnp.where(keep, msgu_ref[0], 0.0).astype(BF16)
    lane_group = lax.broadcasted_iota(jnp.int32, (SGU_LEN, w_sgu), 1) // cg
    vsb = vsn.astype(BF16)
    spat = []
    for ci in range(tm // SGU_LEN):
        vc = vsb[ci * SGU_LEN:(ci + 1) * SGU_LEN]
        stacked = jnp.concatenate(
            [jnp.where(lane_group == g, vc, jnp.zeros_like(vc)) for g in range(G_SGU)], axis=0)
        spat.append(_dot(mix, stacked) + bsgu_ref[0])
    spat = jnp.concatenate(spat, axis=0) if len(spat) > 1 else spat[0]
    ysgu_ref[...] = (u * spat).astype(BF16)


def _inproj(x, mod, layer, gmix, w, gsgu, gq, gk, bf, ind96, ind64, msgu, bsgu, *, tm, period,
            sweep):
    n, d = x.shape
    w_sgu, w_fox = gsgu.shape[2], gq.shape[2]
    w_sb = (w.shape[2] - LANES - 2 * w_sgu - 3 * w_fox) // 3
    row = lambda width: pl.BlockSpec((tm, width), lambda i: (i, 0))
    col = lambda height: pl.BlockSpec((height, tm), lambda i: (0, i))
    sds = jax.ShapeDtypeStruct
    states_specs = [row(w_fox), row(w_fox), row(w_sb), row(w_sb), col(8)]
    states_shape = [sds((n, w_fox), F32), sds((n, w_fox), F32), sds((n, w_sb), F32),
                    sds((n, w_sb), F32), sds((8, n), F32)]
    operands = [x, mod, mod, gmix, w, gsgu, gq, gk, bf, ind96, ind64, msgu, bsgu]
    lay = lambda a: _layer_spec(a, layer)
    in_specs = [row(d), _mod_spec(mod, layer, 1, tm), _mod_spec(mod, layer, 0, tm), lay(gmix), lay(w),
                lay(gsgu), lay(gq), lay(gk), lay(bf), _const_spec(ind96.shape),
                _const_spec(ind64.shape), lay(msgu), lay(bsgu)]
    scratch = []
    if sweep:
        heads = w_fox // HEAD_DIM
        a = np.arange(tm)
        operands.append(jnp.asarray(a[None, :] <= a[:, None], dtype=BF16))
        in_specs.append(_const_spec((tm, tm)))
        out_specs = [row(w_sgu), row(heads * LANES), col(w_fox), col(w_fox), col(w_sb), row(w_sb),
                     col(w_sb)] + states_specs + [col(8)]
        out_shape = [sds((n, w_sgu), BF16), sds((n, heads * LANES), BF16), sds((w_fox, n), BF16),
                     sds((w_fox, n), BF16), sds((w_sb, n), BF16), sds((n, w_sb), BF16),
                     sds((w_sb, n), BF16)] + states_shape + [sds((8, n), F32)]
        scratch = [pltpu.VMEM((8, LANES), F32)]
    else:
        out_specs = [row(w_sgu), row(w_fox), row(w_fox), row(w_fox), row(w_sb), row(w_sb),
                     row(w_sb)] + states_specs + [row(w_sgu)]
        out_shape = [sds((n, w_sgu), BF16), sds((n, w_fox), BF16), sds((n, w_fox), BF16),
                     sds((n, w_fox), BF16), sds((n, w_sb), BF16), sds((n, w_sb), BF16),
                     sds((n, w_sb), BF16)] + states_shape + [sds((n, w_sgu), F32)]
    kern = functools.partial(_inproj_kernel, tm=tm, w_sgu=w_sgu, w_fox=w_fox, w_sb=w_sb,
                             period=period, sweep=sweep)
    return pl.pallas_call(
        kern,
        grid=(n // tm,),
        in_specs=in_specs,
        out_specs=out_specs,
        out_shape=out_shape,
        scratch_shapes=scratch,
        compiler_params=_cparams(("arbitrary",)),
        name="inproj",
    )(*operands)


def _seq_cumsum(x, nc, reverse_exclusive):
    rows = x.shape[0]
    a = lax.broadcasted_iota(jnp.int32, (LANES, LANES), 0)
    b = lax.broadcasted_iota(jnp.int32, (LANES, LANES), 1)
    tri = (a > b) if reverse_exclusive else (a <= b)
    tri = jnp.where(tri, 1.0, 0.0).astype(BF16)
    ones = jnp.ones((LANES, LANES), BF16)
    xh, xm, xl = _split3(x)
    within = _dot(xh, tri) + _dot(xm, tri) + _dot(xl, tri)
    tot = _dot(xh, ones) + _dot(xm, ones) + _dot(xl, ones)
    ra = lax.broadcasted_iota(jnp.int32, (rows, rows), 0)
    rb = lax.broadcasted_iota(jnp.int32, (rows, rows), 1)
    other = (rb > ra) if reverse_exclusive else (rb < ra)
    blk = jnp.where((ra // nc == rb // nc) & other, 1.0, 0.0).astype(BF16)
    th, tm_, tl = _split3(tot)
    return within + _dot(blk, th) + _dot(blk, tm_) + _dot(blk, tl)


def _sample_cumsum_kernel(clf_ref, lf_ref, suf_ref, cum_ref, *, nc, dec_seq):
    suf_ref[...] = _seq_cumsum(clf_ref[...], nc, reverse_exclusive=True)
    a = lax.broadcasted_iota(jnp.int32, (LANES, LANES), 0)
    b = lax.broadcasted_iota(jnp.int32, (LANES, LANES), 1)
    tri = jnp.where((a // dec_seq == b // dec_seq) & (a <= b), 1.0, 0.0).astype(BF16)
    xh, xm, xl = _split3(lf_ref[...])
    cum_ref[...] = _dot(xh, tri) + _dot(xm, tri) + _dot(xl, tri)


def _sample_cumsum(clogf_t, logft, dec_seq):
    rows, past = clogf_t.shape
    nc = past // LANES
    suf, cum = pl.pallas_call(
        functools.partial(_sample_cumsum_kernel, nc=nc, dec_seq=dec_seq),
        out_shape=[jax.ShapeDtypeStruct((rows * nc, LANES), F32),
                   jax.ShapeDtypeStruct(logft.shape, F32)],
        compiler_params=pltpu.CompilerParams(vmem_limit_bytes=VMEM_LIMIT),
        name="sample_logf_cumsum",
    )(clogf_t.reshape(rows * nc, LANES), logft)
    return suf.reshape(rows, past), cum


def _head_query_blocks(qt_ref, heads, tq, ones_rows):
    rowi = lax.broadcasted_iota(jnp.int32, (LANES, tq), 0)
    out = []
    for h in range(heads):
        qp = qt_ref[(h // 2) * LANES:(h // 2 + 1) * LANES, :]
        own = (rowi < HEAD_DIM) if h % 2 == 0 else (rowi >= HEAD_DIM)
        fill = jnp.zeros((LANES, tq), F32)
        for r in ones_rows(h):
            fill = jnp.where(rowi == r, 1.0, fill)
        out.append(jnp.where(own, qp, fill.astype(BF16)))
    return out


def _bias_rows(h):
    base = HEAD_DIM if h % 2 == 0 else 0
    return [base + h, base + 8 + h, base + 16 + h]


def _fox_kernel(fs_ref, fe_ref, cut_ref, qt_ref, ka_ref, vt_ref, o_ref, *, tq, heads):
    i = pl.program_id(0)
    tk = tq
    cpb = tq // LANES
    qa = _head_query_blocks(qt_ref, heads, tq, _bias_rows)
    rowi = lax.broadcasted_iota(jnp.int32, (LANES, tq), 0)
    krow = lax.broadcasted_iota(jnp.int32, (tk, tq), 0)
    qcol = lax.broadcasted_iota(jnp.int32, (tk, tq), 1)

    def step(j, carry, masked):
        k0 = pl.multiple_of(j * tk, tk)
        logits = [_dot(ka_ref[pl.ds(k0, tk), h * LANES:(h + 1) * LANES], qa[h])
                  for h in range(heads)]
        out = []
        for h in range(heads):
            vt = vt_ref[(h // 2) * LANES:(h // 2 + 1) * LANES, pl.ds(k0, tk)]
            m, l, acc = carry[3 * h:3 * h + 3]
            s = logits[h]
            if masked:
                s = jnp.where(krow <= qcol, s, NEG_BIG)
            mn = jnp.maximum(m, jnp.max(s, axis=0, keepdims=True))
            alpha = jnp.exp(m - mn)
            pe = jnp.exp(s - mn)
            l = alpha * l + jnp.sum(pe, axis=0, keepdims=True)
            acc = alpha * acc + _dot(vt, pe.astype(BF16))
            out += [mn, l, acc]
        return tuple(out)

    init = []
    for _ in range(heads):
        init += [jnp.full((1, tq), NEG_BIG, F32), jnp.zeros((1, tq), F32),
                 jnp.zeros((LANES, tq), F32)]
    carry = step(i, tuple(init), True)

    def cond(state):
        j = state[0]
        jc = jnp.maximum(j, 0) * cpb + (cpb - 1)
        live = fs_ref[0, i * cpb] - fe_ref[0, jc] >= cut_ref[0]
        for h in range(1, heads):
            live = live | (fs_ref[h, i * cpb] - fe_ref[h, jc] >= cut_ref[0])
        return (j >= 0) & live

    def body(state):
        j = state[0]
        return (j - 1,) + step(j, state[1:], False)

    res = lax.while_loop(cond, body, (i - 1,) + carry)[1:]
    for p in range(heads // 2):
        _, l0, a0, _, l1, a1 = res[6 * p:6 * p + 6]
        ot = jnp.where(rowi < HEAD_DIM, a0 / l0, a1 / l1)
        o_ref[:, p * LANES:(p + 1) * LANES] = ot.T.astype(BF16)


def _fox_prompt(fs, fe, cut, qt, kaug, vt, *, tq):
    w, n = qt.shape
    heads = w // HEAD_DIM
    grid_spec = pltpu.PrefetchScalarGridSpec(
        num_scalar_prefetch=3,
        grid=(n // tq,),
        in_specs=[
            pl.BlockSpec((w, tq), lambda i, *_: (0, i)),
            pl.BlockSpec(kaug.shape, lambda i, *_: (0, 0), pipeline_mode=pl.Buffered(1)),
            pl.BlockSpec((w, n), lambda i, *_: (0, 0), pipeline_mode=pl.Buffered(1)),
        ],
        out_specs=pl.BlockSpec((tq, w), lambda i, *_: (i, 0)),
    )
    return pl.pallas_call(
        functools.partial(_fox_kernel, tq=tq, heads=heads),
        grid_spec=grid_spec,
        out_shape=jax.ShapeDtypeStruct((n, w), BF16),
        compiler_params=_cparams(("arbitrary",)),
        name="fox_prompt",
    )(fs, fe, cut, qt, kaug, vt)


def _sb_kernel(qt_ref, k_ref, vt_ref, o_ref, *, tq, heads):
    i = pl.program_id(0)
    tk = tq
    qm = _head_query_blocks(qt_ref, heads, tq, lambda h: [])
    rowi = lax.broadcasted_iota(jnp.int32, (LANES, tq), 0)
    krow = lax.broadcasted_iota(jnp.int32, (tk, tq), 0)
    qcol = lax.broadcasted_iota(jnp.int32, (tk, tq), 1)
    a = lax.broadcasted_iota(jnp.int32, (tk, 2 * tk), 0)
    b = lax.broadcasted_iota(jnp.int32, (tk, 2 * tk), 1) % tk
    tri2 = jnp.where(b > a, 1.0, 0.0).astype(BF16)

    def sweep(blocks, carry):
        k0s = [pl.multiple_of(j * tk, tk) for j, _ in blocks]
        z = [[_dot(k_ref[pl.ds(k0, tk), (h // 2) * LANES:(h // 2 + 1) * LANES], qm[h])
              for h in range(heads)] for k0 in k0s]
        logsig, later, mass = {}, {}, {}
        for bi, (_, mask) in enumerate(blocks):
            for h in range(heads):
                zz = z[bi][h]
                lg = jnp.log(1.0 + jnp.exp(-jnp.abs(zz)))
                sp = jnp.maximum(zz, 0.0) + lg
                if mask is not None:
                    sp = jnp.where(mask, sp, 0.0)
                hi, lo = _split2(sp)
                later[bi, h] = _dot(tri2, jnp.concatenate([hi, lo], axis=0))
                logsig[bi, h] = jnp.minimum(zz, 0.0) - lg
                mass[bi, h] = jnp.sum(sp, axis=0, keepdims=True)
        out = []
        for h in range(heads):
            cr, acc = carry[2 * h:2 * h + 2]
            for bi, (_, mask) in enumerate(blocks):
                w = jnp.exp(logsig[bi, h] - later[bi, h] - cr)
                if mask is not None:
                    w = jnp.where(mask, w, 0.0)
                vt = vt_ref[(h // 2) * LANES:(h // 2 + 1) * LANES, pl.ds(k0s[bi], tk)]
                acc = acc + _dot(vt, w.astype(BF16))
                cr = cr + mass[bi, h]
            out += [cr, acc]
        return tuple(out)

    zero = (jnp.zeros((1, tq), F32), jnp.zeros((LANES, tq), F32))
    carry = sweep([(i, krow < qcol), (jnp.maximum(i - 1, 0), jnp.broadcast_to(i > 0, (tk, tq)))],
                  zero * heads)

    def cond(state):
        live = jnp.min(state[1])
        for h in range(1, heads):
            live = jnp.minimum(live, jnp.min(state[1 + 2 * h]))
        return (state[0] >= 0) & (live < PRUNE_LOG)

    def body(state):
        j = state[0]
        return (j - 1,) + sweep([(j, None)], state[1:])

    res = lax.while_loop(cond, body, (i - 2,) + carry)[1:]
    for p in range(heads // 2):
        ot = jnp.where(rowi < HEAD_DIM, res[4 * p + 1], res[4 * p + 3])
        o_ref[:, p * LANES:(p + 1) * LANES] = ot.T.astype(BF16)


def _sb_prompt(qt, k, vt, *, tq):
    w, n = qt.shape
    return pl.pallas_call(
        functools.partial(_sb_kernel, tq=tq, heads=w // HEAD_DIM),
        grid=(n // tq,),
        in_specs=[
            pl.BlockSpec((w, tq), lambda i: (0, i)),
            pl.BlockSpec((n, w), lambda i: (0, 0), pipeline_mode=pl.Buffered(1)),
            pl.BlockSpec((w, n), lambda i: (0, 0), pipeline_mode=pl.Buffered(1)),
        ],
        out_specs=pl.BlockSpec((tq, w), lambda i: (i, 0)),
        out_shape=jax.ShapeDtypeStruct((n, w), BF16),
        compiler_params=_cparams(("arbitrary",)),
        name="sb_prompt",
    )(qt, k, vt)


def _sb_block(z, carry, v, tri, mask, v_feature_major=False):
    lg = jnp.log(1.0 + jnp.exp(-jnp.abs(z)))
    sp = jnp.maximum(z, 0.0) + lg
    if mask is not None:
        sp = jnp.where(mask, sp, 0.0)
    hi, lo = _split2(sp)
    later = _dot(hi, tri) + _dot(lo, tri)
    a = jnp.exp((jnp.minimum(z, 0.0) - lg) - later - carry)
    if mask is not None:
        a = jnp.where(mask, a, 0.0)
    pv = _dot_nt(a.astype(BF16), v) if v_feature_major else _dot(a.astype(BF16), v)
    return pv, carry + jnp.sum(sp, axis=1, keepdims=True)


def _suffix_matrix(tk):
    a = lax.broadcasted_iota(jnp.int32, (tk, tk), 0)
    b = lax.broadcasted_iota(jnp.int32, (tk, tk), 1)
    return jnp.where(a > b, 1.0, 0.0).astype(BF16)


def _head_rows(x, heads):
    lane_head = lax.broadcasted_iota(jnp.int32, x.shape, 1) // HEAD_DIM
    return jnp.concatenate(
        [jnp.where(lane_head == h, x, jnp.zeros_like(x)) for h in range(heads)], axis=0)


def _fold_heads(o, heads, s):
    lane_head = lax.broadcasted_iota(jnp.int32, (s, o.shape[1]), 1) // HEAD_DIM
    out = jnp.zeros((s, o.shape[1]), F32)
    for h in range(heads):
        out = jnp.where(lane_head == h, o[h * s:(h + 1) * s], out)
    return out


def _sample_attn_kernel(qf_ref, kf_ref, vf_ref, ck_ref, cv_ref, suf_ref, cum_ref,
                        qb_ref, kb_ref, vb_ref, cbk_ref, cbv_ref, of_ref, ob_ref,
                        *, s, h_fox, h_sb, past):
    qa = _head_rows(qf_ref[...], h_fox)
    rows = h_fox * s
    ck = ck_ref[0, 0].astype(BF16)
    cv = cv_ref[0, 0].astype(BF16)
    suf = suf_ref[0]
    cum = cum_ref[0]
    bias_c = jnp.concatenate(
        [jnp.broadcast_to(suf[h:h + 1, :], (s, past)) for h in range(h_fox)], axis=0)
    bias_n = jnp.concatenate(
        [jnp.broadcast_to(-cum[h:h + 1, :], (s, s)) for h in range(h_fox)], axis=0)
    lc = _dot(qa, ck) + bias_c
    ln = _dot_nt(qa, kf_ref[...]) + bias_n
    r_pos = lax.broadcasted_iota(jnp.int32, (rows, s), 0) % s
    k_pos = lax.broadcasted_iota(jnp.int32, (rows, s), 1)
    ln = jnp.where(k_pos <= r_pos, ln, NEG_BIG)
    m = jnp.maximum(jnp.max(lc, axis=1, keepdims=True), jnp.max(ln, axis=1, keepdims=True))
    pc = jnp.exp(lc - m)
    pn = jnp.exp(ln - m)
    den = jnp.sum(pc, axis=1, keepdims=True) + jnp.sum(pn, axis=1, keepdims=True)
    o = (_dot_nt(pc.astype(BF16), cv) + _dot(pn.astype(BF16), vf_ref[...])) / den
    of_ref[...] = _fold_heads(o, h_fox, s).astype(BF16)

    qb = _head_rows(qb_ref[...], h_sb)
    rows_b = h_sb * s
    rb = lax.broadcasted_iota(jnp.int32, (rows_b, s), 0) % s
    cb = lax.broadcasted_iota(jnp.int32, (rows_b, s), 1)
    acc, carry = _sb_block(_dot_nt(qb, kb_ref[...]), jnp.zeros((rows_b, 1), F32), vb_ref[...],
                           _suffix_matrix(s), cb < rb)
    tri = _suffix_matrix(LANES)

    def cond(state):
        return (state[0] < past // LANES) & (jnp.min(state[2]) < PRUNE_LOG)

    def body(state):
        t, acc, carry = state
        k0 = pl.multiple_of(past - (t + 1) * LANES, LANES)
        k = cbk_ref[0, 0, :, pl.ds(k0, LANES)].astype(BF16)
        v = cbv_ref[0, 0, :, pl.ds(k0, LANES)].astype(BF16)
        pv, carry = _sb_block(_dot(qb, k), carry, v, tri, None, v_feature_major=True)
        return t + 1, acc + pv, carry

    _, acc, _ = lax.while_loop(cond, body, (0, acc, carry))
    ob_ref[...] = _fold_heads(acc, h_sb, s).astype(BF16)


def _sample_attn(qf, kf, vf, ck, cv, suf, cum, qb, kb, vb, cbk, cbv, *, layer, batch, s):
    w_fox, w_sb = qf.shape[1], qb.shape[1]
    past = ck.shape[3]
    h_fox, h_sb = w_fox // HEAD_DIM, w_sb // HEAD_DIM
    new = lambda w: pl.BlockSpec((s, w), lambda b: (b, 0))
    cache = lambda w: pl.BlockSpec((1, 1, w, past), lambda b: (layer, b, 0, 0))
    kern = functools.partial(_sample_attn_kernel, s=s, h_fox=h_fox, h_sb=h_sb, past=past)
    return pl.pallas_call(
        kern,
        grid=(batch,),
        in_specs=[new(w_fox), new(w_fox), new(w_fox), cache(w_fox), cache(w_fox),
                  pl.BlockSpec((1, 8, past), lambda b: (b, 0, 0)),
                  pl.BlockSpec((1, 8, s), lambda b: (b, 0, 0)),
                  new(w_sb), new(w_sb), new(w_sb), cache(w_sb), cache(w_sb)],
        out_specs=[new(w_fox), new(w_sb)],
        out_shape=[jax.ShapeDtypeStruct((batch * s, w_fox), BF16),
                   jax.ShapeDtypeStruct((batch * s, w_sb), BF16)],
        compiler_params=_cparams(("arbitrary",)),
        name="sample_attn",
    )(qf, kf, vf, ck, cv, suf, cum, qb, kb, vb, cbk, cbv)


def _merge_kernel(x_ref, sc_ref, sh_ref, gt_ref, gmix_ref, ysgu_ref, ofox_ref, osb_ref,
                  wg_ref, bg_ref, wbs_ref, wbf_ref, wbb_ref, wo_ref, o_ref):
    x = x_ref[...]
    d = x.shape[1]
    h = _modulated_norm(x, gmix_ref[0], sc_ref[0], sh_ref[0])
    gates = _dot(h.astype(BF16), wg_ref[0]) + bg_ref[0]
    gates = 1.0 / (1.0 + jnp.exp(-gates))
    merged = gates[:, 0:d] * _dot(ysgu_ref[...], wbs_ref[0]) \
        + gates[:, d:2 * d] * _dot(ofox_ref[...], wbf_ref[0]) \
        + gates[:, 2 * d:3 * d] * _dot(osb_ref[...], wbb_ref[0])
    o_ref[...] = x + gt_ref[0] * _dot(merged.astype(BF16), wo_ref[0])


def _merge(x, mod, layer, gmix, ysgu, ofox, osb, wg, bg, wbs, wbf, wbb, wo, *, tm):
    n, d = x.shape
    row = lambda width: pl.BlockSpec((tm, width), lambda i: (i, 0))
    lay = lambda a: _layer_spec(a, layer)
    return pl.pallas_call(
        _merge_kernel,
        grid=(n // tm,),
        in_specs=[row(d), _mod_spec(mod, layer, 1, tm), _mod_spec(mod, layer, 0, tm),
                  _mod_spec(mod, layer, 2, tm), lay(gmix),
                  row(ysgu.shape[1]), row(ofox.shape[1]), row(osb.shape[1]),
                  lay(wg), lay(bg), lay(wbs), lay(wbf), lay(wbb), lay(wo)],
        out_specs=row(d),
        out_shape=jax.ShapeDtypeStruct((n, d), F32),
        compiler_params=_cparams(("arbitrary",)),
        name="merge",
    )(x, mod, mod, mod, gmix, ysgu, ofox, osb, wg, bg, wbs, wbf, wbb, wo)


def _ffn_kernel(x_ref, sc_ref, sh_ref, gt_ref, g_ref, wi_ref, wo_ref, o_ref, *, d_ff):
    x = x_ref[...]
    h = _modulated_norm(x, g_ref[0], sc_ref[0], sh_ref[0])
    ag = _dot(h.astype(BF16), wi_ref[0])
    a = ag[:, 0:d_ff]
    act = a * (1.0 / (1.0 + jnp.exp(-a))) * ag[:, d_ff:2 * d_ff]
    o_ref[...] = x + gt_ref[0] * _dot(act.astype(BF16), wo_ref[0])


def _ffn(x, mod, layer, g, wi, wo, *, tm):
    n, d = x.shape
    d_ff = wo.shape[1]
    row = pl.BlockSpec((tm, d), lambda i: (i, 0))
    return pl.pallas_call(
        functools.partial(_ffn_kernel, d_ff=d_ff),
        grid=(n // tm,),
        in_specs=[row, _mod_spec(mod, layer, 4, tm), _mod_spec(mod, layer, 3, tm),
                  _mod_spec(mod, layer, 5, tm), _layer_spec(g, layer), _layer_spec(wi, layer),
                  _layer_spec(wo, layer)],
        out_specs=row,
        out_shape=jax.ShapeDtypeStruct((n, d), F32),
        compiler_params=_cparams(("arbitrary",)),
        name="ffn",
    )(x, mod, mod, mod, g, wi, wo)


def _indicator(width, group):
    idx = np.arange(width) // group
    return jnp.asarray(idx[:, None] == idx[None, :], dtype=BF16)


def kernel(x_prompt, x_sample, c_prompt, c_sample, cache_fox_k, cache_fox_v, cache_fox_logf,
           cache_sb_k, cache_sb_v, w_ada, b_ada, g_mix, g_ffn, w_in, g_sgu_v, w_sgu, b_sgu, b_fgt,
           g_q, g_k, w_br_sgu, w_br_fox, w_br_sb, w_gate, b_gate, w_out, w_ffn_in, w_ffn_out):
    batch, seq, d = x_prompt.shape
    dec_batch, dec_seq, _ = x_sample.shape
    depth = w_ada.shape[0]
    past = cache_fox_k.shape[2]
    h_fox, h_sb = cache_fox_k.shape[3], cache_sb_k.shape[3]
    w_fox, w_sb = h_fox * HEAD_DIM, h_sb * HEAD_DIM
    g_sgu, cg = g_sgu_v.shape[1], g_sgu_v.shape[2]
    w_sgu_ = g_sgu * cg
    assert batch == 1 and g_sgu == G_SGU and w_sgu.shape[2] == SGU_LEN
    n_dec = dec_batch * dec_seq

    n_c = batch + dec_batch
    c_rows = -(-n_c // 8) * 8
    c_all = jnp.zeros((c_rows, d), F32).at[:n_c].set(jnp.concatenate([c_prompt, c_sample], axis=0))
    mod = _modulation(c_all, w_ada, b_ada)

    offs = np.cumsum([0, w_sgu_, w_sgu_, w_fox, w_fox, w_fox, h_fox, w_sb, w_sb, w_sb]).tolist()
    f_cols = jnp.zeros((depth, d, LANES), F32).at[:, :, :h_fox].set(w_in[:, :, offs[5]:offs[6]])
    w_main = jnp.concatenate([w_in[:, :, :offs[5]], w_in[:, :, offs[6]:], f_cols],
                             axis=2).astype(BF16)
    bf_pad = jnp.zeros((depth, 1, LANES), F32).at[:, 0, :h_fox].set(b_fgt)
    gmix3, gffn3, bg3 = g_mix.reshape(depth, 1, d), g_ffn.reshape(depth, 1, d), b_gate.reshape(depth, 1, 3 * d)
    mod_p = mod[:, 0:batch]
    mod_s = jnp.repeat(mod[:, batch:batch + dec_batch], dec_seq, axis=1)
    ind96, ind64 = _indicator(w_sgu_, cg), _indicator(w_fox, HEAD_DIM)
    gq_t = jnp.tile(g_q, (1, h_fox)).reshape(depth, 1, w_fox)
    gk_t = jnp.tile(g_k, (1, h_fox)).reshape(depth, 1, w_fox)
    gsgu = g_sgu_v.reshape(depth, 1, w_sgu_)
    msgu_p = jnp.transpose(w_sgu, (0, 2, 1, 3)).reshape(depth, SGU_LEN, g_sgu * SGU_LEN)
    reps = SGU_LEN // dec_seq
    w_small = jnp.tile(w_sgu[:, :, :dec_seq, :dec_seq], (1, 1, reps, reps))
    msgu_s = jnp.transpose(w_small, (0, 2, 1, 3)).reshape(depth, SGU_LEN, g_sgu * SGU_LEN)
    bsgu_p = jnp.repeat(jnp.transpose(b_sgu, (0, 2, 1)), cg, axis=2)
    bsgu_s = jnp.tile(bsgu_p[:, :dec_seq], (1, reps, 1))
    wg, wbs, wbf, wbb = (w.astype(BF16) for w in (w_gate, w_br_sgu, w_br_fox, w_br_sb))
    wo, wfi, wfo = (w.astype(BF16) for w in (w_out, w_ffn_in, w_ffn_out))

    to_fm = lambda c: jnp.transpose(c, (0, 1, 3, 4, 2)).reshape(
        depth, dec_batch, c.shape[3] * HEAD_DIM, past)
    ck_t, cv_t, cbk_t, cbv_t = (to_fm(c) for c in (cache_fox_k, cache_fox_v, cache_sb_k, cache_sb_v))

    xp = x_prompt.reshape(seq, d)
    xs = x_sample.reshape(n_dec, d)
    tm_p = min(512, seq)
    tm_f = min(256, seq)
    tq = min(256, seq)
    st_p, st_s = [], []
    for l in range(depth):
        shared = (l, gmix3, w_main, gsgu, gq_t, gk_t, bf_pad, ind96, ind64)
        branch_w = (wg, bg3, wbs, wbf, wbb, wo)

        (ysgu, kaug, qft, vft, qbt, kb16, vbt, kf32, vf32, kb32, vb32, logft, fcumt) = _inproj(
            xp, mod_p, *shared, msgu_p, bsgu_p, tm=tm_p, period=SGU_LEN, sweep=True)
        bound = 1.01 * HEAD_DIM ** 0.5 * jnp.max(jnp.abs(g_q[l])) * jnp.max(jnp.abs(g_k[l]))
        cut = (-(2.0 * bound + PRUNE_LOG)).reshape(1)
        ofox = _fox_prompt(fcumt[:, 0::LANES], fcumt[:, LANES - 1::LANES], cut, qft, kaug, vft, tq=tq)
        osb = _sb_prompt(qbt, kb16, vbt, tq=min(128, seq))
        x1 = _merge(xp, mod_p, l, gmix3, ysgu, ofox, osb, *branch_w, tm=tm_p)
        xp = _ffn(x1, mod_p, l, gffn3, wfi, wfo, tm=tm_f)
        st_p.append((kf32.reshape(batch, seq, h_fox, HEAD_DIM),
                     vf32.reshape(batch, seq, h_fox, HEAD_DIM),
                     logft[:h_fox].T.reshape(batch, seq, h_fox),
                     kb32.reshape(batch, seq, h_sb, HEAD_DIM),
                     vb32.reshape(batch, seq, h_sb, HEAD_DIM)))

        (ysgu, qf, kf16, vf16, qb, kb16, vb16, kf32, vf32, kb32, vb32, logft, sguv) = _inproj(
            xs, mod_s, *shared, msgu_s, bsgu_s, tm=n_dec, period=dec_seq, sweep=False)
        clf = jnp.zeros((dec_batch, 8, past), F32).at[:, :h_fox].set(
            jnp.transpose(cache_fox_logf[l], (0, 2, 1)))
        suf, cum = _sample_cumsum(clf.reshape(dec_batch * 8, past), logft, dec_seq)
        cum_b = jnp.transpose(cum.reshape(8, dec_batch, dec_seq), (1, 0, 2))
        ofox, osb = _sample_attn(
            qf, kf16, vf16, ck_t, cv_t, suf.reshape(dec_batch, 8, past), cum_b,
            qb, kb16, vb16, cbk_t, cbv_t, layer=l, batch=dec_batch, s=dec_seq)
        x1 = _merge(xs, mod_s, l, gmix3, ysgu, ofox, osb, *branch_w, tm=n_dec)
        xs = _ffn(x1, mod_s, l, gffn3, wfi, wfo, tm=n_dec)
        st_s.append((kf32.reshape(dec_batch, dec_seq, h_fox, HEAD_DIM),
                     vf32.reshape(dec_batch, dec_seq, h_fox, HEAD_DIM),
                     logft[:h_fox].T.reshape(dec_batch, dec_seq, h_fox),
                     kb32.reshape(dec_batch, dec_seq, h_sb, HEAD_DIM),
                     vb32.reshape(dec_batch, dec_seq, h_sb, HEAD_DIM),
                     sguv.reshape(dec_batch, dec_seq, w_sgu_)))

    def stack(states, idx):
        return jnp.stack([s[idx] for s in states], axis=0)

    return (xp.reshape(batch, seq, d), xs.reshape(dec_batch, dec_seq, d),
            stack(st_p, 0), stack(st_p, 1), stack(st_p, 2), stack(st_p, 3), stack(st_p, 4),
            stack(st_s, 0), stack(st_s, 1), stack(st_s, 2), stack(st_s, 3), stack(st_s, 4),
            stack(st_s, 5))
```

```python
import functools

import numpy as np
import jax
import jax.numpy as jnp
from jax import lax
from jax.experimental import pallas as pl
from jax.experimental.pallas import tpu as pltpu

F32 = jnp.float32
BF16 = jnp.bfloat16

EPS = 1e-6
HEAD_DIM = 64
LANES = 128
CHUNK = 64
SGU_LEN = 128
G_SGU = 4
NEG_BIG = -1e30
LOG2E = 1.4426950408889634

PRUNE_LOG = 30.0

VMEM_LIMIT = 56 * 1024 * 1024


def _cparams(sem):
    return pltpu.CompilerParams(dimension_semantics=sem, vmem_limit_bytes=VMEM_LIMIT)


def _const_spec(shape):
    nd = len(shape)
    return pl.BlockSpec(shape, lambda *_: (0,) * nd, pipeline_mode=pl.Buffered(1))


def _layer_spec(arr, layer):
    nd = arr.ndim
    return pl.BlockSpec((1,) + arr.shape[1:], lambda *_: (layer,) + (0,) * (nd - 1),
                        pipeline_mode=pl.Buffered(1))


def _mod_spec(mod, layer, k, tm):
    d = mod.shape[2] // 6
    if mod.shape[1] == 1:
        return pl.BlockSpec((1, 1, d), lambda i: (layer, 0, k))
    return pl.BlockSpec((1, tm, d), lambda i: (layer, i, k))


def _dot(a, b):
    return jnp.dot(a, b, preferred_element_type=F32)


def _dot_nt(a, b):
    return lax.dot_general(a, b, (((1,), (1,)), ((), ())), preferred_element_type=F32)


def _split3(x):
    h = x.astype(BF16)
    r = x - h.astype(F32)
    m = r.astype(BF16)
    l = (r - m.astype(F32)).astype(BF16)
    return h, m, l


def _split2(x):
    h = x.astype(BF16)
    l = (x - h.astype(F32)).astype(BF16)
    return h, l


def _mod_kernel(c_ref, w_ref, b_ref, o_ref):
    c = c_ref[...]
    s = c * (1.0 / (1.0 + jnp.exp(-c)))
    o_ref[0] = _dot(s.astype(BF16), w_ref[0].astype(BF16)) + b_ref[0]


def _modulation(c_all, w_ada, b_ada):
    depth, d, n6 = w_ada.shape
    rows = c_all.shape[0]
    tn = 1024
    return pl.pallas_call(
        _mod_kernel,
        grid=(depth, n6 // tn),
        in_specs=[
            pl.BlockSpec((rows, d), lambda l, j: (0, 0)),
            pl.BlockSpec((1, d, tn), lambda l, j: (l, 0, j)),
            pl.BlockSpec((1, 1, tn), lambda l, j: (l, 0, j)),
        ],
        out_specs=pl.BlockSpec((1, rows, tn), lambda l, j: (l, 0, j)),
        out_shape=jax.ShapeDtypeStruct((depth, rows, n6), F32),
        compiler_params=_cparams(("arbitrary", "arbitrary")),
        name="adaln_mod",
    )(c_all, w_ada, b_ada.reshape(depth, 1, n6))


def _modulated_norm(x, g, sc, sh):
    ms = jnp.mean(x * x, axis=-1, keepdims=True)
    return (x * lax.rsqrt(ms + EPS)) * g * (1.0 + sc) + sh


def _group_rms(t, ind, inv_size, g):
    sq = (t * t).astype(BF16)
    wb = ind.shape[0]
    ss = [_dot(sq[:, c:c + wb], ind) for c in range(0, t.shape[1], wb)]
    ss = jnp.concatenate(ss, axis=1) if len(ss) > 1 else ss[0]
    return t * lax.rsqrt(ss * inv_size + EPS) * g


def _log_sigmoid(x):
    return jnp.minimum(x, 0.0) - jnp.log(1.0 + jnp.exp(-jnp.abs(x)))


def _augmented_keys(kfn, f_cum, heads):
    lane = lax.broadcasted_iota(jnp.int32, f_cum.shape, 1)
    hi, mid, lo = (t.astype(F32) for t in _split3(f_cum * -LOG2E))
    aug_even = pltpu.roll(hi, 64, 1) + pltpu.roll(mid, 72, 1) + pltpu.roll(lo, 80, 1)
    aug_odd = hi + pltpu.roll(mid, 8, 1) + pltpu.roll(lo, 16, 1)
    blocks = []
    for h in range(heads):
        kp = kfn[:, (h // 2) * LANES:(h // 2 + 1) * LANES]
        if h % 2 == 0:
            blocks.append(jnp.where(lane < HEAD_DIM, kp, aug_even))
        else:
            blocks.append(jnp.where(lane >= HEAD_DIM, kp, aug_odd))
    return jnp.concatenate(blocks, axis=1).astype(BF16)


def _inproj_kernel(x_ref, sc_ref, sh_ref, gmix_ref, w_ref, gsgu_ref, gq_ref, gk_ref, bf_ref,
                   ind96_ref, ind64_ref, msgu_ref, bsgu_ref, *rest,
                   tm, w_sgu, w_fox, w_sb, period, sweep):
    if sweep:
        (tril_ref, ysgu_ref, kaug_ref, qft_ref, vft_ref, qbt_ref, kb16_ref, vbt_ref,
         kf32_ref, vf32_ref, kb32_ref, vb32_ref, logft_ref, fcumt_ref, carry_ref) = rest
    else:
        (ysgu_ref, qf_ref, kf16_ref, vf16_ref, qb_ref, kb16_ref, vb16_ref,
         kf32_ref, vf32_ref, kb32_ref, vb32_ref, logft_ref, sguv_ref) = rest
    x = x_ref[...]
    h = _modulated_norm(x, gmix_ref[0], sc_ref[0], sh_ref[0])
    p = _dot(h.astype(BF16), w_ref[0])

    o = 0
    u = p[:, o:o + w_sgu]; o += w_sgu
    vs = p[:, o:o + w_sgu]; o += w_sgu
    qf = p[:, o:o + w_fox]; o += w_fox
    kf = p[:, o:o + w_fox]; o += w_fox
    vf = p[:, o:o + w_fox]; o += w_fox
    qb = p[:, o:o + w_sb]; o += w_sb
    kb = p[:, o:o + w_sb]; o += w_sb
    vb = p[:, o:o + w_sb]; o += w_sb
    fl = p[:, o:o + LANES]

    scale = HEAD_DIM ** -0.5
    ind64 = ind64_ref[...]
    qfn = _group_rms(qf, ind64, 1.0 / HEAD_DIM, gq_ref[0])
    kfn = _group_rms(kf, ind64, 1.0 / HEAD_DIM, gk_ref[0])
    kf32_ref[...] = kfn
    vf32_ref[...] = vf
    kb32_ref[...] = kb
    vb32_ref[...] = vb
    kb16_ref[...] = kb.astype(BF16)
    lf = _log_sigmoid(fl + bf_ref[0])
    logft_ref[...] = lf.T[0:8, :]
    if sweep:
        qft_ref[...] = (qfn * (scale * LOG2E)).T.astype(BF16)
        vft_ref[...] = vf.T.astype(BF16)
        qbt_ref[...] = (qb * scale).T.astype(BF16)
        vbt_ref[...] = vb.T.astype(BF16)
        @pl.when(pl.program_id(0) == 0)
        def _():
            carry_ref[...] = jnp.zeros(carry_ref.shape, F32)
        lane = lax.broadcasted_iota(jnp.int32, lf.shape, 1)
        lfh, lfm, lfl = _split3(jnp.where(lane < 8, lf, 0.0))
        tril = tril_ref[...]
        run = carry_ref[0:1, :]
        chunks = []
        for c in range(0, tm, LANES):
            part = (_dot(tril, lfh[c:c + LANES]) + _dot(tril, lfm[c:c + LANES])
                    + _dot(tril, lfl[c:c + LANES]) + run)
            run = part[LANES - 1:LANES, :]
            chunks.append(part)
        f_cum = jnp.concatenate(chunks, axis=0) if len(chunks) > 1 else chunks[0]
        carry_ref[...] = jnp.broadcast_to(run, carry_ref.shape)
        fcumt_ref[...] = f_cum.T[0:8, :]
        kaug_ref[...] = _augmented_keys(kfn, f_cum, w_fox // HEAD_DIM)
    else:
        qf_ref[...] = (qfn * scale).astype(BF16)
        kf16_ref[...] = kfn.astype(BF16)
        vf16_ref[...] = vf.astype(BF16)
        qb_ref[...] = (qb * scale).astype(BF16)
        vb16_ref[...] = vb.astype(BF16)

    cg = w_sgu // G_SGU
    vsn = _group_rms(vs, ind96_ref[...], 1.0 / cg, gsgu_ref[0])
    if not sweep:
        sguv_ref[...] = vsn
    r = lax.broadcasted_iota(jnp.int32, (SGU_LEN, G_SGU * SGU_LEN), 0)
    c = lax.broadcasted_iota(jnp.int32, (SGU_LEN, G_SGU * SGU_LEN), 1) % SGU_LEN
    keep = (r // period == c // period) & ((c % period) // CHUNK <= (r % period) // CHUNK)
    mix = jnp.where(keep, msgu_ref[0], 0.0).astype(BF16)
    lane_group = lax.broadcasted_iota(jnp.int32, (SGU_LEN, w_sgu), 1) // cg
    vsb = vsn.astype(BF16)
    spat = []
    for ci in range(tm // SGU_LEN):
        vc = vsb[ci * SGU_LEN:(ci + 1) * SGU_LEN]
        stacked = jnp.concatenate(
            [jnp.where(lane_group == g, vc, jnp.zeros_like(vc)) for g in range(G_SGU)], axis=0)
        spat.append(_dot(mix, stacked) + bsgu_ref[0])
    spat = jnp.concatenate(spat, axis=0) if len(spat) > 1 else spat[0]
    ysgu_ref[...] = (u * spat).astype(BF16)


def _inproj(x, mod, layer, gmix, w, gsgu, gq, gk, bf, ind96, ind64, msgu, bsgu, *, tm, period,
            sweep):
    n, d = x.shape
    w_sgu, w_fox = gsgu.shape[2], gq.shape[2]
    w_sb = (w.shape[2] - LANES - 2 * w_sgu - 3 * w_fox) // 3
    row = lambda width: pl.BlockSpec((tm, width), lambda i: (i, 0))
    col = lambda height: pl.BlockSpec((height, tm), lambda i: (0, i))
    sds = jax.ShapeDtypeStruct
    states_specs = [row(w_fox), row(w_fox), row(w_sb), row(w_sb), col(8)]
    states_shape = [sds((n, w_fox), F32), sds((n, w_fox), F32), sds((n, w_sb), F32),
                    sds((n, w_sb), F32), sds((8, n), F32)]
    operands = [x, mod, mod, gmix, w, gsgu, gq, gk, bf, ind96, ind64, msgu, bsgu]
    lay = lambda a: _layer_spec(a, layer)
    in_specs = [row(d), _mod_spec(mod, layer, 1, tm), _mod_spec(mod, layer, 0, tm), lay(gmix), lay(w),
                lay(gsgu), lay(gq), lay(gk), lay(bf), _const_spec(ind96.shape),
                _const_spec(ind64.shape), lay(msgu), lay(bsgu)]
    scratch = []
    if sweep:
        heads = w_fox // HEAD_DIM
        a = np.arange(LANES)
        operands.append(jnp.asarray(a[None, :] <= a[:, None], dtype=BF16))
        in_specs.append(_const_spec((LANES, LANES)))
        out_specs = [row(w_sgu), row(heads * LANES), col(w_fox), col(w_fox), col(w_sb), row(w_sb),
                     col(w_sb)] + states_specs + [col(8)]
        out_shape = [sds((n, w_sgu), BF16), sds((n, heads * LANES), BF16), sds((w_fox, n), BF16),
                     sds((w_fox, n), BF16), sds((w_sb, n), BF16), sds((n, w_sb), BF16),
                     sds((w_sb, n), BF16)] + states_shape + [sds((8, n), F32)]
        scratch = [pltpu.VMEM((8, LANES), F32)]
    else:
        out_specs = [row(w_sgu), row(w_fox), row(w_fox), row(w_fox), row(w_sb), row(w_sb),
                     row(w_sb)] + states_specs + [row(w_sgu)]
        out_shape = [sds((n, w_sgu), BF16), sds((n, w_fox), BF16), sds((n, w_fox), BF16),
                     sds((n, w_fox), BF16), sds((n, w_sb), BF16), sds((n, w_sb), BF16),
                     sds((n, w_sb), BF16)] + states_shape + [sds((n, w_sgu), F32)]
    kern = functools.partial(_inproj_kernel, tm=tm, w_sgu=w_sgu, w_fox=w_fox, w_sb=w_sb,
                             period=period, sweep=sweep)
    return pl.pallas_call(
        kern,
        grid=(n // tm,),
        in_specs=in_specs,
        out_specs=out_specs,
        out_shape=out_shape,
        scratch_shapes=scratch,
        compiler_params=_cparams(("arbitrary",)),
        name="inproj",
    )(*operands)


def _seq_cumsum(x, nc, reverse_exclusive):
    rows = x.shape[0]
    a = lax.broadcasted_iota(jnp.int32, (LANES, LANES), 0)
    b = lax.broadcasted_iota(jnp.int32, (LANES, LANES), 1)
    tri = (a > b) if reverse_exclusive else (a <= b)
    tri = jnp.where(tri, 1.0, 0.0).astype(BF16)
    ones = jnp.ones((LANES, LANES), BF16)
    xh, xm, xl = _split3(x)
    within = _dot(xh, tri) + _dot(xm, tri) + _dot(xl, tri)
    tot = _dot(xh, ones) + _dot(xm, ones) + _dot(xl, ones)
    ra = lax.broadcasted_iota(jnp.int32, (rows, rows), 0)
    rb = lax.broadcasted_iota(jnp.int32, (rows, rows), 1)
    other = (rb > ra) if reverse_exclusive else (rb < ra)
    blk = jnp.where((ra // nc == rb // nc) & other, 1.0, 0.0).astype(BF16)
    th, tm_, tl = _split3(tot)
    return within + _dot(blk, th) + _dot(blk, tm_) + _dot(blk, tl)


def _sample_cumsum_kernel(clf_ref, lf_ref, suf_ref, cum_ref, *, nc, dec_seq):
    suf_ref[...] = _seq_cumsum(clf_ref[...], nc, reverse_exclusive=True)
    a = lax.broadcasted_iota(jnp.int32, (LANES, LANES), 0)
    b = lax.broadcasted_iota(jnp.int32, (LANES, LANES), 1)
    tri = jnp.where((a // dec_seq == b // dec_seq) & (a <= b), 1.0, 0.0).astype(BF16)
    xh, xm, xl = _split3(lf_ref[...])
    cum_ref[...] = _dot(xh, tri) + _dot(xm, tri) + _dot(xl, tri)


def _sample_cumsum(clogf_t, logft, dec_seq):
    rows, past = clogf_t.shape
    nc = past // LANES
    suf, cum = pl.pallas_call(
        functools.partial(_sample_cumsum_kernel, nc=nc, dec_seq=dec_seq),
        out_shape=[jax.ShapeDtypeStruct((rows * nc, LANES), F32),
                   jax.ShapeDtypeStruct(logft.shape, F32)],
        compiler_params=pltpu.CompilerParams(vmem_limit_bytes=VMEM_LIMIT),
        name="sample_logf_cumsum",
    )(clogf_t.reshape(rows * nc, LANES), logft)
    return suf.reshape(rows, past), cum


def _head_query_blocks(qt_ref, heads, tq, ones_rows):
    rowi = lax.broadcasted_iota(jnp.int32, (LANES, tq), 0)
    out = []
    for h in range(heads):
        qp = qt_ref[(h // 2) * LANES:(h // 2 + 1) * LANES, :]
        own = (rowi < HEAD_DIM) if h % 2 == 0 else (rowi >= HEAD_DIM)
        fill = jnp.zeros((LANES, tq), F32)
        for r in ones_rows(h):
            fill = jnp.where(rowi == r, 1.0, fill)
        out.append(jnp.where(own, qp, fill.astype(BF16)))
    return out


def _bias_rows(h):
    base = HEAD_DIM if h % 2 == 0 else 0
    return [base + h, base + 8 + h, base + 16 + h]


def _fox_kernel(fs_ref, fe_ref, cut_ref, qt_ref, ka_ref, vt_ref, o_ref, *, tq, tk, heads):
    i = pl.program_id(0)
    ratio = tq // tk
    cq, ck = tq // LANES, tk // LANES
    qa = _head_query_blocks(qt_ref, heads, tq, _bias_rows)
    rowi = lax.broadcasted_iota(jnp.int32, (LANES, tq), 0)
    krow = lax.broadcasted_iota(jnp.int32, (tk, tq), 0)
    qcol = lax.broadcasted_iota(jnp.int32, (tk, tq), 1)

    def logits_of(j):
        k0 = pl.multiple_of(j * tk, tk)
        return tuple(_dot(ka_ref[pl.ds(k0, tk), h * LANES:(h + 1) * LANES], qa[h])
                     for h in range(heads))

    def absorb(j, logits, carry, masked):
        k0 = pl.multiple_of(j * tk, tk)
        out = []
        for h in range(heads):
            vt = vt_ref[(h // 2) * LANES:(h // 2 + 1) * LANES, pl.ds(k0, tk)]
            m, l, acc = carry[3 * h:3 * h + 3]
            s = logits[h]
            if masked:
                s = jnp.where(krow + (j * tk - i * tq) <= qcol, s, NEG_BIG)
            mn = jnp.maximum(m, jnp.max(s, axis=0, keepdims=True))
            alpha = jnp.exp2(m - mn)
            pe = jnp.exp2(s - mn)
            l = alpha * l + jnp.sum(pe, axis=0, keepdims=True)
            acc = alpha * acc + _dot(vt, pe.astype(BF16))
            out += [mn, l, acc]
        return tuple(out)

    init = []
    for _ in range(heads):
        init += [jnp.full((1, tq), NEG_BIG, F32), jnp.zeros((1, tq), F32),
                 jnp.zeros((LANES, tq), F32)]
    diag = [i * ratio + t for t in range(ratio)]
    diag_logits = [logits_of(j) for j in diag]
    carry = tuple(init)
    for j, lg in zip(diag, diag_logits):
        carry = absorb(j, lg, carry, True)

    def cond(state):
        j = state[0]
        jc = jnp.maximum(j, 0) * ck + (ck - 1)
        live = fs_ref[0, i * cq] - fe_ref[0, jc] >= cut_ref[0]
        for h in range(1, heads):
            live = live | (fs_ref[h, i * cq] - fe_ref[h, jc] >= cut_ref[0])
        return (j >= 0) & live

    def body(state):
        j = state[0]
        return (j - 1,) + absorb(j, logits_of(j), state[1:], False)

    res = lax.while_loop(cond, body, (i * ratio - 1,) + carry)[1:]
    for p in range(heads // 2):
        _, l0, a0, _, l1, a1 = res[6 * p:6 * p + 6]
        ot = jnp.where(rowi < HEAD_DIM, a0 / l0, a1 / l1)
        o_ref[:, p * LANES:(p + 1) * LANES] = ot.T.astype(BF16)


def _fox_prompt(fs, fe, cut, qt, kaug, vt, *, tq, tk):
    w, n = qt.shape
    heads = w // HEAD_DIM
    grid_spec = pltpu.PrefetchScalarGridSpec(
        num_scalar_prefetch=3,
        grid=(n // tq,),
        in_specs=[
            pl.BlockSpec((w, tq), lambda i, *_: (0, i)),
            pl.BlockSpec(kaug.shape, lambda i, *_: (0, 0), pipeline_mode=pl.Buffered(1)),
            pl.BlockSpec((w, n), lambda i, *_: (0, 0), pipeline_mode=pl.Buffered(1)),
        ],
        out_specs=pl.BlockSpec((tq, w), lambda i, *_: (i, 0)),
    )
    return pl.pallas_call(
        functools.partial(_fox_kernel, tq=tq, tk=tk, heads=heads),
        grid_spec=grid_spec,
        out_shape=jax.ShapeDtypeStruct((n, w), BF16),
        compiler_params=_cparams(("arbitrary",)),
        name="fox_prompt",
    )(fs, fe, cut, qt, kaug, vt)


def _sb_kernel(qt_ref, k_ref, vt_ref, o_ref, *, tq, heads):
    i = pl.program_id(0)
    tk = tq
    qm = _head_query_blocks(qt_ref, heads, tq, lambda h: [])
    rowi = lax.broadcasted_iota(jnp.int32, (LANES, tq), 0)
    krow = lax.broadcasted_iota(jnp.int32, (tk, tq), 0)
    qcol = lax.broadcasted_iota(jnp.int32, (tk, tq), 1)
    a = lax.broadcasted_iota(jnp.int32, (tk, 2 * tk), 0)
    b = lax.broadcasted_iota(jnp.int32, (tk, 2 * tk), 1) % tk
    tri2 = jnp.where(b > a, 1.0, 0.0).astype(BF16)

    def sweep(blocks, carry):
        k0s = [pl.multiple_of(b[0] * tk, tk) for b in blocks]
        z = [[_dot(k_ref[pl.ds(k0, tk), (h // 2) * LANES:(h // 2 + 1) * LANES], qm[h])
              for h in range(heads)] for k0 in k0s]
        logsig, later, mass = {}, {}, {}
        for bi, (_, mask, valid) in enumerate(blocks):
            for h in range(heads):
                zz = z[bi][h]
                sp = jnp.maximum(zz, 0.0) + jnp.log(1.0 + jnp.exp(-jnp.abs(zz)))
                logsig[bi, h] = zz - sp
                if mask is not None:
                    sp = jnp.where(mask, sp, 0.0)
                hi, lo = _split2(sp)
                later[bi, h] = _dot(tri2, jnp.concatenate([hi, lo], axis=0))
                mass[bi, h] = jnp.sum(sp, axis=0, keepdims=True)
                if valid is not None:
                    mass[bi, h] = mass[bi, h] * valid
        out = []
        for h in range(heads):
            cr, acc = carry[2 * h:2 * h + 2]
            for bi, (_, mask, valid) in enumerate(blocks):
                w = jnp.exp(logsig[bi, h] - later[bi, h] - cr)
                if mask is not None:
                    w = jnp.where(mask, w, 0.0)
                if valid is not None:
                    w = w * valid
                vt = vt_ref[(h // 2) * LANES:(h // 2 + 1) * LANES, pl.ds(k0s[bi], tk)]
                acc = acc + _dot(vt, w.astype(BF16))
                cr = cr + mass[bi, h]
            out += [cr, acc]
        return tuple(out)

    zero = (jnp.zeros((1, tq), F32), jnp.zeros((LANES, tq), F32))
    carry = sweep([(i, krow < qcol, None),
                   (jnp.maximum(i - 1, 0), None, jnp.where(i > 0, 1.0, 0.0))], zero * heads)

    def cond(state):
        live = jnp.min(state[1])
        for h in range(1, heads):
            live = jnp.minimum(live, jnp.min(state[1 + 2 * h]))
        return (state[0] >= 0) & (live < PRUNE_LOG)

    def body(state):
        j = state[0]
        return (j - 1,) + sweep([(j, None, None)], state[1:])

    res = lax.while_loop(cond, body, (i - 2,) + carry)[1:]
    for p in range(heads // 2):
        ot = jnp.where(rowi < HEAD_DIM, res[4 * p + 1], res[4 * p + 3])
        o_ref[:, p * LANES:(p + 1) * LANES] = ot.T.astype(BF16)


def _sb_prompt(qt, k, vt, *, tq):
    w, n = qt.shape
    return pl.pallas_call(
        functools.partial(_sb_kernel, tq=tq, heads=w // HEAD_DIM),
        grid=(n // tq,),
        in_specs=[
            pl.BlockSpec((w, tq), lambda i: (0, i)),
            pl.BlockSpec((n, w), lambda i: (0, 0), pipeline_mode=pl.Buffered(1)),
            pl.BlockSpec((w, n), lambda i: (0, 0), pipeline_mode=pl.Buffered(1)),
        ],
        out_specs=pl.BlockSpec((tq, w), lambda i: (i, 0)),
        out_shape=jax.ShapeDtypeStruct((n, w), BF16),
        compiler_params=_cparams(("arbitrary",)),
        name="sb_prompt",
    )(qt, k, vt)


def _sb_block(z, carry, v, tri, mask, v_feature_major=False):
    lg = jnp.log(1.0 + jnp.exp(-jnp.abs(z)))
    sp = jnp.maximum(z, 0.0) + lg
    if mask is not None:
        sp = jnp.where(mask, sp, 0.0)
    hi, lo = _split2(sp)
    later = _dot(hi, tri) + _dot(lo, tri)
    a = jnp.exp((jnp.minimum(z, 0.0) - lg) - later - carry)
    if mask is not None:
        a = jnp.where(mask, a, 0.0)
    pv = _dot_nt(a.astype(BF16), v) if v_feature_major else _dot(a.astype(BF16), v)
    return pv, carry + jnp.sum(sp, axis=1, keepdims=True)


def _suffix_matrix(tk):
    a = lax.broadcasted_iota(jnp.int32, (tk, tk), 0)
    b = lax.broadcasted_iota(jnp.int32, (tk, tk), 1)
    return jnp.where(a > b, 1.0, 0.0).astype(BF16)


def _head_rows(x, heads):
    lane_head = lax.broadcasted_iota(jnp.int32, x.shape, 1) // HEAD_DIM
    return jnp.concatenate(
        [jnp.where(lane_head == h, x, jnp.zeros_like(x)) for h in range(heads)], axis=0)


def _fold_heads(o, heads, s):
    lane_head = lax.broadcasted_iota(jnp.int32, (s, o.shape[1]), 1) // HEAD_DIM
    out = jnp.zeros((s, o.shape[1]), F32)
    for h in range(heads):
        out = jnp.where(lane_head == h, o[h * s:(h + 1) * s], out)
    return out


def _sample_attn_kernel(qf_ref, kf_ref, vf_ref, ck_ref, cv_ref, suf_ref, cum_ref,
                        qb_ref, kb_ref, vb_ref, cbk_ref, cbv_ref, of_ref, ob_ref,
                        *, s, h_fox, h_sb, past):
    qa = _head_rows(qf_ref[...], h_fox)
    rows = h_fox * s
    ck = ck_ref[0, 0].astype(BF16)
    cv = cv_ref[0, 0].astype(BF16)
    suf = suf_ref[0]
    cum = cum_ref[0]
    bias_c = jnp.concatenate(
        [jnp.broadcast_to(suf[h:h + 1, :], (s, past)) for h in range(h_fox)], axis=0)
    bias_n = jnp.concatenate(
        [jnp.broadcast_to(-cum[h:h + 1, :], (s, s)) for h in range(h_fox)], axis=0)
    lc = _dot(qa, ck) + bias_c
    ln = _dot_nt(qa, kf_ref[...]) + bias_n
    r_pos = lax.broadcasted_iota(jnp.int32, (rows, s), 0) % s
    k_pos = lax.broadcasted_iota(jnp.int32, (rows, s), 1)
    ln = jnp.where(k_pos <= r_pos, ln, NEG_BIG)
    m = jnp.maximum(jnp.max(lc, axis=1, keepdims=True), jnp.max(ln, axis=1, keepdims=True))
    pc = jnp.exp(lc - m)
    pn = jnp.exp(ln - m)
    den = jnp.sum(pc, axis=1, keepdims=True) + jnp.sum(pn, axis=1, keepdims=True)
    o = (_dot_nt(pc.astype(BF16), cv) + _dot(pn.astype(BF16), vf_ref[...])) / den
    of_ref[...] = _fold_heads(o, h_fox, s).astype(BF16)

    qb = _head_rows(qb_ref[...], h_sb)
    rows_b = h_sb * s
    rb = lax.broadcasted_iota(jnp.int32, (rows_b, s), 0) % s
    cb = lax.broadcasted_iota(jnp.int32, (rows_b, s), 1)
    acc, carry = _sb_block(_dot_nt(qb, kb_ref[...]), jnp.zeros((rows_b, 1), F32), vb_ref[...],
                           _suffix_matrix(s), cb < rb)
    tri = _suffix_matrix(LANES)

    def cond(state):
        return (state[0] < past // LANES) & (jnp.min(state[2]) < PRUNE_LOG)

    def body(state):
        t, acc, carry = state
        k0 = pl.multiple_of(past - (t + 1) * LANES, LANES)
        k = cbk_ref[0, 0, :, pl.ds(k0, LANES)].astype(BF16)
        v = cbv_ref[0, 0, :, pl.ds(k0, LANES)].astype(BF16)
        pv, carry = _sb_block(_dot(qb, k), carry, v, tri, None, v_feature_major=True)
        return t + 1, acc + pv, carry

    _, acc, _ = lax.while_loop(cond, body, (0, acc, carry))
    ob_ref[...] = _fold_heads(acc, h_sb, s).astype(BF16)


def _sample_attn(qf, kf, vf, ck, cv, suf, cum, qb, kb, vb, cbk, cbv, *, layer, batch, s):
    w_fox, w_sb = qf.shape[1], qb.shape[1]
    past = ck.shape[3]
    h_fox, h_sb = w_fox // HEAD_DIM, w_sb // HEAD_DIM
    new = lambda w: pl.BlockSpec((s, w), lambda b: (b, 0))
    cache = lambda w: pl.BlockSpec((1, 1, w, past), lambda b: (layer, b, 0, 0))
    kern = functools.partial(_sample_attn_kernel, s=s, h_fox=h_fox, h_sb=h_sb, past=past)
    return pl.pallas_call(
        kern,
        grid=(batch,),
        in_specs=[new(w_fox), new(w_fox), new(w_fox), cache(w_fox), cache(w_fox),
                  pl.BlockSpec((1, 8, past), lambda b: (b, 0, 0)),
                  pl.BlockSpec((1, 8, s), lambda b: (b, 0, 0)),
                  new(w_sb), new(w_sb), new(w_sb), cache(w_sb), cache(w_sb)],
        out_specs=[new(w_fox), new(w_sb)],
        out_shape=[jax.ShapeDtypeStruct((batch * s, w_fox), BF16),
                   jax.ShapeDtypeStruct((batch * s, w_sb), BF16)],
        compiler_params=_cparams(("arbitrary",)),
        name="sample_attn",
    )(qf, kf, vf, ck, cv, suf, cum, qb, kb, vb, cbk, cbv)


def _merge_kernel(x_ref, sc_ref, sh_ref, gt_ref, gmix_ref, ysgu_ref, ofox_ref, osb_ref,
                  wg_ref, bg_ref, wbs_ref, wbf_ref, wbb_ref, wo_ref, o_ref):
    x = x_ref[...]
    d = x.shape[1]
    h = _modulated_norm(x, gmix_ref[0], sc_ref[0], sh_ref[0])
    gates = _dot(h.astype(BF16), wg_ref[0]) + bg_ref[0]
    gates = 1.0 / (1.0 + jnp.exp(-gates))
    merged = gates[:, 0:d] * _dot(ysgu_ref[...], wbs_ref[0]) \
        + gates[:, d:2 * d] * _dot(ofox_ref[...], wbf_ref[0]) \
        + gates[:, 2 * d:3 * d] * _dot(osb_ref[...], wbb_ref[0])
    o_ref[...] = x + gt_ref[0] * _dot(merged.astype(BF16), wo_ref[0])


def _merge(x, mod, layer, gmix, ysgu, ofox, osb, wg, bg, wbs, wbf, wbb, wo, *, tm):
    n, d = x.shape
    row = lambda width: pl.BlockSpec((tm, width), lambda i: (i, 0))
    lay = lambda a: _layer_spec(a, layer)
    return pl.pallas_call(
        _merge_kernel,
        grid=(n // tm,),
        in_specs=[row(d), _mod_spec(mod, layer, 1, tm), _mod_spec(mod, layer, 0, tm),
                  _mod_spec(mod, layer, 2, tm), lay(gmix),
                  row(ysgu.shape[1]), row(ofox.shape[1]), row(osb.shape[1]),
                  lay(wg), lay(bg), lay(wbs), lay(wbf), lay(wbb), lay(wo)],
        out_specs=row(d),
        out_shape=jax.ShapeDtypeStruct((n, d), F32),
        compiler_params=_cparams(("arbitrary",)),
        name="merge",
    )(x, mod, mod, mod, gmix, ysgu, ofox, osb, wg, bg, wbs, wbf, wbb, wo)


def _ffn_kernel(x_ref, sc_ref, sh_ref, gt_ref, g_ref, wi_ref, wo_ref, o_ref, *, d_ff):
    x = x_ref[...]
    h = _modulated_norm(x, g_ref[0], sc_ref[0], sh_ref[0])
    ag = _dot(h.astype(BF16), wi_ref[0])
    a = ag[:, 0:d_ff]
    act = a * (1.0 / (1.0 + jnp.exp(-a))) * ag[:, d_ff:2 * d_ff]
    o_ref[...] = x + gt_ref[0] * _dot(act.astype(BF16), wo_ref[0])


def _ffn(x, mod, layer, g, wi, wo, *, tm):
    n, d = x.shape
    d_ff = wo.shape[1]
    row = pl.BlockSpec((tm, d), lambda i: (i, 0))
    return pl.pallas_call(
        functools.partial(_ffn_kernel, d_ff=d_ff),
        grid=(n // tm,),
        in_specs=[row, _mod_spec(mod, layer, 4, tm), _mod_spec(mod, layer, 3, tm),
                  _mod_spec(mod, layer, 5, tm), _layer_spec(g, layer), _layer_spec(wi, layer),
                  _layer_spec(wo, layer)],
        out_specs=row,
        out_shape=jax.ShapeDtypeStruct((n, d), F32),
        compiler_params=_cparams(("arbitrary",)),
        name="ffn",
    )(x, mod, mod, mod, g, wi, wo)


def _indicator(width, group):
    idx = np.arange(width) // group
    return jnp.asarray(idx[:, None] == idx[None, :], dtype=BF16)


def kernel(x_prompt, x_sample, c_prompt, c_sample, cache_fox_k, cache_fox_v, cache_fox_logf,
           cache_sb_k, cache_sb_v, w_ada, b_ada, g_mix, g_ffn, w_in, g_sgu_v, w_sgu, b_sgu, b_fgt,
           g_q, g_k, w_br_sgu, w_br_fox, w_br_sb, w_gate, b_gate, w_out, w_ffn_in, w_ffn_out):
    batch, seq, d = x_prompt.shape
    dec_batch, dec_seq, _ = x_sample.shape
    depth = w_ada.shape[0]
    past = cache_fox_k.shape[2]
    h_fox, h_sb = cache_fox_k.shape[3], cache_sb_k.shape[3]
    w_fox, w_sb = h_fox * HEAD_DIM, h_sb * HEAD_DIM
    g_sgu, cg = g_sgu_v.shape[1], g_sgu_v.shape[2]
    w_sgu_ = g_sgu * cg
    assert batch == 1 and g_sgu == G_SGU and w_sgu.shape[2] == SGU_LEN
    n_dec = dec_batch * dec_seq

    n_c = batch + dec_batch
    c_rows = -(-n_c // 8) * 8
    c_all = jnp.zeros((c_rows, d), F32).at[:n_c].set(jnp.concatenate([c_prompt, c_sample], axis=0))
    mod = _modulation(c_all, w_ada, b_ada)

    offs = np.cumsum([0, w_sgu_, w_sgu_, w_fox, w_fox, w_fox, h_fox, w_sb, w_sb, w_sb]).tolist()
    f_cols = jnp.zeros((depth, d, LANES), F32).at[:, :, :h_fox].set(w_in[:, :, offs[5]:offs[6]])
    w_main = jnp.concatenate([w_in[:, :, :offs[5]], w_in[:, :, offs[6]:], f_cols],
                             axis=2).astype(BF16)
    bf_pad = jnp.zeros((depth, 1, LANES), F32).at[:, 0, :h_fox].set(b_fgt)
    gmix3, gffn3, bg3 = g_mix.reshape(depth, 1, d), g_ffn.reshape(depth, 1, d), b_gate.reshape(depth, 1, 3 * d)
    mod_p = mod[:, 0:batch]
    mod_s = jnp.repeat(mod[:, batch:batch + dec_batch], dec_seq, axis=1)
    ind96, ind64 = _indicator(w_sgu_, cg), _indicator(LANES, HEAD_DIM)
    gq_t = jnp.tile(g_q, (1, h_fox)).reshape(depth, 1, w_fox)
    gk_t = jnp.tile(g_k, (1, h_fox)).reshape(depth, 1, w_fox)
    gsgu = g_sgu_v.reshape(depth, 1, w_sgu_)
    msgu_p = jnp.transpose(w_sgu, (0, 2, 1, 3)).reshape(depth, SGU_LEN, g_sgu * SGU_LEN)
    reps = SGU_LEN // dec_seq
    w_small = jnp.tile(w_sgu[:, :, :dec_seq, :dec_seq], (1, 1, reps, reps))
    msgu_s = jnp.transpose(w_small, (0, 2, 1, 3)).reshape(depth, SGU_LEN, g_sgu * SGU_LEN)
    bsgu_p = jnp.repeat(jnp.transpose(b_sgu, (0, 2, 1)), cg, axis=2)
    bsgu_s = jnp.tile(bsgu_p[:, :dec_seq], (1, reps, 1))
    wg, wbs, wbf, wbb = (w.astype(BF16) for w in (w_gate, w_br_sgu, w_br_fox, w_br_sb))
    wo, wfi, wfo = (w.astype(BF16) for w in (w_out, w_ffn_in, w_ffn_out))

    to_fm = lambda c: jnp.transpose(c, (0, 1, 3, 4, 2)).reshape(
        depth, dec_batch, c.shape[3] * HEAD_DIM, past)
    ck_t, cv_t, cbk_t, cbv_t = (to_fm(c) for c in (cache_fox_k, cache_fox_v, cache_sb_k, cache_sb_v))

    xp = x_prompt.reshape(seq, d)
    xs = x_sample.reshape(n_dec, d)
    tm_p = min(512, seq)
    tm_f = min(512, seq)
    tq = tk = min(256, seq)
    st_p, st_s = [], []
    for l in range(depth):
        shared = (l, gmix3, w_main, gsgu, gq_t, gk_t, bf_pad, ind96, ind64)
        branch_w = (wg, bg3, wbs, wbf, wbb, wo)

        (ysgu, kaug, qft, vft, qbt, kb16, vbt, kf32, vf32, kb32, vb32, logft, fcumt) = _inproj(
            xp, mod_p, *shared, msgu_p, bsgu_p, tm=tm_p, period=SGU_LEN, sweep=True)
        bound = 1.01 * HEAD_DIM ** 0.5 * jnp.max(jnp.abs(g_q[l])) * jnp.max(jnp.abs(g_k[l]))
        cut = (-(2.0 * bound + PRUNE_LOG)).reshape(1)
        ofox = _fox_prompt(fcumt[:, 0::LANES], fcumt[:, LANES - 1::LANES], cut, qft, kaug, vft,
                           tq=tq, tk=tk)
        osb = _sb_prompt(qbt, kb16, vbt, tq=min(128, seq))
        x1 = _merge(xp, mod_p, l, gmix3, ysgu, ofox, osb, *branch_w, tm=tm_p)
        xp = _ffn(x1, mod_p, l, gffn3, wfi, wfo, tm=tm_f)
        st_p.append((kf32.reshape(batch, seq, h_fox, HEAD_DIM),
                     vf32.reshape(batch, seq, h_fox, HEAD_DIM),
                     logft[:h_fox].T.reshape(batch, seq, h_fox),
                     kb32.reshape(batch, seq, h_sb, HEAD_DIM),
                     vb32.reshape(batch, seq, h_sb, HEAD_DIM)))

        (ysgu, qf, kf16, vf16, qb, kb16, vb16, kf32, vf32, kb32, vb32, logft, sguv) = _inproj(
            xs, mod_s, *shared, msgu_s, bsgu_s, tm=n_dec, period=dec_seq, sweep=False)
        clf = jnp.zeros((dec_batch, 8, past), F32).at[:, :h_fox].set(
            jnp.transpose(cache_fox_logf[l], (0, 2, 1)))
        suf, cum = _sample_cumsum(clf.reshape(dec_batch * 8, past), logft, dec_seq)
        cum_b = jnp.transpose(cum.reshape(8, dec_batch, dec_seq), (1, 0, 2))
        ofox, osb = _sample_attn(
            qf, kf16, vf16, ck_t, cv_t, suf.reshape(dec_batch, 8, past), cum_b,
            qb, kb16, vb16, cbk_t, cbv_t, layer=l, batch=dec_batch, s=dec_seq)
        x1 = _merge(xs, mod_s, l, gmix3, ysgu, ofox, osb, *branch_w, tm=n_dec)
        xs = _ffn(x1, mod_s, l, gffn3, wfi, wfo, tm=n_dec)
        st_s.append((kf32.reshape(dec_batch, dec_seq, h_fox, HEAD_DIM),
                     vf32.reshape(dec_batch, dec_seq, h_fox, HEAD_DIM),
                     logft[:h_fox].T.reshape(dec_batch, dec_seq, h_fox),
                     kb32.reshape(dec_batch, dec_seq, h_sb, HEAD_DIM),
                     vb32.reshape(dec_batch, dec_seq, h_sb, HEAD_DIM),
                     sguv.reshape(dec_batch, dec_seq, w_sgu_)))

    def stack(states, idx):
        return jnp.stack([s[idx] for s in states], axis=0)

    return (xp.reshape(batch, seq, d), xs.reshape(dec_batch, dec_seq, d),
            stack(st_p, 0), stack(st_p, 1), stack(st_p, 2), stack(st_p, 3), stack(st_p, 4),
            stack(st_s, 0), stack(st_s, 1), stack(st_s, 2), stack(st_s, 3), stack(st_s, 4),
            stack(st_s, 5))
```

```python
import functools

import numpy as np
import jax
import jax.numpy as jnp
from jax import lax
from jax.experimental import pallas as pl
from jax.experimental.pallas import tpu as pltpu

F32 = jnp.float32
BF16 = jnp.bfloat16

EPS = 1e-6
HEAD_DIM = 64
LANES = 128
CHUNK = 64
SGU_LEN = 128
G_SGU = 4
NEG_BIG = -1e30
LOG2E = 1.4426950408889634

PRUNE_LOG = 30.0

VMEM_LIMIT = 56 * 1024 * 1024


def _cparams(sem):
    return pltpu.CompilerParams(dimension_semantics=sem, vmem_limit_bytes=VMEM_LIMIT)


def _const_spec(shape):
    nd = len(shape)
    return pl.BlockSpec(shape, lambda *_: (0,) * nd, pipeline_mode=pl.Buffered(1))


def _layer_spec(arr, layer):
    nd = arr.ndim
    return pl.BlockSpec((1,) + arr.shape[1:], lambda *_: (layer,) + (0,) * (nd - 1),
                        pipeline_mode=pl.Buffered(1))


def _mod_spec(mod, layer, k, tm):
    d = mod.shape[2] // 6
    if mod.shape[1] == 1:
        return pl.BlockSpec((1, 1, d), lambda i: (layer, 0, k))
    return pl.BlockSpec((1, tm, d), lambda i: (layer, i, k))


def _dot(a, b):
    return jnp.dot(a, b, preferred_element_type=F32)


def _dot_nt(a, b):
    return lax.dot_general(a, b, (((1,), (1,)), ((), ())), preferred_element_type=F32)


def _split3(x):
    h = x.astype(BF16)
    r = x - h.astype(F32)
    m = r.astype(BF16)
    l = (r - m.astype(F32)).astype(BF16)
    return h, m, l


def _split2(x):
    h = x.astype(BF16)
    l = (x - h.astype(F32)).astype(BF16)
    return h, l


def _mod_kernel(c_ref, w_ref, b_ref, o_ref):
    c = c_ref[...]
    s = c * (1.0 / (1.0 + jnp.exp(-c)))
    o_ref[0] = _dot(s.astype(BF16), w_ref[0].astype(BF16)) + b_ref[0]


def _modulation(c_all, w_ada, b_ada):
    depth, d, n6 = w_ada.shape
    rows = c_all.shape[0]
    tn = 1024
    return pl.pallas_call(
        _mod_kernel,
        grid=(depth, n6 // tn),
        in_specs=[
            pl.BlockSpec((rows, d), lambda l, j: (0, 0)),
            pl.BlockSpec((1, d, tn), lambda l, j: (l, 0, j)),
            pl.BlockSpec((1, 1, tn), lambda l, j: (l, 0, j)),
        ],
        out_specs=pl.BlockSpec((1, rows, tn), lambda l, j: (l, 0, j)),
        out_shape=jax.ShapeDtypeStruct((depth, rows, n6), F32),
        compiler_params=_cparams(("arbitrary", "arbitrary")),
        name="adaln_mod",
    )(c_all, w_ada, b_ada.reshape(depth, 1, n6))


def _modulated_norm(x, g, sc, sh):
    ms = jnp.mean(x * x, axis=-1, keepdims=True)
    return (x * lax.rsqrt(ms + EPS)) * g * (1.0 + sc) + sh


def _group_rms(t, ind, inv_size, g):
    sq = (t * t).astype(BF16)
    wb = ind.shape[0]
    ss = [_dot(sq[:, c:c + wb], ind) for c in range(0, t.shape[1], wb)]
    ss = jnp.concatenate(ss, axis=1) if len(ss) > 1 else ss[0]
    return t * lax.rsqrt(ss * inv_size + EPS) * g


def _log_sigmoid(x):
    return jnp.minimum(x, 0.0) - jnp.log(1.0 + jnp.exp(-jnp.abs(x)))


def _augmented_keys(kfn, f_cum, heads):
    lane = lax.broadcasted_iota(jnp.int32, f_cum.shape, 1)
    hi, mid, lo = (t.astype(F32) for t in _split3(f_cum * -LOG2E))
    aug_even = pltpu.roll(hi, 64, 1) + pltpu.roll(mid, 72, 1) + pltpu.roll(lo, 80, 1)
    aug_odd = hi + pltpu.roll(mid, 8, 1) + pltpu.roll(lo, 16, 1)
    blocks = []
    for h in range(heads):
        kp = kfn[:, (h // 2) * LANES:(h // 2 + 1) * LANES]
        if h % 2 == 0:
            blocks.append(jnp.where(lane < HEAD_DIM, kp, aug_even))
        else:
            blocks.append(jnp.where(lane >= HEAD_DIM, kp, aug_odd))
    return jnp.concatenate(blocks, axis=1).astype(BF16)


def _inproj_kernel(x_ref, sc_ref, sh_ref, gmix_ref, w_ref, gsgu_ref, gq_ref, gk_ref, bf_ref,
                   ind96_ref, ind64_ref, msgu_ref, bsgu_ref, *rest,
                   tm, w_sgu, w_fox, w_sb, period, sweep):
    if sweep:
        (tril_ref, ysgu_ref, kaug_ref, qft_ref, vft_ref, qbt_ref, kb16_ref, vbt_ref,
         kf32_ref, vf32_ref, kb32_ref, vb32_ref, logft_ref, fcumt_ref, carry_ref) = rest
    else:
        (ysgu_ref, qf_ref, kf16_ref, vf16_ref, qb_ref, kb16_ref, vb16_ref,
         kf32_ref, vf32_ref, kb32_ref, vb32_ref, logft_ref, sguv_ref) = rest
    x = x_ref[...]
    h = _modulated_norm(x, gmix_ref[0], sc_ref[0], sh_ref[0])
    p = _dot(h.astype(BF16), w_ref[0])

    o = 0
    u = p[:, o:o + w_sgu]; o += w_sgu
    vs = p[:, o:o + w_sgu]; o += w_sgu
    qf = p[:, o:o + w_fox]; o += w_fox
    kf = p[:, o:o + w_fox]; o += w_fox
    vf = p[:, o:o + w_fox]; o += w_fox
    qb = p[:, o:o + w_sb]; o += w_sb
    kb = p[:, o:o + w_sb]; o += w_sb
    vb = p[:, o:o + w_sb]; o += w_sb
    fl = p[:, o:o + LANES]

    scale = HEAD_DIM ** -0.5
    ind64 = ind64_ref[...]
    qfn = _group_rms(qf, ind64, 1.0 / HEAD_DIM, gq_ref[0])
    kfn = _group_rms(kf, ind64, 1.0 / HEAD_DIM, gk_ref[0])
    kf32_ref[...] = kfn
    vf32_ref[...] = vf
    kb32_ref[...] = kb
    vb32_ref[...] = vb
    kb16_ref[...] = kb.astype(BF16)
    lf = _log_sigmoid(fl + bf_ref[0])
    logft_ref[...] = lf.T[0:8, :]
    if sweep:
        qft_ref[...] = (qfn * (scale * LOG2E)).T.astype(BF16)
        vft_ref[...] = vf.T.astype(BF16)
        qbt_ref[...] = (qb * (scale * LOG2E)).T.astype(BF16)
        vbt_ref[...] = vb.T.astype(BF16)
        @pl.when(pl.program_id(0) == 0)
        def _():
            carry_ref[...] = jnp.zeros(carry_ref.shape, F32)
        lane = lax.broadcasted_iota(jnp.int32, lf.shape, 1)
        lfh, lfm, lfl = _split3(jnp.where(lane < 8, lf, 0.0))
        tril = tril_ref[...]
        run = carry_ref[0:1, :]
        chunks = []
        for c in range(0, tm, LANES):
            part = (_dot(tril, lfh[c:c + LANES]) + _dot(tril, lfm[c:c + LANES])
                    + _dot(tril, lfl[c:c + LANES]) + run)
            run = part[LANES - 1:LANES, :]
            chunks.append(part)
        f_cum = jnp.concatenate(chunks, axis=0) if len(chunks) > 1 else chunks[0]
        carry_ref[...] = jnp.broadcast_to(run, carry_ref.shape)
        fcumt_ref[...] = f_cum.T[0:8, :]
        kaug_ref[...] = _augmented_keys(kfn, f_cum, w_fox // HEAD_DIM)
    else:
        qf_ref[...] = (qfn * scale).astype(BF16)
        kf16_ref[...] = kfn.astype(BF16)
        vf16_ref[...] = vf.astype(BF16)
        qb_ref[...] = (qb * scale).astype(BF16)
        vb16_ref[...] = vb.astype(BF16)

    cg = w_sgu // G_SGU
    vsn = _group_rms(vs, ind96_ref[...], 1.0 / cg, gsgu_ref[0])
    if not sweep:
        sguv_ref[...] = vsn
    r = lax.broadcasted_iota(jnp.int32, (SGU_LEN, G_SGU * SGU_LEN), 0)
    c = lax.broadcasted_iota(jnp.int32, (SGU_LEN, G_SGU * SGU_LEN), 1) % SGU_LEN
    keep = (r // period == c // period) & ((c % period) // CHUNK <= (r % period) // CHUNK)
    mix = jnp.where(keep, msgu_ref[0], 0.0).astype(BF16)
    lane_group = lax.broadcasted_iota(jnp.int32, (SGU_LEN, w_sgu), 1) // cg
    vsb = vsn.astype(BF16)
    spat = []
    for ci in range(tm // SGU_LEN):
        vc = vsb[ci * SGU_LEN:(ci + 1) * SGU_LEN]
        stacked = jnp.concatenate(
            [jnp.where(lane_group == g, vc, jnp.zeros_like(vc)) for g in range(G_SGU)], axis=0)
        spat.append(_dot(mix, stacked) + bsgu_ref[0])
    spat = jnp.concatenate(spat, axis=0) if len(spat) > 1 else spat[0]
    ysgu_ref[...] = (u * spat).astype(BF16)


def _inproj(x, mod, layer, gmix, w, gsgu, gq, gk, bf, ind96, ind64, msgu, bsgu, *, tm, period,
            sweep):
    n, d = x.shape
    w_sgu, w_fox = gsgu.shape[2], gq.shape[2]
    w_sb = (w.shape[2] - LANES - 2 * w_sgu - 3 * w_fox) // 3
    row = lambda width: pl.BlockSpec((tm, width), lambda i: (i, 0))
    col = lambda height: pl.BlockSpec((height, tm), lambda i: (0, i))
    sds = jax.ShapeDtypeStruct
    states_specs = [row(w_fox), row(w_fox), row(w_sb), row(w_sb), col(8)]
    states_shape = [sds((n, w_fox), F32), sds((n, w_fox), F32), sds((n, w_sb), F32),
                    sds((n, w_sb), F32), sds((8, n), F32)]
    operands = [x, mod, mod, gmix, w, gsgu, gq, gk, bf, ind96, ind64, msgu, bsgu]
    lay = lambda a: _layer_spec(a, layer)
    in_specs = [row(d), _mod_spec(mod, layer, 1, tm), _mod_spec(mod, layer, 0, tm), lay(gmix), lay(w),
                lay(gsgu), lay(gq), lay(gk), lay(bf), _const_spec(ind96.shape),
                _const_spec(ind64.shape), lay(msgu), lay(bsgu)]
    scratch = []
    if sweep:
        heads = w_fox // HEAD_DIM
        a = np.arange(LANES)
        operands.append(jnp.asarray(a[None, :] <= a[:, None], dtype=BF16))
        in_specs.append(_const_spec((LANES, LANES)))
        out_specs = [row(w_sgu), row(heads * LANES), col(w_fox), col(w_fox), col(w_sb), row(w_sb),
                     col(w_sb)] + states_specs + [col(8)]
        out_shape = [sds((n, w_sgu), BF16), sds((n, heads * LANES), BF16), sds((w_fox, n), BF16),
                     sds((w_fox, n), BF16), sds((w_sb, n), BF16), sds((n, w_sb), BF16),
                     sds((w_sb, n), BF16)] + states_shape + [sds((8, n), F32)]
        scratch = [pltpu.VMEM((8, LANES), F32)]
    else:
        out_specs = [row(w_sgu), row(w_fox), row(w_fox), row(w_fox), row(w_sb), row(w_sb),
                     row(w_sb)] + states_specs + [row(w_sgu)]
        out_shape = [sds((n, w_sgu), BF16), sds((n, w_fox), BF16), sds((n, w_fox), BF16),
                     sds((n, w_fox), BF16), sds((n, w_sb), BF16), sds((n, w_sb), BF16),
                     sds((n, w_sb), BF16)] + states_shape + [sds((n, w_sgu), F32)]
    kern = functools.partial(_inproj_kernel, tm=tm, w_sgu=w_sgu, w_fox=w_fox, w_sb=w_sb,
                             period=period, sweep=sweep)
    return pl.pallas_call(
        kern,
        grid=(n // tm,),
        in_specs=in_specs,
        out_specs=out_specs,
        out_shape=out_shape,
        scratch_shapes=scratch,
        compiler_params=_cparams(("arbitrary",)),
        name="inproj",
    )(*operands)


def _seq_cumsum(x, nc, reverse_exclusive):
    rows = x.shape[0]
    a = lax.broadcasted_iota(jnp.int32, (LANES, LANES), 0)
    b = lax.broadcasted_iota(jnp.int32, (LANES, LANES), 1)
    tri = (a > b) if reverse_exclusive else (a <= b)
    tri = jnp.where(tri, 1.0, 0.0).astype(BF16)
    ones = jnp.ones((LANES, LANES), BF16)
    xh, xm, xl = _split3(x)
    within = _dot(xh, tri) + _dot(xm, tri) + _dot(xl, tri)
    tot = _dot(xh, ones) + _dot(xm, ones) + _dot(xl, ones)
    ra = lax.broadcasted_iota(jnp.int32, (rows, rows), 0)
    rb = lax.broadcasted_iota(jnp.int32, (rows, rows), 1)
    other = (rb > ra) if reverse_exclusive else (rb < ra)
    blk = jnp.where((ra // nc == rb // nc) & other, 1.0, 0.0).astype(BF16)
    th, tm_, tl = _split3(tot)
    return within + _dot(blk, th) + _dot(blk, tm_) + _dot(blk, tl)


def _sample_cumsum_kernel(clf_ref, lf_ref, suf_ref, cum_ref, *, nc, dec_seq):
    suf_ref[...] = _seq_cumsum(clf_ref[...], nc, reverse_exclusive=True)
    a = lax.broadcasted_iota(jnp.int32, (LANES, LANES), 0)
    b = lax.broadcasted_iota(jnp.int32, (LANES, LANES), 1)
    tri = jnp.where((a // dec_seq == b // dec_seq) & (a <= b), 1.0, 0.0).astype(BF16)
    xh, xm, xl = _split3(lf_ref[...])
    cum_ref[...] = _dot(xh, tri) + _dot(xm, tri) + _dot(xl, tri)


def _sample_cumsum(clogf_t, logft, dec_seq):
    rows, past = clogf_t.shape
    nc = past // LANES
    suf, cum = pl.pallas_call(
        functools.partial(_sample_cumsum_kernel, nc=nc, dec_seq=dec_seq),
        out_shape=[jax.ShapeDtypeStruct((rows * nc, LANES), F32),
                   jax.ShapeDtypeStruct(logft.shape, F32)],
        compiler_params=pltpu.CompilerParams(vmem_limit_bytes=VMEM_LIMIT),
        name="sample_logf_cumsum",
    )(clogf_t.reshape(rows * nc, LANES), logft)
    return suf.reshape(rows, past), cum


def _head_query_blocks(qt_ref, heads, tq, ones_rows, lane0=0):
    rowi = lax.broadcasted_iota(jnp.int32, (LANES, tq), 0)
    out = []
    for h in range(heads):
        qp = qt_ref[(h // 2) * LANES:(h // 2 + 1) * LANES, lane0:lane0 + tq]
        own = (rowi < HEAD_DIM) if h % 2 == 0 else (rowi >= HEAD_DIM)
        fill = jnp.zeros((LANES, tq), F32)
        for r in ones_rows(h):
            fill = jnp.where(rowi == r, 1.0, fill)
        out.append(jnp.where(own, qp, fill.astype(BF16)))
    return out


def _bias_rows(h):
    base = HEAD_DIM if h % 2 == 0 else 0
    return [base + h, base + 8 + h, base + 16 + h]


def _fox_kernel(fs_ref, fe_ref, cut_ref, qt_ref, ka_ref, vt_ref, o_ref, *, tq, heads):
    i = pl.program_id(0)
    tk = tq
    cpb = tk // LANES
    qa = _head_query_blocks(qt_ref, heads, tq, _bias_rows)
    rowi = lax.broadcasted_iota(jnp.int32, (LANES, tq), 0)
    krow = lax.broadcasted_iota(jnp.int32, (tk, tq), 0)
    qcol = lax.broadcasted_iota(jnp.int32, (tk, tq), 1)

    def logits_of(j):
        k0 = pl.multiple_of(j * tk, tk)
        return tuple(_dot(ka_ref[pl.ds(k0, tk), h * LANES:(h + 1) * LANES], qa[h])
                     for h in range(heads))

    def absorb(j, logits, carry, mask):
        k0 = pl.multiple_of(j * tk, tk)
        out = []
        for h in range(heads):
            vt = vt_ref[(h // 2) * LANES:(h // 2 + 1) * LANES, pl.ds(k0, tk)]
            m, l, acc = carry[3 * h:3 * h + 3]
            s = logits[h]
            if mask is not None:
                s = jnp.where(mask, s, NEG_BIG)
            mn = jnp.maximum(m, jnp.max(s, axis=0, keepdims=True))
            alpha = jnp.exp2(m - mn)
            pe = jnp.exp2(s - mn)
            l = alpha * l + jnp.sum(pe, axis=0, keepdims=True)
            acc = alpha * acc + _dot(vt, pe.astype(BF16))
            out += [mn, l, acc]
        return tuple(out)

    def run(blocks, carry):
        staged = [logits_of(j) for j, _ in blocks]
        for (j, mask), logits in zip(blocks, staged):
            carry = absorb(j, logits, carry, mask)
        return carry

    def live(j):
        jc = j * cpb + (cpb - 1)
        ok = fs_ref[0, i * cpb] - fe_ref[0, jc] >= cut_ref[0]
        for h in range(1, heads):
            ok = ok | (fs_ref[h, i * cpb] - fe_ref[h, jc] >= cut_ref[0])
        return ok

    init = []
    for _ in range(heads):
        init += [jnp.full((1, tq), NEG_BIG, F32), jnp.zeros((1, tq), F32),
                 jnp.zeros((LANES, tq), F32)]
    carry = run([(i, krow <= qcol), (jnp.maximum(i - 1, 0), jnp.broadcast_to(i > 0, (tk, tq)))],
                tuple(init))

    def cond(state):
        return (state[0] >= 1) & live(jnp.maximum(state[0], 0))

    def body(state):
        j = state[0]
        return (j - 2,) + run([(j, None), (j - 1, None)], state[1:])

    state = lax.while_loop(cond, body, (i - 2,) + carry)
    last = (state[0] == 0) & live(0)
    res = lax.cond(last, lambda c: run([(0, None)], c), lambda c: c, state[1:])
    for p in range(heads // 2):
        _, l0, a0, _, l1, a1 = res[6 * p:6 * p + 6]
        ot = jnp.where(rowi < HEAD_DIM, a0 / l0, a1 / l1)
        o_ref[:, p * LANES:(p + 1) * LANES] = ot.T.astype(BF16)


def _fox_prompt(fs, fe, cut, qt, kaug, vt, *, tq):
    w, n = qt.shape
    heads = w // HEAD_DIM
    grid_spec = pltpu.PrefetchScalarGridSpec(
        num_scalar_prefetch=3,
        grid=(n // tq,),
        in_specs=[
            pl.BlockSpec((w, tq), lambda i, *_: (0, i)),
            pl.BlockSpec(kaug.shape, lambda i, *_: (0, 0), pipeline_mode=pl.Buffered(1)),
            pl.BlockSpec((w, n), lambda i, *_: (0, 0), pipeline_mode=pl.Buffered(1)),
        ],
        out_specs=pl.BlockSpec((tq, w), lambda i, *_: (i, 0)),
    )
    return pl.pallas_call(
        functools.partial(_fox_kernel, tq=tq, heads=heads),
        grid_spec=grid_spec,
        out_shape=jax.ShapeDtypeStruct((n, w), BF16),
        compiler_params=_cparams(("arbitrary",)),
        name="fox_prompt",
    )(fs, fe, cut, qt, kaug, vt)


def _sb_kernel(qt_ref, k_ref, vt_ref, o_ref, *, tq, tb, heads):
    i = pl.program_id(0)
    nsub = tq // tb
    qm = [_head_query_blocks(qt_ref, heads, tb, lambda h: [], lane0=s * tb) for s in range(nsub)]
    rowi = lax.broadcasted_iota(jnp.int32, (LANES, tb), 0)
    krow = lax.broadcasted_iota(jnp.int32, (tb, tb), 0)
    qcol = lax.broadcasted_iota(jnp.int32, (tb, tb), 1)
    a = lax.broadcasted_iota(jnp.int32, (tb, 2 * tb), 0)
    b = lax.broadcasted_iota(jnp.int32, (tb, 2 * tb), 1) % tb
    tri2 = jnp.where(b > a, 1.0, 0.0).astype(BF16)

    def sweep(tasks, carry):
        k0s = [pl.multiple_of(t[1] * tb, tb) for t in tasks]
        z = [[_dot(k_ref[pl.ds(k0, tb), (h // 2) * LANES:(h // 2 + 1) * LANES], qm[t[0]][h])
              for h in range(heads)] for t, k0 in zip(tasks, k0s)]
        logsig, later, mass = {}, {}, {}
        for ti, (_, _, mask, valid) in enumerate(tasks):
            for h in range(heads):
                zz = z[ti][h]
                sp = jnp.maximum(zz, 0.0) + jnp.log2(1.0 + jnp.exp2(-jnp.abs(zz)))
                logsig[ti, h] = zz - sp
                if mask is not None:
                    sp = jnp.where(mask, sp, 0.0)
                hi, lo = _split2(sp)
                later[ti, h] = _dot(tri2, jnp.concatenate([hi, lo], axis=0))
                mass[ti, h] = jnp.sum(sp, axis=0, keepdims=True)
                if valid is not None:
                    mass[ti, h] = mass[ti, h] * valid
        carry = dict(carry)
        for ti, (s, _, mask, valid) in enumerate(tasks):
            for h in range(heads):
                cr, acc = carry[s, h]
                w = jnp.exp2(logsig[ti, h] - later[ti, h] - cr)
                if mask is not None:
                    w = jnp.where(mask, w, 0.0)
                if valid is not None:
                    w = w * valid
                vt = vt_ref[(h // 2) * LANES:(h // 2 + 1) * LANES, pl.ds(k0s[ti], tb)]
                carry[s, h] = (cr + mass[ti, h], acc + _dot(vt, w.astype(BF16)))
        return carry

    tasks = []
    for s in range(nsub):
        qb = i * nsub + s
        tasks.append((s, qb, krow < qcol, None))
        tasks.append((s, jnp.maximum(qb - 1, 0), None, jnp.where(qb > 0, 1.0, 0.0) if s == 0 else None))
    zero = (jnp.zeros((1, tb), F32), jnp.zeros((LANES, tb), F32))
    carry = sweep(tasks, {(s, h): zero for s in range(nsub) for h in range(heads)})

    for s in range(nsub):
        def cond(state):
            live = jnp.min(state[1])
            for h in range(1, heads):
                live = jnp.minimum(live, jnp.min(state[1 + 2 * h]))
            return (state[0] >= 0) & (live < PRUNE_LOG * LOG2E)

        def body(state, s=s):
            sub = {(s, h): (state[1 + 2 * h], state[2 + 2 * h]) for h in range(heads)}
            sub = sweep([(s, state[0], None, None)], sub)
            return (state[0] - 1,) + tuple(v for h in range(heads) for v in sub[s, h])

        init = (i * nsub + s - 2,) + tuple(v for h in range(heads) for v in carry[s, h])
        res = lax.while_loop(cond, body, init)[1:]
        for p in range(heads // 2):
            ot = jnp.where(rowi < HEAD_DIM, res[4 * p + 1], res[4 * p + 3])
            o_ref[s * tb:(s + 1) * tb, p * LANES:(p + 1) * LANES] = ot.T.astype(BF16)


def _sb_prompt(qt, k, vt, *, tq, tb):
    w, n = qt.shape
    return pl.pallas_call(
        functools.partial(_sb_kernel, tq=tq, tb=tb, heads=w // HEAD_DIM),
        grid=(n // tq,),
        in_specs=[
            pl.BlockSpec((w, tq), lambda i: (0, i)),
            pl.BlockSpec((n, w), lambda i: (0, 0), pipeline_mode=pl.Buffered(1)),
            pl.BlockSpec((w, n), lambda i: (0, 0), pipeline_mode=pl.Buffered(1)),
        ],
        out_specs=pl.BlockSpec((tq, w), lambda i: (i, 0)),
        out_shape=jax.ShapeDtypeStruct((n, w), BF16),
        compiler_params=_cparams(("arbitrary",)),
        name="sb_prompt",
    )(qt, k, vt)


def _sb_block(z, carry, v, tri, mask, v_feature_major=False):
    lg = jnp.log(1.0 + jnp.exp(-jnp.abs(z)))
    sp = jnp.maximum(z, 0.0) + lg
    if mask is not None:
        sp = jnp.where(mask, sp, 0.0)
    hi, lo = _split2(sp)
    later = _dot(hi, tri) + _dot(lo, tri)
    a = jnp.exp((jnp.minimum(z, 0.0) - lg) - later - carry)
    if mask is not None:
        a = jnp.where(mask, a, 0.0)
    pv = _dot_nt(a.astype(BF16), v) if v_feature_major else _dot(a.astype(BF16), v)
    return pv, carry + jnp.sum(sp, axis=1, keepdims=True)


def _suffix_matrix(tk):
    a = lax.broadcasted_iota(jnp.int32, (tk, tk), 0)
    b = lax.broadcasted_iota(jnp.int32, (tk, tk), 1)
    return jnp.where(a > b, 1.0, 0.0).astype(BF16)


def _head_rows(x, heads):
    lane_head = lax.broadcasted_iota(jnp.int32, x.shape, 1) // HEAD_DIM
    return jnp.concatenate(
        [jnp.where(lane_head == h, x, jnp.zeros_like(x)) for h in range(heads)], axis=0)


def _fold_heads(o, heads, s):
    lane_head = lax.broadcasted_iota(jnp.int32, (s, o.shape[1]), 1) // HEAD_DIM
    out = jnp.zeros((s, o.shape[1]), F32)
    for h in range(heads):
        out = jnp.where(lane_head == h, o[h * s:(h + 1) * s], out)
    return out


def _sample_attn_kernel(qf_ref, kf_ref, vf_ref, ck_ref, cv_ref, suf_ref, cum_ref,
                        qb_ref, kb_ref, vb_ref, cbk_ref, cbv_ref, of_ref, ob_ref,
                        *, s, h_fox, h_sb, past):
    qa = _head_rows(qf_ref[...], h_fox)
    rows = h_fox * s
    ck = ck_ref[0, 0].astype(BF16)
    cv = cv_ref[0, 0].astype(BF16)
    suf = suf_ref[0]
    cum = cum_ref[0]
    bias_c = jnp.concatenate(
        [jnp.broadcast_to(suf[h:h + 1, :], (s, past)) for h in range(h_fox)], axis=0)
    bias_n = jnp.concatenate(
        [jnp.broadcast_to(-cum[h:h + 1, :], (s, s)) for h in range(h_fox)], axis=0)
    lc = _dot(qa, ck) + bias_c
    ln = _dot_nt(qa, kf_ref[...]) + bias_n
    r_pos = lax.broadcasted_iota(jnp.int32, (rows, s), 0) % s
    k_pos = lax.broadcasted_iota(jnp.int32, (rows, s), 1)
    ln = jnp.where(k_pos <= r_pos, ln, NEG_BIG)
    m = jnp.maximum(jnp.max(lc, axis=1, keepdims=True), jnp.max(ln, axis=1, keepdims=True))
    pc = jnp.exp(lc - m)
    pn = jnp.exp(ln - m)
    den = jnp.sum(pc, axis=1, keepdims=True) + jnp.sum(pn, axis=1, keepdims=True)
    o = (_dot_nt(pc.astype(BF16), cv) + _dot(pn.astype(BF16), vf_ref[...])) / den
    of_ref[...] = _fold_heads(o, h_fox, s).astype(BF16)

    qb = _head_rows(qb_ref[...], h_sb)
    rows_b = h_sb * s
    rb = lax.broadcasted_iota(jnp.int32, (rows_b, s), 0) % s
    cb = lax.broadcasted_iota(jnp.int32, (rows_b, s), 1)
    acc, carry = _sb_block(_dot_nt(qb, kb_ref[...]), jnp.zeros((rows_b, 1), F32), vb_ref[...],
                           _suffix_matrix(s), cb < rb)
    tri = _suffix_matrix(LANES)

    def cond(state):
        return (state[0] < past // LANES) & (jnp.min(state[2]) < PRUNE_LOG)

    def body(state):
        t, acc, carry = state
        k0 = pl.multiple_of(past - (t + 1) * LANES, LANES)
        k = cbk_ref[0, 0, :, pl.ds(k0, LANES)].astype(BF16)
        v = cbv_ref[0, 0, :, pl.ds(k0, LANES)].astype(BF16)
        pv, carry = _sb_block(_dot(qb, k), carry, v, tri, None, v_feature_major=True)
        return t + 1, acc + pv, carry

    _, acc, _ = lax.while_loop(cond, body, (0, acc, carry))
    ob_ref[...] = _fold_heads(acc, h_sb, s).astype(BF16)


def _sample_attn(qf, kf, vf, ck, cv, suf, cum, qb, kb, vb, cbk, cbv, *, layer, batch, s):
    w_fox, w_sb = qf.shape[1], qb.shape[1]
    past = ck.shape[3]
    h_fox, h_sb = w_fox // HEAD_DIM, w_sb // HEAD_DIM
    new = lambda w: pl.BlockSpec((s, w), lambda b: (b, 0))
    cache = lambda w: pl.BlockSpec((1, 1, w, past), lambda b: (layer, b, 0, 0))
    kern = functools.partial(_sample_attn_kernel, s=s, h_fox=h_fox, h_sb=h_sb, past=past)
    return pl.pallas_call(
        kern,
        grid=(batch,),
        in_specs=[new(w_fox), new(w_fox), new(w_fox), cache(w_fox), cache(w_fox),
                  pl.BlockSpec((1, 8, past), lambda b: (b, 0, 0)),
                  pl.BlockSpec((1, 8, s), lambda b: (b, 0, 0)),
                  new(w_sb), new(w_sb), new(w_sb), cache(w_sb), cache(w_sb)],
        out_specs=[new(w_fox), new(w_sb)],
        out_shape=[jax.ShapeDtypeStruct((batch * s, w_fox), BF16),
                   jax.ShapeDtypeStruct((batch * s, w_sb), BF16)],
        compiler_params=_cparams(("arbitrary",)),
        name="sample_attn",
    )(qf, kf, vf, ck, cv, suf, cum, qb, kb, vb, cbk, cbv)


def _merge_kernel(x_ref, sc_ref, sh_ref, gt_ref, gmix_ref, ysgu_ref, ofox_ref, osb_ref,
                  wg_ref, bg_ref, wbs_ref, wbf_ref, wbb_ref, wo_ref, o_ref):
    x = x_ref[...]
    d = x.shape[1]
    h = _modulated_norm(x, gmix_ref[0], sc_ref[0], sh_ref[0])
    gates = _dot(h.astype(BF16), wg_ref[0]) + bg_ref[0]
    gates = 1.0 / (1.0 + jnp.exp(-gates))
    merged = gates[:, 0:d] * _dot(ysgu_ref[...], wbs_ref[0]) \
        + gates[:, d:2 * d] * _dot(ofox_ref[...], wbf_ref[0]) \
        + gates[:, 2 * d:3 * d] * _dot(osb_ref[...], wbb_ref[0])
    o_ref[...] = x + gt_ref[0] * _dot(merged.astype(BF16), wo_ref[0])


def _merge(x, mod, layer, gmix, ysgu, ofox, osb, wg, bg, wbs, wbf, wbb, wo, *, tm):
    n, d = x.shape
    row = lambda width: pl.BlockSpec((tm, width), lambda i: (i, 0))
    lay = lambda a: _layer_spec(a, layer)
    return pl.pallas_call(
        _merge_kernel,
        grid=(n // tm,),
        in_specs=[row(d), _mod_spec(mod, layer, 1, tm), _mod_spec(mod, layer, 0, tm),
                  _mod_spec(mod, layer, 2, tm), lay(gmix),
                  row(ysgu.shape[1]), row(ofox.shape[1]), row(osb.shape[1]),
                  lay(wg), lay(bg), lay(wbs), lay(wbf), lay(wbb), lay(wo)],
        out_specs=row(d),
        out_shape=jax.ShapeDtypeStruct((n, d), F32),
        compiler_params=_cparams(("arbitrary",)),
        name="merge",
    )(x, mod, mod, mod, gmix, ysgu, ofox, osb, wg, bg, wbs, wbf, wbb, wo)


def _ffn_kernel(x_ref, sc_ref, sh_ref, gt_ref, g_ref, wi_ref, wo_ref, o_ref, *, d_ff):
    x = x_ref[...]
    h = _modulated_norm(x, g_ref[0], sc_ref[0], sh_ref[0])
    ag = _dot(h.astype(BF16), wi_ref[0])
    a = ag[:, 0:d_ff]
    act = a * (1.0 / (1.0 + jnp.exp(-a))) * ag[:, d_ff:2 * d_ff]
    o_ref[...] = x + gt_ref[0] * _dot(act.astype(BF16), wo_ref[0])


def _ffn(x, mod, layer, g, wi, wo, *, tm):
    n, d = x.shape
    d_ff = wo.shape[1]
    row = pl.BlockSpec((tm, d), lambda i: (i, 0))
    return pl.pallas_call(
        functools.partial(_ffn_kernel, d_ff=d_ff),
        grid=(n // tm,),
        in_specs=[row, _mod_spec(mod, layer, 4, tm), _mod_spec(mod, layer, 3, tm),
                  _mod_spec(mod, layer, 5, tm), _layer_spec(g, layer), _layer_spec(wi, layer),
                  _layer_spec(wo, layer)],
        out_specs=row,
        out_shape=jax.ShapeDtypeStruct((n, d), F32),
        compiler_params=_cparams(("arbitrary",)),
        name="ffn",
    )(x, mod, mod, mod, g, wi, wo)


def _indicator(width, group):
    idx = np.arange(width) // group
    return jnp.asarray(idx[:, None] == idx[None, :], dtype=BF16)


def kernel(x_prompt, x_sample, c_prompt, c_sample, cache_fox_k, cache_fox_v, cache_fox_logf,
           cache_sb_k, cache_sb_v, w_ada, b_ada, g_mix, g_ffn, w_in, g_sgu_v, w_sgu, b_sgu, b_fgt,
           g_q, g_k, w_br_sgu, w_br_fox, w_br_sb, w_gate, b_gate, w_out, w_ffn_in, w_ffn_out):
    batch, seq, d = x_prompt.shape
    dec_batch, dec_seq, _ = x_sample.shape
    depth = w_ada.shape[0]
    past = cache_fox_k.shape[2]
    h_fox, h_sb = cache_fox_k.shape[3], cache_sb_k.shape[3]
    w_fox, w_sb = h_fox * HEAD_DIM, h_sb * HEAD_DIM
    g_sgu, cg = g_sgu_v.shape[1], g_sgu_v.shape[2]
    w_sgu_ = g_sgu * cg
    assert batch == 1 and g_sgu == G_SGU and w_sgu.shape[2] == SGU_LEN
    n_dec = dec_batch * dec_seq

    n_c = batch + dec_batch
    c_rows = -(-n_c // 8) * 8
    c_all = jnp.zeros((c_rows, d), F32).at[:n_c].set(jnp.concatenate([c_prompt, c_sample], axis=0))
    mod = _modulation(c_all, w_ada, b_ada)

    offs = np.cumsum([0, w_sgu_, w_sgu_, w_fox, w_fox, w_fox, h_fox, w_sb, w_sb, w_sb]).tolist()
    f_cols = jnp.zeros((depth, d, LANES), F32).at[:, :, :h_fox].set(w_in[:, :, offs[5]:offs[6]])
    w_main = jnp.concatenate([w_in[:, :, :offs[5]], w_in[:, :, offs[6]:], f_cols],
                             axis=2).astype(BF16)
    bf_pad = jnp.zeros((depth, 1, LANES), F32).at[:, 0, :h_fox].set(b_fgt)
    gmix3, gffn3, bg3 = g_mix.reshape(depth, 1, d), g_ffn.reshape(depth, 1, d), b_gate.reshape(depth, 1, 3 * d)
    mod_p = mod[:, 0:batch]
    mod_s = jnp.repeat(mod[:, batch:batch + dec_batch], dec_seq, axis=1)
    ind96, ind64 = _indicator(w_sgu_, cg), _indicator(LANES, HEAD_DIM)
    gq_t = jnp.tile(g_q, (1, h_fox)).reshape(depth, 1, w_fox)
    gk_t = jnp.tile(g_k, (1, h_fox)).reshape(depth, 1, w_fox)
    gsgu = g_sgu_v.reshape(depth, 1, w_sgu_)
    msgu_p = jnp.transpose(w_sgu, (0, 2, 1, 3)).reshape(depth, SGU_LEN, g_sgu * SGU_LEN)
    reps = SGU_LEN // dec_seq
    w_small = jnp.tile(w_sgu[:, :, :dec_seq, :dec_seq], (1, 1, reps, reps))
    msgu_s = jnp.transpose(w_small, (0, 2, 1, 3)).reshape(depth, SGU_LEN, g_sgu * SGU_LEN)
    bsgu_p = jnp.repeat(jnp.transpose(b_sgu, (0, 2, 1)), cg, axis=2)
    bsgu_s = jnp.tile(bsgu_p[:, :dec_seq], (1, reps, 1))
    wg, wbs, wbf, wbb = (w.astype(BF16) for w in (w_gate, w_br_sgu, w_br_fox, w_br_sb))
    wo, wfi, wfo = (w.astype(BF16) for w in (w_out, w_ffn_in, w_ffn_out))

    to_fm = lambda c: jnp.transpose(c, (0, 1, 3, 4, 2)).reshape(
        depth, dec_batch, c.shape[3] * HEAD_DIM, past)
    ck_t, cv_t, cbk_t, cbv_t = (to_fm(c) for c in (cache_fox_k, cache_fox_v, cache_sb_k, cache_sb_v))

    xp = x_prompt.reshape(seq, d)
    xs = x_sample.reshape(n_dec, d)
    tm_p = min(512, seq)
    tm_f = min(512, seq)
    tq = min(256, seq)
    st_p, st_s = [], []
    for l in range(depth):
        shared = (l, gmix3, w_main, gsgu, gq_t, gk_t, bf_pad, ind96, ind64)
        branch_w = (wg, bg3, wbs, wbf, wbb, wo)

        (ysgu, kaug, qft, vft, qbt, kb16, vbt, kf32, vf32, kb32, vb32, logft, fcumt) = _inproj(
            xp, mod_p, *shared, msgu_p, bsgu_p, tm=tm_p, period=SGU_LEN, sweep=True)
        bound = 1.01 * HEAD_DIM ** 0.5 * jnp.max(jnp.abs(g_q[l])) * jnp.max(jnp.abs(g_k[l]))
        cut = (-(2.0 * bound + PRUNE_LOG)).reshape(1)
        ofox = _fox_prompt(fcumt[:, 0::LANES], fcumt[:, LANES - 1::LANES], cut, qft, kaug, vft, tq=tq)
        osb = _sb_prompt(qbt, kb16, vbt, tq=min(512, seq), tb=min(128, seq))
        x1 = _merge(xp, mod_p, l, gmix3, ysgu, ofox, osb, *branch_w, tm=tm_p)
        xp = _ffn(x1, mod_p, l, gffn3, wfi, wfo, tm=tm_f)
        st_p.append((kf32.reshape(batch, seq, h_fox, HEAD_DIM),
                     vf32.reshape(batch, seq, h_fox, HEAD_DIM),
                     logft[:h_fox].T.reshape(batch, seq, h_fox),
                     kb32.reshape(batch, seq, h_sb, HEAD_DIM),
                     vb32.reshape(batch, seq, h_sb, HEAD_DIM)))

        (ysgu, qf, kf16, vf16, qb, kb16, vb16, kf32, vf32, kb32, vb32, logft, sguv) = _inproj(
            xs, mod_s, *shared, msgu_s, bsgu_s, tm=n_dec, period=dec_seq, sweep=False)
        clf = jnp.zeros((dec_batch, 8, past), F32).at[:, :h_fox].set(
            jnp.transpose(cache_fox_logf[l], (0, 2, 1)))
        suf, cum = _sample_cumsum(clf.reshape(dec_batch * 8, past), logft, dec_seq)
        cum_b = jnp.transpose(cum.reshape(8, dec_batch, dec_seq), (1, 0, 2))
        ofox, osb = _sample_attn(
            qf, kf16, vf16, ck_t, cv_t, suf.reshape(dec_batch, 8, past), cum_b,
            qb, kb16, vb16, cbk_t, cbv_t, layer=l, batch=dec_batch, s=dec_seq)
        x1 = _merge(xs, mod_s, l, gmix3, ysgu, ofox, osb, *branch_w, tm=n_dec)
        xs = _ffn(x1, mod_s, l, gffn3, wfi, wfo, tm=n_dec)
        st_s.append((kf32.reshape(dec_batch, dec_seq, h_fox, HEAD_DIM),
                     vf32.reshape(dec_batch, dec_seq, h_fox, HEAD_DIM),
                     logft[:h_fox].T.reshape(dec_batch, dec_seq, h_fox),
                     kb32.reshape(dec_batch, dec_seq, h_sb, HEAD_DIM),
                     vb32.reshape(dec_batch, dec_seq, h_sb, HEAD_DIM),
                     sguv.reshape(dec_batch, dec_seq, w_sgu_)))

    def stack(states, idx):
        return jnp.stack([s[idx] for s in states], axis=0)

    return (xp.reshape(batch, seq, d), xs.reshape(dec_batch, dec_seq, d),
            stack(st_p, 0), stack(st_p, 1), stack(st_p, 2), stack(st_p, 3), stack(st_p, 4),
            stack(st_s, 0), stack(st_s, 1), stack(st_s, 2), stack(st_s, 3), stack(st_s, 4),
            stack(st_s, 5))
```

```python
import functools

import numpy as np
import jax
import jax.numpy as jnp
from jax import lax
from jax.experimental import pallas as pl
from jax.experimental.pallas import tpu as pltpu

F32 = jnp.float32
BF16 = jnp.bfloat16

EPS = 1e-6
HEAD_DIM = 64
LANES = 128
CHUNK = 64
SGU_LEN = 128
G_SGU = 4
NEG_BIG = -1e30
LOG2E = 1.4426950408889634

PRUNE_LOG = 30.0

VMEM_LIMIT = 56 * 1024 * 1024


def _cparams(sem):
    return pltpu.CompilerParams(dimension_semantics=sem, vmem_limit_bytes=VMEM_LIMIT)


def _const_spec(shape):
    nd = len(shape)
    return pl.BlockSpec(shape, lambda *_: (0,) * nd, pipeline_mode=pl.Buffered(1))


def _layer_spec(arr, layer):
    nd = arr.ndim
    return pl.BlockSpec((1,) + arr.shape[1:], lambda *_: (layer,) + (0,) * (nd - 1),
                        pipeline_mode=pl.Buffered(1))


def _mod_spec(mod, layer, k, tm):
    d = mod.shape[2] // 6
    if mod.shape[1] == 1:
        return pl.BlockSpec((1, 1, d), lambda i: (layer, 0, k))
    return pl.BlockSpec((1, tm, d), lambda i: (layer, i, k))


def _dot(a, b):
    return jnp.dot(a, b, preferred_element_type=F32)


def _dot_nt(a, b):
    return lax.dot_general(a, b, (((1,), (1,)), ((), ())), preferred_element_type=F32)


def _split3(x):
    h = x.astype(BF16)
    r = x - h.astype(F32)
    m = r.astype(BF16)
    l = (r - m.astype(F32)).astype(BF16)
    return h, m, l


def _split2(x):
    h = x.astype(BF16)
    l = (x - h.astype(F32)).astype(BF16)
    return h, l


def _mod_kernel(c_ref, w_ref, b_ref, o_ref):
    c = c_ref[...]
    s = c * (1.0 / (1.0 + jnp.exp(-c)))
    o_ref[0] = _dot(s.astype(BF16), w_ref[0].astype(BF16)) + b_ref[0]


def _modulation(c_all, w_ada, b_ada):
    depth, d, n6 = w_ada.shape
    rows = c_all.shape[0]
    tn = 1024
    return pl.pallas_call(
        _mod_kernel,
        grid=(depth, n6 // tn),
        in_specs=[
            pl.BlockSpec((rows, d), lambda l, j: (0, 0)),
            pl.BlockSpec((1, d, tn), lambda l, j: (l, 0, j)),
            pl.BlockSpec((1, 1, tn), lambda l, j: (l, 0, j)),
        ],
        out_specs=pl.BlockSpec((1, rows, tn), lambda l, j: (l, 0, j)),
        out_shape=jax.ShapeDtypeStruct((depth, rows, n6), F32),
        compiler_params=_cparams(("arbitrary", "arbitrary")),
        name="adaln_mod",
    )(c_all, w_ada, b_ada.reshape(depth, 1, n6))


def _modulated_norm(x, g, sc, sh):
    ms = jnp.mean(x * x, axis=-1, keepdims=True)
    return (x * lax.rsqrt(ms + EPS)) * g * (1.0 + sc) + sh


def _group_rms(t, ind, inv_size, g):
    sq = (t * t).astype(BF16)
    wb = ind.shape[0]
    ss = [_dot(sq[:, c:c + wb], ind) for c in range(0, t.shape[1], wb)]
    ss = jnp.concatenate(ss, axis=1) if len(ss) > 1 else ss[0]
    return t * lax.rsqrt(ss * inv_size + EPS) * g


def _log_sigmoid(x):
    return jnp.minimum(x, 0.0) - jnp.log(1.0 + jnp.exp(-jnp.abs(x)))


def _augmented_keys(kfn, f_cum, heads):
    lane = lax.broadcasted_iota(jnp.int32, f_cum.shape, 1)
    hi, mid, lo = (t.astype(F32) for t in _split3(f_cum * -LOG2E))
    aug_even = pltpu.roll(hi, 64, 1) + pltpu.roll(mid, 72, 1) + pltpu.roll(lo, 80, 1)
    aug_odd = hi + pltpu.roll(mid, 8, 1) + pltpu.roll(lo, 16, 1)
    blocks = []
    for h in range(heads):
        kp = kfn[:, (h // 2) * LANES:(h // 2 + 1) * LANES]
        if h % 2 == 0:
            blocks.append(jnp.where(lane < HEAD_DIM, kp, aug_even))
        else:
            blocks.append(jnp.where(lane >= HEAD_DIM, kp, aug_odd))
    return jnp.concatenate(blocks, axis=1).astype(BF16)


def _inproj_kernel(x_ref, sc_ref, sh_ref, gmix_ref, w_ref, gsgu_ref, gq_ref, gk_ref, bf_ref,
                   ind96_ref, ind64_ref, msgu_ref, bsgu_ref, *rest,
                   tm, w_sgu, w_fox, w_sb, period, sweep, layer):
    if sweep:
        prev = rest[1:1 + 4 * bool(layer)]
        (ysgu_ref, kaug_ref, qft_ref, vft_ref, qbt_ref, kb16_ref, vbt_ref,
         kft_ref, vft32_ref, kbt_ref, vbt32_ref, logft_ref, fcumt_ref, carry_ref) = rest[1 + len(prev):]
        tril_ref = rest[0]
    else:
        (ysgu_ref, qf_ref, kf16_ref, vf16_ref, qb_ref, kb16_ref, vb16_ref,
         kf32_ref, vf32_ref, kb32_ref, vb32_ref, logft_ref, sguv_ref) = rest
    x = x_ref[...]
    h = _modulated_norm(x, gmix_ref[0], sc_ref[0], sh_ref[0])
    p = _dot(h.astype(BF16), w_ref[0])

    o = 0
    u = p[:, o:o + w_sgu]; o += w_sgu
    vs = p[:, o:o + w_sgu]; o += w_sgu
    qf = p[:, o:o + w_fox]; o += w_fox
    kf = p[:, o:o + w_fox]; o += w_fox
    vf = p[:, o:o + w_fox]; o += w_fox
    qb = p[:, o:o + w_sb]; o += w_sb
    kb = p[:, o:o + w_sb]; o += w_sb
    vb = p[:, o:o + w_sb]; o += w_sb
    fl = p[:, o:o + LANES]

    scale = HEAD_DIM ** -0.5
    ind64 = ind64_ref[...]
    qfn = _group_rms(qf, ind64, 1.0 / HEAD_DIM, gq_ref[0])
    kfn = _group_rms(kf, ind64, 1.0 / HEAD_DIM, gk_ref[0])
    kb16_ref[...] = kb.astype(BF16)
    lf = _log_sigmoid(fl + bf_ref[0])
    logft_ref[...] = lf.T[0:8, :]
    if sweep:
        qft_ref[...] = (qfn * (scale * LOG2E)).T.astype(BF16)
        qbt_ref[...] = (qb * (scale * LOG2E)).T.astype(BF16)
        vf_t, vb_t = vf.T, vb.T
        vft_ref[...] = vf_t.astype(BF16)
        vbt_ref[...] = vb_t.astype(BF16)
        for dst, own, earlier in zip((kft_ref, vft32_ref, kbt_ref, vbt32_ref),
                                     (kfn.T, vf_t, kb.T, vb_t), prev or (None,) * 4):
            if earlier is not None:
                dst[0:layer] = earlier[...]
            dst[layer] = own
        @pl.when(pl.program_id(0) == 0)
        def _():
            carry_ref[...] = jnp.zeros(carry_ref.shape, F32)
        lane = lax.broadcasted_iota(jnp.int32, lf.shape, 1)
        lfh, lfm, lfl = _split3(jnp.where(lane < 8, lf, 0.0))
        tril = tril_ref[...]
        run = carry_ref[0:1, :]
        chunks = []
        for c in range(0, tm, LANES):
            part = (_dot(tril, lfh[c:c + LANES]) + _dot(tril, lfm[c:c + LANES])
                    + _dot(tril, lfl[c:c + LANES]) + run)
            run = part[LANES - 1:LANES, :]
            chunks.append(part)
        f_cum = jnp.concatenate(chunks, axis=0) if len(chunks) > 1 else chunks[0]
        carry_ref[...] = jnp.broadcast_to(run, carry_ref.shape)
        fcumt_ref[...] = f_cum.T[0:8, :]
        kaug_ref[...] = _augmented_keys(kfn, f_cum, w_fox // HEAD_DIM)
    else:
        kf32_ref[...] = kfn
        vf32_ref[...] = vf
        kb32_ref[...] = kb
        vb32_ref[...] = vb
        qf_ref[...] = (qfn * scale).astype(BF16)
        kf16_ref[...] = kfn.astype(BF16)
        vf16_ref[...] = vf.astype(BF16)
        qb_ref[...] = (qb * scale).astype(BF16)
        vb16_ref[...] = vb.astype(BF16)

    cg = w_sgu // G_SGU
    vsn = _group_rms(vs, ind96_ref[...], 1.0 / cg, gsgu_ref[0])
    if not sweep:
        sguv_ref[...] = vsn
    r = lax.broadcasted_iota(jnp.int32, (SGU_LEN, G_SGU * SGU_LEN), 0)
    c = lax.broadcasted_iota(jnp.int32, (SGU_LEN, G_SGU * SGU_LEN), 1) % SGU_LEN
    keep = (r // period == c // period) & ((c % period) // CHUNK <= (r % period) // CHUNK)
    mix = jnp.where(keep, msgu_ref[0], 0.0).astype(BF16)
    lane_group = lax.broadcasted_iota(jnp.int32, (SGU_LEN, w_sgu), 1) // cg
    vsb = vsn.astype(BF16)
    spat = []
    for ci in range(tm // SGU_LEN):
        vc = vsb[ci * SGU_LEN:(ci + 1) * SGU_LEN]
        stacked = jnp.concatenate(
            [jnp.where(lane_group == g, vc, jnp.zeros_like(vc)) for g in range(G_SGU)], axis=0)
        spat.append(_dot(mix, stacked) + bsgu_ref[0])
    spat = jnp.concatenate(spat, axis=0) if len(spat) > 1 else spat[0]
    ysgu_ref[...] = (u * spat).astype(BF16)


def _inproj(x, mod, layer, gmix, w, gsgu, gq, gk, bf, ind96, ind64, msgu, bsgu, *, tm, period,
            sweep, prev_states=()):
    n, d = x.shape
    w_sgu, w_fox = gsgu.shape[2], gq.shape[2]
    w_sb = (w.shape[2] - LANES - 2 * w_sgu - 3 * w_fox) // 3
    row = lambda width: pl.BlockSpec((tm, width), lambda i: (i, 0))
    col = lambda height: pl.BlockSpec((height, tm), lambda i: (0, i))
    sds = jax.ShapeDtypeStruct
    if sweep:
        stack = lambda width: pl.BlockSpec((layer + 1, width, tm), lambda i: (0, 0, i))
        states_specs = [stack(w_fox), stack(w_fox), stack(w_sb), stack(w_sb), col(8)]
        states_shape = [sds((layer + 1, wd, n), F32) for wd in (w_fox, w_fox, w_sb, w_sb)] \
            + [sds((8, n), F32)]
    else:
        states_specs = [row(w_fox), row(w_fox), row(w_sb), row(w_sb), col(8)]
        states_shape = [sds((n, w_fox), F32), sds((n, w_fox), F32), sds((n, w_sb), F32),
                        sds((n, w_sb), F32), sds((8, n), F32)]
    operands = [x, mod, mod, gmix, w, gsgu, gq, gk, bf, ind96, ind64, msgu, bsgu]
    lay = lambda a: _layer_spec(a, layer)
    in_specs = [row(d), _mod_spec(mod, layer, 1, tm), _mod_spec(mod, layer, 0, tm), lay(gmix), lay(w),
                lay(gsgu), lay(gq), lay(gk), lay(bf), _const_spec(ind96.shape),
                _const_spec(ind64.shape), lay(msgu), lay(bsgu)]
    scratch = []
    if sweep:
        heads = w_fox // HEAD_DIM
        a = np.arange(LANES)
        operands.append(jnp.asarray(a[None, :] <= a[:, None], dtype=BF16))
        in_specs.append(_const_spec((LANES, LANES)))
        for st in prev_states:
            operands.append(st)
            in_specs.append(pl.BlockSpec((layer, st.shape[1], tm), lambda i: (0, 0, i)))
        out_specs = [row(w_sgu), row(heads * LANES), col(w_fox), col(w_fox), col(w_sb), row(w_sb),
                     col(w_sb)] + states_specs + [col(8)]
        out_shape = [sds((n, w_sgu), BF16), sds((n, heads * LANES), BF16), sds((w_fox, n), BF16),
                     sds((w_fox, n), BF16), sds((w_sb, n), BF16), sds((n, w_sb), BF16),
                     sds((w_sb, n), BF16)] + states_shape + [sds((8, n), F32)]
        scratch = [pltpu.VMEM((8, LANES), F32)]
    else:
        out_specs = [row(w_sgu), row(w_fox), row(w_fox), row(w_fox), row(w_sb), row(w_sb),
                     row(w_sb)] + states_specs + [row(w_sgu)]
        out_shape = [sds((n, w_sgu), BF16), sds((n, w_fox), BF16), sds((n, w_fox), BF16),
                     sds((n, w_fox), BF16), sds((n, w_sb), BF16), sds((n, w_sb), BF16),
                     sds((n, w_sb), BF16)] + states_shape + [sds((n, w_sgu), F32)]
    kern = functools.partial(_inproj_kernel, tm=tm, w_sgu=w_sgu, w_fox=w_fox, w_sb=w_sb,
                             period=period, sweep=sweep, layer=layer)
    return pl.pallas_call(
        kern,
        grid=(n // tm,),
        in_specs=in_specs,
        out_specs=out_specs,
        out_shape=out_shape,
        scratch_shapes=scratch,
        compiler_params=_cparams(("arbitrary",)),
        name="inproj",
    )(*operands)


def _seq_cumsum(x, nc, reverse_exclusive):
    rows = x.shape[0]
    a = lax.broadcasted_iota(jnp.int32, (LANES, LANES), 0)
    b = lax.broadcasted_iota(jnp.int32, (LANES, LANES), 1)
    tri = (a > b) if reverse_exclusive else (a <= b)
    tri = jnp.where(tri, 1.0, 0.0).astype(BF16)
    ones = jnp.ones((LANES, LANES), BF16)
    xh, xm, xl = _split3(x)
    within = _dot(xh, tri) + _dot(xm, tri) + _dot(xl, tri)
    tot = _dot(xh, ones) + _dot(xm, ones) + _dot(xl, ones)
    ra = lax.broadcasted_iota(jnp.int32, (rows, rows), 0)
    rb = lax.broadcasted_iota(jnp.int32, (rows, rows), 1)
    other = (rb > ra) if reverse_exclusive else (rb < ra)
    blk = jnp.where((ra // nc == rb // nc) & other, 1.0, 0.0).astype(BF16)
    th, tm_, tl = _split3(tot)
    return within + _dot(blk, th) + _dot(blk, tm_) + _dot(blk, tl)


def _sample_cumsum_kernel(clf_ref, lf_ref, suf_ref, cum_ref, *, nc, dec_seq):
    suf_ref[...] = _seq_cumsum(clf_ref[...], nc, reverse_exclusive=True)
    a = lax.broadcasted_iota(jnp.int32, (LANES, LANES), 0)
    b = lax.broadcasted_iota(jnp.int32, (LANES, LANES), 1)
    tri = jnp.where((a // dec_seq == b // dec_seq) & (a <= b), 1.0, 0.0).astype(BF16)
    xh, xm, xl = _split3(lf_ref[...])
    cum_ref[...] = _dot(xh, tri) + _dot(xm, tri) + _dot(xl, tri)


def _sample_cumsum(clogf_t, logft, dec_seq):
    rows, past = clogf_t.shape
    nc = past // LANES
    suf, cum = pl.pallas_call(
        functools.partial(_sample_cumsum_kernel, nc=nc, dec_seq=dec_seq),
        out_shape=[jax.ShapeDtypeStruct((rows * nc, LANES), F32),
                   jax.ShapeDtypeStruct(logft.shape, F32)],
        compiler_params=pltpu.CompilerParams(vmem_limit_bytes=VMEM_LIMIT),
        name="sample_logf_cumsum",
    )(clogf_t.reshape(rows * nc, LANES), logft)
    return suf.reshape(rows, past), cum


def _head_query_blocks(qt_ref, heads, tq, ones_rows, lane0=0):
    rowi = lax.broadcasted_iota(jnp.int32, (LANES, tq), 0)
    out = []
    for h in range(heads):
        qp = qt_ref[(h // 2) * LANES:(h // 2 + 1) * LANES, lane0:lane0 + tq]
        own = (rowi < HEAD_DIM) if h % 2 == 0 else (rowi >= HEAD_DIM)
        fill = jnp.zeros((LANES, tq), F32)
        for r in ones_rows(h):
            fill = jnp.where(rowi == r, 1.0, fill)
        out.append(jnp.where(own, qp, fill.astype(BF16)))
    return out


def _bias_rows(h):
    base = HEAD_DIM if h % 2 == 0 else 0
    return [base + h, base + 8 + h, base + 16 + h]


def _fox_kernel(fs_ref, fe_ref, cut_ref, qt_ref, ka_ref, vt_ref, o_ref, *, tq, heads):
    i = pl.program_id(0)
    tk = tq
    cpb = tk // LANES
    qa = _head_query_blocks(qt_ref, heads, tq, _bias_rows)
    rowi = lax.broadcasted_iota(jnp.int32, (LANES, tq), 0)
    krow = lax.broadcasted_iota(jnp.int32, (tk, tq), 0)
    qcol = lax.broadcasted_iota(jnp.int32, (tk, tq), 1)

    def logits_of(j):
        k0 = pl.multiple_of(j * tk, tk)
        return tuple(_dot(ka_ref[pl.ds(k0, tk), h * LANES:(h + 1) * LANES], qa[h])
                     for h in range(heads))

    def absorb(j, logits, carry, mask):
        k0 = pl.multiple_of(j * tk, tk)
        out = []
        for h in range(heads):
            vt = vt_ref[(h // 2) * LANES:(h // 2 + 1) * LANES, pl.ds(k0, tk)]
            m, l, acc = carry[3 * h:3 * h + 3]
            s = logits[h]
            if mask is not None:
                s = jnp.where(mask, s, NEG_BIG)
            mn = jnp.maximum(m, jnp.max(s, axis=0, keepdims=True))
            alpha = jnp.exp2(m - mn)
            pe = jnp.exp2(s - mn)
            l = alpha * l + jnp.sum(pe, axis=0, keepdims=True)
            acc = alpha * acc + _dot(vt, pe.astype(BF16))
            out += [mn, l, acc]
        return tuple(out)

    def run(blocks, carry):
        staged = [logits_of(j) for j, _ in blocks]
        for (j, mask), logits in zip(blocks, staged):
            carry = absorb(j, logits, carry, mask)
        return carry

    def live(j):
        jc = j * cpb + (cpb - 1)
        ok = fs_ref[0, i * cpb] - fe_ref[0, jc] >= cut_ref[0]
        for h in range(1, heads):
            ok = ok | (fs_ref[h, i * cpb] - fe_ref[h, jc] >= cut_ref[0])
        return ok

    init = []
    for _ in range(heads):
        init += [jnp.full((1, tq), NEG_BIG, F32), jnp.zeros((1, tq), F32),
                 jnp.zeros((LANES, tq), F32)]
    carry = run([(i, krow <= qcol), (jnp.maximum(i - 1, 0), jnp.broadcast_to(i > 0, (tk, tq)))],
                tuple(init))

    def cond(state):
        return (state[0] >= 1) & live(jnp.maximum(state[0], 0))

    def body(state):
        j = state[0]
        return (j - 2,) + run([(j, None), (j - 1, None)], state[1:])

    state = lax.while_loop(cond, body, (i - 2,) + carry)
    last = (state[0] == 0) & live(0)
    res = lax.cond(last, lambda c: run([(0, None)], c), lambda c: c, state[1:])
    for p in range(heads // 2):
        _, l0, a0, _, l1, a1 = res[6 * p:6 * p + 6]
        ot = jnp.where(rowi < HEAD_DIM, a0 / l0, a1 / l1)
        o_ref[:, p * LANES:(p + 1) * LANES] = ot.T.astype(BF16)


def _fox_prompt(fs, fe, cut, qt, kaug, vt, *, tq):
    w, n = qt.shape
    heads = w // HEAD_DIM
    grid_spec = pltpu.PrefetchScalarGridSpec(
        num_scalar_prefetch=3,
        grid=(n // tq,),
        in_specs=[
            pl.BlockSpec((w, tq), lambda i, *_: (0, i)),
            pl.BlockSpec(kaug.shape, lambda i, *_: (0, 0), pipeline_mode=pl.Buffered(1)),
            pl.BlockSpec((w, n), lambda i, *_: (0, 0), pipeline_mode=pl.Buffered(1)),
        ],
        out_specs=pl.BlockSpec((tq, w), lambda i, *_: (i, 0)),
    )
    return pl.pallas_call(
        functools.partial(_fox_kernel, tq=tq, heads=heads),
        grid_spec=grid_spec,
        out_shape=jax.ShapeDtypeStruct((n, w), BF16),
        compiler_params=_cparams(("arbitrary",)),
        name="fox_prompt",
    )(fs, fe, cut, qt, kaug, vt)


def _sb_kernel(qt_ref, k_ref, vt_ref, o_ref, *, tq, tb, heads):
    i = pl.program_id(0)
    nsub = tq // tb
    qm = [_head_query_blocks(qt_ref, heads, tb, lambda h: [], lane0=s * tb) for s in range(nsub)]
    rowi = lax.broadcasted_iota(jnp.int32, (LANES, tb), 0)
    krow = lax.broadcasted_iota(jnp.int32, (tb, tb), 0)
    qcol = lax.broadcasted_iota(jnp.int32, (tb, tb), 1)
    a = lax.broadcasted_iota(jnp.int32, (tb, 2 * tb), 0)
    b = lax.broadcasted_iota(jnp.int32, (tb, 2 * tb), 1) % tb
    tri2 = jnp.where(b > a, 1.0, 0.0).astype(BF16)

    def sweep(tasks, carry):
        k0s = [pl.multiple_of(t[1] * tb, tb) for t in tasks]
        z = [[_dot(k_ref[pl.ds(k0, tb), (h // 2) * LANES:(h // 2 + 1) * LANES], qm[t[0]][h])
              for h in range(heads)] for t, k0 in zip(tasks, k0s)]
        logsig, later, mass = {}, {}, {}
        for ti, (_, _, mask, valid) in enumerate(tasks):
            for h in range(heads):
                zz = z[ti][h]
                sp = jnp.maximum(zz, 0.0) + jnp.log2(1.0 + jnp.exp2(-jnp.abs(zz)))
                logsig[ti, h] = zz - sp
                if mask is not None:
                    sp = jnp.where(mask, sp, 0.0)
                hi, lo = _split2(sp)
                later[ti, h] = _dot(tri2, jnp.concatenate([hi, lo], axis=0))
                mass[ti, h] = jnp.sum(sp, axis=0, keepdims=True)
                if valid is not None:
                    mass[ti, h] = mass[ti, h] * valid
        carry = dict(carry)
        for ti, (s, _, mask, valid) in enumerate(tasks):
            for h in range(heads):
                cr, acc = carry[s, h]
                w = jnp.exp2(logsig[ti, h] - later[ti, h] - cr)
                if mask is not None:
                    w = jnp.where(mask, w, 0.0)
                if valid is not None:
                    w = w * valid
                vt = vt_ref[(h // 2) * LANES:(h // 2 + 1) * LANES, pl.ds(k0s[ti], tb)]
                carry[s, h] = (cr + mass[ti, h], acc + _dot(vt, w.astype(BF16)))
        return carry

    tasks = []
    for s in range(nsub):
        qb = i * nsub + s
        tasks.append((s, qb, krow < qcol, None))
        tasks.append((s, jnp.maximum(qb - 1, 0), None, jnp.where(qb > 0, 1.0, 0.0) if s == 0 else None))
    zero = (jnp.zeros((1, tb), F32), jnp.zeros((LANES, tb), F32))
    carry = sweep(tasks, {(s, h): zero for s in range(nsub) for h in range(heads)})

    for s in range(nsub):
        def cond(state):
            live = jnp.min(state[1])
            for h in range(1, heads):
                live = jnp.minimum(live, jnp.min(state[1 + 2 * h]))
            return (state[0] >= 0) & (live < PRUNE_LOG * LOG2E)

        def body(state, s=s):
            sub = {(s, h): (state[1 + 2 * h], state[2 + 2 * h]) for h in range(heads)}
            sub = sweep([(s, state[0], None, None)], sub)
            return (state[0] - 1,) + tuple(v for h in range(heads) for v in sub[s, h])

        init = (i * nsub + s - 2,) + tuple(v for h in range(heads) for v in carry[s, h])
        res = lax.while_loop(cond, body, init)[1:]
        for p in range(heads // 2):
            ot = jnp.where(rowi < HEAD_DIM, res[4 * p + 1], res[4 * p + 3])
            o_ref[s * tb:(s + 1) * tb, p * LANES:(p + 1) * LANES] = ot.T.astype(BF16)


def _sb_prompt(qt, k, vt, *, tq, tb):
    w, n = qt.shape
    return pl.pallas_call(
        functools.partial(_sb_kernel, tq=tq, tb=tb, heads=w // HEAD_DIM),
        grid=(n // tq,),
        in_specs=[
            pl.BlockSpec((w, tq), lambda i: (0, i)),
            pl.BlockSpec((n, w), lambda i: (0, 0), pipeline_mode=pl.Buffered(1)),
            pl.BlockSpec((w, n), lambda i: (0, 0), pipeline_mode=pl.Buffered(1)),
        ],
        out_specs=pl.BlockSpec((tq, w), lambda i: (i, 0)),
        out_shape=jax.ShapeDtypeStruct((n, w), BF16),
        compiler_params=_cparams(("arbitrary",)),
        name="sb_prompt",
    )(qt, k, vt)


def _sb_block(z, carry, v, tri, mask, v_feature_major=False):
    lg = jnp.log(1.0 + jnp.exp(-jnp.abs(z)))
    sp = jnp.maximum(z, 0.0) + lg
    if mask is not None:
        sp = jnp.where(mask, sp, 0.0)
    hi, lo = _split2(sp)
    later = _dot(hi, tri) + _dot(lo, tri)
    a = jnp.exp((jnp.minimum(z, 0.0) - lg) - later - carry)
    if mask is not None:
        a = jnp.where(mask, a, 0.0)
    pv = _dot_nt(a.astype(BF16), v) if v_feature_major else _dot(a.astype(BF16), v)
    return pv, carry + jnp.sum(sp, axis=1, keepdims=True)


def _suffix_matrix(tk):
    a = lax.broadcasted_iota(jnp.int32, (tk, tk), 0)
    b = lax.broadcasted_iota(jnp.int32, (tk, tk), 1)
    return jnp.where(a > b, 1.0, 0.0).astype(BF16)


def _head_rows(x, heads):
    lane_head = lax.broadcasted_iota(jnp.int32, x.shape, 1) // HEAD_DIM
    return jnp.concatenate(
        [jnp.where(lane_head == h, x, jnp.zeros_like(x)) for h in range(heads)], axis=0)


def _fold_heads(o, heads, s):
    lane_head = lax.broadcasted_iota(jnp.int32, (s, o.shape[1]), 1) // HEAD_DIM
    out = jnp.zeros((s, o.shape[1]), F32)
    for h in range(heads):
        out = jnp.where(lane_head == h, o[h * s:(h + 1) * s], out)
    return out


def _sample_attn_kernel(qf_ref, kf_ref, vf_ref, ck_ref, cv_ref, suf_ref, cum_ref,
                        qb_ref, kb_ref, vb_ref, cbk_ref, cbv_ref, of_ref, ob_ref,
                        *, s, h_fox, h_sb, past):
    qa = _head_rows(qf_ref[...], h_fox)
    rows = h_fox * s
    ck = ck_ref[0, 0].astype(BF16)
    cv = cv_ref[0, 0].astype(BF16)
    suf = suf_ref[0]
    cum = cum_ref[0]
    bias_c = jnp.concatenate(
        [jnp.broadcast_to(suf[h:h + 1, :], (s, past)) for h in range(h_fox)], axis=0)
    bias_n = jnp.concatenate(
        [jnp.broadcast_to(-cum[h:h + 1, :], (s, s)) for h in range(h_fox)], axis=0)
    lc = _dot(qa, ck) + bias_c
    ln = _dot_nt(qa, kf_ref[...]) + bias_n
    r_pos = lax.broadcasted_iota(jnp.int32, (rows, s), 0) % s
    k_pos = lax.broadcasted_iota(jnp.int32, (rows, s), 1)
    ln = jnp.where(k_pos <= r_pos, ln, NEG_BIG)
    m = jnp.maximum(jnp.max(lc, axis=1, keepdims=True), jnp.max(ln, axis=1, keepdims=True))
    pc = jnp.exp(lc - m)
    pn = jnp.exp(ln - m)
    den = jnp.sum(pc, axis=1, keepdims=True) + jnp.sum(pn, axis=1, keepdims=True)
    o = (_dot_nt(pc.astype(BF16), cv) + _dot(pn.astype(BF16), vf_ref[...])) / den
    of_ref[...] = _fold_heads(o, h_fox, s).astype(BF16)

    qb = _head_rows(qb_ref[...], h_sb)
    rows_b = h_sb * s
    rb = lax.broadcasted_iota(jnp.int32, (rows_b, s), 0) % s
    cb = lax.broadcasted_iota(jnp.int32, (rows_b, s), 1)
    acc, carry = _sb_block(_dot_nt(qb, kb_ref[...]), jnp.zeros((rows_b, 1), F32), vb_ref[...],
                           _suffix_matrix(s), cb < rb)
    tri = _suffix_matrix(LANES)

    def cond(state):
        return (state[0] < past // LANES) & (jnp.min(state[2]) < PRUNE_LOG)

    def body(state):
        t, acc, carry = state
        k0 = pl.multiple_of(past - (t + 1) * LANES, LANES)
        k = cbk_ref[0, 0, :, pl.ds(k0, LANES)].astype(BF16)
        v = cbv_ref[0, 0, :, pl.ds(k0, LANES)].astype(BF16)
        pv, carry = _sb_block(_dot(qb, k), carry, v, tri, None, v_feature_major=True)
        return t + 1, acc + pv, carry

    _, acc, _ = lax.while_loop(cond, body, (0, acc, carry))
    ob_ref[...] = _fold_heads(acc, h_sb, s).astype(BF16)


def _sample_attn(qf, kf, vf, ck, cv, suf, cum, qb, kb, vb, cbk, cbv, *, layer, batch, s):
    w_fox, w_sb = qf.shape[1], qb.shape[1]
    past = ck.shape[3]
    h_fox, h_sb = w_fox // HEAD_DIM, w_sb // HEAD_DIM
    new = lambda w: pl.BlockSpec((s, w), lambda b: (b, 0))
    cache = lambda w: pl.BlockSpec((1, 1, w, past), lambda b: (layer, b, 0, 0))
    kern = functools.partial(_sample_attn_kernel, s=s, h_fox=h_fox, h_sb=h_sb, past=past)
    return pl.pallas_call(
        kern,
        grid=(batch,),
        in_specs=[new(w_fox), new(w_fox), new(w_fox), cache(w_fox), cache(w_fox),
                  pl.BlockSpec((1, 8, past), lambda b: (b, 0, 0)),
                  pl.BlockSpec((1, 8, s), lambda b: (b, 0, 0)),
                  new(w_sb), new(w_sb), new(w_sb), cache(w_sb), cache(w_sb)],
        out_specs=[new(w_fox), new(w_sb)],
        out_shape=[jax.ShapeDtypeStruct((batch * s, w_fox), BF16),
                   jax.ShapeDtypeStruct((batch * s, w_sb), BF16)],
        compiler_params=_cparams(("arbitrary",)),
        name="sample_attn",
    )(qf, kf, vf, ck, cv, suf, cum, qb, kb, vb, cbk, cbv)


def _merge_kernel(x_ref, sc_ref, sh_ref, gt_ref, gmix_ref, ysgu_ref, ofox_ref, osb_ref,
                  wg_ref, bg_ref, wbs_ref, wbf_ref, wbb_ref, wo_ref, o_ref):
    x = x_ref[...]
    d = x.shape[1]
    h = _modulated_norm(x, gmix_ref[0], sc_ref[0], sh_ref[0])
    gates = _dot(h.astype(BF16), wg_ref[0]) + bg_ref[0]
    gates = 1.0 / (1.0 + jnp.exp(-gates))
    merged = gates[:, 0:d] * _dot(ysgu_ref[...], wbs_ref[0]) \
        + gates[:, d:2 * d] * _dot(ofox_ref[...], wbf_ref[0]) \
        + gates[:, 2 * d:3 * d] * _dot(osb_ref[...], wbb_ref[0])
    o_ref[...] = x + gt_ref[0] * _dot(merged.astype(BF16), wo_ref[0])


def _merge(x, mod, layer, gmix, ysgu, ofox, osb, wg, bg, wbs, wbf, wbb, wo, *, tm):
    n, d = x.shape
    row = lambda width: pl.BlockSpec((tm, width), lambda i: (i, 0))
    lay = lambda a: _layer_spec(a, layer)
    return pl.pallas_call(
        _merge_kernel,
        grid=(n // tm,),
        in_specs=[row(d), _mod_spec(mod, layer, 1, tm), _mod_spec(mod, layer, 0, tm),
                  _mod_spec(mod, layer, 2, tm), lay(gmix),
                  row(ysgu.shape[1]), row(ofox.shape[1]), row(osb.shape[1]),
                  lay(wg), lay(bg), lay(wbs), lay(wbf), lay(wbb), lay(wo)],
        out_specs=row(d),
        out_shape=jax.ShapeDtypeStruct((n, d), F32),
        compiler_params=_cparams(("arbitrary",)),
        name="merge",
    )(x, mod, mod, mod, gmix, ysgu, ofox, osb, wg, bg, wbs, wbf, wbb, wo)


def _ffn_kernel(x_ref, sc_ref, sh_ref, gt_ref, g_ref, wi_ref, wo_ref, o_ref, *, d_ff):
    x = x_ref[...]
    h = _modulated_norm(x, g_ref[0], sc_ref[0], sh_ref[0])
    ag = _dot(h.astype(BF16), wi_ref[0])
    a = ag[:, 0:d_ff]
    act = a * (1.0 / (1.0 + jnp.exp(-a))) * ag[:, d_ff:2 * d_ff]
    o_ref[...] = x + gt_ref[0] * _dot(act.astype(BF16), wo_ref[0])


def _ffn(x, mod, layer, g, wi, wo, *, tm):
    n, d = x.shape
    d_ff = wo.shape[1]
    row = pl.BlockSpec((tm, d), lambda i: (i, 0))
    return pl.pallas_call(
        functools.partial(_ffn_kernel, d_ff=d_ff),
        grid=(n // tm,),
        in_specs=[row, _mod_spec(mod, layer, 4, tm), _mod_spec(mod, layer, 3, tm),
                  _mod_spec(mod, layer, 5, tm), _layer_spec(g, layer), _layer_spec(wi, layer),
                  _layer_spec(wo, layer)],
        out_specs=row,
        out_shape=jax.ShapeDtypeStruct((n, d), F32),
        compiler_params=_cparams(("arbitrary",)),
        name="ffn",
    )(x, mod, mod, mod, g, wi, wo)


def _indicator(width, group):
    idx = np.arange(width) // group
    return jnp.asarray(idx[:, None] == idx[None, :], dtype=BF16)


def kernel(x_prompt, x_sample, c_prompt, c_sample, cache_fox_k, cache_fox_v, cache_fox_logf,
           cache_sb_k, cache_sb_v, w_ada, b_ada, g_mix, g_ffn, w_in, g_sgu_v, w_sgu, b_sgu, b_fgt,
           g_q, g_k, w_br_sgu, w_br_fox, w_br_sb, w_gate, b_gate, w_out, w_ffn_in, w_ffn_out):
    batch, seq, d = x_prompt.shape
    dec_batch, dec_seq, _ = x_sample.shape
    depth = w_ada.shape[0]
    past = cache_fox_k.shape[2]
    h_fox, h_sb = cache_fox_k.shape[3], cache_sb_k.shape[3]
    w_fox, w_sb = h_fox * HEAD_DIM, h_sb * HEAD_DIM
    g_sgu, cg = g_sgu_v.shape[1], g_sgu_v.shape[2]
    w_sgu_ = g_sgu * cg
    assert batch == 1 and g_sgu == G_SGU and w_sgu.shape[2] == SGU_LEN
    n_dec = dec_batch * dec_seq

    n_c = batch + dec_batch
    c_rows = -(-n_c // 8) * 8
    c_all = jnp.zeros((c_rows, d), F32).at[:n_c].set(jnp.concatenate([c_prompt, c_sample], axis=0))
    mod = _modulation(c_all, w_ada, b_ada)

    offs = np.cumsum([0, w_sgu_, w_sgu_, w_fox, w_fox, w_fox, h_fox, w_sb, w_sb, w_sb]).tolist()
    f_cols = jnp.zeros((depth, d, LANES), F32).at[:, :, :h_fox].set(w_in[:, :, offs[5]:offs[6]])
    w_main = jnp.concatenate([w_in[:, :, :offs[5]], w_in[:, :, offs[6]:], f_cols],
                             axis=2).astype(BF16)
    bf_pad = jnp.zeros((depth, 1, LANES), F32).at[:, 0, :h_fox].set(b_fgt)
    gmix3, gffn3, bg3 = g_mix.reshape(depth, 1, d), g_ffn.reshape(depth, 1, d), b_gate.reshape(depth, 1, 3 * d)
    mod_p = mod[:, 0:batch]
    mod_s = jnp.repeat(mod[:, batch:batch + dec_batch], dec_seq, axis=1)
    ind96, ind64 = _indicator(w_sgu_, cg), _indicator(LANES, HEAD_DIM)
    gq_t = jnp.tile(g_q, (1, h_fox)).reshape(depth, 1, w_fox)
    gk_t = jnp.tile(g_k, (1, h_fox)).reshape(depth, 1, w_fox)
    gsgu = g_sgu_v.reshape(depth, 1, w_sgu_)
    msgu_p = jnp.transpose(w_sgu, (0, 2, 1, 3)).reshape(depth, SGU_LEN, g_sgu * SGU_LEN)
    reps = SGU_LEN // dec_seq
    w_small = jnp.tile(w_sgu[:, :, :dec_seq, :dec_seq], (1, 1, reps, reps))
    msgu_s = jnp.transpose(w_small, (0, 2, 1, 3)).reshape(depth, SGU_LEN, g_sgu * SGU_LEN)
    bsgu_p = jnp.repeat(jnp.transpose(b_sgu, (0, 2, 1)), cg, axis=2)
    bsgu_s = jnp.tile(bsgu_p[:, :dec_seq], (1, reps, 1))
    wg, wbs, wbf, wbb = (w.astype(BF16) for w in (w_gate, w_br_sgu, w_br_fox, w_br_sb))
    wo, wfi, wfo = (w.astype(BF16) for w in (w_out, w_ffn_in, w_ffn_out))

    to_fm = lambda c: jnp.transpose(c, (0, 1, 3, 4, 2)).reshape(
        depth, dec_batch, c.shape[3] * HEAD_DIM, past)
    ck_t, cv_t, cbk_t, cbv_t = (to_fm(c) for c in (cache_fox_k, cache_fox_v, cache_sb_k, cache_sb_v))

    xp = x_prompt.reshape(seq, d)
    xs = x_sample.reshape(n_dec, d)
    tm_p = min(512, seq)
    tm_f = min(512, seq)
    tq = min(256, seq)
    stacks, logf_p, st_s = (), [], []
    for l in range(depth):
        shared = (l, gmix3, w_main, gsgu, gq_t, gk_t, bf_pad, ind96, ind64)
        branch_w = (wg, bg3, wbs, wbf, wbb, wo)

        (ysgu, kaug, qft, vft, qbt, kb16, vbt, *stacks, logft, fcumt) = _inproj(
            xp, mod_p, *shared, msgu_p, bsgu_p, tm=tm_p, period=SGU_LEN, sweep=True,
            prev_states=stacks)
        bound = 1.01 * HEAD_DIM ** 0.5 * jnp.max(jnp.abs(g_q[l])) * jnp.max(jnp.abs(g_k[l]))
        cut = (-(2.0 * bound + PRUNE_LOG)).reshape(1)
        ofox = _fox_prompt(fcumt[:, 0::LANES], fcumt[:, LANES - 1::LANES], cut, qft, kaug, vft, tq=tq)
        osb = _sb_prompt(qbt, kb16, vbt, tq=min(512, seq), tb=min(128, seq))
        x1 = _merge(xp, mod_p, l, gmix3, ysgu, ofox, osb, *branch_w, tm=tm_p)
        xp = _ffn(x1, mod_p, l, gffn3, wfi, wfo, tm=tm_f)
        logf_p.append(logft[:h_fox].T.reshape(batch, seq, h_fox))

        (ysgu, qf, kf16, vf16, qb, kb16, vb16, kf32, vf32, kb32, vb32, logft, sguv) = _inproj(
            xs, mod_s, *shared, msgu_s, bsgu_s, tm=n_dec, period=dec_seq, sweep=False)
        clf = jnp.zeros((dec_batch, 8, past), F32).at[:, :h_fox].set(
            jnp.transpose(cache_fox_logf[l], (0, 2, 1)))
        suf, cum = _sample_cumsum(clf.reshape(dec_batch * 8, past), logft, dec_seq)
        cum_b = jnp.transpose(cum.reshape(8, dec_batch, dec_seq), (1, 0, 2))
        ofox, osb = _sample_attn(
            qf, kf16, vf16, ck_t, cv_t, suf.reshape(dec_batch, 8, past), cum_b,
            qb, kb16, vb16, cbk_t, cbv_t, layer=l, batch=dec_batch, s=dec_seq)
        x1 = _merge(xs, mod_s, l, gmix3, ysgu, ofox, osb, *branch_w, tm=n_dec)
        xs = _ffn(x1, mod_s, l, gffn3, wfi, wfo, tm=n_dec)
        st_s.append((kf32.reshape(dec_batch, dec_seq, h_fox, HEAD_DIM),
                     vf32.reshape(dec_batch, dec_seq, h_fox, HEAD_DIM),
                     logft[:h_fox].T.reshape(dec_batch, dec_seq, h_fox),
                     kb32.reshape(dec_batch, dec_seq, h_sb, HEAD_DIM),
                     vb32.reshape(dec_batch, dec_seq, h_sb, HEAD_DIM),
                     sguv.reshape(dec_batch, dec_seq, w_sgu_)))

    def stack(states, idx):
        return jnp.stack([s[idx] for s in states], axis=0)

    per_head = lambda st: jnp.transpose(
        st.reshape(depth, batch, st.shape[1] // HEAD_DIM, HEAD_DIM, seq), (0, 1, 4, 2, 3))
    kf_p, vf_p, kb_p, vb_p = (per_head(st) for st in stacks)
    return (xp.reshape(batch, seq, d), xs.reshape(dec_batch, dec_seq, d),
            kf_p, vf_p, jnp.stack(logf_p, axis=0), kb_p, vb_p,
            stack(st_s, 0), stack(st_s, 1), stack(st_s, 2), stack(st_s, 3), stack(st_s, 4),
            stack(st_s, 5))
```

```python
import functools

import numpy as np
import jax
import jax.numpy as jnp
from jax import lax
from jax.experimental import pallas as pl
from jax.experimental.pallas import tpu as pltpu

F32 = jnp.float32
BF16 = jnp.bfloat16

EPS = 1e-6
HEAD_DIM = 64
LANES = 128
CHUNK = 64
SGU_LEN = 128
G_SGU = 4
NEG_BIG = -1e30
LOG2E = 1.4426950408889634

PRUNE_LOG = 30.0

VMEM_LIMIT = 56 * 1024 * 1024


def _cparams(sem):
    return pltpu.CompilerParams(dimension_semantics=sem, vmem_limit_bytes=VMEM_LIMIT)


def _const_spec(shape):
    nd = len(shape)
    return pl.BlockSpec(shape, lambda *_: (0,) * nd, pipeline_mode=pl.Buffered(1))


def _layer_spec(arr, layer):
    nd = arr.ndim
    return pl.BlockSpec((1,) + arr.shape[1:], lambda *_: (layer,) + (0,) * (nd - 1),
                        pipeline_mode=pl.Buffered(1))


def _mod_spec(mod, layer, k, tm):
    d = mod.shape[2] // 6
    if mod.shape[1] == 1:
        return pl.BlockSpec((1, 1, d), lambda i: (layer, 0, k))
    return pl.BlockSpec((1, tm, d), lambda i: (layer, i, k))


def _dot(a, b):
    return jnp.dot(a, b, preferred_element_type=F32)


def _dot_nt(a, b):
    return lax.dot_general(a, b, (((1,), (1,)), ((), ())), preferred_element_type=F32)


def _split3(x):
    h = x.astype(BF16)
    r = x - h.astype(F32)
    m = r.astype(BF16)
    l = (r - m.astype(F32)).astype(BF16)
    return h, m, l


def _split2(x):
    h = x.astype(BF16)
    l = (x - h.astype(F32)).astype(BF16)
    return h, l


def _mod_kernel(c_ref, w_ref, b_ref, o_ref):
    c = c_ref[...]
    s = c * (1.0 / (1.0 + jnp.exp(-c)))
    o_ref[0] = _dot(s.astype(BF16), w_ref[0].astype(BF16)) + b_ref[0]


def _modulation(c_all, w_ada, b_ada):
    depth, d, n6 = w_ada.shape
    rows = c_all.shape[0]
    tn = 1024
    return pl.pallas_call(
        _mod_kernel,
        grid=(depth, n6 // tn),
        in_specs=[
            pl.BlockSpec((rows, d), lambda l, j: (0, 0)),
            pl.BlockSpec((1, d, tn), lambda l, j: (l, 0, j)),
            pl.BlockSpec((1, 1, tn), lambda l, j: (l, 0, j)),
        ],
        out_specs=pl.BlockSpec((1, rows, tn), lambda l, j: (l, 0, j)),
        out_shape=jax.ShapeDtypeStruct((depth, rows, n6), F32),
        compiler_params=_cparams(("arbitrary", "arbitrary")),
        name="adaln_mod",
    )(c_all, w_ada, b_ada.reshape(depth, 1, n6))


def _modulated_norm(x, g, sc, sh):
    ms = jnp.mean(x * x, axis=-1, keepdims=True)
    return (x * lax.rsqrt(ms + EPS)) * g * (1.0 + sc) + sh


def _group_rms(t, ind, inv_size, g):
    sq = (t * t).astype(BF16)
    wb = ind.shape[0]
    ss = [_dot(sq[:, c:c + wb], ind) for c in range(0, t.shape[1], wb)]
    ss = jnp.concatenate(ss, axis=1) if len(ss) > 1 else ss[0]
    return t * lax.rsqrt(ss * inv_size + EPS) * g


def _log_sigmoid(x):
    return jnp.minimum(x, 0.0) - jnp.log(1.0 + jnp.exp(-jnp.abs(x)))


def _augmented_keys(kfn, f_cum, heads):
    lane = lax.broadcasted_iota(jnp.int32, f_cum.shape, 1)
    hi, mid, lo = (t.astype(F32) for t in _split3(f_cum * -LOG2E))
    aug_even = pltpu.roll(hi, 64, 1) + pltpu.roll(mid, 72, 1) + pltpu.roll(lo, 80, 1)
    aug_odd = hi + pltpu.roll(mid, 8, 1) + pltpu.roll(lo, 16, 1)
    blocks = []
    for h in range(heads):
        kp = kfn[:, (h // 2) * LANES:(h // 2 + 1) * LANES]
        if h % 2 == 0:
            blocks.append(jnp.where(lane < HEAD_DIM, kp, aug_even))
        else:
            blocks.append(jnp.where(lane >= HEAD_DIM, kp, aug_odd))
    return jnp.concatenate(blocks, axis=1).astype(BF16)


def _inproj_kernel(x_ref, sc_ref, sh_ref, gmix_ref, w_ref, gsgu_ref, gq_ref, gk_ref, bf_ref,
                   ind96_ref, ind64_ref, msgu_ref, bsgu_ref, *rest,
                   tm, w_sgu, w_fox, w_sb, period, sweep, layer):
    if sweep:
        prev = rest[1:1 + 4 * bool(layer)]
        (ysgu_ref, kaug_ref, qft_ref, vft_ref, qbt_ref, kb16_ref, vbt_ref,
         kft_ref, vft32_ref, kbt_ref, vbt32_ref, logft_ref, fcumt_ref, carry_ref) = rest[1 + len(prev):]
        tril_ref = rest[0]
    else:
        (ysgu_ref, qf_ref, kf16_ref, vf16_ref, qb_ref, kb16_ref, vb16_ref,
         kf32_ref, vf32_ref, kb32_ref, vb32_ref, logft_ref, sguv_ref) = rest
    x = x_ref[...]
    h = _modulated_norm(x, gmix_ref[0], sc_ref[0], sh_ref[0])
    p = _dot(h.astype(BF16), w_ref[0])

    o = 0
    u = p[:, o:o + w_sgu]; o += w_sgu
    vs = p[:, o:o + w_sgu]; o += w_sgu
    qf = p[:, o:o + w_fox]; o += w_fox
    kf = p[:, o:o + w_fox]; o += w_fox
    vf = p[:, o:o + w_fox]; o += w_fox
    qb = p[:, o:o + w_sb]; o += w_sb
    kb = p[:, o:o + w_sb]; o += w_sb
    vb = p[:, o:o + w_sb]; o += w_sb
    fl = p[:, o:o + LANES]

    scale = HEAD_DIM ** -0.5
    ind64 = ind64_ref[...]
    qfn = _group_rms(qf, ind64, 1.0 / HEAD_DIM, gq_ref[0])
    kfn = _group_rms(kf, ind64, 1.0 / HEAD_DIM, gk_ref[0])
    kb16_ref[...] = kb.astype(BF16)
    lf = _log_sigmoid(fl + bf_ref[0])
    logft_ref[...] = lf.T[0:8, :]
    if sweep:
        qft_ref[...] = (qfn * (scale * LOG2E)).T.astype(BF16)
        qbt_ref[...] = (qb * (scale * LOG2E)).T.astype(BF16)
        vf_t, vb_t = vf.T, vb.T
        vft_ref[...] = vf_t.astype(BF16)
        vbt_ref[...] = vb_t.astype(BF16)
        for dst, own, earlier in zip((kft_ref, vft32_ref, kbt_ref, vbt32_ref),
                                     (kfn.T, vf_t, kb.T, vb_t), prev or (None,) * 4):
            if earlier is not None:
                dst[0:layer] = earlier[...]
            dst[layer] = own
        @pl.when(pl.program_id(0) == 0)
        def _():
            carry_ref[...] = jnp.zeros(carry_ref.shape, F32)
        lane = lax.broadcasted_iota(jnp.int32, lf.shape, 1)
        lfh, lfm, lfl = _split3(jnp.where(lane < 8, lf, 0.0))
        tril = tril_ref[...]
        run = carry_ref[0:1, :]
        chunks = []
        for c in range(0, tm, LANES):
            part = (_dot(tril, lfh[c:c + LANES]) + _dot(tril, lfm[c:c + LANES])
                    + _dot(tril, lfl[c:c + LANES]) + run)
            run = part[LANES - 1:LANES, :]
            chunks.append(part)
        f_cum = jnp.concatenate(chunks, axis=0) if len(chunks) > 1 else chunks[0]
        carry_ref[...] = jnp.broadcast_to(run, carry_ref.shape)
        fcumt_ref[...] = f_cum.T[0:8, :]
        kaug_ref[...] = _augmented_keys(kfn, f_cum, w_fox // HEAD_DIM)
    else:
        kf32_ref[...] = kfn
        vf32_ref[...] = vf
        kb32_ref[...] = kb
        vb32_ref[...] = vb
        qf_ref[...] = (qfn * scale).astype(BF16)
        kf16_ref[...] = kfn.astype(BF16)
        vf16_ref[...] = vf.astype(BF16)
        qb_ref[...] = (qb * scale).astype(BF16)
        vb16_ref[...] = vb.astype(BF16)

    cg = w_sgu // G_SGU
    vsn = _group_rms(vs, ind96_ref[...], 1.0 / cg, gsgu_ref[0])
    if not sweep:
        sguv_ref[...] = vsn
    r = lax.broadcasted_iota(jnp.int32, (SGU_LEN, G_SGU * SGU_LEN), 0)
    c = lax.broadcasted_iota(jnp.int32, (SGU_LEN, G_SGU * SGU_LEN), 1) % SGU_LEN
    keep = (r // period == c // period) & ((c % period) // CHUNK <= (r % period) // CHUNK)
    mix = jnp.where(keep, msgu_ref[0], 0.0).astype(BF16)
    lane_group = lax.broadcasted_iota(jnp.int32, (SGU_LEN, w_sgu), 1) // cg
    vsb = vsn.astype(BF16)
    spat = []
    for ci in range(tm // SGU_LEN):
        vc = vsb[ci * SGU_LEN:(ci + 1) * SGU_LEN]
        stacked = jnp.concatenate(
            [jnp.where(lane_group == g, vc, jnp.zeros_like(vc)) for g in range(G_SGU)], axis=0)
        spat.append(_dot(mix, stacked) + bsgu_ref[0])
    spat = jnp.concatenate(spat, axis=0) if len(spat) > 1 else spat[0]
    ysgu_ref[...] = (u * spat).astype(BF16)


def _inproj(x, mod, layer, gmix, w, gsgu, gq, gk, bf, ind96, ind64, msgu, bsgu, *, tm, period,
            sweep, prev_states=()):
    n, d = x.shape
    w_sgu, w_fox = gsgu.shape[2], gq.shape[2]
    w_sb = (w.shape[2] - LANES - 2 * w_sgu - 3 * w_fox) // 3
    row = lambda width: pl.BlockSpec((tm, width), lambda i: (i, 0))
    col = lambda height: pl.BlockSpec((height, tm), lambda i: (0, i))
    sds = jax.ShapeDtypeStruct
    if sweep:
        stack = lambda width: pl.BlockSpec((layer + 1, width, tm), lambda i: (0, 0, i))
        states_specs = [stack(w_fox), stack(w_fox), stack(w_sb), stack(w_sb), col(8)]
        states_shape = [sds((layer + 1, wd, n), F32) for wd in (w_fox, w_fox, w_sb, w_sb)] \
            + [sds((8, n), F32)]
    else:
        states_specs = [row(w_fox), row(w_fox), row(w_sb), row(w_sb), col(8)]
        states_shape = [sds((n, w_fox), F32), sds((n, w_fox), F32), sds((n, w_sb), F32),
                        sds((n, w_sb), F32), sds((8, n), F32)]
    operands = [x, mod, mod, gmix, w, gsgu, gq, gk, bf, ind96, ind64, msgu, bsgu]
    lay = lambda a: _layer_spec(a, layer)
    in_specs = [row(d), _mod_spec(mod, layer, 1, tm), _mod_spec(mod, layer, 0, tm), lay(gmix), lay(w),
                lay(gsgu), lay(gq), lay(gk), lay(bf), _const_spec(ind96.shape),
                _const_spec(ind64.shape), lay(msgu), lay(bsgu)]
    scratch = []
    if sweep:
        heads = w_fox // HEAD_DIM
        a = np.arange(LANES)
        operands.append(jnp.asarray(a[None, :] <= a[:, None], dtype=BF16))
        in_specs.append(_const_spec((LANES, LANES)))
        for st in prev_states:
            operands.append(st)
            in_specs.append(pl.BlockSpec((layer, st.shape[1], tm), lambda i: (0, 0, i)))
        out_specs = [row(w_sgu), row(heads * LANES), col(w_fox), col(w_fox), col(w_sb), row(w_sb),
                     col(w_sb)] + states_specs + [col(8)]
        out_shape = [sds((n, w_sgu), BF16), sds((n, heads * LANES), BF16), sds((w_fox, n), BF16),
                     sds((w_fox, n), BF16), sds((w_sb, n), BF16), sds((n, w_sb), BF16),
                     sds((w_sb, n), BF16)] + states_shape + [sds((8, n), F32)]
        scratch = [pltpu.VMEM((8, LANES), F32)]
    else:
        out_specs = [row(w_sgu), row(w_fox), row(w_fox), row(w_fox), row(w_sb), row(w_sb),
                     row(w_sb)] + states_specs + [row(w_sgu)]
        out_shape = [sds((n, w_sgu), BF16), sds((n, w_fox), BF16), sds((n, w_fox), BF16),
                     sds((n, w_fox), BF16), sds((n, w_sb), BF16), sds((n, w_sb), BF16),
                     sds((n, w_sb), BF16)] + states_shape + [sds((n, w_sgu), F32)]
    kern = functools.partial(_inproj_kernel, tm=tm, w_sgu=w_sgu, w_fox=w_fox, w_sb=w_sb,
                             period=period, sweep=sweep, layer=layer)
    return pl.pallas_call(
        kern,
        grid=(n // tm,),
        in_specs=in_specs,
        out_specs=out_specs,
        out_shape=out_shape,
        scratch_shapes=scratch,
        compiler_params=_cparams(("arbitrary",)),
        name="inproj",
    )(*operands)


def _seq_cumsum(x, nc, reverse_exclusive):
    rows = x.shape[0]
    a = lax.broadcasted_iota(jnp.int32, (LANES, LANES), 0)
    b = lax.broadcasted_iota(jnp.int32, (LANES, LANES), 1)
    tri = (a > b) if reverse_exclusive else (a <= b)
    tri = jnp.where(tri, 1.0, 0.0).astype(BF16)
    ones = jnp.ones((LANES, LANES), BF16)
    xh, xm, xl = _split3(x)
    within = _dot(xh, tri) + _dot(xm, tri) + _dot(xl, tri)
    tot = _dot(xh, ones) + _dot(xm, ones) + _dot(xl, ones)
    ra = lax.broadcasted_iota(jnp.int32, (rows, rows), 0)
    rb = lax.broadcasted_iota(jnp.int32, (rows, rows), 1)
    other = (rb > ra) if reverse_exclusive else (rb < ra)
    blk = jnp.where((ra // nc == rb // nc) & other, 1.0, 0.0).astype(BF16)
    th, tm_, tl = _split3(tot)
    return within + _dot(blk, th) + _dot(blk, tm_) + _dot(blk, tl)


def _sample_cumsum_kernel(clf_ref, lf_ref, suf_ref, cum_ref, *, nc, dec_seq):
    suf_ref[...] = _seq_cumsum(clf_ref[...], nc, reverse_exclusive=True)
    a = lax.broadcasted_iota(jnp.int32, (LANES, LANES), 0)
    b = lax.broadcasted_iota(jnp.int32, (LANES, LANES), 1)
    tri = jnp.where((a // dec_seq == b // dec_seq) & (a <= b), 1.0, 0.0).astype(BF16)
    xh, xm, xl = _split3(lf_ref[...])
    cum_ref[...] = _dot(xh, tri) + _dot(xm, tri) + _dot(xl, tri)


def _sample_cumsum(clogf_t, logft, dec_seq):
    rows, past = clogf_t.shape
    nc = past // LANES
    suf, cum = pl.pallas_call(
        functools.partial(_sample_cumsum_kernel, nc=nc, dec_seq=dec_seq),
        out_shape=[jax.ShapeDtypeStruct((rows * nc, LANES), F32),
                   jax.ShapeDtypeStruct(logft.shape, F32)],
        compiler_params=pltpu.CompilerParams(vmem_limit_bytes=VMEM_LIMIT),
        name="sample_logf_cumsum",
    )(clogf_t.reshape(rows * nc, LANES), logft)
    return suf.reshape(rows, past), cum


def _head_query_blocks(qt_ref, heads, tq, ones_rows, lane0=0):
    rowi = lax.broadcasted_iota(jnp.int32, (LANES, tq), 0)
    out = []
    for h in range(heads):
        qp = qt_ref[(h // 2) * LANES:(h // 2 + 1) * LANES, lane0:lane0 + tq]
        own = (rowi < HEAD_DIM) if h % 2 == 0 else (rowi >= HEAD_DIM)
        fill = jnp.zeros((LANES, tq), F32)
        for r in ones_rows(h):
            fill = jnp.where(rowi == r, 1.0, fill)
        out.append(jnp.where(own, qp, fill.astype(BF16)))
    return out


def _bias_rows(h):
    base = HEAD_DIM if h % 2 == 0 else 0
    return [base + h, base + 8 + h, base + 16 + h]


def _fox_kernel(fs_ref, fe_ref, cut_ref, qt_ref, ka_ref, vt_ref, o_ref, *, tq, heads):
    i = pl.program_id(0)
    tk = tq
    cpb = tk // LANES
    qa = _head_query_blocks(qt_ref, heads, tq, _bias_rows)
    rowi = lax.broadcasted_iota(jnp.int32, (LANES, tq), 0)
    krow = lax.broadcasted_iota(jnp.int32, (tk, tq), 0)
    qcol = lax.broadcasted_iota(jnp.int32, (tk, tq), 1)

    def logits_of(j):
        k0 = pl.multiple_of(j * tk, tk)
        return tuple(_dot(ka_ref[pl.ds(k0, tk), h * LANES:(h + 1) * LANES], qa[h])
                     for h in range(heads))

    def absorb(j, logits, carry, mask):
        k0 = pl.multiple_of(j * tk, tk)
        out = []
        for h in range(heads):
            vt = vt_ref[(h // 2) * LANES:(h // 2 + 1) * LANES, pl.ds(k0, tk)]
            m, l, acc = carry[3 * h:3 * h + 3]
            s = logits[h]
            if mask is not None:
                s = jnp.where(mask, s, NEG_BIG)
            mn = jnp.maximum(m, jnp.max(s, axis=0, keepdims=True))
            alpha = jnp.exp2(m - mn)
            pe = jnp.exp2(s - mn)
            l = alpha * l + jnp.sum(pe, axis=0, keepdims=True)
            acc = alpha * acc + _dot(vt, pe.astype(BF16))
            out += [mn, l, acc]
        return tuple(out)

    def run(blocks, carry):
        staged = [logits_of(j) for j, _ in blocks]
        for (j, mask), logits in zip(blocks, staged):
            carry = absorb(j, logits, carry, mask)
        return carry

    def live(j):
        jc = j * cpb + (cpb - 1)
        ok = fs_ref[0, i * cpb] - fe_ref[0, jc] >= cut_ref[0]
        for h in range(1, heads):
            ok = ok | (fs_ref[h, i * cpb] - fe_ref[h, jc] >= cut_ref[0])
        return ok

    init = []
    for _ in range(heads):
        init += [jnp.full((1, tq), NEG_BIG, F32), jnp.zeros((1, tq), F32),
                 jnp.zeros((LANES, tq), F32)]
    carry = run([(i, krow <= qcol), (jnp.maximum(i - 1, 0), jnp.broadcast_to(i > 0, (tk, tq)))],
                tuple(init))

    def cond(state):
        return (state[0] >= 1) & live(jnp.maximum(state[0] - 1, 0))

    def body(state):
        j = state[0]
        return (j - 2,) + run([(j, None), (j - 1, None)], state[1:])

    state = lax.while_loop(cond, body, (i - 2,) + carry)
    j = jnp.maximum(state[0], 0)
    last = (state[0] >= 0) & live(j)
    res = lax.cond(last, lambda c: run([(j, None)], c), lambda c: c, state[1:])
    for p in range(heads // 2):
        _, l0, a0, _, l1, a1 = res[6 * p:6 * p + 6]
        ot = jnp.where(rowi < HEAD_DIM, a0 / l0, a1 / l1)
        o_ref[:, p * LANES:(p + 1) * LANES] = ot.T.astype(BF16)


def _fox_prompt(fs, fe, cut, qt, kaug, vt, *, tq):
    w, n = qt.shape
    heads = w // HEAD_DIM
    grid_spec = pltpu.PrefetchScalarGridSpec(
        num_scalar_prefetch=3,
        grid=(n // tq,),
        in_specs=[
            pl.BlockSpec((w, tq), lambda i, *_: (0, i)),
            pl.BlockSpec(kaug.shape, lambda i, *_: (0, 0), pipeline_mode=pl.Buffered(1)),
            pl.BlockSpec((w, n), lambda i, *_: (0, 0), pipeline_mode=pl.Buffered(1)),
        ],
        out_specs=pl.BlockSpec((tq, w), lambda i, *_: (i, 0)),
    )
    return pl.pallas_call(
        functools.partial(_fox_kernel, tq=tq, heads=heads),
        grid_spec=grid_spec,
        out_shape=jax.ShapeDtypeStruct((n, w), BF16),
        compiler_params=_cparams(("arbitrary",)),
        name="fox_prompt",
    )(fs, fe, cut, qt, kaug, vt)


def _sb_kernel(qt_ref, k_ref, vt_ref, o_ref, *, tq, tb, heads):
    i = pl.program_id(0)
    nsub = tq // tb
    qm = [_head_query_blocks(qt_ref, heads, tb, lambda h: [], lane0=s * tb) for s in range(nsub)]
    rowi = lax.broadcasted_iota(jnp.int32, (LANES, tb), 0)
    krow = lax.broadcasted_iota(jnp.int32, (tb, tb), 0)
    qcol = lax.broadcasted_iota(jnp.int32, (tb, tb), 1)
    a = lax.broadcasted_iota(jnp.int32, (tb, 2 * tb), 0)
    b = lax.broadcasted_iota(jnp.int32, (tb, 2 * tb), 1) % tb
    tri2 = jnp.where(b > a, 1.0, 0.0).astype(BF16)

    def sweep(tasks, carry):
        k0s = [pl.multiple_of(t[1] * tb, tb) for t in tasks]
        z = [[_dot(k_ref[pl.ds(k0, tb), (h // 2) * LANES:(h // 2 + 1) * LANES], qm[t[0]][h])
              for h in range(heads)] for t, k0 in zip(tasks, k0s)]
        logsig, later, mass = {}, {}, {}
        for ti, (_, _, mask, valid) in enumerate(tasks):
            for h in range(heads):
                zz = z[ti][h]
                sp = jnp.maximum(zz, 0.0) + jnp.log2(1.0 + jnp.exp2(-jnp.abs(zz)))
                logsig[ti, h] = zz - sp
                if mask is not None:
                    sp = jnp.where(mask, sp, 0.0)
                hi, lo = _split2(sp)
                later[ti, h] = _dot(tri2, jnp.concatenate([hi, lo], axis=0))
                mass[ti, h] = jnp.sum(sp, axis=0, keepdims=True)
                if valid is not None:
                    mass[ti, h] = mass[ti, h] * valid
        carry = dict(carry)
        for ti, (s, _, mask, valid) in enumerate(tasks):
            for h in range(heads):
                cr, acc = carry[s, h]
                w = jnp.exp2(logsig[ti, h] - later[ti, h] - cr)
                if mask is not None:
                    w = jnp.where(mask, w, 0.0)
                if valid is not None:
                    w = w * valid
                vt = vt_ref[(h // 2) * LANES:(h // 2 + 1) * LANES, pl.ds(k0s[ti], tb)]
                carry[s, h] = (cr + mass[ti, h], acc + _dot(vt, w.astype(BF16)))
        return carry

    tasks = []
    for s in range(nsub):
        qb = i * nsub + s
        tasks.append((s, qb, krow < qcol, None))
        tasks.append((s, jnp.maximum(qb - 1, 0), None, jnp.where(qb > 0, 1.0, 0.0) if s == 0 else None))
    zero = (jnp.zeros((1, tb), F32), jnp.zeros((LANES, tb), F32))
    carry = sweep(tasks, {(s, h): zero for s in range(nsub) for h in range(heads)})

    keys = [(s, h) for s in range(nsub) for h in range(heads)]
    flat = lambda c: tuple(v for key in keys for v in c[key])

    def cond(state):
        left = state[1]
        for idx in range(1, len(keys)):
            left = jnp.minimum(left, state[1 + 2 * idx])
        return (i * nsub + nsub - 3 - state[0] >= 0) & (jnp.min(left) < PRUNE_LOG * LOG2E)

    def body(state):
        t = state[0]
        more = []
        for s in range(nsub):
            j = i * nsub + s - 2 - t
            more.append((s, jnp.maximum(j, 0), None, jnp.where(j >= 0, 1.0, 0.0)))
        c = {key: (state[1 + 2 * idx], state[2 + 2 * idx]) for idx, key in enumerate(keys)}
        return (t + 1,) + flat(sweep(more, c))

    res = lax.while_loop(cond, body, (0,) + flat(carry))[1:]
    for s in range(nsub):
        for p in range(heads // 2):
            e, o = keys.index((s, 2 * p)), keys.index((s, 2 * p + 1))
            ot = jnp.where(rowi < HEAD_DIM, res[2 * e + 1], res[2 * o + 1])
            o_ref[s * tb:(s + 1) * tb, p * LANES:(p + 1) * LANES] = ot.T.astype(BF16)


def _sb_prompt(qt, k, vt, *, tq, tb):
    w, n = qt.shape
    return pl.pallas_call(
        functools.partial(_sb_kernel, tq=tq, tb=tb, heads=w // HEAD_DIM),
        grid=(n // tq,),
        in_specs=[
            pl.BlockSpec((w, tq), lambda i: (0, i)),
            pl.BlockSpec((n, w), lambda i: (0, 0), pipeline_mode=pl.Buffered(1)),
            pl.BlockSpec((w, n), lambda i: (0, 0), pipeline_mode=pl.Buffered(1)),
        ],
        out_specs=pl.BlockSpec((tq, w), lambda i: (i, 0)),
        out_shape=jax.ShapeDtypeStruct((n, w), BF16),
        compiler_params=_cparams(("arbitrary",)),
        name="sb_prompt",
    )(qt, k, vt)


def _sb_block(z, carry, v, tri, mask, v_feature_major=False):
    lg = jnp.log(1.0 + jnp.exp(-jnp.abs(z)))
    sp = jnp.maximum(z, 0.0) + lg
    if mask is not None:
        sp = jnp.where(mask, sp, 0.0)
    hi, lo = _split2(sp)
    later = _dot(hi, tri) + _dot(lo, tri)
    a = jnp.exp((jnp.minimum(z, 0.0) - lg) - later - carry)
    if mask is not None:
        a = jnp.where(mask, a, 0.0)
    pv = _dot_nt(a.astype(BF16), v) if v_feature_major else _dot(a.astype(BF16), v)
    return pv, carry + jnp.sum(sp, axis=1, keepdims=True)


def _suffix_matrix(tk):
    a = lax.broadcasted_iota(jnp.int32, (tk, tk), 0)
    b = lax.broadcasted_iota(jnp.int32, (tk, tk), 1)
    return jnp.where(a > b, 1.0, 0.0).astype(BF16)


def _head_rows(x, heads):
    lane_head = lax.broadcasted_iota(jnp.int32, x.shape, 1) // HEAD_DIM
    return jnp.concatenate(
        [jnp.where(lane_head == h, x, jnp.zeros_like(x)) for h in range(heads)], axis=0)


def _fold_heads(o, heads, s):
    lane_head = lax.broadcasted_iota(jnp.int32, (s, o.shape[1]), 1) // HEAD_DIM
    out = jnp.zeros((s, o.shape[1]), F32)
    for h in range(heads):
        out = jnp.where(lane_head == h, o[h * s:(h + 1) * s], out)
    return out


def _sample_attn_kernel(qf_ref, kf_ref, vf_ref, ck_ref, cv_ref, suf_ref, cum_ref,
                        qb_ref, kb_ref, vb_ref, cbk_ref, cbv_ref, of_ref, ob_ref,
                        *, s, h_fox, h_sb, past):
    qa = _head_rows(qf_ref[...], h_fox)
    rows = h_fox * s
    ck = ck_ref[0, 0].astype(BF16)
    cv = cv_ref[0, 0].astype(BF16)
    suf = suf_ref[0]
    cum = cum_ref[0]
    bias_c = jnp.concatenate(
        [jnp.broadcast_to(suf[h:h + 1, :], (s, past)) for h in range(h_fox)], axis=0)
    bias_n = jnp.concatenate(
        [jnp.broadcast_to(-cum[h:h + 1, :], (s, s)) for h in range(h_fox)], axis=0)
    lc = _dot(qa, ck) + bias_c
    ln = _dot_nt(qa, kf_ref[...]) + bias_n
    r_pos = lax.broadcasted_iota(jnp.int32, (rows, s), 0) % s
    k_pos = lax.broadcasted_iota(jnp.int32, (rows, s), 1)
    ln = jnp.where(k_pos <= r_pos, ln, NEG_BIG)
    m = jnp.maximum(jnp.max(lc, axis=1, keepdims=True), jnp.max(ln, axis=1, keepdims=True))
    pc = jnp.exp(lc - m)
    pn = jnp.exp(ln - m)
    den = jnp.sum(pc, axis=1, keepdims=True) + jnp.sum(pn, axis=1, keepdims=True)
    o = (_dot_nt(pc.astype(BF16), cv) + _dot(pn.astype(BF16), vf_ref[...])) / den
    of_ref[...] = _fold_heads(o, h_fox, s).astype(BF16)

    qb = _head_rows(qb_ref[...], h_sb)
    rows_b = h_sb * s
    rb = lax.broadcasted_iota(jnp.int32, (rows_b, s), 0) % s
    cb = lax.broadcasted_iota(jnp.int32, (rows_b, s), 1)
    acc, carry = _sb_block(_dot_nt(qb, kb_ref[...]), jnp.zeros((rows_b, 1), F32), vb_ref[...],
                           _suffix_matrix(s), cb < rb)
    tri = _suffix_matrix(LANES)

    def cond(state):
        return (state[0] < past // LANES) & (jnp.min(state[2]) < PRUNE_LOG)

    def body(state):
        t, acc, carry = state
        k0 = pl.multiple_of(past - (t + 1) * LANES, LANES)
        k = cbk_ref[0, 0, :, pl.ds(k0, LANES)].astype(BF16)
        v = cbv_ref[0, 0, :, pl.ds(k0, LANES)].astype(BF16)
        pv, carry = _sb_block(_dot(qb, k), carry, v, tri, None, v_feature_major=True)
        return t + 1, acc + pv, carry

    _, acc, _ = lax.while_loop(cond, body, (0, acc, carry))
    ob_ref[...] = _fold_heads(acc, h_sb, s).astype(BF16)


def _sample_attn(qf, kf, vf, ck, cv, suf, cum, qb, kb, vb, cbk, cbv, *, layer, batch, s):
    w_fox, w_sb = qf.shape[1], qb.shape[1]
    past = ck.shape[3]
    h_fox, h_sb = w_fox // HEAD_DIM, w_sb // HEAD_DIM
    new = lambda w: pl.BlockSpec((s, w), lambda b: (b, 0))
    cache = lambda w: pl.BlockSpec((1, 1, w, past), lambda b: (layer, b, 0, 0))
    kern = functools.partial(_sample_attn_kernel, s=s, h_fox=h_fox, h_sb=h_sb, past=past)
    return pl.pallas_call(
        kern,
        grid=(batch,),
        in_specs=[new(w_fox), new(w_fox), new(w_fox), cache(w_fox), cache(w_fox),
                  pl.BlockSpec((1, 8, past), lambda b: (b, 0, 0)),
                  pl.BlockSpec((1, 8, s), lambda b: (b, 0, 0)),
                  new(w_sb), new(w_sb), new(w_sb), cache(w_sb), cache(w_sb)],
        out_specs=[new(w_fox), new(w_sb)],
        out_shape=[jax.ShapeDtypeStruct((batch * s, w_fox), BF16),
                   jax.ShapeDtypeStruct((batch * s, w_sb), BF16)],
        compiler_params=_cparams(("arbitrary",)),
        name="sample_attn",
    )(qf, kf, vf, ck, cv, suf, cum, qb, kb, vb, cbk, cbv)


def _merge_kernel(x_ref, sc_ref, sh_ref, gt_ref, gmix_ref, ysgu_ref, ofox_ref, osb_ref,
                  wg_ref, bg_ref, wbs_ref, wbf_ref, wbb_ref, wo_ref, o_ref):
    x = x_ref[...]
    d = x.shape[1]
    h = _modulated_norm(x, gmix_ref[0], sc_ref[0], sh_ref[0])
    gates = _dot(h.astype(BF16), wg_ref[0]) + bg_ref[0]
    gates = 1.0 / (1.0 + jnp.exp(-gates))
    merged = gates[:, 0:d] * _dot(ysgu_ref[...], wbs_ref[0]) \
        + gates[:, d:2 * d] * _dot(ofox_ref[...], wbf_ref[0]) \
        + gates[:, 2 * d:3 * d] * _dot(osb_ref[...], wbb_ref[0])
    o_ref[...] = x + gt_ref[0] * _dot(merged.astype(BF16), wo_ref[0])


def _merge(x, mod, layer, gmix, ysgu, ofox, osb, wg, bg, wbs, wbf, wbb, wo, *, tm):
    n, d = x.shape
    row = lambda width: pl.BlockSpec((tm, width), lambda i: (i, 0))
    lay = lambda a: _layer_spec(a, layer)
    return pl.pallas_call(
        _merge_kernel,
        grid=(n // tm,),
        in_specs=[row(d), _mod_spec(mod, layer, 1, tm), _mod_spec(mod, layer, 0, tm),
                  _mod_spec(mod, layer, 2, tm), lay(gmix),
                  row(ysgu.shape[1]), row(ofox.shape[1]), row(osb.shape[1]),
                  lay(wg), lay(bg), lay(wbs), lay(wbf), lay(wbb), lay(wo)],
        out_specs=row(d),
        out_shape=jax.ShapeDtypeStruct((n, d), F32),
        compiler_params=_cparams(("arbitrary",)),
        name="merge",
    )(x, mod, mod, mod, gmix, ysgu, ofox, osb, wg, bg, wbs, wbf, wbb, wo)


def _ffn_kernel(x_ref, sc_ref, sh_ref, gt_ref, g_ref, wi_ref, wo_ref, o_ref, *, d_ff):
    x = x_ref[...]
    h = _modulated_norm(x, g_ref[0], sc_ref[0], sh_ref[0])
    ag = _dot(h.astype(BF16), wi_ref[0])
    a = ag[:, 0:d_ff]
    act = a * (1.0 / (1.0 + jnp.exp(-a))) * ag[:, d_ff:2 * d_ff]
    o_ref[...] = x + gt_ref[0] * _dot(act.astype(BF16), wo_ref[0])


def _ffn(x, mod, layer, g, wi, wo, *, tm):
    n, d = x.shape
    d_ff = wo.shape[1]
    row = pl.BlockSpec((tm, d), lambda i: (i, 0))
    return pl.pallas_call(
        functools.partial(_ffn_kernel, d_ff=d_ff),
        grid=(n // tm,),
        in_specs=[row, _mod_spec(mod, layer, 4, tm), _mod_spec(mod, layer, 3, tm),
                  _mod_spec(mod, layer, 5, tm), _layer_spec(g, layer), _layer_spec(wi, layer),
                  _layer_spec(wo, layer)],
        out_specs=row,
        out_shape=jax.ShapeDtypeStruct((n, d), F32),
        compiler_params=_cparams(("arbitrary",)),
        name="ffn",
    )(x, mod, mod, mod, g, wi, wo)


def _indicator(width, group):
    idx = np.arange(width) // group
    return jnp.asarray(idx[:, None] == idx[None, :], dtype=BF16)


def kernel(x_prompt, x_sample, c_prompt, c_sample, cache_fox_k, cache_fox_v, cache_fox_logf,
           cache_sb_k, cache_sb_v, w_ada, b_ada, g_mix, g_ffn, w_in, g_sgu_v, w_sgu, b_sgu, b_fgt,
           g_q, g_k, w_br_sgu, w_br_fox, w_br_sb, w_gate, b_gate, w_out, w_ffn_in, w_ffn_out):
    batch, seq, d = x_prompt.shape
    dec_batch, dec_seq, _ = x_sample.shape
    depth = w_ada.shape[0]
    past = cache_fox_k.shape[2]
    h_fox, h_sb = cache_fox_k.shape[3], cache_sb_k.shape[3]
    w_fox, w_sb = h_fox * HEAD_DIM, h_sb * HEAD_DIM
    g_sgu, cg = g_sgu_v.shape[1], g_sgu_v.shape[2]
    w_sgu_ = g_sgu * cg
    assert batch == 1 and g_sgu == G_SGU and w_sgu.shape[2] == SGU_LEN
    n_dec = dec_batch * dec_seq

    n_c = batch + dec_batch
    c_rows = -(-n_c // 8) * 8
    c_all = jnp.zeros((c_rows, d), F32).at[:n_c].set(jnp.concatenate([c_prompt, c_sample], axis=0))
    mod = _modulation(c_all, w_ada, b_ada)

    offs = np.cumsum([0, w_sgu_, w_sgu_, w_fox, w_fox, w_fox, h_fox, w_sb, w_sb, w_sb]).tolist()
    f_cols = jnp.zeros((depth, d, LANES), F32).at[:, :, :h_fox].set(w_in[:, :, offs[5]:offs[6]])
    w_main = jnp.concatenate([w_in[:, :, :offs[5]], w_in[:, :, offs[6]:], f_cols],
                             axis=2).astype(BF16)
    bf_pad = jnp.zeros((depth, 1, LANES), F32).at[:, 0, :h_fox].set(b_fgt)
    gmix3, gffn3, bg3 = g_mix.reshape(depth, 1, d), g_ffn.reshape(depth, 1, d), b_gate.reshape(depth, 1, 3 * d)
    mod_p = mod[:, 0:batch]
    mod_s = jnp.repeat(mod[:, batch:batch + dec_batch], dec_seq, axis=1)
    ind96, ind64 = _indicator(w_sgu_, cg), _indicator(LANES, HEAD_DIM)
    gq_t = jnp.tile(g_q, (1, h_fox)).reshape(depth, 1, w_fox)
    gk_t = jnp.tile(g_k, (1, h_fox)).reshape(depth, 1, w_fox)
    gsgu = g_sgu_v.reshape(depth, 1, w_sgu_)
    msgu_p = jnp.transpose(w_sgu, (0, 2, 1, 3)).reshape(depth, SGU_LEN, g_sgu * SGU_LEN)
    reps = SGU_LEN // dec_seq
    w_small = jnp.tile(w_sgu[:, :, :dec_seq, :dec_seq], (1, 1, reps, reps))
    msgu_s = jnp.transpose(w_small, (0, 2, 1, 3)).reshape(depth, SGU_LEN, g_sgu * SGU_LEN)
    bsgu_p = jnp.repeat(jnp.transpose(b_sgu, (0, 2, 1)), cg, axis=2)
    bsgu_s = jnp.tile(bsgu_p[:, :dec_seq], (1, reps, 1))
    wg, wbs, wbf, wbb = (w.astype(BF16) for w in (w_gate, w_br_sgu, w_br_fox, w_br_sb))
    wo, wfi, wfo = (w.astype(BF16) for w in (w_out, w_ffn_in, w_ffn_out))

    to_fm = lambda c: jnp.transpose(c, (0, 1, 3, 4, 2)).reshape(
        depth, dec_batch, c.shape[3] * HEAD_DIM, past)
    ck_t, cv_t, cbk_t, cbv_t = (to_fm(c) for c in (cache_fox_k, cache_fox_v, cache_sb_k, cache_sb_v))

    xp = x_prompt.reshape(seq, d)
    xs = x_sample.reshape(n_dec, d)
    tm_p = min(512, seq)
    tm_f = min(512, seq)
    tq = min(256, seq)
    stacks, logf_p, st_s = (), [], []
    for l in range(depth):
        shared = (l, gmix3, w_main, gsgu, gq_t, gk_t, bf_pad, ind96, ind64)
        branch_w = (wg, bg3, wbs, wbf, wbb, wo)

        (ysgu, kaug, qft, vft, qbt, kb16, vbt, *stacks, logft, fcumt) = _inproj(
            xp, mod_p, *shared, msgu_p, bsgu_p, tm=tm_p, period=SGU_LEN, sweep=True,
            prev_states=stacks)
        bound = 1.01 * HEAD_DIM ** 0.5 * jnp.max(jnp.abs(g_q[l])) * jnp.max(jnp.abs(g_k[l]))
        cut = (-(2.0 * bound + PRUNE_LOG)).reshape(1)
        ofox = _fox_prompt(fcumt[:, 0::LANES], fcumt[:, LANES - 1::LANES], cut, qft, kaug, vft, tq=tq)
        osb = _sb_prompt(qbt, kb16, vbt, tq=min(512, seq), tb=min(128, seq))
        x1 = _merge(xp, mod_p, l, gmix3, ysgu, ofox, osb, *branch_w, tm=tm_p)
        xp = _ffn(x1, mod_p, l, gffn3, wfi, wfo, tm=tm_f)
        logf_p.append(logft[:h_fox].T.reshape(batch, seq, h_fox))

        (ysgu, qf, kf16, vf16, qb, kb16, vb16, kf32, vf32, kb32, vb32, logft, sguv) = _inproj(
            xs, mod_s, *shared, msgu_s, bsgu_s, tm=n_dec, period=dec_seq, sweep=False)
        clf = jnp.zeros((dec_batch, 8, past), F32).at[:, :h_fox].set(
            jnp.transpose(cache_fox_logf[l], (0, 2, 1)))
        suf, cum = _sample_cumsum(clf.reshape(dec_batch * 8, past), logft, dec_seq)
        cum_b = jnp.transpose(cum.reshape(8, dec_batch, dec_seq), (1, 0, 2))
        ofox, osb = _sample_attn(
            qf, kf16, vf16, ck_t, cv_t, suf.reshape(dec_batch, 8, past), cum_b,
            qb, kb16, vb16, cbk_t, cbv_t, layer=l, batch=dec_batch, s=dec_seq)
        x1 = _merge(xs, mod_s, l, gmix3, ysgu, ofox, osb, *branch_w, tm=n_dec)
        xs = _ffn(x1, mod_s, l, gffn3, wfi, wfo, tm=n_dec)
        st_s.append((kf32.reshape(dec_batch, dec_seq, h_fox, HEAD_DIM),
                     vf32.reshape(dec_batch, dec_seq, h_fox, HEAD_DIM),
                     logft[:h_fox].T.reshape(dec_batch, dec_seq, h_fox),
                     kb32.reshape(dec_batch, dec_seq, h_sb, HEAD_DIM),
                     vb32.reshape(dec_batch, dec_seq, h_sb, HEAD_DIM),
                     sguv.reshape(dec_batch, dec_seq, w_sgu_)))

    def stack(states, idx):
        return jnp.stack([s[idx] for s in states], axis=0)

    per_head = lambda st: jnp.transpose(
        st.reshape(depth, batch, st.shape[1] // HEAD_DIM, HEAD_DIM, seq), (0, 1, 4, 2, 3))
    kf_p, vf_p, kb_p, vb_p = (per_head(st) for st in stacks)
    return (xp.reshape(batch, seq, d), xs.reshape(dec_batch, dec_seq, d),
            kf_p, vf_p, jnp.stack(logf_p, axis=0), kb_p, vb_p,
            stack(st_s, 0), stack(st_s, 1), stack(st_s, 2), stack(st_s, 3), stack(st_s, 4),
            stack(st_s, 5))
```

```python
import functools

import numpy as np
import jax
import jax.numpy as jnp
from jax import lax
from jax.experimental import pallas as pl
from jax.experimental.pallas import tpu as pltpu

F32 = jnp.float32
BF16 = jnp.bfloat16

EPS = 1e-6
HEAD_DIM = 64
LANES = 128
CHUNK = 64
SGU_LEN = 128
G_SGU = 4
NEG_BIG = -1e30
LOG2E = 1.4426950408889634

PRUNE_LOG = 30.0

VMEM_LIMIT = 56 * 1024 * 1024


def _cparams(sem):
    return pltpu.CompilerParams(dimension_semantics=sem, vmem_limit_bytes=VMEM_LIMIT)


def _const_spec(shape):
    nd = len(shape)
    return pl.BlockSpec(shape, lambda *_: (0,) * nd, pipeline_mode=pl.Buffered(1))


def _layer_spec(arr, layer):
    nd = arr.ndim
    return pl.BlockSpec((1,) + arr.shape[1:], lambda *_: (layer,) + (0,) * (nd - 1),
                        pipeline_mode=pl.Buffered(1))


def _mod_spec(mod, layer, k, tm):
    d = mod.shape[2] // 6
    if mod.shape[1] == 1:
        return pl.BlockSpec((1, 1, d), lambda i: (layer, 0, k))
    return pl.BlockSpec((1, tm, d), lambda i: (layer, i, k))


def _dot(a, b):
    return jnp.dot(a, b, preferred_element_type=F32)


def _dot_nt(a, b):
    return lax.dot_general(a, b, (((1,), (1,)), ((), ())), preferred_element_type=F32)


def _split3(x):
    h = x.astype(BF16)
    r = x - h.astype(F32)
    m = r.astype(BF16)
    l = (r - m.astype(F32)).astype(BF16)
    return h, m, l


def _split2(x):
    h = x.astype(BF16)
    l = (x - h.astype(F32)).astype(BF16)
    return h, l


def _mod_kernel(c_ref, w_ref, b_ref, o_ref):
    c = c_ref[...]
    s = c * (1.0 / (1.0 + jnp.exp(-c)))
    o_ref[0] = _dot(s.astype(BF16), w_ref[0].astype(BF16)) + b_ref[0]


def _modulation(c_all, w_ada, b_ada):
    depth, d, n6 = w_ada.shape
    rows = c_all.shape[0]
    tn = 1024
    return pl.pallas_call(
        _mod_kernel,
        grid=(depth, n6 // tn),
        in_specs=[
            pl.BlockSpec((rows, d), lambda l, j: (0, 0)),
            pl.BlockSpec((1, d, tn), lambda l, j: (l, 0, j)),
            pl.BlockSpec((1, 1, tn), lambda l, j: (l, 0, j)),
        ],
        out_specs=pl.BlockSpec((1, rows, tn), lambda l, j: (l, 0, j)),
        out_shape=jax.ShapeDtypeStruct((depth, rows, n6), F32),
        compiler_params=_cparams(("arbitrary", "arbitrary")),
        name="adaln_mod",
    )(c_all, w_ada, b_ada.reshape(depth, 1, n6))


def _modulated_norm(x, g, sc, sh):
    ms = jnp.mean(x * x, axis=-1, keepdims=True)
    return (x * lax.rsqrt(ms + EPS)) * g * (1.0 + sc) + sh


def _group_rms(t, ind, inv_size, g):
    sq = (t * t).astype(BF16)
    wb = ind.shape[0]
    ss = [_dot(sq[:, c:c + wb], ind) for c in range(0, t.shape[1], wb)]
    ss = jnp.concatenate(ss, axis=1) if len(ss) > 1 else ss[0]
    return t * lax.rsqrt(ss * inv_size + EPS) * g


def _log_sigmoid(x):
    return jnp.minimum(x, 0.0) - jnp.log(1.0 + jnp.exp(-jnp.abs(x)))


def _augmented_keys(kfn, f_cum, heads):
    lane = lax.broadcasted_iota(jnp.int32, f_cum.shape, 1)
    hi, mid, lo = (t.astype(F32) for t in _split3(f_cum * -LOG2E))
    aug_even = pltpu.roll(hi, 64, 1) + pltpu.roll(mid, 72, 1) + pltpu.roll(lo, 80, 1)
    aug_odd = hi + pltpu.roll(mid, 8, 1) + pltpu.roll(lo, 16, 1)
    blocks = []
    for h in range(heads):
        kp = kfn[:, (h // 2) * LANES:(h // 2 + 1) * LANES]
        if h % 2 == 0:
            blocks.append(jnp.where(lane < HEAD_DIM, kp, aug_even))
        else:
            blocks.append(jnp.where(lane >= HEAD_DIM, kp, aug_odd))
    return jnp.concatenate(blocks, axis=1).astype(BF16)


def _inproj_kernel(x_ref, sc_ref, sh_ref, gmix_ref, w_ref, gsgu_ref, gq_ref, gk_ref, bf_ref,
                   ind96_ref, ind64_ref, msgu_ref, bsgu_ref, *rest,
                   tm, w_sgu, w_fox, w_sb, period, sweep, layer):
    if sweep:
        prev = rest[1:1 + 4 * bool(layer)]
        (ysgu_ref, kaug_ref, qft_ref, vft_ref, qbt_ref, kb16_ref, vbt_ref,
         kft_ref, vft32_ref, kbt_ref, vbt32_ref, logft_ref, fcumt_ref, carry_ref) = rest[1 + len(prev):]
        tril_ref = rest[0]
    else:
        (ysgu_ref, qf_ref, kf16_ref, vf16_ref, qb_ref, kb16_ref, vb16_ref,
         kf32_ref, vf32_ref, kb32_ref, vb32_ref, logft_ref, sguv_ref) = rest
    x = x_ref[...]
    h = _modulated_norm(x, gmix_ref[0], sc_ref[0], sh_ref[0])
    p = _dot(h.astype(BF16), w_ref[0])

    o = 0
    u = p[:, o:o + w_sgu]; o += w_sgu
    vs = p[:, o:o + w_sgu]; o += w_sgu
    qf = p[:, o:o + w_fox]; o += w_fox
    kf = p[:, o:o + w_fox]; o += w_fox
    vf = p[:, o:o + w_fox]; o += w_fox
    qb = p[:, o:o + w_sb]; o += w_sb
    kb = p[:, o:o + w_sb]; o += w_sb
    vb = p[:, o:o + w_sb]; o += w_sb
    fl = p[:, o:o + LANES]

    scale = HEAD_DIM ** -0.5
    ind64 = ind64_ref[...]
    qfn = _group_rms(qf, ind64, 1.0 / HEAD_DIM, gq_ref[0])
    kfn = _group_rms(kf, ind64, 1.0 / HEAD_DIM, gk_ref[0])
    kb16_ref[...] = kb.astype(BF16)
    lf = _log_sigmoid(fl + bf_ref[0])
    logft_ref[...] = lf.T[0:8, :]
    if sweep:
        qft_ref[...] = (qfn * (scale * LOG2E)).T.astype(BF16)
        qbt_ref[...] = (qb * (scale * LOG2E)).T.astype(BF16)
        vf_t, vb_t = vf.T, vb.T
        vft_ref[...] = vf_t.astype(BF16)
        vbt_ref[...] = vb_t.astype(BF16)
        for dst, own, earlier in zip((kft_ref, vft32_ref, kbt_ref, vbt32_ref),
                                     (kfn.T, vf_t, kb.T, vb_t), prev or (None,) * 4):
            if earlier is not None:
                dst[0:layer] = earlier[...]
            dst[layer] = own
        @pl.when(pl.program_id(0) == 0)
        def _():
            carry_ref[...] = jnp.zeros(carry_ref.shape, F32)
        lane = lax.broadcasted_iota(jnp.int32, lf.shape, 1)
        lfh, lfm, lfl = _split3(jnp.where(lane < 8, lf, 0.0))
        tril = tril_ref[...]
        run = carry_ref[0:1, :]
        chunks = []
        for c in range(0, tm, LANES):
            part = (_dot(tril, lfh[c:c + LANES]) + _dot(tril, lfm[c:c + LANES])
                    + _dot(tril, lfl[c:c + LANES]) + run)
            run = part[LANES - 1:LANES, :]
            chunks.append(part)
        f_cum = jnp.concatenate(chunks, axis=0) if len(chunks) > 1 else chunks[0]
        carry_ref[...] = jnp.broadcast_to(run, carry_ref.shape)
        fcumt_ref[...] = f_cum.T[0:8, :]
        kaug_ref[...] = _augmented_keys(kfn, f_cum, w_fox // HEAD_DIM)
    else:
        kf32_ref[...] = kfn
        vf32_ref[...] = vf
        kb32_ref[...] = kb
        vb32_ref[...] = vb
        qf_ref[...] = (qfn * scale).astype(BF16)
        kf16_ref[...] = kfn.astype(BF16)
        vf16_ref[...] = vf.astype(BF16)
        qb_ref[...] = (qb * scale).astype(BF16)
        vb16_ref[...] = vb.astype(BF16)

    cg = w_sgu // G_SGU
    vsn = _group_rms(vs, ind96_ref[...], 1.0 / cg, gsgu_ref[0])
    if not sweep:
        sguv_ref[...] = vsn
    r = lax.broadcasted_iota(jnp.int32, (SGU_LEN, G_SGU * SGU_LEN), 0)
    c = lax.broadcasted_iota(jnp.int32, (SGU_LEN, G_SGU * SGU_LEN), 1) % SGU_LEN
    keep = (r // period == c // period) & ((c % period) // CHUNK <= (r % period) // CHUNK)
    mix = jnp.where(keep, msgu_ref[0], 0.0).astype(BF16)
    lane_group = lax.broadcasted_iota(jnp.int32, (SGU_LEN, w_sgu), 1) // cg
    vsb = vsn.astype(BF16)
    spat = []
    for ci in range(tm // SGU_LEN):
        vc = vsb[ci * SGU_LEN:(ci + 1) * SGU_LEN]
        stacked = jnp.concatenate(
            [jnp.where(lane_group == g, vc, jnp.zeros_like(vc)) for g in range(G_SGU)], axis=0)
        spat.append(_dot(mix, stacked) + bsgu_ref[0])
    spat = jnp.concatenate(spat, axis=0) if len(spat) > 1 else spat[0]
    ysgu_ref[...] = (u * spat).astype(BF16)


def _inproj(x, mod, layer, gmix, w, gsgu, gq, gk, bf, ind96, ind64, msgu, bsgu, *, tm, period,
            sweep, prev_states=()):
    n, d = x.shape
    w_sgu, w_fox = gsgu.shape[2], gq.shape[2]
    w_sb = (w.shape[2] - LANES - 2 * w_sgu - 3 * w_fox) // 3
    row = lambda width: pl.BlockSpec((tm, width), lambda i: (i, 0))
    col = lambda height: pl.BlockSpec((height, tm), lambda i: (0, i))
    sds = jax.ShapeDtypeStruct
    if sweep:
        stack = lambda width: pl.BlockSpec((layer + 1, width, tm), lambda i: (0, 0, i))
        states_specs = [stack(w_fox), stack(w_fox), stack(w_sb), stack(w_sb), col(8)]
        states_shape = [sds((layer + 1, wd, n), F32) for wd in (w_fox, w_fox, w_sb, w_sb)] \
            + [sds((8, n), F32)]
    else:
        states_specs = [row(w_fox), row(w_fox), row(w_sb), row(w_sb), col(8)]
        states_shape = [sds((n, w_fox), F32), sds((n, w_fox), F32), sds((n, w_sb), F32),
                        sds((n, w_sb), F32), sds((8, n), F32)]
    operands = [x, mod, mod, gmix, w, gsgu, gq, gk, bf, ind96, ind64, msgu, bsgu]
    lay = lambda a: _layer_spec(a, layer)
    in_specs = [row(d), _mod_spec(mod, layer, 1, tm), _mod_spec(mod, layer, 0, tm), lay(gmix), lay(w),
                lay(gsgu), lay(gq), lay(gk), lay(bf), _const_spec(ind96.shape),
                _const_spec(ind64.shape), lay(msgu), lay(bsgu)]
    scratch = []
    if sweep:
        heads = w_fox // HEAD_DIM
        a = np.arange(LANES)
        operands.append(jnp.asarray(a[None, :] <= a[:, None], dtype=BF16))
        in_specs.append(_const_spec((LANES, LANES)))
        for st in prev_states:
            operands.append(st)
            in_specs.append(pl.BlockSpec((layer, st.shape[1], tm), lambda i: (0, 0, i)))
        out_specs = [row(w_sgu), row(heads * LANES), col(w_fox), col(w_fox), col(w_sb), row(w_sb),
                     col(w_sb)] + states_specs + [col(8)]
        out_shape = [sds((n, w_sgu), BF16), sds((n, heads * LANES), BF16), sds((w_fox, n), BF16),
                     sds((w_fox, n), BF16), sds((w_sb, n), BF16), sds((n, w_sb), BF16),
                     sds((w_sb, n), BF16)] + states_shape + [sds((8, n), F32)]
        scratch = [pltpu.VMEM((8, LANES), F32)]
    else:
        out_specs = [row(w_sgu), row(w_fox), row(w_fox), row(w_fox), row(w_sb), row(w_sb),
                     row(w_sb)] + states_specs + [row(w_sgu)]
        out_shape = [sds((n, w_sgu), BF16), sds((n, w_fox), BF16), sds((n, w_fox), BF16),
                     sds((n, w_fox), BF16), sds((n, w_sb), BF16), sds((n, w_sb), BF16),
                     sds((n, w_sb), BF16)] + states_shape + [sds((n, w_sgu), F32)]
    kern = functools.partial(_inproj_kernel, tm=tm, w_sgu=w_sgu, w_fox=w_fox, w_sb=w_sb,
                             period=period, sweep=sweep, layer=layer)
    return pl.pallas_call(
        kern,
        grid=(n // tm,),
        in_specs=in_specs,
        out_specs=out_specs,
        out_shape=out_shape,
        scratch_shapes=scratch,
        compiler_params=_cparams(("arbitrary",)),
        name="inproj",
    )(*operands)


def _seq_cumsum(x, nc, reverse_exclusive):
    rows = x.shape[0]
    a = lax.broadcasted_iota(jnp.int32, (LANES, LANES), 0)
    b = lax.broadcasted_iota(jnp.int32, (LANES, LANES), 1)
    tri = (a > b) if reverse_exclusive else (a <= b)
    tri = jnp.where(tri, 1.0, 0.0).astype(BF16)
    ones = jnp.ones((LANES, LANES), BF16)
    xh, xm, xl = _split3(x)
    within = _dot(xh, tri) + _dot(xm, tri) + _dot(xl, tri)
    tot = _dot(xh, ones) + _dot(xm, ones) + _dot(xl, ones)
    ra = lax.broadcasted_iota(jnp.int32, (rows, rows), 0)
    rb = lax.broadcasted_iota(jnp.int32, (rows, rows), 1)
    other = (rb > ra) if reverse_exclusive else (rb < ra)
    blk = jnp.where((ra // nc == rb // nc) & other, 1.0, 0.0).astype(BF16)
    th, tm_, tl = _split3(tot)
    return within + _dot(blk, th) + _dot(blk, tm_) + _dot(blk, tl)


def _sample_cumsum_kernel(clf_ref, lf_ref, suf_ref, cum_ref, *, nc, dec_seq):
    suf_ref[...] = _seq_cumsum(clf_ref[...], nc, reverse_exclusive=True)
    a = lax.broadcasted_iota(jnp.int32, (LANES, LANES), 0)
    b = lax.broadcasted_iota(jnp.int32, (LANES, LANES), 1)
    tri = jnp.where((a // dec_seq == b // dec_seq) & (a <= b), 1.0, 0.0).astype(BF16)
    xh, xm, xl = _split3(lf_ref[...])
    cum_ref[...] = _dot(xh, tri) + _dot(xm, tri) + _dot(xl, tri)


def _sample_cumsum(clogf_t, logft, dec_seq):
    rows, past = clogf_t.shape
    nc = past // LANES
    suf, cum = pl.pallas_call(
        functools.partial(_sample_cumsum_kernel, nc=nc, dec_seq=dec_seq),
        out_shape=[jax.ShapeDtypeStruct((rows * nc, LANES), F32),
                   jax.ShapeDtypeStruct(logft.shape, F32)],
        compiler_params=pltpu.CompilerParams(vmem_limit_bytes=VMEM_LIMIT),
        name="sample_logf_cumsum",
    )(clogf_t.reshape(rows * nc, LANES), logft)
    return suf.reshape(rows, past), cum


def _head_query_blocks(qt_ref, heads, tq, ones_rows, lane0=0):
    rowi = lax.broadcasted_iota(jnp.int32, (LANES, tq), 0)
    out = []
    for h in range(heads):
        qp = qt_ref[(h // 2) * LANES:(h // 2 + 1) * LANES, lane0:lane0 + tq]
        own = (rowi < HEAD_DIM) if h % 2 == 0 else (rowi >= HEAD_DIM)
        fill = jnp.zeros((LANES, tq), F32)
        for r in ones_rows(h):
            fill = jnp.where(rowi == r, 1.0, fill)
        out.append(jnp.where(own, qp, fill.astype(BF16)))
    return out


def _bias_rows(h):
    base = HEAD_DIM if h % 2 == 0 else 0
    return [base + h, base + 8 + h, base + 16 + h]


def _fox_kernel(fs_ref, fe_ref, cut_ref, qt_ref, ka_ref, vt_ref, o_ref, *, tq, heads):
    i = pl.program_id(0)
    tk = tq
    cpb = tk // LANES
    qa = _head_query_blocks(qt_ref, heads, tq, _bias_rows)
    krow = lax.broadcasted_iota(jnp.int32, (tk, tq), 0)
    qcol = lax.broadcasted_iota(jnp.int32, (tk, tq), 1)

    def logits_of(j):
        k0 = pl.multiple_of(j * tk, tk)
        return tuple(_dot(ka_ref[pl.ds(k0, tk), h * LANES:(h + 1) * LANES], qa[h])
                     for h in range(heads))

    def absorb(j, logits, carry, mask):
        k0 = pl.multiple_of(j * tk, tk)
        out = []
        for h in range(heads):
            vt = vt_ref[h * HEAD_DIM:(h + 1) * HEAD_DIM, pl.ds(k0, tk)]
            m, l, acc = carry[3 * h:3 * h + 3]
            s = logits[h]
            if mask is not None:
                s = jnp.where(mask, s, NEG_BIG)
            mn = jnp.maximum(m, jnp.max(s, axis=0, keepdims=True))
            alpha = jnp.exp2(m - mn)
            pe = jnp.exp2(s - mn)
            l = alpha * l + jnp.sum(pe, axis=0, keepdims=True)
            acc = alpha * acc + _dot(vt, pe.astype(BF16))
            out += [mn, l, acc]
        return tuple(out)

    def run(blocks, carry):
        staged = [logits_of(j) for j, _ in blocks]
        for (j, mask), logits in zip(blocks, staged):
            carry = absorb(j, logits, carry, mask)
        return carry

    def live(j):
        jc = j * cpb + (cpb - 1)
        ok = fs_ref[0, i * cpb] - fe_ref[0, jc] >= cut_ref[0]
        for h in range(1, heads):
            ok = ok | (fs_ref[h, i * cpb] - fe_ref[h, jc] >= cut_ref[0])
        return ok

    init = []
    for _ in range(heads):
        init += [jnp.full((1, tq), NEG_BIG, F32), jnp.zeros((1, tq), F32),
                 jnp.zeros((HEAD_DIM, tq), F32)]
    carry = run([(i, krow <= qcol), (jnp.maximum(i - 1, 0), jnp.broadcast_to(i > 0, (tk, tq)))],
                tuple(init))

    def cond(state):
        return (state[0] >= 1) & live(jnp.maximum(state[0] - 1, 0))

    def body(state):
        j = state[0]
        return (j - 2,) + run([(j, None), (j - 1, None)], state[1:])

    state = lax.while_loop(cond, body, (i - 2,) + carry)
    j = jnp.maximum(state[0], 0)
    last = (state[0] >= 0) & live(j)
    res = lax.cond(last, lambda c: run([(j, None)], c), lambda c: c, state[1:])
    for p in range(heads // 2):
        _, l0, a0, _, l1, a1 = res[6 * p:6 * p + 6]
        ot = jnp.concatenate([a0 / l0, a1 / l1], axis=0)
        o_ref[:, p * LANES:(p + 1) * LANES] = ot.T.astype(BF16)


def _fox_prompt(fs, fe, cut, qt, kaug, vt, *, tq):
    w, n = qt.shape
    heads = w // HEAD_DIM
    grid_spec = pltpu.PrefetchScalarGridSpec(
        num_scalar_prefetch=3,
        grid=(n // tq,),
        in_specs=[
            pl.BlockSpec((w, tq), lambda i, *_: (0, i)),
            pl.BlockSpec(kaug.shape, lambda i, *_: (0, 0), pipeline_mode=pl.Buffered(1)),
            pl.BlockSpec((w, n), lambda i, *_: (0, 0), pipeline_mode=pl.Buffered(1)),
        ],
        out_specs=pl.BlockSpec((tq, w), lambda i, *_: (i, 0)),
    )
    return pl.pallas_call(
        functools.partial(_fox_kernel, tq=tq, heads=heads),
        grid_spec=grid_spec,
        out_shape=jax.ShapeDtypeStruct((n, w), BF16),
        compiler_params=_cparams(("arbitrary",)),
        name="fox_prompt",
    )(fs, fe, cut, qt, kaug, vt)


def _sb_kernel(qt_ref, k_ref, vt_ref, o_ref, *, tq, tb, heads):
    i = pl.program_id(0)
    nsub = tq // tb
    qm = [_head_query_blocks(qt_ref, heads, tb, lambda h: [], lane0=s * tb) for s in range(nsub)]
    krow = lax.broadcasted_iota(jnp.int32, (tb, tb), 0)
    qcol = lax.broadcasted_iota(jnp.int32, (tb, tb), 1)
    a = lax.broadcasted_iota(jnp.int32, (tb, 2 * tb), 0)
    b = lax.broadcasted_iota(jnp.int32, (tb, 2 * tb), 1) % tb
    tri2 = jnp.where(b > a, 1.0, 0.0).astype(BF16)

    def sweep(tasks, carry):
        k0s = [pl.multiple_of(t[1] * tb, tb) for t in tasks]
        z = [[_dot(k_ref[pl.ds(k0, tb), (h // 2) * LANES:(h // 2 + 1) * LANES], qm[t[0]][h])
              for h in range(heads)] for t, k0 in zip(tasks, k0s)]
        logsig, later, mass = {}, {}, {}
        for ti, (_, _, mask, valid) in enumerate(tasks):
            for h in range(heads):
                zz = z[ti][h]
                sp = jnp.maximum(zz, 0.0) + jnp.log2(1.0 + jnp.exp2(-jnp.abs(zz)))
                logsig[ti, h] = zz - sp
                if mask is not None:
                    sp = jnp.where(mask, sp, 0.0)
                hi, lo = _split2(sp)
                later[ti, h] = _dot(tri2, jnp.concatenate([hi, lo], axis=0))
                mass[ti, h] = jnp.sum(sp, axis=0, keepdims=True)
                if valid is not None:
                    mass[ti, h] = mass[ti, h] * valid
        carry = dict(carry)
        for ti, (s, _, mask, valid) in enumerate(tasks):
            for h in range(heads):
                cr, acc = carry[s, h]
                w = jnp.exp2(logsig[ti, h] - later[ti, h] - cr)
                if mask is not None:
                    w = jnp.where(mask, w, 0.0)
                if valid is not None:
                    w = w * valid
                vt = vt_ref[h * HEAD_DIM:(h + 1) * HEAD_DIM, pl.ds(k0s[ti], tb)]
                carry[s, h] = (cr + mass[ti, h], acc + _dot(vt, w.astype(BF16)))
        return carry

    tasks = []
    for s in range(nsub):
        qb = i * nsub + s
        tasks.append((s, qb, krow < qcol, None))
        tasks.append((s, jnp.maximum(qb - 1, 0), None, jnp.where(qb > 0, 1.0, 0.0) if s == 0 else None))
    zero = (jnp.zeros((1, tb), F32), jnp.zeros((HEAD_DIM, tb), F32))
    carry = sweep(tasks, {(s, h): zero for s in range(nsub) for h in range(heads)})

    keys = [(s, h) for s in range(nsub) for h in range(heads)]
    flat = lambda c: tuple(v for key in keys for v in c[key])

    def cond(state):
        left = state[1]
        for idx in range(1, len(keys)):
            left = jnp.minimum(left, state[1 + 2 * idx])
        return (i * nsub + nsub - 3 - state[0] >= 0) & (jnp.min(left) < PRUNE_LOG * LOG2E)

    def body(state):
        t = state[0]
        more = []
        for s in range(nsub):
            j = i * nsub + s - 2 - t
            more.append((s, jnp.maximum(j, 0), None, jnp.where(j >= 0, 1.0, 0.0)))
        c = {key: (state[1 + 2 * idx], state[2 + 2 * idx]) for idx, key in enumerate(keys)}
        return (t + 1,) + flat(sweep(more, c))

    res = lax.while_loop(cond, body, (0,) + flat(carry))[1:]
    for s in range(nsub):
        for p in range(heads // 2):
            e, o = keys.index((s, 2 * p)), keys.index((s, 2 * p + 1))
            ot = jnp.concatenate([res[2 * e + 1], res[2 * o + 1]], axis=0)
            o_ref[s * tb:(s + 1) * tb, p * LANES:(p + 1) * LANES] = ot.T.astype(BF16)


def _sb_prompt(qt, k, vt, *, tq, tb):
    w, n = qt.shape
    return pl.pallas_call(
        functools.partial(_sb_kernel, tq=tq, tb=tb, heads=w // HEAD_DIM),
        grid=(n // tq,),
        in_specs=[
            pl.BlockSpec((w, tq), lambda i: (0, i)),
            pl.BlockSpec((n, w), lambda i: (0, 0), pipeline_mode=pl.Buffered(1)),
            pl.BlockSpec((w, n), lambda i: (0, 0), pipeline_mode=pl.Buffered(1)),
        ],
        out_specs=pl.BlockSpec((tq, w), lambda i: (i, 0)),
        out_shape=jax.ShapeDtypeStruct((n, w), BF16),
        compiler_params=_cparams(("arbitrary",)),
        name="sb_prompt",
    )(qt, k, vt)


def _sb_block(z, carry, v, tri, mask, v_feature_major=False):
    lg = jnp.log(1.0 + jnp.exp(-jnp.abs(z)))
    sp = jnp.maximum(z, 0.0) + lg
    if mask is not None:
        sp = jnp.where(mask, sp, 0.0)
    hi, lo = _split2(sp)
    later = _dot(hi, tri) + _dot(lo, tri)
    a = jnp.exp((jnp.minimum(z, 0.0) - lg) - later - carry)
    if mask is not None:
        a = jnp.where(mask, a, 0.0)
    pv = _dot_nt(a.astype(BF16), v) if v_feature_major else _dot(a.astype(BF16), v)
    return pv, carry + jnp.sum(sp, axis=1, keepdims=True)


def _suffix_matrix(tk):
    a = lax.broadcasted_iota(jnp.int32, (tk, tk), 0)
    b = lax.broadcasted_iota(jnp.int32, (tk, tk), 1)
    return jnp.where(a > b, 1.0, 0.0).astype(BF16)


def _head_rows(x, heads):
    lane_head = lax.broadcasted_iota(jnp.int32, x.shape, 1) // HEAD_DIM
    return jnp.concatenate(
        [jnp.where(lane_head == h, x, jnp.zeros_like(x)) for h in range(heads)], axis=0)


def _fold_heads(o, heads, s):
    lane_head = lax.broadcasted_iota(jnp.int32, (s, o.shape[1]), 1) // HEAD_DIM
    out = jnp.zeros((s, o.shape[1]), F32)
    for h in range(heads):
        out = jnp.where(lane_head == h, o[h * s:(h + 1) * s], out)
    return out


def _sample_attn_kernel(qf_ref, kf_ref, vf_ref, ck_ref, cv_ref, suf_ref, cum_ref,
                        qb_ref, kb_ref, vb_ref, cbk_ref, cbv_ref, of_ref, ob_ref,
                        *, s, h_fox, h_sb, past):
    qa = _head_rows(qf_ref[...], h_fox)
    rows = h_fox * s
    ck = ck_ref[0, 0].astype(BF16)
    cv = cv_ref[0, 0].astype(BF16)
    suf = suf_ref[0]
    cum = cum_ref[0]
    bias_c = jnp.concatenate(
        [jnp.broadcast_to(suf[h:h + 1, :], (s, past)) for h in range(h_fox)], axis=0)
    bias_n = jnp.concatenate(
        [jnp.broadcast_to(-cum[h:h + 1, :], (s, s)) for h in range(h_fox)], axis=0)
    lc = _dot(qa, ck) + bias_c
    ln = _dot_nt(qa, kf_ref[...]) + bias_n
    r_pos = lax.broadcasted_iota(jnp.int32, (rows, s), 0) % s
    k_pos = lax.broadcasted_iota(jnp.int32, (rows, s), 1)
    ln = jnp.where(k_pos <= r_pos, ln, NEG_BIG)
    m = jnp.maximum(jnp.max(lc, axis=1, keepdims=True), jnp.max(ln, axis=1, keepdims=True))
    pc = jnp.exp(lc - m)
    pn = jnp.exp(ln - m)
    den = jnp.sum(pc, axis=1, keepdims=True) + jnp.sum(pn, axis=1, keepdims=True)
    o = (_dot_nt(pc.astype(BF16), cv) + _dot(pn.astype(BF16), vf_ref[...])) / den
    of_ref[...] = _fold_heads(o, h_fox, s).astype(BF16)

    qb = _head_rows(qb_ref[...], h_sb)
    rows_b = h_sb * s
    rb = lax.broadcasted_iota(jnp.int32, (rows_b, s), 0) % s
    cb = lax.broadcasted_iota(jnp.int32, (rows_b, s), 1)
    acc, carry = _sb_block(_dot_nt(qb, kb_ref[...]), jnp.zeros((rows_b, 1), F32), vb_ref[...],
                           _suffix_matrix(s), cb < rb)
    tri = _suffix_matrix(LANES)

    def cond(state):
        return (state[0] < past // LANES) & (jnp.min(state[2]) < PRUNE_LOG)

    def body(state):
        t, acc, carry = state
        k0 = pl.multiple_of(past - (t + 1) * LANES, LANES)
        k = cbk_ref[0, 0, :, pl.ds(k0, LANES)].astype(BF16)
        v = cbv_ref[0, 0, :, pl.ds(k0, LANES)].astype(BF16)
        pv, carry = _sb_block(_dot(qb, k), carry, v, tri, None, v_feature_major=True)
        return t + 1, acc + pv, carry

    _, acc, _ = lax.while_loop(cond, body, (0, acc, carry))
    ob_ref[...] = _fold_heads(acc, h_sb, s).astype(BF16)


def _sample_attn(qf, kf, vf, ck, cv, suf, cum, qb, kb, vb, cbk, cbv, *, layer, batch, s):
    w_fox, w_sb = qf.shape[1], qb.shape[1]
    past = ck.shape[3]
    h_fox, h_sb = w_fox // HEAD_DIM, w_sb // HEAD_DIM
    new = lambda w: pl.BlockSpec((s, w), lambda b: (b, 0))
    cache = lambda w: pl.BlockSpec((1, 1, w, past), lambda b: (layer, b, 0, 0))
    kern = functools.partial(_sample_attn_kernel, s=s, h_fox=h_fox, h_sb=h_sb, past=past)
    return pl.pallas_call(
        kern,
        grid=(batch,),
        in_specs=[new(w_fox), new(w_fox), new(w_fox), cache(w_fox), cache(w_fox),
                  pl.BlockSpec((1, 8, past), lambda b: (b, 0, 0)),
                  pl.BlockSpec((1, 8, s), lambda b: (b, 0, 0)),
                  new(w_sb), new(w_sb), new(w_sb), cache(w_sb), cache(w_sb)],
        out_specs=[new(w_fox), new(w_sb)],
        out_shape=[jax.ShapeDtypeStruct((batch * s, w_fox), BF16),
                   jax.ShapeDtypeStruct((batch * s, w_sb), BF16)],
        compiler_params=_cparams(("arbitrary",)),
        name="sample_attn",
    )(qf, kf, vf, ck, cv, suf, cum, qb, kb, vb, cbk, cbv)


def _merge_kernel(x_ref, sc_ref, sh_ref, gt_ref, gmix_ref, ysgu_ref, ofox_ref, osb_ref,
                  wg_ref, bg_ref, wbs_ref, wbf_ref, wbb_ref, wo_ref, o_ref):
    x = x_ref[...]
    d = x.shape[1]
    h = _modulated_norm(x, gmix_ref[0], sc_ref[0], sh_ref[0])
    gates = _dot(h.astype(BF16), wg_ref[0]) + bg_ref[0]
    gates = 1.0 / (1.0 + jnp.exp(-gates))
    merged = gates[:, 0:d] * _dot(ysgu_ref[...], wbs_ref[0]) \
        + gates[:, d:2 * d] * _dot(ofox_ref[...], wbf_ref[0]) \
        + gates[:, 2 * d:3 * d] * _dot(osb_ref[...], wbb_ref[0])
    o_ref[...] = x + gt_ref[0] * _dot(merged.astype(BF16), wo_ref[0])


def _merge(x, mod, layer, gmix, ysgu, ofox, osb, wg, bg, wbs, wbf, wbb, wo, *, tm):
    n, d = x.shape
    row = lambda width: pl.BlockSpec((tm, width), lambda i: (i, 0))
    lay = lambda a: _layer_spec(a, layer)
    return pl.pallas_call(
        _merge_kernel,
        grid=(n // tm,),
        in_specs=[row(d), _mod_spec(mod, layer, 1, tm), _mod_spec(mod, layer, 0, tm),
                  _mod_spec(mod, layer, 2, tm), lay(gmix),
                  row(ysgu.shape[1]), row(ofox.shape[1]), row(osb.shape[1]),
                  lay(wg), lay(bg), lay(wbs), lay(wbf), lay(wbb), lay(wo)],
        out_specs=row(d),
        out_shape=jax.ShapeDtypeStruct((n, d), F32),
        compiler_params=_cparams(("arbitrary",)),
        name="merge",
    )(x, mod, mod, mod, gmix, ysgu, ofox, osb, wg, bg, wbs, wbf, wbb, wo)


def _ffn_kernel(x_ref, sc_ref, sh_ref, gt_ref, g_ref, wi_ref, wo_ref, o_ref, *, d_ff):
    x = x_ref[...]
    h = _modulated_norm(x, g_ref[0], sc_ref[0], sh_ref[0])
    ag = _dot(h.astype(BF16), wi_ref[0])
    a = ag[:, 0:d_ff]
    act = a * (1.0 / (1.0 + jnp.exp(-a))) * ag[:, d_ff:2 * d_ff]
    o_ref[...] = x + gt_ref[0] * _dot(act.astype(BF16), wo_ref[0])


def _ffn(x, mod, layer, g, wi, wo, *, tm):
    n, d = x.shape
    d_ff = wo.shape[1]
    row = pl.BlockSpec((tm, d), lambda i: (i, 0))
    return pl.pallas_call(
        functools.partial(_ffn_kernel, d_ff=d_ff),
        grid=(n // tm,),
        in_specs=[row, _mod_spec(mod, layer, 4, tm), _mod_spec(mod, layer, 3, tm),
                  _mod_spec(mod, layer, 5, tm), _layer_spec(g, layer), _layer_spec(wi, layer),
                  _layer_spec(wo, layer)],
        out_specs=row,
        out_shape=jax.ShapeDtypeStruct((n, d), F32),
        compiler_params=_cparams(("arbitrary",)),
        name="ffn",
    )(x, mod, mod, mod, g, wi, wo)


def _indicator(width, group):
    idx = np.arange(width) // group
    return jnp.asarray(idx[:, None] == idx[None, :], dtype=BF16)


def kernel(x_prompt, x_sample, c_prompt, c_sample, cache_fox_k, cache_fox_v, cache_fox_logf,
           cache_sb_k, cache_sb_v, w_ada, b_ada, g_mix, g_ffn, w_in, g_sgu_v, w_sgu, b_sgu, b_fgt,
           g_q, g_k, w_br_sgu, w_br_fox, w_br_sb, w_gate, b_gate, w_out, w_ffn_in, w_ffn_out):
    batch, seq, d = x_prompt.shape
    dec_batch, dec_seq, _ = x_sample.shape
    depth = w_ada.shape[0]
    past = cache_fox_k.shape[2]
    h_fox, h_sb = cache_fox_k.shape[3], cache_sb_k.shape[3]
    w_fox, w_sb = h_fox * HEAD_DIM, h_sb * HEAD_DIM
    g_sgu, cg = g_sgu_v.shape[1], g_sgu_v.shape[2]
    w_sgu_ = g_sgu * cg
    assert batch == 1 and g_sgu == G_SGU and w_sgu.shape[2] == SGU_LEN
    n_dec = dec_batch * dec_seq

    n_c = batch + dec_batch
    c_rows = -(-n_c // 8) * 8
    c_all = jnp.zeros((c_rows, d), F32).at[:n_c].set(jnp.concatenate([c_prompt, c_sample], axis=0))
    mod = _modulation(c_all, w_ada, b_ada)

    offs = np.cumsum([0, w_sgu_, w_sgu_, w_fox, w_fox, w_fox, h_fox, w_sb, w_sb, w_sb]).tolist()
    f_cols = jnp.zeros((depth, d, LANES), F32).at[:, :, :h_fox].set(w_in[:, :, offs[5]:offs[6]])
    w_main = jnp.concatenate([w_in[:, :, :offs[5]], w_in[:, :, offs[6]:], f_cols],
                             axis=2).astype(BF16)
    bf_pad = jnp.zeros((depth, 1, LANES), F32).at[:, 0, :h_fox].set(b_fgt)
    gmix3, gffn3, bg3 = g_mix.reshape(depth, 1, d), g_ffn.reshape(depth, 1, d), b_gate.reshape(depth, 1, 3 * d)
    mod_p = mod[:, 0:batch]
    mod_s = jnp.repeat(mod[:, batch:batch + dec_batch], dec_seq, axis=1)
    ind96, ind64 = _indicator(w_sgu_, cg), _indicator(LANES, HEAD_DIM)
    gq_t = jnp.tile(g_q, (1, h_fox)).reshape(depth, 1, w_fox)
    gk_t = jnp.tile(g_k, (1, h_fox)).reshape(depth, 1, w_fox)
    gsgu = g_sgu_v.reshape(depth, 1, w_sgu_)
    msgu_p = jnp.transpose(w_sgu, (0, 2, 1, 3)).reshape(depth, SGU_LEN, g_sgu * SGU_LEN)
    reps = SGU_LEN // dec_seq
    w_small = jnp.tile(w_sgu[:, :, :dec_seq, :dec_seq], (1, 1, reps, reps))
    msgu_s = jnp.transpose(w_small, (0, 2, 1, 3)).reshape(depth, SGU_LEN, g_sgu * SGU_LEN)
    bsgu_p = jnp.repeat(jnp.transpose(b_sgu, (0, 2, 1)), cg, axis=2)
    bsgu_s = jnp.tile(bsgu_p[:, :dec_seq], (1, reps, 1))
    wg, wbs, wbf, wbb = (w.astype(BF16) for w in (w_gate, w_br_sgu, w_br_fox, w_br_sb))
    wo, wfi, wfo = (w.astype(BF16) for w in (w_out, w_ffn_in, w_ffn_out))

    to_fm = lambda c: jnp.transpose(c, (0, 1, 3, 4, 2)).reshape(
        depth, dec_batch, c.shape[3] * HEAD_DIM, past)
    ck_t, cv_t, cbk_t, cbv_t = (to_fm(c) for c in (cache_fox_k, cache_fox_v, cache_sb_k, cache_sb_v))

    xp = x_prompt.reshape(seq, d)
    xs = x_sample.reshape(n_dec, d)
    tm_p = min(512, seq)
    tm_f = min(512, seq)
    tq = min(256, seq)
    stacks, logf_p, st_s = (), [], []
    for l in range(depth):
        shared = (l, gmix3, w_main, gsgu, gq_t, gk_t, bf_pad, ind96, ind64)
        branch_w = (wg, bg3, wbs, wbf, wbb, wo)

        (ysgu, kaug, qft, vft, qbt, kb16, vbt, *stacks, logft, fcumt) = _inproj(
            xp, mod_p, *shared, msgu_p, bsgu_p, tm=tm_p, period=SGU_LEN, sweep=True,
            prev_states=stacks)
        bound = 1.01 * HEAD_DIM ** 0.5 * jnp.max(jnp.abs(g_q[l])) * jnp.max(jnp.abs(g_k[l]))
        cut = (-(2.0 * bound + PRUNE_LOG)).reshape(1)
        ofox = _fox_prompt(fcumt[:, 0::LANES], fcumt[:, LANES - 1::LANES], cut, qft, kaug, vft, tq=tq)
        osb = _sb_prompt(qbt, kb16, vbt, tq=min(512, seq), tb=min(128, seq))
        x1 = _merge(xp, mod_p, l, gmix3, ysgu, ofox, osb, *branch_w, tm=tm_p)
        xp = _ffn(x1, mod_p, l, gffn3, wfi, wfo, tm=tm_f)
        logf_p.append(logft[:h_fox].T.reshape(batch, seq, h_fox))

        (ysgu, qf, kf16, vf16, qb, kb16, vb16, kf32, vf32, kb32, vb32, logft, sguv) = _inproj(
            xs, mod_s, *shared, msgu_s, bsgu_s, tm=n_dec, period=dec_seq, sweep=False)
        clf = jnp.zeros((dec_batch, 8, past), F32).at[:, :h_fox].set(
            jnp.transpose(cache_fox_logf[l], (0, 2, 1)))
        suf, cum = _sample_cumsum(clf.reshape(dec_batch * 8, past), logft, dec_seq)
        cum_b = jnp.transpose(cum.reshape(8, dec_batch, dec_seq), (1, 0, 2))
        ofox, osb = _sample_attn(
            qf, kf16, vf16, ck_t, cv_t, suf.reshape(dec_batch, 8, past), cum_b,
            qb, kb16, vb16, cbk_t, cbv_t, layer=l, batch=dec_batch, s=dec_seq)
        x1 = _merge(xs, mod_s, l, gmix3, ysgu, ofox, osb, *branch_w, tm=n_dec)
        xs = _ffn(x1, mod_s, l, gffn3, wfi, wfo, tm=n_dec)
        st_s.append((kf32.reshape(dec_batch, dec_seq, h_fox, HEAD_DIM),
                     vf32.reshape(dec_batch, dec_seq, h_fox, HEAD_DIM),
                     logft[:h_fox].T.reshape(dec_batch, dec_seq, h_fox),
                     kb32.reshape(dec_batch, dec_seq, h_sb, HEAD_DIM),
                     vb32.reshape(dec_batch, dec_seq, h_sb, HEAD_DIM),
                     sguv.reshape(dec_batch, dec_seq, w_sgu_)))

    def stack(states, idx):
        return jnp.stack([s[idx] for s in states], axis=0)

    per_head = lambda st: jnp.transpose(
        st.reshape(depth, batch, st.shape[1] // HEAD_DIM, HEAD_DIM, seq), (0, 1, 4, 2, 3))
    kf_p, vf_p, kb_p, vb_p = (per_head(st) for st in stacks)
    return (xp.reshape(batch, seq, d), xs.reshape(dec_batch, dec_seq, d),
            kf_p, vf_p, jnp.stack(logf_p, axis=0), kb_p, vb_p,
            stack(st_s, 0), stack(st_s, 1), stack(st_s, 2), stack(st_s, 3), stack(st_s, 4),
            stack(st_s, 5))
```

```python
import functools
from typing import NamedTuple

import numpy as np
import jax
import jax.numpy as jnp
from jax import lax
from jax.experimental import pallas as pl
from jax.experimental.pallas import tpu as pltpu

F32 = jnp.float32
BF16 = jnp.bfloat16

EPS = 1e-6
HEAD_DIM = 64
LANES = 128
SUBLANES = 8
CHUNK = 64
SGU_LEN = 128
G_SGU = 4
NEG_BIG = -1e30
LOG2E = 1.4426950408889634

PRUNE_LOG = 30.0

VMEM_LIMIT = 56 * 1024 * 1024


def _cparams(sem):
    return pltpu.CompilerParams(dimension_semantics=sem, vmem_limit_bytes=VMEM_LIMIT)


def _const_spec(shape):
    nd = len(shape)
    return pl.BlockSpec(shape, lambda *_: (0,) * nd, pipeline_mode=pl.Buffered(1))


def _layer_spec(arr, layer):
    nd = arr.ndim
    return pl.BlockSpec((1,) + arr.shape[1:], lambda *_: (layer,) + (0,) * (nd - 1),
                        pipeline_mode=pl.Buffered(1))


def _mod_spec(mod, layer, k, tm):
    d = mod.shape[2] // 6
    if mod.shape[1] == 1:
        return pl.BlockSpec((1, 1, d), lambda i: (layer, 0, k))
    return pl.BlockSpec((1, tm, d), lambda i: (layer, i, k))


def _dot(a, b):
    return jnp.dot(a, b, preferred_element_type=F32)


def _dot_nt(a, b):
    return lax.dot_general(a, b, (((1,), (1,)), ((), ())), preferred_element_type=F32)


def _split3(x):
    h = x.astype(BF16)
    r = x - h.astype(F32)
    m = r.astype(BF16)
    l = (r - m.astype(F32)).astype(BF16)
    return h, m, l


def _split2(x):
    h = x.astype(BF16)
    l = (x - h.astype(F32)).astype(BF16)
    return h, l


def _mod_kernel(c_ref, w_ref, b_ref, o_ref):
    c = c_ref[...]
    s = c * (1.0 / (1.0 + jnp.exp(-c)))
    o_ref[0] = _dot(s.astype(BF16), w_ref[0].astype(BF16)) + b_ref[0]


def _modulation(c_all, w_ada, b_ada):
    depth, d, n6 = w_ada.shape
    rows = c_all.shape[0]
    tn = 1024
    return pl.pallas_call(
        _mod_kernel,
        grid=(depth, n6 // tn),
        in_specs=[
            pl.BlockSpec((rows, d), lambda l, j: (0, 0)),
            pl.BlockSpec((1, d, tn), lambda l, j: (l, 0, j)),
            pl.BlockSpec((1, 1, tn), lambda l, j: (l, 0, j)),
        ],
        out_specs=pl.BlockSpec((1, rows, tn), lambda l, j: (l, 0, j)),
        out_shape=jax.ShapeDtypeStruct((depth, rows, n6), F32),
        compiler_params=_cparams(("arbitrary", "arbitrary")),
        name="adaln_mod",
    )(c_all, w_ada, b_ada.reshape(depth, 1, n6))


def _modulated_norm(x, g, sc, sh):
    ms = jnp.mean(x * x, axis=-1, keepdims=True)
    return (x * lax.rsqrt(ms + EPS)) * g * (1.0 + sc) + sh


def _group_rms(t, ind, inv_size, g):
    sq = (t * t).astype(BF16)
    wb = ind.shape[0]
    ss = [_dot(sq[:, c:c + wb], ind) for c in range(0, t.shape[1], wb)]
    ss = jnp.concatenate(ss, axis=1) if len(ss) > 1 else ss[0]
    return t * lax.rsqrt(ss * inv_size + EPS) * g


def _log_sigmoid(x):
    return jnp.minimum(x, 0.0) - jnp.log(1.0 + jnp.exp(-jnp.abs(x)))


def _augmented_keys(kfn, f_cum, heads):
    lane = lax.broadcasted_iota(jnp.int32, f_cum.shape, 1)
    hi, mid, lo = (t.astype(F32) for t in _split3(f_cum * -LOG2E))
    aug_even = (pltpu.roll(hi, HEAD_DIM, 1) + pltpu.roll(mid, HEAD_DIM + SUBLANES, 1)
                + pltpu.roll(lo, HEAD_DIM + 2 * SUBLANES, 1))
    aug_odd = hi + pltpu.roll(mid, SUBLANES, 1) + pltpu.roll(lo, 2 * SUBLANES, 1)
    blocks = []
    for h in range(heads):
        kp = kfn[:, (h // 2) * LANES:(h // 2 + 1) * LANES]
        if h % 2 == 0:
            blocks.append(jnp.where(lane < HEAD_DIM, kp, aug_even))
        else:
            blocks.append(jnp.where(lane >= HEAD_DIM, kp, aug_odd))
    return jnp.concatenate(blocks, axis=1).astype(BF16)


def _inproj_kernel(x_ref, sc_ref, sh_ref, gmix_ref, w_ref, gsgu_ref, gq_ref, gk_ref, bf_ref,
                   ind96_ref, ind64_ref, msgu_ref, bsgu_ref, *rest,
                   tm, w_sgu, w_fox, w_sb, period, sweep, layer):
    if sweep:
        prev = rest[1:1 + 4 * bool(layer)]
        (ysgu_ref, kaug_ref, qft_ref, vft_ref, qbt_ref, kb16_ref, vbt_ref,
         kft_ref, vft32_ref, kbt_ref, vbt32_ref, logft_ref, fcumt_ref, carry_ref) = rest[1 + len(prev):]
        tril_ref = rest[0]
    else:
        (ysgu_ref, qf_ref, kf16_ref, vf16_ref, qb_ref, kb16_ref, vb16_ref,
         kf32_ref, vf32_ref, kb32_ref, vb32_ref, logft_ref, sguv_ref) = rest
    x = x_ref[...]
    h = _modulated_norm(x, gmix_ref[0], sc_ref[0], sh_ref[0])
    p = _dot(h.astype(BF16), w_ref[0])

    o = 0
    u = p[:, o:o + w_sgu]; o += w_sgu
    vs = p[:, o:o + w_sgu]; o += w_sgu
    qf = p[:, o:o + w_fox]; o += w_fox
    kf = p[:, o:o + w_fox]; o += w_fox
    vf = p[:, o:o + w_fox]; o += w_fox
    qb = p[:, o:o + w_sb]; o += w_sb
    kb = p[:, o:o + w_sb]; o += w_sb
    vb = p[:, o:o + w_sb]; o += w_sb
    fl = p[:, o:o + LANES]

    scale = HEAD_DIM ** -0.5
    ind64 = ind64_ref[...]
    qfn = _group_rms(qf, ind64, 1.0 / HEAD_DIM, gq_ref[0])
    kfn = _group_rms(kf, ind64, 1.0 / HEAD_DIM, gk_ref[0])
    kb16_ref[...] = kb.astype(BF16)
    lf = _log_sigmoid(fl + bf_ref[0])
    logft_ref[...] = lf.T[0:SUBLANES, :]
    if sweep:
        qft_ref[...] = (qfn * (scale * LOG2E)).T.astype(BF16)
        qbt_ref[...] = (qb * (scale * LOG2E)).T.astype(BF16)
        vf_t, vb_t = vf.T, vb.T
        vft_ref[...] = vf_t.astype(BF16)
        vbt_ref[...] = vb_t.astype(BF16)
        for dst, own, earlier in zip((kft_ref, vft32_ref, kbt_ref, vbt32_ref),
                                     (kfn.T, vf_t, kb.T, vb_t), prev or (None,) * 4):
            if earlier is not None:
                dst[0:layer] = earlier[...]
            dst[layer] = own
        @pl.when(pl.program_id(0) == 0)
        def _():
            carry_ref[...] = jnp.zeros(carry_ref.shape, F32)
        lane = lax.broadcasted_iota(jnp.int32, lf.shape, 1)
        lfh, lfm, lfl = _split3(jnp.where(lane < SUBLANES, lf, 0.0))
        tril = tril_ref[...]
        run = carry_ref[0:1, :]
        chunks = []
        for c in range(0, tm, LANES):
            part = (_dot(tril, lfh[c:c + LANES]) + _dot(tril, lfm[c:c + LANES])
                    + _dot(tril, lfl[c:c + LANES]) + run)
            run = part[LANES - 1:LANES, :]
            chunks.append(part)
        f_cum = jnp.concatenate(chunks, axis=0) if len(chunks) > 1 else chunks[0]
        carry_ref[...] = jnp.broadcast_to(run, carry_ref.shape)
        fcumt_ref[...] = f_cum.T[0:SUBLANES, :]
        kaug_ref[...] = _augmented_keys(kfn, f_cum, w_fox // HEAD_DIM)
    else:
        kf32_ref[...] = kfn
        vf32_ref[...] = vf
        kb32_ref[...] = kb
        vb32_ref[...] = vb
        qf_ref[...] = (qfn * scale).astype(BF16)
        kf16_ref[...] = kfn.astype(BF16)
        vf16_ref[...] = vf.astype(BF16)
        qb_ref[...] = (qb * scale).astype(BF16)
        vb16_ref[...] = vb.astype(BF16)

    cg = w_sgu // G_SGU
    vsn = _group_rms(vs, ind96_ref[...], 1.0 / cg, gsgu_ref[0])
    if not sweep:
        sguv_ref[...] = vsn
    r = lax.broadcasted_iota(jnp.int32, (SGU_LEN, G_SGU * SGU_LEN), 0)
    c = lax.broadcasted_iota(jnp.int32, (SGU_LEN, G_SGU * SGU_LEN), 1) % SGU_LEN
    keep = (r // period == c // period) & ((c % period) // CHUNK <= (r % period) // CHUNK)
    mix = jnp.where(keep, msgu_ref[0], 0.0).astype(BF16)
    lane_group = lax.broadcasted_iota(jnp.int32, (SGU_LEN, w_sgu), 1) // cg
    vsb = vsn.astype(BF16)
    spat = []
    for ci in range(tm // SGU_LEN):
        vc = vsb[ci * SGU_LEN:(ci + 1) * SGU_LEN]
        stacked = jnp.concatenate(
            [jnp.where(lane_group == g, vc, jnp.zeros_like(vc)) for g in range(G_SGU)], axis=0)
        spat.append(_dot(mix, stacked) + bsgu_ref[0])
    spat = jnp.concatenate(spat, axis=0) if len(spat) > 1 else spat[0]
    ysgu_ref[...] = (u * spat).astype(BF16)


def _inproj(x, mod, layer, gmix, w, gsgu, gq, gk, bf, ind96, ind64, msgu, bsgu, *, tm, period,
            sweep, prev_states=()):
    n, d = x.shape
    w_sgu, w_fox = gsgu.shape[2], gq.shape[2]
    w_sb = (w.shape[2] - LANES - 2 * w_sgu - 3 * w_fox) // 3
    row = lambda width: pl.BlockSpec((tm, width), lambda i: (i, 0))
    col = lambda height: pl.BlockSpec((height, tm), lambda i: (0, i))
    sds = jax.ShapeDtypeStruct
    if sweep:
        stack = lambda width: pl.BlockSpec((layer + 1, width, tm), lambda i: (0, 0, i))
        states_specs = [stack(w_fox), stack(w_fox), stack(w_sb), stack(w_sb), col(SUBLANES)]
        states_shape = [sds((layer + 1, wd, n), F32) for wd in (w_fox, w_fox, w_sb, w_sb)] \
            + [sds((SUBLANES, n), F32)]
    else:
        states_specs = [row(w_fox), row(w_fox), row(w_sb), row(w_sb), col(SUBLANES)]
        states_shape = [sds((n, w_fox), F32), sds((n, w_fox), F32), sds((n, w_sb), F32),
                        sds((n, w_sb), F32), sds((SUBLANES, n), F32)]
    operands = [x, mod, mod, gmix, w, gsgu, gq, gk, bf, ind96, ind64, msgu, bsgu]
    lay = lambda a: _layer_spec(a, layer)
    in_specs = [row(d), _mod_spec(mod, layer, 1, tm), _mod_spec(mod, layer, 0, tm), lay(gmix), lay(w),
                lay(gsgu), lay(gq), lay(gk), lay(bf), _const_spec(ind96.shape),
                _const_spec(ind64.shape), lay(msgu), lay(bsgu)]
    scratch = []
    if sweep:
        heads = w_fox // HEAD_DIM
        a = np.arange(LANES)
        operands.append(jnp.asarray(a[None, :] <= a[:, None], dtype=BF16))
        in_specs.append(_const_spec((LANES, LANES)))
        for st in prev_states:
            operands.append(st)
            in_specs.append(pl.BlockSpec((layer, st.shape[1], tm), lambda i: (0, 0, i)))
        out_specs = [row(w_sgu), row(heads * LANES), col(w_fox), col(w_fox), col(w_sb), row(w_sb),
                     col(w_sb)] + states_specs + [col(SUBLANES)]
        out_shape = [sds((n, w_sgu), BF16), sds((n, heads * LANES), BF16), sds((w_fox, n), BF16),
                     sds((w_fox, n), BF16), sds((w_sb, n), BF16), sds((n, w_sb), BF16),
                     sds((w_sb, n), BF16)] + states_shape + [sds((SUBLANES, n), F32)]
        scratch = [pltpu.VMEM((SUBLANES, LANES), F32)]
    else:
        out_specs = [row(w_sgu), row(w_fox), row(w_fox), row(w_fox), row(w_sb), row(w_sb),
                     row(w_sb)] + states_specs + [row(w_sgu)]
        out_shape = [sds((n, w_sgu), BF16), sds((n, w_fox), BF16), sds((n, w_fox), BF16),
                     sds((n, w_fox), BF16), sds((n, w_sb), BF16), sds((n, w_sb), BF16),
                     sds((n, w_sb), BF16)] + states_shape + [sds((n, w_sgu), F32)]
    kern = functools.partial(_inproj_kernel, tm=tm, w_sgu=w_sgu, w_fox=w_fox, w_sb=w_sb,
                             period=period, sweep=sweep, layer=layer)
    return pl.pallas_call(
        kern,
        grid=(n // tm,),
        in_specs=in_specs,
        out_specs=out_specs,
        out_shape=out_shape,
        scratch_shapes=scratch,
        compiler_params=_cparams(("arbitrary",)),
        name="inproj",
    )(*operands)


def _seq_cumsum(x, nc, reverse_exclusive):
    rows = x.shape[0]
    a = lax.broadcasted_iota(jnp.int32, (LANES, LANES), 0)
    b = lax.broadcasted_iota(jnp.int32, (LANES, LANES), 1)
    tri = (a > b) if reverse_exclusive else (a <= b)
    tri = jnp.where(tri, 1.0, 0.0).astype(BF16)
    ones = jnp.ones((LANES, LANES), BF16)
    xh, xm, xl = _split3(x)
    within = _dot(xh, tri) + _dot(xm, tri) + _dot(xl, tri)
    tot = _dot(xh, ones) + _dot(xm, ones) + _dot(xl, ones)
    ra = lax.broadcasted_iota(jnp.int32, (rows, rows), 0)
    rb = lax.broadcasted_iota(jnp.int32, (rows, rows), 1)
    other = (rb > ra) if reverse_exclusive else (rb < ra)
    blk = jnp.where((ra // nc == rb // nc) & other, 1.0, 0.0).astype(BF16)
    th, tm_, tl = _split3(tot)
    return within + _dot(blk, th) + _dot(blk, tm_) + _dot(blk, tl)


def _sample_cumsum_kernel(clf_ref, lf_ref, suf_ref, cum_ref, *, nc, dec_seq):
    suf_ref[...] = _seq_cumsum(clf_ref[...], nc, reverse_exclusive=True)
    a = lax.broadcasted_iota(jnp.int32, (LANES, LANES), 0)
    b = lax.broadcasted_iota(jnp.int32, (LANES, LANES), 1)
    tri = jnp.where((a // dec_seq == b // dec_seq) & (a <= b), 1.0, 0.0).astype(BF16)
    xh, xm, xl = _split3(lf_ref[...])
    cum_ref[...] = _dot(xh, tri) + _dot(xm, tri) + _dot(xl, tri)


def _sample_cumsum(clogf_t, logft, dec_seq):
    rows, past = clogf_t.shape
    nc = past // LANES
    suf, cum = pl.pallas_call(
        functools.partial(_sample_cumsum_kernel, nc=nc, dec_seq=dec_seq),
        out_shape=[jax.ShapeDtypeStruct((rows * nc, LANES), F32),
                   jax.ShapeDtypeStruct(logft.shape, F32)],
        compiler_params=pltpu.CompilerParams(vmem_limit_bytes=VMEM_LIMIT),
        name="sample_logf_cumsum",
    )(clogf_t.reshape(rows * nc, LANES), logft)
    return suf.reshape(rows, past), cum


def _head_query_blocks(qt_ref, heads, tq, ones_rows, lane0=0):
    rowi = lax.broadcasted_iota(jnp.int32, (LANES, tq), 0)
    out = []
    for h in range(heads):
        qp = qt_ref[(h // 2) * LANES:(h // 2 + 1) * LANES, lane0:lane0 + tq]
        own = (rowi < HEAD_DIM) if h % 2 == 0 else (rowi >= HEAD_DIM)
        fill = jnp.zeros((LANES, tq), F32)
        for r in ones_rows(h):
            fill = jnp.where(rowi == r, 1.0, fill)
        out.append(jnp.where(own, qp, fill.astype(BF16)))
    return out


def _bias_rows(h):
    base = HEAD_DIM if h % 2 == 0 else 0
    return [base + h, base + SUBLANES + h, base + 2 * SUBLANES + h]


def _fox_kernel(fs_ref, fe_ref, cut_ref, qt_ref, ka_ref, vt_ref, *rest, tq, heads):
    n_cast = len(rest) // 2
    o_ref = rest[n_cast]
    for src_ref, dst_ref in zip(rest[:n_cast], rest[n_cast + 1:]):
        dst_ref[...] = src_ref[...].astype(BF16)
    i = pl.program_id(0)
    tk = tq
    cpb = tk // LANES
    qa = _head_query_blocks(qt_ref, heads, tq, _bias_rows)
    krow = lax.broadcasted_iota(jnp.int32, (tk, tq), 0)
    qcol = lax.broadcasted_iota(jnp.int32, (tk, tq), 1)

    def logits_of(j):
        k0 = pl.multiple_of(j * tk, tk)
        return tuple(_dot(ka_ref[pl.ds(k0, tk), h * LANES:(h + 1) * LANES], qa[h])
                     for h in range(heads))

    def absorb(j, logits, carry, mask):
        k0 = pl.multiple_of(j * tk, tk)
        out = []
        for h in range(heads):
            vt = vt_ref[h * HEAD_DIM:(h + 1) * HEAD_DIM, pl.ds(k0, tk)]
            m, l, acc = carry[3 * h:3 * h + 3]
            s = logits[h]
            if mask is not None:
                s = jnp.where(mask, s, NEG_BIG)
            mn = jnp.maximum(m, jnp.max(s, axis=0, keepdims=True))
            alpha = jnp.exp2(m - mn)
            pe = jnp.exp2(s - mn)
            l = alpha * l + jnp.sum(pe, axis=0, keepdims=True)
            acc = alpha * acc + _dot(vt, pe.astype(BF16))
            out += [mn, l, acc]
        return tuple(out)

    def run(blocks, carry):
        staged = [logits_of(j) for j, _ in blocks]
        for (j, mask), logits in zip(blocks, staged):
            carry = absorb(j, logits, carry, mask)
        return carry

    def live(j):
        jc = j * cpb + (cpb - 1)
        ok = fs_ref[0, i * cpb] - fe_ref[0, jc] >= cut_ref[0]
        for h in range(1, heads):
            ok = ok | (fs_ref[h, i * cpb] - fe_ref[h, jc] >= cut_ref[0])
        return ok

    init = []
    for _ in range(heads):
        init += [jnp.full((1, tq), NEG_BIG, F32), jnp.zeros((1, tq), F32),
                 jnp.zeros((HEAD_DIM, tq), F32)]
    carry = run([(i, krow <= qcol), (jnp.maximum(i - 1, 0), jnp.broadcast_to(i > 0, (tk, tq)))],
                tuple(init))

    def cond(state):
        return (state[0] >= 1) & live(jnp.maximum(state[0] - 1, 0))

    def body(state):
        j = state[0]
        return (j - 2,) + run([(j, None), (j - 1, None)], state[1:])

    state = lax.while_loop(cond, body, (i - 2,) + carry)
    j = jnp.maximum(state[0], 0)
    last = (state[0] >= 0) & live(j)
    res = lax.cond(last, lambda c: run([(j, None)], c), lambda c: c, state[1:])
    for p in range(heads // 2):
        _, l0, a0, _, l1, a1 = res[6 * p:6 * p + 6]
        ot = jnp.concatenate([a0 / l0, a1 / l1], axis=0)
        o_ref[:, p * LANES:(p + 1) * LANES] = ot.T.astype(BF16)


def _cast_spec(rows, cols, steps):
    rb = next(r for r in range(2 * SUBLANES, rows + 1, 2 * SUBLANES)
              if rows % r == 0 and rows // r <= steps)
    last = rows // rb - 1
    return pl.BlockSpec((rb, cols), lambda i, *_: (jnp.minimum(i, last), 0))


def _fox_prompt(fs, fe, cut, qt, kaug, vt, *, tq, cast=()):
    w, n = qt.shape
    heads = w // HEAD_DIM
    steps = n // tq
    cast_specs = [_cast_spec(a.shape[0], a.shape[1], steps) for a in cast]
    grid_spec = pltpu.PrefetchScalarGridSpec(
        num_scalar_prefetch=3,
        grid=(steps,),
        in_specs=[
            pl.BlockSpec((w, tq), lambda i, *_: (0, i)),
            pl.BlockSpec(kaug.shape, lambda i, *_: (0, 0), pipeline_mode=pl.Buffered(1)),
            pl.BlockSpec((w, n), lambda i, *_: (0, 0), pipeline_mode=pl.Buffered(1)),
        ] + cast_specs,
        out_specs=[pl.BlockSpec((tq, w), lambda i, *_: (i, 0))] + cast_specs,
    )
    return pl.pallas_call(
        functools.partial(_fox_kernel, tq=tq, heads=heads),
        grid_spec=grid_spec,
        out_shape=[jax.ShapeDtypeStruct((n, w), BF16)]
        + [jax.ShapeDtypeStruct(a.shape, BF16) for a in cast],
        compiler_params=_cparams(("arbitrary",)),
        name="fox_prompt",
    )(fs, fe, cut, qt, kaug, vt, *cast)


def _sb_kernel(qt_ref, k_ref, vt_ref, o_ref, *, tq, tb, heads):
    i = pl.program_id(0)
    nsub = tq // tb
    qm = [_head_query_blocks(qt_ref, heads, tb, lambda h: [], lane0=s * tb) for s in range(nsub)]
    krow = lax.broadcasted_iota(jnp.int32, (tb, tb), 0)
    qcol = lax.broadcasted_iota(jnp.int32, (tb, tb), 1)
    a = lax.broadcasted_iota(jnp.int32, (tb, 2 * tb), 0)
    b = lax.broadcasted_iota(jnp.int32, (tb, 2 * tb), 1) % tb
    tri2 = jnp.where(b > a, 1.0, 0.0).astype(BF16)

    def sweep(tasks, carry):
        k0s = [pl.multiple_of(t[1] * tb, tb) for t in tasks]
        z = [[_dot(k_ref[pl.ds(k0, tb), (h // 2) * LANES:(h // 2 + 1) * LANES], qm[t[0]][h])
              for h in range(heads)] for t, k0 in zip(tasks, k0s)]
        logsig, later, mass = {}, {}, {}
        for ti, (_, _, mask, valid) in enumerate(tasks):
            for h in range(heads):
                zz = z[ti][h]
                sp = jnp.maximum(zz, 0.0) + jnp.log2(1.0 + jnp.exp2(-jnp.abs(zz)))
                logsig[ti, h] = zz - sp
                if mask is not None:
                    sp = jnp.where(mask, sp, 0.0)
                hi, lo = _split2(sp)
                later[ti, h] = _dot(tri2, jnp.concatenate([hi, lo], axis=0))
                mass[ti, h] = jnp.sum(sp, axis=0, keepdims=True)
                if valid is not None:
                    mass[ti, h] = mass[ti, h] * valid
        carry = dict(carry)
        for ti, (s, _, mask, valid) in enumerate(tasks):
            for h in range(heads):
                cr, acc = carry[s, h]
                w = jnp.exp2(logsig[ti, h] - later[ti, h] - cr)
                if mask is not None:
                    w = jnp.where(mask, w, 0.0)
                if valid is not None:
                    w = w * valid
                vt = vt_ref[h * HEAD_DIM:(h + 1) * HEAD_DIM, pl.ds(k0s[ti], tb)]
                carry[s, h] = (cr + mass[ti, h], acc + _dot(vt, w.astype(BF16)))
        return carry

    tasks = []
    for s in range(nsub):
        qb = i * nsub + s
        tasks.append((s, qb, krow < qcol, None))
        tasks.append((s, jnp.maximum(qb - 1, 0), None, jnp.where(qb > 0, 1.0, 0.0) if s == 0 else None))
    zero = (jnp.zeros((1, tb), F32), jnp.zeros((HEAD_DIM, tb), F32))
    carry = sweep(tasks, {(s, h): zero for s in range(nsub) for h in range(heads)})

    keys = [(s, h) for s in range(nsub) for h in range(heads)]
    flat = lambda c: tuple(v for key in keys for v in c[key])

    def cond(state):
        left = state[1]
        for idx in range(1, len(keys)):
            left = jnp.minimum(left, state[1 + 2 * idx])
        return (i * nsub + nsub - 3 - state[0] >= 0) & (jnp.min(left) < PRUNE_LOG * LOG2E)

    def body(state):
        t = state[0]
        more = []
        for s in range(nsub):
            j = i * nsub + s - 2 - t
            more.append((s, jnp.maximum(j, 0), None, jnp.where(j >= 0, 1.0, 0.0)))
        c = {key: (state[1 + 2 * idx], state[2 + 2 * idx]) for idx, key in enumerate(keys)}
        return (t + 1,) + flat(sweep(more, c))

    res = lax.while_loop(cond, body, (0,) + flat(carry))[1:]
    for s in range(nsub):
        for p in range(heads // 2):
            e, o = keys.index((s, 2 * p)), keys.index((s, 2 * p + 1))
            ot = jnp.concatenate([res[2 * e + 1], res[2 * o + 1]], axis=0)
            o_ref[s * tb:(s + 1) * tb, p * LANES:(p + 1) * LANES] = ot.T.astype(BF16)


def _sb_prompt(qt, k, vt, *, tq, tb):
    w, n = qt.shape
    return pl.pallas_call(
        functools.partial(_sb_kernel, tq=tq, tb=tb, heads=w // HEAD_DIM),
        grid=(n // tq,),
        in_specs=[
            pl.BlockSpec((w, tq), lambda i: (0, i)),
            pl.BlockSpec((n, w), lambda i: (0, 0), pipeline_mode=pl.Buffered(1)),
            pl.BlockSpec((w, n), lambda i: (0, 0), pipeline_mode=pl.Buffered(1)),
        ],
        out_specs=pl.BlockSpec((tq, w), lambda i: (i, 0)),
        out_shape=jax.ShapeDtypeStruct((n, w), BF16),
        compiler_params=_cparams(("arbitrary",)),
        name="sb_prompt",
    )(qt, k, vt)


def _sb_block(z, carry, v, tri, mask, v_feature_major=False):
    lg = jnp.log(1.0 + jnp.exp(-jnp.abs(z)))
    sp = jnp.maximum(z, 0.0) + lg
    if mask is not None:
        sp = jnp.where(mask, sp, 0.0)
    hi, lo = _split2(sp)
    later = _dot(hi, tri) + _dot(lo, tri)
    a = jnp.exp((jnp.minimum(z, 0.0) - lg) - later - carry)
    if mask is not None:
        a = jnp.where(mask, a, 0.0)
    pv = _dot_nt(a.astype(BF16), v) if v_feature_major else _dot(a.astype(BF16), v)
    return pv, carry + jnp.sum(sp, axis=1, keepdims=True)


def _suffix_matrix(tk):
    a = lax.broadcasted_iota(jnp.int32, (tk, tk), 0)
    b = lax.broadcasted_iota(jnp.int32, (tk, tk), 1)
    return jnp.where(a > b, 1.0, 0.0).astype(BF16)


def _head_rows(x, heads):
    lane_head = lax.broadcasted_iota(jnp.int32, x.shape, 1) // HEAD_DIM
    return jnp.concatenate(
        [jnp.where(lane_head == h, x, jnp.zeros_like(x)) for h in range(heads)], axis=0)


def _fold_heads(o, heads, s):
    lane_head = lax.broadcasted_iota(jnp.int32, (s, o.shape[1]), 1) // HEAD_DIM
    out = jnp.zeros((s, o.shape[1]), F32)
    for h in range(heads):
        out = jnp.where(lane_head == h, o[h * s:(h + 1) * s], out)
    return out


def _sample_attn_kernel(qf_ref, kf_ref, vf_ref, ck_ref, cv_ref, suf_ref, cum_ref,
                        qb_ref, kb_ref, vb_ref, cbk_ref, cbv_ref, of_ref, ob_ref,
                        *, s, h_fox, h_sb, past):
    qa = _head_rows(qf_ref[...], h_fox)
    rows = h_fox * s
    ck = ck_ref[0, 0].astype(BF16)
    cv = cv_ref[0, 0].astype(BF16)
    suf = suf_ref[0]
    cum = cum_ref[0]
    bias_c = jnp.concatenate(
        [jnp.broadcast_to(suf[h:h + 1, :], (s, past)) for h in range(h_fox)], axis=0)
    bias_n = jnp.concatenate(
        [jnp.broadcast_to(-cum[h:h + 1, :], (s, s)) for h in range(h_fox)], axis=0)
    lc = _dot(qa, ck) + bias_c
    ln = _dot_nt(qa, kf_ref[...]) + bias_n
    r_pos = lax.broadcasted_iota(jnp.int32, (rows, s), 0) % s
    k_pos = lax.broadcasted_iota(jnp.int32, (rows, s), 1)
    ln = jnp.where(k_pos <= r_pos, ln, NEG_BIG)
    m = jnp.maximum(jnp.max(lc, axis=1, keepdims=True), jnp.max(ln, axis=1, keepdims=True))
    pc = jnp.exp(lc - m)
    pn = jnp.exp(ln - m)
    den = jnp.sum(pc, axis=1, keepdims=True) + jnp.sum(pn, axis=1, keepdims=True)
    o = (_dot_nt(pc.astype(BF16), cv) + _dot(pn.astype(BF16), vf_ref[...])) / den
    of_ref[...] = _fold_heads(o, h_fox, s).astype(BF16)

    qb = _head_rows(qb_ref[...], h_sb)
    rows_b = h_sb * s
    rb = lax.broadcasted_iota(jnp.int32, (rows_b, s), 0) % s
    cb = lax.broadcasted_iota(jnp.int32, (rows_b, s), 1)
    acc, carry = _sb_block(_dot_nt(qb, kb_ref[...]), jnp.zeros((rows_b, 1), F32), vb_ref[...],
                           _suffix_matrix(s), cb < rb)
    tri = _suffix_matrix(LANES)

    def cond(state):
        return (state[0] < past // LANES) & (jnp.min(state[2]) < PRUNE_LOG)

    def body(state):
        t, acc, carry = state
        k0 = pl.multiple_of(past - (t + 1) * LANES, LANES)
        k = cbk_ref[0, 0, :, pl.ds(k0, LANES)].astype(BF16)
        v = cbv_ref[0, 0, :, pl.ds(k0, LANES)].astype(BF16)
        pv, carry = _sb_block(_dot(qb, k), carry, v, tri, None, v_feature_major=True)
        return t + 1, acc + pv, carry

    _, acc, _ = lax.while_loop(cond, body, (0, acc, carry))
    ob_ref[...] = _fold_heads(acc, h_sb, s).astype(BF16)


def _sample_attn(qf, kf, vf, ck, cv, suf, cum, qb, kb, vb, cbk, cbv, *, layer, batch, s):
    w_fox, w_sb = qf.shape[1], qb.shape[1]
    past = ck.shape[3]
    h_fox, h_sb = w_fox // HEAD_DIM, w_sb // HEAD_DIM
    new = lambda w: pl.BlockSpec((s, w), lambda b: (b, 0))
    cache = lambda w: pl.BlockSpec((1, 1, w, past), lambda b: (layer, b, 0, 0))
    kern = functools.partial(_sample_attn_kernel, s=s, h_fox=h_fox, h_sb=h_sb, past=past)
    return pl.pallas_call(
        kern,
        grid=(batch,),
        in_specs=[new(w_fox), new(w_fox), new(w_fox), cache(w_fox), cache(w_fox),
                  pl.BlockSpec((1, SUBLANES, past), lambda b: (b, 0, 0)),
                  pl.BlockSpec((1, SUBLANES, s), lambda b: (b, 0, 0)),
                  new(w_sb), new(w_sb), new(w_sb), cache(w_sb), cache(w_sb)],
        out_specs=[new(w_fox), new(w_sb)],
        out_shape=[jax.ShapeDtypeStruct((batch * s, w_fox), BF16),
                   jax.ShapeDtypeStruct((batch * s, w_sb), BF16)],
        compiler_params=_cparams(("arbitrary",)),
        name="sample_attn",
    )(qf, kf, vf, ck, cv, suf, cum, qb, kb, vb, cbk, cbv)


def _merge_kernel(x_ref, sc_ref, sh_ref, gt_ref, gmix_ref, ysgu_ref, ofox_ref, osb_ref,
                  wg_ref, bg_ref, wbs_ref, wbf_ref, wbb_ref, wo_ref, o_ref):
    x = x_ref[...]
    d = x.shape[1]
    h = _modulated_norm(x, gmix_ref[0], sc_ref[0], sh_ref[0])
    gates = _dot(h.astype(BF16), wg_ref[0]) + bg_ref[0]
    gates = 1.0 / (1.0 + jnp.exp(-gates))
    merged = gates[:, 0:d] * _dot(ysgu_ref[...], wbs_ref[0]) \
        + gates[:, d:2 * d] * _dot(ofox_ref[...], wbf_ref[0]) \
        + gates[:, 2 * d:3 * d] * _dot(osb_ref[...], wbb_ref[0])
    o_ref[...] = x + gt_ref[0] * _dot(merged.astype(BF16), wo_ref[0])


def _merge(x, mod, layer, gmix, ysgu, ofox, osb, wg, bg, wbs, wbf, wbb, wo, *, tm):
    n, d = x.shape
    row = lambda width: pl.BlockSpec((tm, width), lambda i: (i, 0))
    lay = lambda a: _layer_spec(a, layer)
    return pl.pallas_call(
        _merge_kernel,
        grid=(n // tm,),
        in_specs=[row(d), _mod_spec(mod, layer, 1, tm), _mod_spec(mod, layer, 0, tm),
                  _mod_spec(mod, layer, 2, tm), lay(gmix),
                  row(ysgu.shape[1]), row(ofox.shape[1]), row(osb.shape[1]),
                  lay(wg), lay(bg), lay(wbs), lay(wbf), lay(wbb), lay(wo)],
        out_specs=row(d),
        out_shape=jax.ShapeDtypeStruct((n, d), F32),
        compiler_params=_cparams(("arbitrary",)),
        name="merge",
    )(x, mod, mod, mod, gmix, ysgu, ofox, osb, wg, bg, wbs, wbf, wbb, wo)


def _ffn_kernel(x_ref, sc_ref, sh_ref, gt_ref, g_ref, wi_ref, wo_ref, o_ref, *, d_ff):
    x = x_ref[...]
    h = _modulated_norm(x, g_ref[0], sc_ref[0], sh_ref[0])
    ag = _dot(h.astype(BF16), wi_ref[0])
    a = ag[:, 0:d_ff]
    act = a * (1.0 / (1.0 + jnp.exp(-a))) * ag[:, d_ff:2 * d_ff]
    o_ref[...] = x + gt_ref[0] * _dot(act.astype(BF16), wo_ref[0])


def _ffn(x, mod, layer, g, wi, wo, *, tm):
    n, d = x.shape
    d_ff = wo.shape[1]
    row = pl.BlockSpec((tm, d), lambda i: (i, 0))
    return pl.pallas_call(
        functools.partial(_ffn_kernel, d_ff=d_ff),
        grid=(n // tm,),
        in_specs=[row, _mod_spec(mod, layer, 4, tm), _mod_spec(mod, layer, 3, tm),
                  _mod_spec(mod, layer, 5, tm), _layer_spec(g, layer), _layer_spec(wi, layer),
                  _layer_spec(wo, layer)],
        out_specs=row,
        out_shape=jax.ShapeDtypeStruct((n, d), F32),
        compiler_params=_cparams(("arbitrary",)),
        name="ffn",
    )(x, mod, mod, mod, g, wi, wo)


class _Tiles(NamedTuple):
    dense: int
    ffn: int
    fox: int
    sb: int
    sb_sub: int


def _prompt_tiles(seq):
    return _Tiles(dense=min(512, seq), ffn=min(512, seq), fox=min(256, seq), sb=min(512, seq),
                  sb_sub=min(128, seq))


def _indicator(width, group):
    idx = np.arange(width) // group
    return jnp.asarray(idx[:, None] == idx[None, :], dtype=BF16)


def kernel(x_prompt, x_sample, c_prompt, c_sample, cache_fox_k, cache_fox_v, cache_fox_logf,
           cache_sb_k, cache_sb_v, w_ada, b_ada, g_mix, g_ffn, w_in, g_sgu_v, w_sgu, b_sgu, b_fgt,
           g_q, g_k, w_br_sgu, w_br_fox, w_br_sb, w_gate, b_gate, w_out, w_ffn_in, w_ffn_out):
    batch, seq, d = x_prompt.shape
    dec_batch, dec_seq, _ = x_sample.shape
    depth = w_ada.shape[0]
    past = cache_fox_k.shape[2]
    h_fox, h_sb = cache_fox_k.shape[3], cache_sb_k.shape[3]
    w_fox, w_sb = h_fox * HEAD_DIM, h_sb * HEAD_DIM
    g_sgu, cg = g_sgu_v.shape[1], g_sgu_v.shape[2]
    w_sgu_ = g_sgu * cg
    assert batch == 1 and g_sgu == G_SGU and w_sgu.shape[2] == SGU_LEN
    n_dec = dec_batch * dec_seq

    n_c = batch + dec_batch
    c_rows = -(-n_c // 8) * 8
    c_all = jnp.zeros((c_rows, d), F32).at[:n_c].set(jnp.concatenate([c_prompt, c_sample], axis=0))
    mod = _modulation(c_all, w_ada, b_ada)

    offs = np.cumsum([0, w_sgu_, w_sgu_, w_fox, w_fox, w_fox, h_fox, w_sb, w_sb, w_sb]).tolist()
    f_cols = jnp.zeros((depth, d, LANES), F32).at[:, :, :h_fox].set(w_in[:, :, offs[5]:offs[6]])
    w_main = jnp.concatenate([w_in[:, :, :offs[5]], w_in[:, :, offs[6]:], f_cols],
                             axis=2).astype(BF16)
    bf_pad = jnp.zeros((depth, 1, LANES), F32).at[:, 0, :h_fox].set(b_fgt)
    gmix3, gffn3, bg3 = g_mix.reshape(depth, 1, d), g_ffn.reshape(depth, 1, d), b_gate.reshape(depth, 1, 3 * d)
    mod_p = mod[:, 0:batch]
    mod_s = jnp.repeat(mod[:, batch:batch + dec_batch], dec_seq, axis=1)
    ind96, ind64 = _indicator(w_sgu_, cg), _indicator(LANES, HEAD_DIM)
    gq_t = jnp.tile(g_q, (1, h_fox)).reshape(depth, 1, w_fox)
    gk_t = jnp.tile(g_k, (1, h_fox)).reshape(depth, 1, w_fox)
    gsgu = g_sgu_v.reshape(depth, 1, w_sgu_)
    msgu_p = jnp.transpose(w_sgu, (0, 2, 1, 3)).reshape(depth, SGU_LEN, g_sgu * SGU_LEN)
    reps = SGU_LEN // dec_seq
    w_small = jnp.tile(w_sgu[:, :, :dec_seq, :dec_seq], (1, 1, reps, reps))
    msgu_s = jnp.transpose(w_small, (0, 2, 1, 3)).reshape(depth, SGU_LEN, g_sgu * SGU_LEN)
    bsgu_p = jnp.repeat(jnp.transpose(b_sgu, (0, 2, 1)), cg, axis=2)
    bsgu_s = jnp.tile(bsgu_p[:, :dec_seq], (1, reps, 1))
    dense_w = (w_gate, w_br_sgu, w_br_fox, w_br_sb, w_out, w_ffn_in, w_ffn_out)

    to_fm = lambda c: jnp.transpose(c, (0, 1, 3, 4, 2)).reshape(
        depth, dec_batch, c.shape[3] * HEAD_DIM, past)
    ck_t, cv_t, cbk_t, cbv_t = (to_fm(c) for c in (cache_fox_k, cache_fox_v, cache_sb_k, cache_sb_v))

    xp = x_prompt.reshape(seq, d)
    xs = x_sample.reshape(n_dec, d)
    tiles = _prompt_tiles(seq)
    stacks, logf_p, st_s = (), [], []
    for l in range(depth):
        shared = (l, gmix3, w_main, gsgu, gq_t, gk_t, bf_pad, ind96, ind64)

        (ysgu, kaug, qft, vft, qbt, kb16, vbt, *stacks, logft, fcumt) = _inproj(
            xp, mod_p, *shared, msgu_p, bsgu_p, tm=tiles.dense, period=SGU_LEN, sweep=True,
            prev_states=stacks)
        bound = 1.01 * HEAD_DIM ** 0.5 * jnp.max(jnp.abs(g_q[l])) * jnp.max(jnp.abs(g_k[l]))
        cut = (-(2.0 * bound + PRUNE_LOG)).reshape(1)
        ofox, *cast = _fox_prompt(
            fcumt[:, 0::LANES], fcumt[:, LANES - 1::LANES], cut, qft, kaug, vft, tq=tiles.fox,
            cast=[a.reshape(-1, a.shape[2]) for a in dense_w] if l == 0 else ())
        if l == 0:
            wg, wbs, wbf, wbb, wo, wfi, wfo = (c.reshape(a.shape) for c, a in zip(cast, dense_w))
            branch_w = (wg, bg3, wbs, wbf, wbb, wo)
        osb = _sb_prompt(qbt, kb16, vbt, tq=tiles.sb, tb=tiles.sb_sub)
        x1 = _merge(xp, mod_p, l, gmix3, ysgu, ofox, osb, *branch_w, tm=tiles.dense)
        xp = _ffn(x1, mod_p, l, gffn3, wfi, wfo, tm=tiles.ffn)
        logf_p.append(logft[:h_fox].T.reshape(batch, seq, h_fox))

        (ysgu, qf, kf16, vf16, qb, kb16, vb16, kf32, vf32, kb32, vb32, logft, sguv) = _inproj(
            xs, mod_s, *shared, msgu_s, bsgu_s, tm=n_dec, period=dec_seq, sweep=False)
        clf = jnp.zeros((dec_batch, SUBLANES, past), F32).at[:, :h_fox].set(
            jnp.transpose(cache_fox_logf[l], (0, 2, 1)))
        suf, cum = _sample_cumsum(clf.reshape(dec_batch * SUBLANES, past), logft, dec_seq)
        cum_b = jnp.transpose(cum.reshape(SUBLANES, dec_batch, dec_seq), (1, 0, 2))
        ofox, osb = _sample_attn(
            qf, kf16, vf16, ck_t, cv_t, suf.reshape(dec_batch, SUBLANES, past), cum_b,
            qb, kb16, vb16, cbk_t, cbv_t, layer=l, batch=dec_batch, s=dec_seq)
        x1 = _merge(xs, mod_s, l, gmix3, ysgu, ofox, osb, *branch_w, tm=n_dec)
        xs = _ffn(x1, mod_s, l, gffn3, wfi, wfo, tm=n_dec)
        st_s.append((kf32.reshape(dec_batch, dec_seq, h_fox, HEAD_DIM),
                     vf32.reshape(dec_batch, dec_seq, h_fox, HEAD_DIM),
                     logft[:h_fox].T.reshape(dec_batch, dec_seq, h_fox),
                     kb32.reshape(dec_batch, dec_seq, h_sb, HEAD_DIM),
                     vb32.reshape(dec_batch, dec_seq, h_sb, HEAD_DIM),
                     sguv.reshape(dec_batch, dec_seq, w_sgu_)))

    def stack(states, idx):
        return jnp.stack([s[idx] for s in states], axis=0)

    per_head = lambda st: jnp.transpose(
        st.reshape(depth, batch, st.shape[1] // HEAD_DIM, HEAD_DIM, seq), (0, 1, 4, 2, 3))
    kf_p, vf_p, kb_p, vb_p = (per_head(st) for st in stacks)
    return (xp.reshape(batch, seq, d), xs.reshape(dec_batch, dec_seq, d),
            kf_p, vf_p, jnp.stack(logf_p, axis=0), kb_p, vb_p,
            stack(st_s, 0), stack(st_s, 1), stack(st_s, 2), stack(st_s, 3), stack(st_s, 4),
            stack(st_s, 5))
```

```python
import functools
from typing import NamedTuple

import numpy as np
import jax
import jax.numpy as jnp
from jax import lax
from jax.experimental import pallas as pl
from jax.experimental.pallas import tpu as pltpu

F32 = jnp.float32
BF16 = jnp.bfloat16

EPS = 1e-6
HEAD_DIM = 64
LANES = 128
SUBLANES = 8
CHUNK = 64
SGU_LEN = 128
G_SGU = 4
NEG_BIG = -1e30
LOG2E = 1.4426950408889634

PRUNE_LOG = 30.0
FOX_EAGER_BLOCKS = 2

VMEM_LIMIT = 56 * 1024 * 1024


def _cparams(sem):
    return pltpu.CompilerParams(dimension_semantics=sem, vmem_limit_bytes=VMEM_LIMIT)


def _const_spec(shape):
    nd = len(shape)
    return pl.BlockSpec(shape, lambda *_: (0,) * nd, pipeline_mode=pl.Buffered(1))


def _layer_spec(arr, layer):
    nd = arr.ndim
    return pl.BlockSpec((1,) + arr.shape[1:], lambda *_: (layer,) + (0,) * (nd - 1),
                        pipeline_mode=pl.Buffered(1))


def _mod_spec(mod, layer, k, tm):
    d = mod.shape[2] // 6
    if mod.shape[1] == 1:
        return pl.BlockSpec((1, 1, d), lambda i: (layer, 0, k))
    return pl.BlockSpec((1, tm, d), lambda i: (layer, i, k))


def _dot(a, b):
    return jnp.dot(a, b, preferred_element_type=F32)


def _dot_nt(a, b):
    return lax.dot_general(a, b, (((1,), (1,)), ((), ())), preferred_element_type=F32)


def _split3(x):
    h = x.astype(BF16)
    r = x - h.astype(F32)
    m = r.astype(BF16)
    l = (r - m.astype(F32)).astype(BF16)
    return h, m, l


def _split2(x):
    h = x.astype(BF16)
    l = (x - h.astype(F32)).astype(BF16)
    return h, l


def _mod_kernel(c_ref, w_ref, b_ref, o_ref):
    c = c_ref[...]
    s = c * (1.0 / (1.0 + jnp.exp(-c)))
    o_ref[0] = _dot(s.astype(BF16), w_ref[0].astype(BF16)) + b_ref[0]


def _modulation(c_all, w_ada, b_ada):
    depth, d, n6 = w_ada.shape
    rows = c_all.shape[0]
    tn = 1024
    return pl.pallas_call(
        _mod_kernel,
        grid=(depth, n6 // tn),
        in_specs=[
            pl.BlockSpec((rows, d), lambda l, j: (0, 0)),
            pl.BlockSpec((1, d, tn), lambda l, j: (l, 0, j)),
            pl.BlockSpec((1, 1, tn), lambda l, j: (l, 0, j)),
        ],
        out_specs=pl.BlockSpec((1, rows, tn), lambda l, j: (l, 0, j)),
        out_shape=jax.ShapeDtypeStruct((depth, rows, n6), F32),
        compiler_params=_cparams(("arbitrary", "arbitrary")),
        name="adaln_mod",
    )(c_all, w_ada, b_ada.reshape(depth, 1, n6))


def _modulated_norm(x, g, sc, sh):
    ms = jnp.mean(x * x, axis=-1, keepdims=True)
    return (x * lax.rsqrt(ms + EPS)) * g * (1.0 + sc) + sh


def _group_rms(t, ind, inv_size, g):
    sq = (t * t).astype(BF16)
    wb = ind.shape[0]
    ss = [_dot(sq[:, c:c + wb], ind) for c in range(0, t.shape[1], wb)]
    ss = jnp.concatenate(ss, axis=1) if len(ss) > 1 else ss[0]
    return t * lax.rsqrt(ss * inv_size + EPS) * g


def _log_sigmoid(x):
    return jnp.minimum(x, 0.0) - jnp.log(1.0 + jnp.exp(-jnp.abs(x)))


def _augmented_keys(kfn, f_cum, heads):
    lane = lax.broadcasted_iota(jnp.int32, f_cum.shape, 1)
    hi, mid, lo = (t.astype(F32) for t in _split3(f_cum * -LOG2E))
    aug_even = (pltpu.roll(hi, HEAD_DIM, 1) + pltpu.roll(mid, HEAD_DIM + SUBLANES, 1)
                + pltpu.roll(lo, HEAD_DIM + 2 * SUBLANES, 1))
    aug_odd = hi + pltpu.roll(mid, SUBLANES, 1) + pltpu.roll(lo, 2 * SUBLANES, 1)
    blocks = []
    for h in range(heads):
        kp = kfn[:, (h // 2) * LANES:(h // 2 + 1) * LANES]
        if h % 2 == 0:
            blocks.append(jnp.where(lane < HEAD_DIM, kp, aug_even))
        else:
            blocks.append(jnp.where(lane >= HEAD_DIM, kp, aug_odd))
    return jnp.concatenate(blocks, axis=1).astype(BF16)


def _inproj_kernel(x_ref, sc_ref, sh_ref, gmix_ref, w_ref, gsgu_ref, gq_ref, gk_ref, bf_ref,
                   ind96_ref, ind64_ref, msgu_ref, bsgu_ref, *rest,
                   tm, w_sgu, w_fox, w_sb, period, sweep, layer):
    if sweep:
        prev = rest[1:1 + 4 * bool(layer)]
        (ysgu_ref, kaug_ref, qft_ref, vft_ref, qbt_ref, kb16_ref, vbt_ref,
         kft_ref, vft32_ref, kbt_ref, vbt32_ref, logft_ref, fcumt_ref, carry_ref) = rest[1 + len(prev):]
        tril_ref = rest[0]
    else:
        (ysgu_ref, qf_ref, kf16_ref, vf16_ref, qb_ref, kb16_ref, vb16_ref,
         kf32_ref, vf32_ref, kb32_ref, vb32_ref, logft_ref, sguv_ref) = rest
    x = x_ref[...]
    h = _modulated_norm(x, gmix_ref[0], sc_ref[0], sh_ref[0])
    p = _dot(h.astype(BF16), w_ref[0])

    o = 0
    u = p[:, o:o + w_sgu]; o += w_sgu
    vs = p[:, o:o + w_sgu]; o += w_sgu
    qf = p[:, o:o + w_fox]; o += w_fox
    kf = p[:, o:o + w_fox]; o += w_fox
    vf = p[:, o:o + w_fox]; o += w_fox
    qb = p[:, o:o + w_sb]; o += w_sb
    kb = p[:, o:o + w_sb]; o += w_sb
    vb = p[:, o:o + w_sb]; o += w_sb
    fl = p[:, o:o + LANES]

    scale = HEAD_DIM ** -0.5
    ind64 = ind64_ref[...]
    qfn = _group_rms(qf, ind64, 1.0 / HEAD_DIM, gq_ref[0])
    kfn = _group_rms(kf, ind64, 1.0 / HEAD_DIM, gk_ref[0])
    kb16_ref[...] = kb.astype(BF16)
    lf = _log_sigmoid(fl + bf_ref[0])
    logft_ref[...] = lf.T[0:SUBLANES, :]
    if sweep:
        qft_ref[...] = (qfn * (scale * LOG2E)).T.astype(BF16)
        qbt_ref[...] = (qb * (scale * LOG2E)).T.astype(BF16)
        vf_t, vb_t = vf.T, vb.T
        vft_ref[...] = vf_t.astype(BF16)
        vbt_ref[...] = vb_t.astype(BF16)
        for dst, own, earlier in zip((kft_ref, vft32_ref, kbt_ref, vbt32_ref),
                                     (kfn.T, vf_t, kb.T, vb_t), prev or (None,) * 4):
            if earlier is not None:
                dst[0:layer] = earlier[...]
            dst[layer] = own
        @pl.when(pl.program_id(0) == 0)
        def _():
            carry_ref[...] = jnp.zeros(carry_ref.shape, F32)
        lane = lax.broadcasted_iota(jnp.int32, lf.shape, 1)
        lfh, lfm, lfl = _split3(jnp.where(lane < SUBLANES, lf, 0.0))
        tril = tril_ref[...]
        run = carry_ref[0:1, :]
        chunks = []
        for c in range(0, tm, LANES):
            part = (_dot(tril, lfh[c:c + LANES]) + _dot(tril, lfm[c:c + LANES])
                    + _dot(tril, lfl[c:c + LANES]) + run)
            run = part[LANES - 1:LANES, :]
            chunks.append(part)
        f_cum = jnp.concatenate(chunks, axis=0) if len(chunks) > 1 else chunks[0]
        carry_ref[...] = jnp.broadcast_to(run, carry_ref.shape)
        fcumt_ref[...] = f_cum.T[0:SUBLANES, :]
        kaug_ref[...] = _augmented_keys(kfn, f_cum, w_fox // HEAD_DIM)
    else:
        kf32_ref[...] = kfn
        vf32_ref[...] = vf
        kb32_ref[...] = kb
        vb32_ref[...] = vb
        qf_ref[...] = (qfn * scale).astype(BF16)
        kf16_ref[...] = kfn.astype(BF16)
        vf16_ref[...] = vf.astype(BF16)
        qb_ref[...] = (qb * scale).astype(BF16)
        vb16_ref[...] = vb.astype(BF16)

    cg = w_sgu // G_SGU
    vsn = _group_rms(vs, ind96_ref[...], 1.0 / cg, gsgu_ref[0])
    if not sweep:
        sguv_ref[...] = vsn
    r = lax.broadcasted_iota(jnp.int32, (SGU_LEN, G_SGU * SGU_LEN), 0)
    c = lax.broadcasted_iota(jnp.int32, (SGU_LEN, G_SGU * SGU_LEN), 1) % SGU_LEN
    keep = (r // period == c // period) & ((c % period) // CHUNK <= (r % period) // CHUNK)
    mix = jnp.where(keep, msgu_ref[0], 0.0).astype(BF16)
    lane_group = lax.broadcasted_iota(jnp.int32, (SGU_LEN, w_sgu), 1) // cg
    vsb = vsn.astype(BF16)
    spat = []
    for ci in range(tm // SGU_LEN):
        vc = vsb[ci * SGU_LEN:(ci + 1) * SGU_LEN]
        stacked = jnp.concatenate(
            [jnp.where(lane_group == g, vc, jnp.zeros_like(vc)) for g in range(G_SGU)], axis=0)
        spat.append(_dot(mix, stacked) + bsgu_ref[0])
    spat = jnp.concatenate(spat, axis=0) if len(spat) > 1 else spat[0]
    ysgu_ref[...] = (u * spat).astype(BF16)


def _inproj(x, mod, layer, gmix, w, gsgu, gq, gk, bf, ind96, ind64, msgu, bsgu, *, tm, period,
            sweep, prev_states=()):
    n, d = x.shape
    w_sgu, w_fox = gsgu.shape[2], gq.shape[2]
    w_sb = (w.shape[2] - LANES - 2 * w_sgu - 3 * w_fox) // 3
    row = lambda width: pl.BlockSpec((tm, width), lambda i: (i, 0))
    col = lambda height: pl.BlockSpec((height, tm), lambda i: (0, i))
    sds = jax.ShapeDtypeStruct
    if sweep:
        stack = lambda width: pl.BlockSpec((layer + 1, width, tm), lambda i: (0, 0, i))
        states_specs = [stack(w_fox), stack(w_fox), stack(w_sb), stack(w_sb), col(SUBLANES)]
        states_shape = [sds((layer + 1, wd, n), F32) for wd in (w_fox, w_fox, w_sb, w_sb)] \
            + [sds((SUBLANES, n), F32)]
    else:
        states_specs = [row(w_fox), row(w_fox), row(w_sb), row(w_sb), col(SUBLANES)]
        states_shape = [sds((n, w_fox), F32), sds((n, w_fox), F32), sds((n, w_sb), F32),
                        sds((n, w_sb), F32), sds((SUBLANES, n), F32)]
    operands = [x, mod, mod, gmix, w, gsgu, gq, gk, bf, ind96, ind64, msgu, bsgu]
    lay = lambda a: _layer_spec(a, layer)
    in_specs = [row(d), _mod_spec(mod, layer, 1, tm), _mod_spec(mod, layer, 0, tm), lay(gmix), lay(w),
                lay(gsgu), lay(gq), lay(gk), lay(bf), _const_spec(ind96.shape),
                _const_spec(ind64.shape), lay(msgu), lay(bsgu)]
    scratch = []
    if sweep:
        heads = w_fox // HEAD_DIM
        a = np.arange(LANES)
        operands.append(jnp.asarray(a[None, :] <= a[:, None], dtype=BF16))
        in_specs.append(_const_spec((LANES, LANES)))
        for st in prev_states:
            operands.append(st)
            in_specs.append(pl.BlockSpec((layer, st.shape[1], tm), lambda i: (0, 0, i)))
        out_specs = [row(w_sgu), row(heads * LANES), col(w_fox), col(w_fox), col(w_sb), row(w_sb),
                     col(w_sb)] + states_specs + [col(SUBLANES)]
        out_shape = [sds((n, w_sgu), BF16), sds((n, heads * LANES), BF16), sds((w_fox, n), BF16),
                     sds((w_fox, n), BF16), sds((w_sb, n), BF16), sds((n, w_sb), BF16),
                     sds((w_sb, n), BF16)] + states_shape + [sds((SUBLANES, n), F32)]
        scratch = [pltpu.VMEM((SUBLANES, LANES), F32)]
    else:
        out_specs = [row(w_sgu), row(w_fox), row(w_fox), row(w_fox), row(w_sb), row(w_sb),
                     row(w_sb)] + states_specs + [row(w_sgu)]
        out_shape = [sds((n, w_sgu), BF16), sds((n, w_fox), BF16), sds((n, w_fox), BF16),
                     sds((n, w_fox), BF16), sds((n, w_sb), BF16), sds((n, w_sb), BF16),
                     sds((n, w_sb), BF16)] + states_shape + [sds((n, w_sgu), F32)]
    kern = functools.partial(_inproj_kernel, tm=tm, w_sgu=w_sgu, w_fox=w_fox, w_sb=w_sb,
                             period=period, sweep=sweep, layer=layer)
    return pl.pallas_call(
        kern,
        grid=(n // tm,),
        in_specs=in_specs,
        out_specs=out_specs,
        out_shape=out_shape,
        scratch_shapes=scratch,
        compiler_params=_cparams(("arbitrary",)),
        name="inproj",
    )(*operands)


def _seq_cumsum(x, nc, reverse_exclusive):
    rows = x.shape[0]
    a = lax.broadcasted_iota(jnp.int32, (LANES, LANES), 0)
    b = lax.broadcasted_iota(jnp.int32, (LANES, LANES), 1)
    tri = (a > b) if reverse_exclusive else (a <= b)
    tri = jnp.where(tri, 1.0, 0.0).astype(BF16)
    ones = jnp.ones((LANES, LANES), BF16)
    xh, xm, xl = _split3(x)
    within = _dot(xh, tri) + _dot(xm, tri) + _dot(xl, tri)
    tot = _dot(xh, ones) + _dot(xm, ones) + _dot(xl, ones)
    ra = lax.broadcasted_iota(jnp.int32, (rows, rows), 0)
    rb = lax.broadcasted_iota(jnp.int32, (rows, rows), 1)
    other = (rb > ra) if reverse_exclusive else (rb < ra)
    blk = jnp.where((ra // nc == rb // nc) & other, 1.0, 0.0).astype(BF16)
    th, tm_, tl = _split3(tot)
    return within + _dot(blk, th) + _dot(blk, tm_) + _dot(blk, tl)


def _sample_cumsum_kernel(clf_ref, lf_ref, suf_ref, cum_ref, *, nc, dec_seq):
    suf_ref[...] = _seq_cumsum(clf_ref[...], nc, reverse_exclusive=True)
    a = lax.broadcasted_iota(jnp.int32, (LANES, LANES), 0)
    b = lax.broadcasted_iota(jnp.int32, (LANES, LANES), 1)
    tri = jnp.where((a // dec_seq == b // dec_seq) & (a <= b), 1.0, 0.0).astype(BF16)
    xh, xm, xl = _split3(lf_ref[...])
    cum_ref[...] = _dot(xh, tri) + _dot(xm, tri) + _dot(xl, tri)


def _sample_cumsum(clogf_t, logft, dec_seq):
    rows, past = clogf_t.shape
    nc = past // LANES
    suf, cum = pl.pallas_call(
        functools.partial(_sample_cumsum_kernel, nc=nc, dec_seq=dec_seq),
        out_shape=[jax.ShapeDtypeStruct((rows * nc, LANES), F32),
                   jax.ShapeDtypeStruct(logft.shape, F32)],
        compiler_params=pltpu.CompilerParams(vmem_limit_bytes=VMEM_LIMIT),
        name="sample_logf_cumsum",
    )(clogf_t.reshape(rows * nc, LANES), logft)
    return suf.reshape(rows, past), cum


def _head_query_blocks(qt_ref, heads, tq, ones_rows, lane0=0):
    rowi = lax.broadcasted_iota(jnp.int32, (LANES, tq), 0)
    out = []
    for h in range(heads):
        qp = qt_ref[(h // 2) * LANES:(h // 2 + 1) * LANES, lane0:lane0 + tq]
        own = (rowi < HEAD_DIM) if h % 2 == 0 else (rowi >= HEAD_DIM)
        fill = jnp.zeros((LANES, tq), F32)
        for r in ones_rows(h):
            fill = jnp.where(rowi == r, 1.0, fill)
        out.append(jnp.where(own, qp, fill.astype(BF16)))
    return out


def _bias_rows(h):
    base = HEAD_DIM if h % 2 == 0 else 0
    return [base + h, base + SUBLANES + h, base + 2 * SUBLANES + h]


def _fox_kernel(fs_ref, fe_ref, cut_ref, qt_ref, ka_ref, vt_ref, *rest, tq, heads):
    n_cast = len(rest) // 2
    o_ref = rest[n_cast]
    for src_ref, dst_ref in zip(rest[:n_cast], rest[n_cast + 1:]):
        dst_ref[...] = src_ref[...].astype(BF16)
    i = pl.program_id(0)
    tk = tq
    cpb = tk // LANES
    qa = _head_query_blocks(qt_ref, heads, tq, _bias_rows)
    krow = lax.broadcasted_iota(jnp.int32, (tk, tq), 0)
    qcol = lax.broadcasted_iota(jnp.int32, (tk, tq), 1)

    def logits_of(j):
        k0 = pl.multiple_of(j * tk, tk)
        return tuple(_dot(ka_ref[pl.ds(k0, tk), h * LANES:(h + 1) * LANES], qa[h])
                     for h in range(heads))

    def absorb(j, logits, carry, mask):
        k0 = pl.multiple_of(j * tk, tk)
        out = []
        for h in range(heads):
            vt = vt_ref[h * HEAD_DIM:(h + 1) * HEAD_DIM, pl.ds(k0, tk)]
            m, l, acc = carry[3 * h:3 * h + 3]
            s = logits[h]
            if mask is not None:
                s = jnp.where(mask, s, NEG_BIG)
            mn = jnp.maximum(m, jnp.max(s, axis=0, keepdims=True))
            alpha = jnp.exp2(m - mn)
            pe = jnp.exp2(s - mn)
            l = alpha * l + jnp.sum(pe, axis=0, keepdims=True)
            acc = alpha * acc + _dot(vt, pe.astype(BF16))
            out += [mn, l, acc]
        return tuple(out)

    def run(blocks, carry):
        staged = [logits_of(j) for j, _ in blocks]
        for (j, mask), logits in zip(blocks, staged):
            carry = absorb(j, logits, carry, mask)
        return carry

    def live(j):
        jc = j * cpb + (cpb - 1)
        ok = fs_ref[0, i * cpb] - fe_ref[0, jc] >= cut_ref[0]
        for h in range(1, heads):
            ok = ok | (fs_ref[h, i * cpb] - fe_ref[h, jc] >= cut_ref[0])
        return ok

    init = []
    for _ in range(heads):
        init += [jnp.full((1, tq), NEG_BIG, F32), jnp.zeros((1, tq), F32),
                 jnp.zeros((HEAD_DIM, tq), F32)]
    carry = run([(i, krow <= qcol)]
                + [(jnp.maximum(i - t, 0), jnp.broadcast_to(i >= t, (tk, tq)))
                   for t in range(1, FOX_EAGER_BLOCKS + 1)], tuple(init))

    def cond(state):
        return (state[0] >= 1) & live(jnp.maximum(state[0] - 1, 0))

    def body(state):
        j = state[0]
        return (j - 2,) + run([(j, None), (j - 1, None)], state[1:])

    state = lax.while_loop(cond, body, (i - 1 - FOX_EAGER_BLOCKS,) + carry)
    j = jnp.maximum(state[0], 0)
    last = (state[0] >= 0) & live(j)
    res = lax.cond(last, lambda c: run([(j, None)], c), lambda c: c, state[1:])
    for p in range(heads // 2):
        _, l0, a0, _, l1, a1 = res[6 * p:6 * p + 6]
        ot = jnp.concatenate([a0 / l0, a1 / l1], axis=0)
        o_ref[:, p * LANES:(p + 1) * LANES] = ot.T.astype(BF16)


def _cast_spec(rows, cols, steps):
    rb = next(r for r in range(2 * SUBLANES, rows + 1, 2 * SUBLANES)
              if rows % r == 0 and rows // r <= steps)
    last = rows // rb - 1
    return pl.BlockSpec((rb, cols), lambda i, *_: (jnp.minimum(i, last), 0))


def _fox_prompt(fs, fe, cut, qt, kaug, vt, *, tq, cast=()):
    w, n = qt.shape
    heads = w // HEAD_DIM
    steps = n // tq
    cast_specs = [_cast_spec(a.shape[0], a.shape[1], steps) for a in cast]
    grid_spec = pltpu.PrefetchScalarGridSpec(
        num_scalar_prefetch=3,
        grid=(steps,),
        in_specs=[
            pl.BlockSpec((w, tq), lambda i, *_: (0, i)),
            pl.BlockSpec(kaug.shape, lambda i, *_: (0, 0), pipeline_mode=pl.Buffered(1)),
            pl.BlockSpec((w, n), lambda i, *_: (0, 0), pipeline_mode=pl.Buffered(1)),
        ] + cast_specs,
        out_specs=[pl.BlockSpec((tq, w), lambda i, *_: (i, 0))] + cast_specs,
    )
    return pl.pallas_call(
        functools.partial(_fox_kernel, tq=tq, heads=heads),
        grid_spec=grid_spec,
        out_shape=[jax.ShapeDtypeStruct((n, w), BF16)]
        + [jax.ShapeDtypeStruct(a.shape, BF16) for a in cast],
        compiler_params=_cparams(("arbitrary",)),
        name="fox_prompt",
    )(fs, fe, cut, qt, kaug, vt, *cast)


def _sb_kernel(qt_ref, k_ref, vt_ref, o_ref, *, tq, tb, heads):
    i = pl.program_id(0)
    nsub = tq // tb
    qm = [_head_query_blocks(qt_ref, heads, tb, lambda h: [], lane0=s * tb) for s in range(nsub)]
    krow = lax.broadcasted_iota(jnp.int32, (tb, tb), 0)
    qcol = lax.broadcasted_iota(jnp.int32, (tb, tb), 1)
    a = lax.broadcasted_iota(jnp.int32, (tb, 2 * tb), 0)
    b = lax.broadcasted_iota(jnp.int32, (tb, 2 * tb), 1) % tb
    tri2 = jnp.where(b > a, 1.0, 0.0).astype(BF16)

    def sweep(tasks, carry):
        k0s = [pl.multiple_of(t[1] * tb, tb) for t in tasks]
        z = [[_dot(k_ref[pl.ds(k0, tb), (h // 2) * LANES:(h // 2 + 1) * LANES], qm[t[0]][h])
              for h in range(heads)] for t, k0 in zip(tasks, k0s)]
        logsig, later, mass = {}, {}, {}
        for ti, (_, _, mask, valid) in enumerate(tasks):
            for h in range(heads):
                zz = z[ti][h]
                sp = jnp.maximum(zz, 0.0) + jnp.log2(1.0 + jnp.exp2(-jnp.abs(zz)))
                logsig[ti, h] = zz - sp
                if mask is not None:
                    sp = jnp.where(mask, sp, 0.0)
                hi, lo = _split2(sp)
                later[ti, h] = _dot(tri2, jnp.concatenate([hi, lo], axis=0))
                mass[ti, h] = jnp.sum(sp, axis=0, keepdims=True)
                if valid is not None:
                    mass[ti, h] = mass[ti, h] * valid
        carry = dict(carry)
        for ti, (s, _, mask, valid) in enumerate(tasks):
            for h in range(heads):
                cr, acc = carry[s, h]
                w = jnp.exp2(logsig[ti, h] - later[ti, h] - cr)
                if mask is not None:
                    w = jnp.where(mask, w, 0.0)
                if valid is not None:
                    w = w * valid
                vt = vt_ref[h * HEAD_DIM:(h + 1) * HEAD_DIM, pl.ds(k0s[ti], tb)]
                carry[s, h] = (cr + mass[ti, h], acc + _dot(vt, w.astype(BF16)))
        return carry

    tasks = []
    for s in range(nsub):
        qb = i * nsub + s
        tasks.append((s, qb, krow < qcol, None))
        tasks.append((s, jnp.maximum(qb - 1, 0), None, jnp.where(qb > 0, 1.0, 0.0) if s == 0 else None))
    zero = (jnp.zeros((1, tb), F32), jnp.zeros((HEAD_DIM, tb), F32))
    carry = sweep(tasks, {(s, h): zero for s in range(nsub) for h in range(heads)})

    keys = [(s, h) for s in range(nsub) for h in range(heads)]
    flat = lambda c: tuple(v for key in keys for v in c[key])

    def cond(state):
        left = state[1]
        for idx in range(1, len(keys)):
            left = jnp.minimum(left, state[1 + 2 * idx])
        return (i * nsub + nsub - 3 - state[0] >= 0) & (jnp.min(left) < PRUNE_LOG * LOG2E)

    def body(state):
        t = state[0]
        more = []
        for s in range(nsub):
            j = i * nsub + s - 2 - t
            more.append((s, jnp.maximum(j, 0), None, jnp.where(j >= 0, 1.0, 0.0)))
        c = {key: (state[1 + 2 * idx], state[2 + 2 * idx]) for idx, key in enumerate(keys)}
        return (t + 1,) + flat(sweep(more, c))

    res = lax.while_loop(cond, body, (0,) + flat(carry))[1:]
    for s in range(nsub):
        for p in range(heads // 2):
            e, o = keys.index((s, 2 * p)), keys.index((s, 2 * p + 1))
            ot = jnp.concatenate([res[2 * e + 1], res[2 * o + 1]], axis=0)
            o_ref[s * tb:(s + 1) * tb, p * LANES:(p + 1) * LANES] = ot.T.astype(BF16)


def _sb_prompt(qt, k, vt, *, tq, tb):
    w, n = qt.shape
    return pl.pallas_call(
        functools.partial(_sb_kernel, tq=tq, tb=tb, heads=w // HEAD_DIM),
        grid=(n // tq,),
        in_specs=[
            pl.BlockSpec((w, tq), lambda i: (0, i)),
            pl.BlockSpec((n, w), lambda i: (0, 0), pipeline_mode=pl.Buffered(1)),
            pl.BlockSpec((w, n), lambda i: (0, 0), pipeline_mode=pl.Buffered(1)),
        ],
        out_specs=pl.BlockSpec((tq, w), lambda i: (i, 0)),
        out_shape=jax.ShapeDtypeStruct((n, w), BF16),
        compiler_params=_cparams(("arbitrary",)),
        name="sb_prompt",
    )(qt, k, vt)


def _sb_block(z, carry, v, tri, mask, v_feature_major=False):
    lg = jnp.log(1.0 + jnp.exp(-jnp.abs(z)))
    sp = jnp.maximum(z, 0.0) + lg
    if mask is not None:
        sp = jnp.where(mask, sp, 0.0)
    hi, lo = _split2(sp)
    later = _dot(hi, tri) + _dot(lo, tri)
    a = jnp.exp((jnp.minimum(z, 0.0) - lg) - later - carry)
    if mask is not None:
        a = jnp.where(mask, a, 0.0)
    pv = _dot_nt(a.astype(BF16), v) if v_feature_major else _dot(a.astype(BF16), v)
    return pv, carry + jnp.sum(sp, axis=1, keepdims=True)


def _suffix_matrix(tk):
    a = lax.broadcasted_iota(jnp.int32, (tk, tk), 0)
    b = lax.broadcasted_iota(jnp.int32, (tk, tk), 1)
    return jnp.where(a > b, 1.0, 0.0).astype(BF16)


def _head_rows(x, heads):
    lane_head = lax.broadcasted_iota(jnp.int32, x.shape, 1) // HEAD_DIM
    return jnp.concatenate(
        [jnp.where(lane_head == h, x, jnp.zeros_like(x)) for h in range(heads)], axis=0)


def _fold_heads(o, heads, s):
    lane_head = lax.broadcasted_iota(jnp.int32, (s, o.shape[1]), 1) // HEAD_DIM
    out = jnp.zeros((s, o.shape[1]), F32)
    for h in range(heads):
        out = jnp.where(lane_head == h, o[h * s:(h + 1) * s], out)
    return out


def _sample_attn_kernel(qf_ref, kf_ref, vf_ref, ck_ref, cv_ref, suf_ref, cum_ref,
                        qb_ref, kb_ref, vb_ref, cbk_ref, cbv_ref, of_ref, ob_ref,
                        *, s, h_fox, h_sb, past):
    qa = _head_rows(qf_ref[...], h_fox)
    rows = h_fox * s
    ck = ck_ref[0, 0].astype(BF16)
    cv = cv_ref[0, 0].astype(BF16)
    suf = suf_ref[0]
    cum = cum_ref[0]
    bias_c = jnp.concatenate(
        [jnp.broadcast_to(suf[h:h + 1, :], (s, past)) for h in range(h_fox)], axis=0)
    bias_n = jnp.concatenate(
        [jnp.broadcast_to(-cum[h:h + 1, :], (s, s)) for h in range(h_fox)], axis=0)
    lc = _dot(qa, ck) + bias_c
    ln = _dot_nt(qa, kf_ref[...]) + bias_n
    r_pos = lax.broadcasted_iota(jnp.int32, (rows, s), 0) % s
    k_pos = lax.broadcasted_iota(jnp.int32, (rows, s), 1)
    ln = jnp.where(k_pos <= r_pos, ln, NEG_BIG)
    m = jnp.maximum(jnp.max(lc, axis=1, keepdims=True), jnp.max(ln, axis=1, keepdims=True))
    pc = jnp.exp(lc - m)
    pn = jnp.exp(ln - m)
    den = jnp.sum(pc, axis=1, keepdims=True) + jnp.sum(pn, axis=1, keepdims=True)
    o = (_dot_nt(pc.astype(BF16), cv) + _dot(pn.astype(BF16), vf_ref[...])) / den
    of_ref[...] = _fold_heads(o, h_fox, s).astype(BF16)

    qb = _head_rows(qb_ref[...], h_sb)
    rows_b = h_sb * s
    rb = lax.broadcasted_iota(jnp.int32, (rows_b, s), 0) % s
    cb = lax.broadcasted_iota(jnp.int32, (rows_b, s), 1)
    acc, carry = _sb_block(_dot_nt(qb, kb_ref[...]), jnp.zeros((rows_b, 1), F32), vb_ref[...],
                           _suffix_matrix(s), cb < rb)
    tri = _suffix_matrix(LANES)

    def cond(state):
        return (state[0] < past // LANES) & (jnp.min(state[2]) < PRUNE_LOG)

    def body(state):
        t, acc, carry = state
        k0 = pl.multiple_of(past - (t + 1) * LANES, LANES)
        k = cbk_ref[0, 0, :, pl.ds(k0, LANES)].astype(BF16)
        v = cbv_ref[0, 0, :, pl.ds(k0, LANES)].astype(BF16)
        pv, carry = _sb_block(_dot(qb, k), carry, v, tri, None, v_feature_major=True)
        return t + 1, acc + pv, carry

    _, acc, _ = lax.while_loop(cond, body, (0, acc, carry))
    ob_ref[...] = _fold_heads(acc, h_sb, s).astype(BF16)


def _sample_attn(qf, kf, vf, ck, cv, suf, cum, qb, kb, vb, cbk, cbv, *, layer, batch, s):
    w_fox, w_sb = qf.shape[1], qb.shape[1]
    past = ck.shape[3]
    h_fox, h_sb = w_fox // HEAD_DIM, w_sb // HEAD_DIM
    new = lambda w: pl.BlockSpec((s, w), lambda b: (b, 0))
    cache = lambda w: pl.BlockSpec((1, 1, w, past), lambda b: (layer, b, 0, 0))
    kern = functools.partial(_sample_attn_kernel, s=s, h_fox=h_fox, h_sb=h_sb, past=past)
    return pl.pallas_call(
        kern,
        grid=(batch,),
        in_specs=[new(w_fox), new(w_fox), new(w_fox), cache(w_fox), cache(w_fox),
                  pl.BlockSpec((1, SUBLANES, past), lambda b: (b, 0, 0)),
                  pl.BlockSpec((1, SUBLANES, s), lambda b: (b, 0, 0)),
                  new(w_sb), new(w_sb), new(w_sb), cache(w_sb), cache(w_sb)],
        out_specs=[new(w_fox), new(w_sb)],
        out_shape=[jax.ShapeDtypeStruct((batch * s, w_fox), BF16),
                   jax.ShapeDtypeStruct((batch * s, w_sb), BF16)],
        compiler_params=_cparams(("arbitrary",)),
        name="sample_attn",
    )(qf, kf, vf, ck, cv, suf, cum, qb, kb, vb, cbk, cbv)


def _merge_kernel(x_ref, sc_ref, sh_ref, gt_ref, gmix_ref, ysgu_ref, ofox_ref, osb_ref,
                  wg_ref, bg_ref, wbs_ref, wbf_ref, wbb_ref, wo_ref, o_ref):
    x = x_ref[...]
    d = x.shape[1]
    h = _modulated_norm(x, gmix_ref[0], sc_ref[0], sh_ref[0])
    gates = _dot(h.astype(BF16), wg_ref[0]) + bg_ref[0]
    gates = 1.0 / (1.0 + jnp.exp(-gates))
    merged = gates[:, 0:d] * _dot(ysgu_ref[...], wbs_ref[0]) \
        + gates[:, d:2 * d] * _dot(ofox_ref[...], wbf_ref[0]) \
        + gates[:, 2 * d:3 * d] * _dot(osb_ref[...], wbb_ref[0])
    o_ref[...] = x + gt_ref[0] * _dot(merged.astype(BF16), wo_ref[0])


def _merge(x, mod, layer, gmix, ysgu, ofox, osb, wg, bg, wbs, wbf, wbb, wo, *, tm):
    n, d = x.shape
    row = lambda width: pl.BlockSpec((tm, width), lambda i: (i, 0))
    lay = lambda a: _layer_spec(a, layer)
    return pl.pallas_call(
        _merge_kernel,
        grid=(n // tm,),
        in_specs=[row(d), _mod_spec(mod, layer, 1, tm), _mod_spec(mod, layer, 0, tm),
                  _mod_spec(mod, layer, 2, tm), lay(gmix),
                  row(ysgu.shape[1]), row(ofox.shape[1]), row(osb.shape[1]),
                  lay(wg), lay(bg), lay(wbs), lay(wbf), lay(wbb), lay(wo)],
        out_specs=row(d),
        out_shape=jax.ShapeDtypeStruct((n, d), F32),
        compiler_params=_cparams(("arbitrary",)),
        name="merge",
    )(x, mod, mod, mod, gmix, ysgu, ofox, osb, wg, bg, wbs, wbf, wbb, wo)


def _ffn_kernel(x_ref, sc_ref, sh_ref, gt_ref, g_ref, wi_ref, wo_ref, o_ref, *, d_ff):
    x = x_ref[...]
    h = _modulated_norm(x, g_ref[0], sc_ref[0], sh_ref[0])
    ag = _dot(h.astype(BF16), wi_ref[0])
    a = ag[:, 0:d_ff]
    act = a * (1.0 / (1.0 + jnp.exp(-a))) * ag[:, d_ff:2 * d_ff]
    o_ref[...] = x + gt_ref[0] * _dot(act.astype(BF16), wo_ref[0])


def _ffn(x, mod, layer, g, wi, wo, *, tm):
    n, d = x.shape
    d_ff = wo.shape[1]
    row = pl.BlockSpec((tm, d), lambda i: (i, 0))
    return pl.pallas_call(
        functools.partial(_ffn_kernel, d_ff=d_ff),
        grid=(n // tm,),
        in_specs=[row, _mod_spec(mod, layer, 4, tm), _mod_spec(mod, layer, 3, tm),
                  _mod_spec(mod, layer, 5, tm), _layer_spec(g, layer), _layer_spec(wi, layer),
                  _layer_spec(wo, layer)],
        out_specs=row,
        out_shape=jax.ShapeDtypeStruct((n, d), F32),
        compiler_params=_cparams(("arbitrary",)),
        name="ffn",
    )(x, mod, mod, mod, g, wi, wo)


class _Tiles(NamedTuple):
    dense: int
    ffn: int
    fox: int
    sb: int
    sb_sub: int


def _prompt_tiles(seq):
    return _Tiles(dense=min(512, seq), ffn=min(512, seq), fox=min(256, seq), sb=min(512, seq),
                  sb_sub=min(128, seq))


def _indicator(width, group):
    idx = np.arange(width) // group
    return jnp.asarray(idx[:, None] == idx[None, :], dtype=BF16)


def kernel(x_prompt, x_sample, c_prompt, c_sample, cache_fox_k, cache_fox_v, cache_fox_logf,
           cache_sb_k, cache_sb_v, w_ada, b_ada, g_mix, g_ffn, w_in, g_sgu_v, w_sgu, b_sgu, b_fgt,
           g_q, g_k, w_br_sgu, w_br_fox, w_br_sb, w_gate, b_gate, w_out, w_ffn_in, w_ffn_out):
    batch, seq, d = x_prompt.shape
    dec_batch, dec_seq, _ = x_sample.shape
    depth = w_ada.shape[0]
    past = cache_fox_k.shape[2]
    h_fox, h_sb = cache_fox_k.shape[3], cache_sb_k.shape[3]
    w_fox, w_sb = h_fox * HEAD_DIM, h_sb * HEAD_DIM
    g_sgu, cg = g_sgu_v.shape[1], g_sgu_v.shape[2]
    w_sgu_ = g_sgu * cg
    assert batch == 1 and g_sgu == G_SGU and w_sgu.shape[2] == SGU_LEN
    n_dec = dec_batch * dec_seq

    n_c = batch + dec_batch
    c_rows = -(-n_c // 8) * 8
    c_all = jnp.zeros((c_rows, d), F32).at[:n_c].set(jnp.concatenate([c_prompt, c_sample], axis=0))
    mod = _modulation(c_all, w_ada, b_ada)

    offs = np.cumsum([0, w_sgu_, w_sgu_, w_fox, w_fox, w_fox, h_fox, w_sb, w_sb, w_sb]).tolist()
    f_cols = jnp.zeros((depth, d, LANES), F32).at[:, :, :h_fox].set(w_in[:, :, offs[5]:offs[6]])
    w_main = jnp.concatenate([w_in[:, :, :offs[5]], w_in[:, :, offs[6]:], f_cols],
                             axis=2).astype(BF16)
    bf_pad = jnp.zeros((depth, 1, LANES), F32).at[:, 0, :h_fox].set(b_fgt)
    gmix3, gffn3, bg3 = g_mix.reshape(depth, 1, d), g_ffn.reshape(depth, 1, d), b_gate.reshape(depth, 1, 3 * d)
    mod_p = mod[:, 0:batch]
    mod_s = jnp.repeat(mod[:, batch:batch + dec_batch], dec_seq, axis=1)
    ind96, ind64 = _indicator(w_sgu_, cg), _indicator(LANES, HEAD_DIM)
    gq_t = jnp.tile(g_q, (1, h_fox)).reshape(depth, 1, w_fox)
    gk_t = jnp.tile(g_k, (1, h_fox)).reshape(depth, 1, w_fox)
    gsgu = g_sgu_v.reshape(depth, 1, w_sgu_)
    msgu_p = jnp.transpose(w_sgu, (0, 2, 1, 3)).reshape(depth, SGU_LEN, g_sgu * SGU_LEN)
    reps = SGU_LEN // dec_seq
    w_small = jnp.tile(w_sgu[:, :, :dec_seq, :dec_seq], (1, 1, reps, reps))
    msgu_s = jnp.transpose(w_small, (0, 2, 1, 3)).reshape(depth, SGU_LEN, g_sgu * SGU_LEN)
    bsgu_p = jnp.repeat(jnp.transpose(b_sgu, (0, 2, 1)), cg, axis=2)
    bsgu_s = jnp.tile(bsgu_p[:, :dec_seq], (1, reps, 1))
    dense_w = (w_gate, w_br_sgu, w_br_fox, w_br_sb, w_out, w_ffn_in, w_ffn_out)

    to_fm = lambda c: jnp.transpose(c, (0, 1, 3, 4, 2)).reshape(
        depth, dec_batch, c.shape[3] * HEAD_DIM, past)
    ck_t, cv_t, cbk_t, cbv_t = (to_fm(c) for c in (cache_fox_k, cache_fox_v, cache_sb_k, cache_sb_v))

    xp = x_prompt.reshape(seq, d)
    xs = x_sample.reshape(n_dec, d)
    tiles = _prompt_tiles(seq)
    stacks, logf_p, st_s = (), [], []
    for l in range(depth):
        shared = (l, gmix3, w_main, gsgu, gq_t, gk_t, bf_pad, ind96, ind64)

        (ysgu, kaug, qft, vft, qbt, kb16, vbt, *stacks, logft, fcumt) = _inproj(
            xp, mod_p, *shared, msgu_p, bsgu_p, tm=tiles.dense, period=SGU_LEN, sweep=True,
            prev_states=stacks)
        bound = 1.01 * HEAD_DIM ** 0.5 * jnp.max(jnp.abs(g_q[l])) * jnp.max(jnp.abs(g_k[l]))
        cut = (-(2.0 * bound + PRUNE_LOG)).reshape(1)
        ofox, *cast = _fox_prompt(
            fcumt[:, 0::LANES], fcumt[:, LANES - 1::LANES], cut, qft, kaug, vft, tq=tiles.fox,
            cast=[a.reshape(-1, a.shape[2]) for a in dense_w] if l == 0 else ())
        if l == 0:
            wg, wbs, wbf, wbb, wo, wfi, wfo = (c.reshape(a.shape) for c, a in zip(cast, dense_w))
            branch_w = (wg, bg3, wbs, wbf, wbb, wo)
        osb = _sb_prompt(qbt, kb16, vbt, tq=tiles.sb, tb=tiles.sb_sub)
        x1 = _merge(xp, mod_p, l, gmix3, ysgu, ofox, osb, *branch_w, tm=tiles.dense)
        xp = _ffn(x1, mod_p, l, gffn3, wfi, wfo, tm=tiles.ffn)
        logf_p.append(logft[:h_fox].T.reshape(batch, seq, h_fox))

        (ysgu, qf, kf16, vf16, qb, kb16, vb16, kf32, vf32, kb32, vb32, logft, sguv) = _inproj(
            xs, mod_s, *shared, msgu_s, bsgu_s, tm=n_dec, period=dec_seq, sweep=False)
        clf = jnp.zeros((dec_batch, SUBLANES, past), F32).at[:, :h_fox].set(
            jnp.transpose(cache_fox_logf[l], (0, 2, 1)))
        suf, cum = _sample_cumsum(clf.reshape(dec_batch * SUBLANES, past), logft, dec_seq)
        cum_b = jnp.transpose(cum.reshape(SUBLANES, dec_batch, dec_seq), (1, 0, 2))
        ofox, osb = _sample_attn(
            qf, kf16, vf16, ck_t, cv_t, suf.reshape(dec_batch, SUBLANES, past), cum_b,
            qb, kb16, vb16, cbk_t, cbv_t, layer=l, batch=dec_batch, s=dec_seq)
        x1 = _merge(xs, mod_s, l, gmix3, ysgu, ofox, osb, *branch_w, tm=n_dec)
        xs = _ffn(x1, mod_s, l, gffn3, wfi, wfo, tm=n_dec)
        st_s.append((kf32.reshape(dec_batch, dec_seq, h_fox, HEAD_DIM),
                     vf32.reshape(dec_batch, dec_seq, h_fox, HEAD_DIM),
                     logft[:h_fox].T.reshape(dec_batch, dec_seq, h_fox),
                     kb32.reshape(dec_batch, dec_seq, h_sb, HEAD_DIM),
                     vb32.reshape(dec_batch, dec_seq, h_sb, HEAD_DIM),
                     sguv.reshape(dec_batch, dec_seq, w_sgu_)))

    def stack(states, idx):
        return jnp.stack([s[idx] for s in states], axis=0)

    per_head = lambda st: jnp.transpose(
        st.reshape(depth, batch, st.shape[1] // HEAD_DIM, HEAD_DIM, seq), (0, 1, 4, 2, 3))
    kf_p, vf_p, kb_p, vb_p = (per_head(st) for st in stacks)
    return (xp.reshape(batch, seq, d), xs.reshape(dec_batch, dec_seq, d),
            kf_p, vf_p, jnp.stack(logf_p, axis=0), kb_p, vb_p,
            stack(st_s, 0), stack(st_s, 1), stack(st_s, 2), stack(st_s, 3), stack(st_s, 4),
            stack(st_s, 5))
```

```python
import functools
from typing import NamedTuple

import numpy as np
import jax
import jax.numpy as jnp
from jax import lax
from jax.experimental import pallas as pl
from jax.experimental.pallas import tpu as pltpu

F32 = jnp.float32
BF16 = jnp.bfloat16

EPS = 1e-6
HEAD_DIM = 64
LANES = 128
SUBLANES = 8
CHUNK = 64
SGU_LEN = 128
G_SGU = 4
NEG_BIG = -1e30
LOG2E = 1.4426950408889634

PRUNE_LOG = 30.0

VMEM_LIMIT = 56 * 1024 * 1024


def _cparams(sem):
    return pltpu.CompilerParams(dimension_semantics=sem, vmem_limit_bytes=VMEM_LIMIT)


def _const_spec(shape):
    nd = len(shape)
    return pl.BlockSpec(shape, lambda *_: (0,) * nd, pipeline_mode=pl.Buffered(1))


def _layer_spec(arr, layer):
    nd = arr.ndim
    return pl.BlockSpec((1,) + arr.shape[1:], lambda *_: (layer,) + (0,) * (nd - 1),
                        pipeline_mode=pl.Buffered(1))


def _mod_spec(mod, layer, k, tm):
    d = mod.shape[2] // 6
    if mod.shape[1] == 1:
        return pl.BlockSpec((1, 1, d), lambda i: (layer, 0, k))
    return pl.BlockSpec((1, tm, d), lambda i: (layer, i, k))


def _dot(a, b):
    return jnp.dot(a, b, preferred_element_type=F32)


def _dot_nt(a, b):
    return lax.dot_general(a, b, (((1,), (1,)), ((), ())), preferred_element_type=F32)


def _split3(x):
    h = x.astype(BF16)
    r = x - h.astype(F32)
    m = r.astype(BF16)
    l = (r - m.astype(F32)).astype(BF16)
    return h, m, l


def _split2(x):
    h = x.astype(BF16)
    l = (x - h.astype(F32)).astype(BF16)
    return h, l


def _mod_kernel(c_ref, w_ref, b_ref, o_ref):
    c = c_ref[...]
    s = c * (1.0 / (1.0 + jnp.exp(-c)))
    o_ref[0] = _dot(s.astype(BF16), w_ref[0].astype(BF16)) + b_ref[0]


def _modulation(c_all, w_ada, b_ada):
    depth, d, n6 = w_ada.shape
    rows = c_all.shape[0]
    tn = 2048
    return pl.pallas_call(
        _mod_kernel,
        grid=(depth, n6 // tn),
        in_specs=[
            pl.BlockSpec((rows, d), lambda l, j: (0, 0)),
            pl.BlockSpec((1, d, tn), lambda l, j: (l, 0, j)),
            pl.BlockSpec((1, 1, tn), lambda l, j: (l, 0, j)),
        ],
        out_specs=pl.BlockSpec((1, rows, tn), lambda l, j: (l, 0, j)),
        out_shape=jax.ShapeDtypeStruct((depth, rows, n6), F32),
        compiler_params=_cparams(("arbitrary", "arbitrary")),
        name="adaln_mod",
    )(c_all, w_ada, b_ada.reshape(depth, 1, n6))


def _modulated_norm(x, g, sc, sh):
    ms = jnp.mean(x * x, axis=-1, keepdims=True)
    return (x * lax.rsqrt(ms + EPS)) * g * (1.0 + sc) + sh


def _group_rms(t, ind, inv_size, g):
    sq = (t * t).astype(BF16)
    wb = ind.shape[0]
    ss = [_dot(sq[:, c:c + wb], ind) for c in range(0, t.shape[1], wb)]
    ss = jnp.concatenate(ss, axis=1) if len(ss) > 1 else ss[0]
    return t * lax.rsqrt(ss * inv_size + EPS) * g


def _log_sigmoid(x):
    return jnp.minimum(x, 0.0) - jnp.log(1.0 + jnp.exp(-jnp.abs(x)))


def _augmented_keys(kfn, f_cum, heads):
    lane = lax.broadcasted_iota(jnp.int32, f_cum.shape, 1)
    hi, mid, lo = (t.astype(F32) for t in _split3(f_cum * -LOG2E))
    aug_even = (pltpu.roll(hi, HEAD_DIM, 1) + pltpu.roll(mid, HEAD_DIM + SUBLANES, 1)
                + pltpu.roll(lo, HEAD_DIM + 2 * SUBLANES, 1))
    aug_odd = hi + pltpu.roll(mid, SUBLANES, 1) + pltpu.roll(lo, 2 * SUBLANES, 1)
    blocks = []
    for h in range(heads):
        kp = kfn[:, (h // 2) * LANES:(h // 2 + 1) * LANES]
        if h % 2 == 0:
            blocks.append(jnp.where(lane < HEAD_DIM, kp, aug_even))
        else:
            blocks.append(jnp.where(lane >= HEAD_DIM, kp, aug_odd))
    return jnp.concatenate(blocks, axis=1).astype(BF16)


def _inproj_kernel(x_ref, sc_ref, sh_ref, gmix_ref, w_ref, gsgu_ref, gq_ref, gk_ref, bf_ref,
                   ind96_ref, ind64_ref, msgu_ref, bsgu_ref, *rest,
                   tm, w_sgu, w_fox, w_sb, period, sweep, layer):
    if sweep:
        prev = rest[1:1 + 4 * bool(layer)]
        (ysgu_ref, kaug_ref, qft_ref, vft_ref, qbt_ref, kb16_ref, vbt_ref,
         kft_ref, vft32_ref, kbt_ref, vbt32_ref, logft_ref, fcumt_ref, carry_ref) = rest[1 + len(prev):]
        tril_ref = rest[0]
    else:
        (ysgu_ref, qf_ref, kf16_ref, vf16_ref, qb_ref, kb16_ref, vb16_ref,
         kf32_ref, vf32_ref, kb32_ref, vb32_ref, logft_ref, sguv_ref) = rest
    x = x_ref[...]
    h = _modulated_norm(x, gmix_ref[0], sc_ref[0], sh_ref[0])
    p = _dot(h.astype(BF16), w_ref[0])

    o = 0
    u = p[:, o:o + w_sgu]; o += w_sgu
    vs = p[:, o:o + w_sgu]; o += w_sgu
    qf = p[:, o:o + w_fox]; o += w_fox
    kf = p[:, o:o + w_fox]; o += w_fox
    vf = p[:, o:o + w_fox]; o += w_fox
    qb = p[:, o:o + w_sb]; o += w_sb
    kb = p[:, o:o + w_sb]; o += w_sb
    vb = p[:, o:o + w_sb]; o += w_sb
    fl = p[:, o:o + LANES]

    scale = HEAD_DIM ** -0.5
    ind64 = ind64_ref[...]
    qfn = _group_rms(qf, ind64, 1.0 / HEAD_DIM, gq_ref[0])
    kfn = _group_rms(kf, ind64, 1.0 / HEAD_DIM, gk_ref[0])
    kb16_ref[...] = kb.astype(BF16)
    lf = _log_sigmoid(fl + bf_ref[0])
    logft_ref[...] = lf.T[0:SUBLANES, :]
    if sweep:
        qft_ref[...] = (qfn * (scale * LOG2E)).T.astype(BF16)
        qbt_ref[...] = (qb * (scale * LOG2E)).T.astype(BF16)
        vf_t, vb_t = vf.T, vb.T
        vft_ref[...] = vf_t.astype(BF16)
        vbt_ref[...] = vb_t.astype(BF16)
        for dst, own, earlier in zip((kft_ref, vft32_ref, kbt_ref, vbt32_ref),
                                     (kfn.T, vf_t, kb.T, vb_t), prev or (None,) * 4):
            if earlier is not None:
                dst[0:layer] = earlier[...]
            dst[layer] = own
        @pl.when(pl.program_id(0) == 0)
        def _():
            carry_ref[...] = jnp.zeros(carry_ref.shape, F32)
        lane = lax.broadcasted_iota(jnp.int32, lf.shape, 1)
        lfh, lfm, lfl = _split3(jnp.where(lane < SUBLANES, lf, 0.0))
        tril = tril_ref[...]
        run = carry_ref[0:1, :]
        chunks = []
        for c in range(0, tm, LANES):
            part = (_dot(tril, lfh[c:c + LANES]) + _dot(tril, lfm[c:c + LANES])
                    + _dot(tril, lfl[c:c + LANES]) + run)
            run = part[LANES - 1:LANES, :]
            chunks.append(part)
        f_cum = jnp.concatenate(chunks, axis=0) if len(chunks) > 1 else chunks[0]
        carry_ref[...] = jnp.broadcast_to(run, carry_ref.shape)
        fcumt_ref[...] = f_cum.T[0:SUBLANES, :]
        kaug_ref[...] = _augmented_keys(kfn, f_cum, w_fox // HEAD_DIM)
    else:
        kf32_ref[...] = kfn
        vf32_ref[...] = vf
        kb32_ref[...] = kb
        vb32_ref[...] = vb
        qf_ref[...] = (qfn * scale).astype(BF16)
        kf16_ref[...] = kfn.astype(BF16)
        vf16_ref[...] = vf.astype(BF16)
        qb_ref[...] = (qb * scale).astype(BF16)
        vb16_ref[...] = vb.astype(BF16)

    cg = w_sgu // G_SGU
    vsn = _group_rms(vs, ind96_ref[...], 1.0 / cg, gsgu_ref[0])
    if not sweep:
        sguv_ref[...] = vsn
    r = lax.broadcasted_iota(jnp.int32, (SGU_LEN, G_SGU * SGU_LEN), 0)
    c = lax.broadcasted_iota(jnp.int32, (SGU_LEN, G_SGU * SGU_LEN), 1) % SGU_LEN
    keep = (r // period == c // period) & ((c % period) // CHUNK <= (r % period) // CHUNK)
    mix = jnp.where(keep, msgu_ref[0], 0.0).astype(BF16)
    lane_group = lax.broadcasted_iota(jnp.int32, (SGU_LEN, w_sgu), 1) // cg
    vsb = vsn.astype(BF16)
    spat = []
    for ci in range(tm // SGU_LEN):
        vc = vsb[ci * SGU_LEN:(ci + 1) * SGU_LEN]
        stacked = jnp.concatenate(
            [jnp.where(lane_group == g, vc, jnp.zeros_like(vc)) for g in range(G_SGU)], axis=0)
        spat.append(_dot(mix, stacked) + bsgu_ref[0])
    spat = jnp.concatenate(spat, axis=0) if len(spat) > 1 else spat[0]
    ysgu_ref[...] = (u * spat).astype(BF16)


def _inproj(x, mod, layer, gmix, w, gsgu, gq, gk, bf, ind96, ind64, msgu, bsgu, *, tm, period,
            sweep, prev_states=()):
    n, d = x.shape
    w_sgu, w_fox = gsgu.shape[2], gq.shape[2]
    w_sb = (w.shape[2] - LANES - 2 * w_sgu - 3 * w_fox) // 3
    row = lambda width: pl.BlockSpec((tm, width), lambda i: (i, 0))
    col = lambda height: pl.BlockSpec((height, tm), lambda i: (0, i))
    sds = jax.ShapeDtypeStruct
    if sweep:
        stack = lambda width: pl.BlockSpec((layer + 1, width, tm), lambda i: (0, 0, i))
        states_specs = [stack(w_fox), stack(w_fox), stack(w_sb), stack(w_sb), col(SUBLANES)]
        states_shape = [sds((layer + 1, wd, n), F32) for wd in (w_fox, w_fox, w_sb, w_sb)] \
            + [sds((SUBLANES, n), F32)]
    else:
        states_specs = [row(w_fox), row(w_fox), row(w_sb), row(w_sb), col(SUBLANES)]
        states_shape = [sds((n, w_fox), F32), sds((n, w_fox), F32), sds((n, w_sb), F32),
                        sds((n, w_sb), F32), sds((SUBLANES, n), F32)]
    operands = [x, mod, mod, gmix, w, gsgu, gq, gk, bf, ind96, ind64, msgu, bsgu]
    lay = lambda a: _layer_spec(a, layer)
    in_specs = [row(d), _mod_spec(mod, layer, 1, tm), _mod_spec(mod, layer, 0, tm), lay(gmix), lay(w),
                lay(gsgu), lay(gq), lay(gk), lay(bf), _const_spec(ind96.shape),
                _const_spec(ind64.shape), lay(msgu), lay(bsgu)]
    scratch = []
    if sweep:
        heads = w_fox // HEAD_DIM
        a = np.arange(LANES)
        operands.append(jnp.asarray(a[None, :] <= a[:, None], dtype=BF16))
        in_specs.append(_const_spec((LANES, LANES)))
        for st in prev_states:
            operands.append(st)
            in_specs.append(pl.BlockSpec((layer, st.shape[1], tm), lambda i: (0, 0, i)))
        out_specs = [row(w_sgu), row(heads * LANES), col(w_fox), col(w_fox), col(w_sb), row(w_sb),
                     col(w_sb)] + states_specs + [col(SUBLANES)]
        out_shape = [sds((n, w_sgu), BF16), sds((n, heads * LANES), BF16), sds((w_fox, n), BF16),
                     sds((w_fox, n), BF16), sds((w_sb, n), BF16), sds((n, w_sb), BF16),
                     sds((w_sb, n), BF16)] + states_shape + [sds((SUBLANES, n), F32)]
        scratch = [pltpu.VMEM((SUBLANES, LANES), F32)]
    else:
        out_specs = [row(w_sgu), row(w_fox), row(w_fox), row(w_fox), row(w_sb), row(w_sb),
                     row(w_sb)] + states_specs + [row(w_sgu)]
        out_shape = [sds((n, w_sgu), BF16), sds((n, w_fox), BF16), sds((n, w_fox), BF16),
                     sds((n, w_fox), BF16), sds((n, w_sb), BF16), sds((n, w_sb), BF16),
                     sds((n, w_sb), BF16)] + states_shape + [sds((n, w_sgu), F32)]
    kern = functools.partial(_inproj_kernel, tm=tm, w_sgu=w_sgu, w_fox=w_fox, w_sb=w_sb,
                             period=period, sweep=sweep, layer=layer)
    return pl.pallas_call(
        kern,
        grid=(n // tm,),
        in_specs=in_specs,
        out_specs=out_specs,
        out_shape=out_shape,
        scratch_shapes=scratch,
        compiler_params=_cparams(("arbitrary",)),
        name="inproj",
    )(*operands)


def _seq_cumsum(x, nc, reverse_exclusive):
    rows = x.shape[0]
    a = lax.broadcasted_iota(jnp.int32, (LANES, LANES), 0)
    b = lax.broadcasted_iota(jnp.int32, (LANES, LANES), 1)
    tri = (a > b) if reverse_exclusive else (a <= b)
    tri = jnp.where(tri, 1.0, 0.0).astype(BF16)
    ones = jnp.ones((LANES, LANES), BF16)
    xh, xm, xl = _split3(x)
    within = _dot(xh, tri) + _dot(xm, tri) + _dot(xl, tri)
    tot = _dot(xh, ones) + _dot(xm, ones) + _dot(xl, ones)
    ra = lax.broadcasted_iota(jnp.int32, (rows, rows), 0)
    rb = lax.broadcasted_iota(jnp.int32, (rows, rows), 1)
    other = (rb > ra) if reverse_exclusive else (rb < ra)
    blk = jnp.where((ra // nc == rb // nc) & other, 1.0, 0.0).astype(BF16)
    th, tm_, tl = _split3(tot)
    return within + _dot(blk, th) + _dot(blk, tm_) + _dot(blk, tl)


def _sample_cumsum_kernel(clf_ref, lf_ref, suf_ref, cum_ref, *, nc, dec_seq):
    suf_ref[...] = _seq_cumsum(clf_ref[...], nc, reverse_exclusive=True)
    a = lax.broadcasted_iota(jnp.int32, (LANES, LANES), 0)
    b = lax.broadcasted_iota(jnp.int32, (LANES, LANES), 1)
    tri = jnp.where((a // dec_seq == b // dec_seq) & (a <= b), 1.0, 0.0).astype(BF16)
    xh, xm, xl = _split3(lf_ref[...])
    cum_ref[...] = _dot(xh, tri) + _dot(xm, tri) + _dot(xl, tri)


def _sample_cumsum(clogf_t, logft, dec_seq):
    rows, past = clogf_t.shape
    nc = past // LANES
    suf, cum = pl.pallas_call(
        functools.partial(_sample_cumsum_kernel, nc=nc, dec_seq=dec_seq),
        out_shape=[jax.ShapeDtypeStruct((rows * nc, LANES), F32),
                   jax.ShapeDtypeStruct(logft.shape, F32)],
        compiler_params=pltpu.CompilerParams(vmem_limit_bytes=VMEM_LIMIT),
        name="sample_logf_cumsum",
    )(clogf_t.reshape(rows * nc, LANES), logft)
    return suf.reshape(rows, past), cum


def _head_query_blocks(qt_ref, heads, tq, ones_rows, lane0=0):
    rowi = lax.broadcasted_iota(jnp.int32, (LANES, tq), 0)
    out = []
    for h in range(heads):
        qp = qt_ref[(h // 2) * LANES:(h // 2 + 1) * LANES, lane0:lane0 + tq]
        own = (rowi < HEAD_DIM) if h % 2 == 0 else (rowi >= HEAD_DIM)
        fill = jnp.zeros((LANES, tq), F32)
        for r in ones_rows(h):
            fill = jnp.where(rowi == r, 1.0, fill)
        out.append(jnp.where(own, qp, fill.astype(BF16)))
    return out


def _bias_rows(h):
    base = HEAD_DIM if h % 2 == 0 else 0
    return [base + h, base + SUBLANES + h, base + 2 * SUBLANES + h]


def _fox_kernel(fs_ref, fe_ref, cut_ref, qt_ref, ka_ref, vt_ref, *rest, tq, heads):
    n_cast = len(rest) // 2
    o_ref = rest[n_cast]
    for src_ref, dst_ref in zip(rest[:n_cast], rest[n_cast + 1:]):
        dst_ref[...] = src_ref[...].astype(BF16)
    i = pl.program_id(0)
    tk = tq
    cpb = tk // LANES
    qa = _head_query_blocks(qt_ref, heads, tq, _bias_rows)
    krow = lax.broadcasted_iota(jnp.int32, (tk, tq), 0)
    qcol = lax.broadcasted_iota(jnp.int32, (tk, tq), 1)

    def logits_of(j):
        k0 = pl.multiple_of(j * tk, tk)
        return tuple(_dot(ka_ref[pl.ds(k0, tk), h * LANES:(h + 1) * LANES], qa[h])
                     for h in range(heads))

    def absorb(j, logits, carry, mask):
        k0 = pl.multiple_of(j * tk, tk)
        out = []
        for h in range(heads):
            vt = vt_ref[h * HEAD_DIM:(h + 1) * HEAD_DIM, pl.ds(k0, tk)]
            m, l, acc = carry[3 * h:3 * h + 3]
            s = logits[h]
            if mask is not None:
                s = jnp.where(mask, s, NEG_BIG)
            mn = jnp.maximum(m, jnp.max(s, axis=0, keepdims=True))
            alpha = jnp.exp2(m - mn)
            pe = jnp.exp2(s - mn)
            l = alpha * l + jnp.sum(pe, axis=0, keepdims=True)
            acc = alpha * acc + _dot(vt, pe.astype(BF16))
            out += [mn, l, acc]
        return tuple(out)

    def run(blocks, carry):
        staged = [logits_of(j) for j, _ in blocks]
        for (j, mask), logits in zip(blocks, staged):
            carry = absorb(j, logits, carry, mask)
        return carry

    def live(j):
        jc = j * cpb + (cpb - 1)
        ok = fs_ref[0, i * cpb] - fe_ref[0, jc] >= cut_ref[0]
        for h in range(1, heads):
            ok = ok | (fs_ref[h, i * cpb] - fe_ref[h, jc] >= cut_ref[0])
        return ok

    init = []
    for _ in range(heads):
        init += [jnp.full((1, tq), NEG_BIG, F32), jnp.zeros((1, tq), F32),
                 jnp.zeros((HEAD_DIM, tq), F32)]
    carry = run([(i, krow <= qcol), (jnp.maximum(i - 1, 0), jnp.broadcast_to(i > 0, (tk, tq)))],
                tuple(init))

    def cond(state):
        return (state[0] >= 1) & live(jnp.maximum(state[0] - 1, 0))

    def body(state):
        j = state[0]
        return (j - 2,) + run([(j, None), (j - 1, None)], state[1:])

    state = lax.while_loop(cond, body, (i - 2,) + carry)
    j = jnp.maximum(state[0], 0)
    last = (state[0] >= 0) & live(j)
    res = lax.cond(last, lambda c: run([(j, None)], c), lambda c: c, state[1:])
    for p in range(heads // 2):
        _, l0, a0, _, l1, a1 = res[6 * p:6 * p + 6]
        ot = jnp.concatenate([a0 / l0, a1 / l1], axis=0)
        o_ref[:, p * LANES:(p + 1) * LANES] = ot.T.astype(BF16)


def _cast_spec(rows, cols, steps):
    rb = next(r for r in range(2 * SUBLANES, rows + 1, 2 * SUBLANES)
              if rows % r == 0 and rows // r <= steps)
    last = rows // rb - 1
    return pl.BlockSpec((rb, cols), lambda i, *_: (jnp.minimum(i, last), 0))


def _fox_prompt(fs, fe, cut, qt, kaug, vt, *, tq, cast=()):
    w, n = qt.shape
    heads = w // HEAD_DIM
    steps = n // tq
    cast_specs = [_cast_spec(a.shape[0], a.shape[1], steps) for a in cast]
    grid_spec = pltpu.PrefetchScalarGridSpec(
        num_scalar_prefetch=3,
        grid=(steps,),
        in_specs=[
            pl.BlockSpec((w, tq), lambda i, *_: (0, i)),
            pl.BlockSpec(kaug.shape, lambda i, *_: (0, 0), pipeline_mode=pl.Buffered(1)),
            pl.BlockSpec((w, n), lambda i, *_: (0, 0), pipeline_mode=pl.Buffered(1)),
        ] + cast_specs,
        out_specs=[pl.BlockSpec((tq, w), lambda i, *_: (i, 0))] + cast_specs,
    )
    return pl.pallas_call(
        functools.partial(_fox_kernel, tq=tq, heads=heads),
        grid_spec=grid_spec,
        out_shape=[jax.ShapeDtypeStruct((n, w), BF16)]
        + [jax.ShapeDtypeStruct(a.shape, BF16) for a in cast],
        compiler_params=_cparams(("arbitrary",)),
        name="fox_prompt",
    )(fs, fe, cut, qt, kaug, vt, *cast)


def _sb_kernel(qt_ref, k_ref, vt_ref, o_ref, *, tq, tb, heads):
    i = pl.program_id(0)
    nsub = tq // tb
    qm = [_head_query_blocks(qt_ref, heads, tb, lambda h: [], lane0=s * tb) for s in range(nsub)]
    krow = lax.broadcasted_iota(jnp.int32, (tb, tb), 0)
    qcol = lax.broadcasted_iota(jnp.int32, (tb, tb), 1)
    a = lax.broadcasted_iota(jnp.int32, (tb, 2 * tb), 0)
    b = lax.broadcasted_iota(jnp.int32, (tb, 2 * tb), 1) % tb
    tri2 = jnp.where(b > a, 1.0, 0.0).astype(BF16)

    def sweep(tasks, carry):
        k0s = [pl.multiple_of(t[1] * tb, tb) for t in tasks]
        z = [[_dot(k_ref[pl.ds(k0, tb), (h // 2) * LANES:(h // 2 + 1) * LANES], qm[t[0]][h])
              for h in range(heads)] for t, k0 in zip(tasks, k0s)]
        logsig, later, mass = {}, {}, {}
        for ti, (_, _, mask, valid) in enumerate(tasks):
            for h in range(heads):
                zz = z[ti][h]
                sp = jnp.maximum(zz, 0.0) + jnp.log2(1.0 + jnp.exp2(-jnp.abs(zz)))
                logsig[ti, h] = zz - sp
                if mask is not None:
                    sp = jnp.where(mask, sp, 0.0)
                hi, lo = _split2(sp)
                later[ti, h] = _dot(tri2, jnp.concatenate([hi, lo], axis=0))
                mass[ti, h] = jnp.sum(sp, axis=0, keepdims=True)
                if valid is not None:
                    mass[ti, h] = mass[ti, h] * valid
        carry = dict(carry)
        for ti, (s, _, mask, valid) in enumerate(tasks):
            for h in range(heads):
                cr, acc = carry[s, h]
                w = jnp.exp2(logsig[ti, h] - later[ti, h] - cr)
                if mask is not None:
                    w = jnp.where(mask, w, 0.0)
                if valid is not None:
                    w = w * valid
                vt = vt_ref[h * HEAD_DIM:(h + 1) * HEAD_DIM, pl.ds(k0s[ti], tb)]
                carry[s, h] = (cr + mass[ti, h], acc + _dot(vt, w.astype(BF16)))
        return carry

    tasks = []
    for s in range(nsub):
        qb = i * nsub + s
        tasks.append((s, qb, krow < qcol, None))
        tasks.append((s, jnp.maximum(qb - 1, 0), None, jnp.where(qb > 0, 1.0, 0.0) if s == 0 else None))
    zero = (jnp.zeros((1, tb), F32), jnp.zeros((HEAD_DIM, tb), F32))
    carry = sweep(tasks, {(s, h): zero for s in range(nsub) for h in range(heads)})

    keys = [(s, h) for s in range(nsub) for h in range(heads)]
    flat = lambda c: tuple(v for key in keys for v in c[key])

    def cond(state):
        left = state[1]
        for idx in range(1, len(keys)):
            left = jnp.minimum(left, state[1 + 2 * idx])
        return (i * nsub + nsub - 3 - state[0] >= 0) & (jnp.min(left) < PRUNE_LOG * LOG2E)

    def body(state):
        t = state[0]
        more = []
        for s in range(nsub):
            j = i * nsub + s - 2 - t
            more.append((s, jnp.maximum(j, 0), None, jnp.where(j >= 0, 1.0, 0.0)))
        c = {key: (state[1 + 2 * idx], state[2 + 2 * idx]) for idx, key in enumerate(keys)}
        return (t + 1,) + flat(sweep(more, c))

    res = lax.while_loop(cond, body, (0,) + flat(carry))[1:]
    for s in range(nsub):
        for p in range(heads // 2):
            e, o = keys.index((s, 2 * p)), keys.index((s, 2 * p + 1))
            ot = jnp.concatenate([res[2 * e + 1], res[2 * o + 1]], axis=0)
            o_ref[s * tb:(s + 1) * tb, p * LANES:(p + 1) * LANES] = ot.T.astype(BF16)


def _sb_prompt(qt, k, vt, *, tq, tb):
    w, n = qt.shape
    return pl.pallas_call(
        functools.partial(_sb_kernel, tq=tq, tb=tb, heads=w // HEAD_DIM),
        grid=(n // tq,),
        in_specs=[
            pl.BlockSpec((w, tq), lambda i: (0, i)),
            pl.BlockSpec((n, w), lambda i: (0, 0), pipeline_mode=pl.Buffered(1)),
            pl.BlockSpec((w, n), lambda i: (0, 0), pipeline_mode=pl.Buffered(1)),
        ],
        out_specs=pl.BlockSpec((tq, w), lambda i: (i, 0)),
        out_shape=jax.ShapeDtypeStruct((n, w), BF16),
        compiler_params=_cparams(("arbitrary",)),
        name="sb_prompt",
    )(qt, k, vt)


def _sb_block(z, carry, v, tri, mask, v_feature_major=False):
    lg = jnp.log(1.0 + jnp.exp(-jnp.abs(z)))
    sp = jnp.maximum(z, 0.0) + lg
    if mask is not None:
        sp = jnp.where(mask, sp, 0.0)
    hi, lo = _split2(sp)
    later = _dot(hi, tri) + _dot(lo, tri)
    a = jnp.exp((jnp.minimum(z, 0.0) - lg) - later - carry)
    if mask is not None:
        a = jnp.where(mask, a, 0.0)
    pv = _dot_nt(a.astype(BF16), v) if v_feature_major else _dot(a.astype(BF16), v)
    return pv, carry + jnp.sum(sp, axis=1, keepdims=True)


def _suffix_matrix(tk):
    a = lax.broadcasted_iota(jnp.int32, (tk, tk), 0)
    b = lax.broadcasted_iota(jnp.int32, (tk, tk), 1)
    return jnp.where(a > b, 1.0, 0.0).astype(BF16)


def _head_rows(x, heads):
    lane_head = lax.broadcasted_iota(jnp.int32, x.shape, 1) // HEAD_DIM
    return jnp.concatenate(
        [jnp.where(lane_head == h, x, jnp.zeros_like(x)) for h in range(heads)], axis=0)


def _fold_heads(o, heads, s):
    lane_head = lax.broadcasted_iota(jnp.int32, (s, o.shape[1]), 1) // HEAD_DIM
    out = jnp.zeros((s, o.shape[1]), F32)
    for h in range(heads):
        out = jnp.where(lane_head == h, o[h * s:(h + 1) * s], out)
    return out


def _sample_attn_kernel(qf_ref, kf_ref, vf_ref, ck_ref, cv_ref, suf_ref, cum_ref,
                        qb_ref, kb_ref, vb_ref, cbk_ref, cbv_ref, of_ref, ob_ref,
                        *, s, h_fox, h_sb, past):
    qa = _head_rows(qf_ref[...], h_fox)
    rows = h_fox * s
    ck = ck_ref[0, 0].astype(BF16)
    cv = cv_ref[0, 0].astype(BF16)
    suf = suf_ref[0]
    cum = cum_ref[0]
    bias_c = jnp.concatenate(
        [jnp.broadcast_to(suf[h:h + 1, :], (s, past)) for h in range(h_fox)], axis=0)
    bias_n = jnp.concatenate(
        [jnp.broadcast_to(-cum[h:h + 1, :], (s, s)) for h in range(h_fox)], axis=0)
    lc = _dot(qa, ck) + bias_c
    ln = _dot_nt(qa, kf_ref[...]) + bias_n
    r_pos = lax.broadcasted_iota(jnp.int32, (rows, s), 0) % s
    k_pos = lax.broadcasted_iota(jnp.int32, (rows, s), 1)
    ln = jnp.where(k_pos <= r_pos, ln, NEG_BIG)
    m = jnp.maximum(jnp.max(lc, axis=1, keepdims=True), jnp.max(ln, axis=1, keepdims=True))
    pc = jnp.exp(lc - m)
    pn = jnp.exp(ln - m)
    den = jnp.sum(pc, axis=1, keepdims=True) + jnp.sum(pn, axis=1, keepdims=True)
    o = (_dot_nt(pc.astype(BF16), cv) + _dot(pn.astype(BF16), vf_ref[...])) / den
    of_ref[...] = _fold_heads(o, h_fox, s).astype(BF16)

    qb = _head_rows(qb_ref[...], h_sb)
    rows_b = h_sb * s
    rb = lax.broadcasted_iota(jnp.int32, (rows_b, s), 0) % s
    cb = lax.broadcasted_iota(jnp.int32, (rows_b, s), 1)
    acc, carry = _sb_block(_dot_nt(qb, kb_ref[...]), jnp.zeros((rows_b, 1), F32), vb_ref[...],
                           _suffix_matrix(s), cb < rb)
    tri = _suffix_matrix(LANES)

    def cond(state):
        return (state[0] < past // LANES) & (jnp.min(state[2]) < PRUNE_LOG)

    def body(state):
        t, acc, carry = state
        k0 = pl.multiple_of(past - (t + 1) * LANES, LANES)
        k = cbk_ref[0, 0, :, pl.ds(k0, LANES)].astype(BF16)
        v = cbv_ref[0, 0, :, pl.ds(k0, LANES)].astype(BF16)
        pv, carry = _sb_block(_dot(qb, k), carry, v, tri, None, v_feature_major=True)
        return t + 1, acc + pv, carry

    _, acc, _ = lax.while_loop(cond, body, (0, acc, carry))
    ob_ref[...] = _fold_heads(acc, h_sb, s).astype(BF16)


def _sample_attn(qf, kf, vf, ck, cv, suf, cum, qb, kb, vb, cbk, cbv, *, layer, batch, s):
    w_fox, w_sb = qf.shape[1], qb.shape[1]
    past = ck.shape[3]
    h_fox, h_sb = w_fox // HEAD_DIM, w_sb // HEAD_DIM
    new = lambda w: pl.BlockSpec((s, w), lambda b: (b, 0))
    cache = lambda w: pl.BlockSpec((1, 1, w, past), lambda b: (layer, b, 0, 0))
    kern = functools.partial(_sample_attn_kernel, s=s, h_fox=h_fox, h_sb=h_sb, past=past)
    return pl.pallas_call(
        kern,
        grid=(batch,),
        in_specs=[new(w_fox), new(w_fox), new(w_fox), cache(w_fox), cache(w_fox),
                  pl.BlockSpec((1, SUBLANES, past), lambda b: (b, 0, 0)),
                  pl.BlockSpec((1, SUBLANES, s), lambda b: (b, 0, 0)),
                  new(w_sb), new(w_sb), new(w_sb), cache(w_sb), cache(w_sb)],
        out_specs=[new(w_fox), new(w_sb)],
        out_shape=[jax.ShapeDtypeStruct((batch * s, w_fox), BF16),
                   jax.ShapeDtypeStruct((batch * s, w_sb), BF16)],
        compiler_params=_cparams(("arbitrary",)),
        name="sample_attn",
    )(qf, kf, vf, ck, cv, suf, cum, qb, kb, vb, cbk, cbv)


def _merge_kernel(x_ref, sc_ref, sh_ref, gt_ref, gmix_ref, ysgu_ref, ofox_ref, osb_ref,
                  wg_ref, bg_ref, wbs_ref, wbf_ref, wbb_ref, wo_ref, o_ref):
    x = x_ref[...]
    d = x.shape[1]
    h = _modulated_norm(x, gmix_ref[0], sc_ref[0], sh_ref[0])
    gates = _dot(h.astype(BF16), wg_ref[0]) + bg_ref[0]
    gates = 1.0 / (1.0 + jnp.exp(-gates))
    merged = gates[:, 0:d] * _dot(ysgu_ref[...], wbs_ref[0]) \
        + gates[:, d:2 * d] * _dot(ofox_ref[...], wbf_ref[0]) \
        + gates[:, 2 * d:3 * d] * _dot(osb_ref[...], wbb_ref[0])
    o_ref[...] = x + gt_ref[0] * _dot(merged.astype(BF16), wo_ref[0])


def _merge(x, mod, layer, gmix, ysgu, ofox, osb, wg, bg, wbs, wbf, wbb, wo, *, tm):
    n, d = x.shape
    row = lambda width: pl.BlockSpec((tm, width), lambda i: (i, 0))
    lay = lambda a: _layer_spec(a, layer)
    return pl.pallas_call(
        _merge_kernel,
        grid=(n // tm,),
        in_specs=[row(d), _mod_spec(mod, layer, 1, tm), _mod_spec(mod, layer, 0, tm),
                  _mod_spec(mod, layer, 2, tm), lay(gmix),
                  row(ysgu.shape[1]), row(ofox.shape[1]), row(osb.shape[1]),
                  lay(wg), lay(bg), lay(wbs), lay(wbf), lay(wbb), lay(wo)],
        out_specs=row(d),
        out_shape=jax.ShapeDtypeStruct((n, d), F32),
        compiler_params=_cparams(("arbitrary",)),
        name="merge",
    )(x, mod, mod, mod, gmix, ysgu, ofox, osb, wg, bg, wbs, wbf, wbb, wo)


def _ffn_kernel(x_ref, sc_ref, sh_ref, gt_ref, g_ref, wi_ref, wo_ref, o_ref, *, d_ff):
    x = x_ref[...]
    h = _modulated_norm(x, g_ref[0], sc_ref[0], sh_ref[0])
    ag = _dot(h.astype(BF16), wi_ref[0])
    a = ag[:, 0:d_ff]
    act = a * (1.0 / (1.0 + jnp.exp(-a))) * ag[:, d_ff:2 * d_ff]
    o_ref[...] = x + gt_ref[0] * _dot(act.astype(BF16), wo_ref[0])


def _ffn(x, mod, layer, g, wi, wo, *, tm):
    n, d = x.shape
    d_ff = wo.shape[1]
    row = pl.BlockSpec((tm, d), lambda i: (i, 0))
    return pl.pallas_call(
        functools.partial(_ffn_kernel, d_ff=d_ff),
        grid=(n // tm,),
        in_specs=[row, _mod_spec(mod, layer, 4, tm), _mod_spec(mod, layer, 3, tm),
                  _mod_spec(mod, layer, 5, tm), _layer_spec(g, layer), _layer_spec(wi, layer),
                  _layer_spec(wo, layer)],
        out_specs=row,
        out_shape=jax.ShapeDtypeStruct((n, d), F32),
        compiler_params=_cparams(("arbitrary",)),
        name="ffn",
    )(x, mod, mod, mod, g, wi, wo)


class _Tiles(NamedTuple):
    inproj: int
    merge: int
    ffn: int
    fox: int
    sb: int
    sb_sub: int


def _prompt_tiles(seq):
    return _Tiles(inproj=min(512, seq), merge=min(1024, seq), ffn=min(512, seq), fox=min(256, seq),
                  sb=min(1024, seq), sb_sub=min(128, seq))


def _indicator(width, group):
    idx = np.arange(width) // group
    return jnp.asarray(idx[:, None] == idx[None, :], dtype=BF16)


def kernel(x_prompt, x_sample, c_prompt, c_sample, cache_fox_k, cache_fox_v, cache_fox_logf,
           cache_sb_k, cache_sb_v, w_ada, b_ada, g_mix, g_ffn, w_in, g_sgu_v, w_sgu, b_sgu, b_fgt,
           g_q, g_k, w_br_sgu, w_br_fox, w_br_sb, w_gate, b_gate, w_out, w_ffn_in, w_ffn_out):
    batch, seq, d = x_prompt.shape
    dec_batch, dec_seq, _ = x_sample.shape
    depth = w_ada.shape[0]
    past = cache_fox_k.shape[2]
    h_fox, h_sb = cache_fox_k.shape[3], cache_sb_k.shape[3]
    w_fox, w_sb = h_fox * HEAD_DIM, h_sb * HEAD_DIM
    g_sgu, cg = g_sgu_v.shape[1], g_sgu_v.shape[2]
    w_sgu_ = g_sgu * cg
    assert batch == 1 and g_sgu == G_SGU and w_sgu.shape[2] == SGU_LEN
    n_dec = dec_batch * dec_seq

    n_c = batch + dec_batch
    c_rows = -(-n_c // 8) * 8
    c_all = jnp.zeros((c_rows, d), F32).at[:n_c].set(jnp.concatenate([c_prompt, c_sample], axis=0))
    mod = _modulation(c_all, w_ada, b_ada)

    offs = np.cumsum([0, w_sgu_, w_sgu_, w_fox, w_fox, w_fox, h_fox, w_sb, w_sb, w_sb]).tolist()
    f_cols = jnp.zeros((depth, d, LANES), F32).at[:, :, :h_fox].set(w_in[:, :, offs[5]:offs[6]])
    w_main = jnp.concatenate([w_in[:, :, :offs[5]], w_in[:, :, offs[6]:], f_cols],
                             axis=2).astype(BF16)
    bf_pad = jnp.zeros((depth, 1, LANES), F32).at[:, 0, :h_fox].set(b_fgt)
    gmix3, gffn3, bg3 = g_mix.reshape(depth, 1, d), g_ffn.reshape(depth, 1, d), b_gate.reshape(depth, 1, 3 * d)
    mod_p = mod[:, 0:batch]
    mod_s = jnp.repeat(mod[:, batch:batch + dec_batch], dec_seq, axis=1)
    ind96, ind64 = _indicator(w_sgu_, cg), _indicator(LANES, HEAD_DIM)
    gq_t = jnp.tile(g_q, (1, h_fox)).reshape(depth, 1, w_fox)
    gk_t = jnp.tile(g_k, (1, h_fox)).reshape(depth, 1, w_fox)
    gsgu = g_sgu_v.reshape(depth, 1, w_sgu_)
    msgu_p = jnp.transpose(w_sgu, (0, 2, 1, 3)).reshape(depth, SGU_LEN, g_sgu * SGU_LEN)
    reps = SGU_LEN // dec_seq
    w_small = jnp.tile(w_sgu[:, :, :dec_seq, :dec_seq], (1, 1, reps, reps))
    msgu_s = jnp.transpose(w_small, (0, 2, 1, 3)).reshape(depth, SGU_LEN, g_sgu * SGU_LEN)
    bsgu_p = jnp.repeat(jnp.transpose(b_sgu, (0, 2, 1)), cg, axis=2)
    bsgu_s = jnp.tile(bsgu_p[:, :dec_seq], (1, reps, 1))
    dense_w = (w_gate, w_br_sgu, w_br_fox, w_br_sb, w_out, w_ffn_in, w_ffn_out)

    to_fm = lambda c: jnp.transpose(c, (0, 1, 3, 4, 2)).reshape(
        depth, dec_batch, c.shape[3] * HEAD_DIM, past)
    ck_t, cv_t, cbk_t, cbv_t = (to_fm(c) for c in (cache_fox_k, cache_fox_v, cache_sb_k, cache_sb_v))

    xp = x_prompt.reshape(seq, d)
    xs = x_sample.reshape(n_dec, d)
    tiles = _prompt_tiles(seq)
    stacks, logf_p, st_s = (), [], []
    for l in range(depth):
        shared = (l, gmix3, w_main, gsgu, gq_t, gk_t, bf_pad, ind96, ind64)

        (ysgu, kaug, qft, vft, qbt, kb16, vbt, *stacks, logft, fcumt) = _inproj(
            xp, mod_p, *shared, msgu_p, bsgu_p, tm=tiles.inproj, period=SGU_LEN, sweep=True,
            prev_states=stacks)
        bound = 1.01 * HEAD_DIM ** 0.5 * jnp.max(jnp.abs(g_q[l])) * jnp.max(jnp.abs(g_k[l]))
        cut = (-(2.0 * bound + PRUNE_LOG)).reshape(1)
        ofox, *cast = _fox_prompt(
            fcumt[:, 0::LANES], fcumt[:, LANES - 1::LANES], cut, qft, kaug, vft, tq=tiles.fox,
            cast=[a.reshape(-1, a.shape[2]) for a in dense_w] if l == 0 else ())
        if l == 0:
            wg, wbs, wbf, wbb, wo, wfi, wfo = (c.reshape(a.shape) for c, a in zip(cast, dense_w))
            branch_w = (wg, bg3, wbs, wbf, wbb, wo)
        osb = _sb_prompt(qbt, kb16, vbt, tq=tiles.sb, tb=tiles.sb_sub)
        x1 = _merge(xp, mod_p, l, gmix3, ysgu, ofox, osb, *branch_w, tm=tiles.merge)
        xp = _ffn(x1, mod_p, l, gffn3, wfi, wfo, tm=tiles.ffn)
        logf_p.append(logft[:h_fox].T.reshape(batch, seq, h_fox))

        (ysgu, qf, kf16, vf16, qb, kb16, vb16, kf32, vf32, kb32, vb32, logft, sguv) = _inproj(
            xs, mod_s, *shared, msgu_s, bsgu_s, tm=n_dec, period=dec_seq, sweep=False)
        clf = jnp.zeros((dec_batch, SUBLANES, past), F32).at[:, :h_fox].set(
            jnp.transpose(cache_fox_logf[l], (0, 2, 1)))
        suf, cum = _sample_cumsum(clf.reshape(dec_batch * SUBLANES, past), logft, dec_seq)
        cum_b = jnp.transpose(cum.reshape(SUBLANES, dec_batch, dec_seq), (1, 0, 2))
        ofox, osb = _sample_attn(
            qf, kf16, vf16, ck_t, cv_t, suf.reshape(dec_batch, SUBLANES, past), cum_b,
            qb, kb16, vb16, cbk_t, cbv_t, layer=l, batch=dec_batch, s=dec_seq)
        x1 = _merge(xs, mod_s, l, gmix3, ysgu, ofox, osb, *branch_w, tm=n_dec)
        xs = _ffn(x1, mod_s, l, gffn3, wfi, wfo, tm=n_dec)
        st_s.append((kf32.reshape(dec_batch, dec_seq, h_fox, HEAD_DIM),
                     vf32.reshape(dec_batch, dec_seq, h_fox, HEAD_DIM),
                     logft[:h_fox].T.reshape(dec_batch, dec_seq, h_fox),
                     kb32.reshape(dec_batch, dec_seq, h_sb, HEAD_DIM),
                     vb32.reshape(dec_batch, dec_seq, h_sb, HEAD_DIM),
                     sguv.reshape(dec_batch, dec_seq, w_sgu_)))

    def stack(states, idx):
        return jnp.stack([s[idx] for s in states], axis=0)

    per_head = lambda st: jnp.transpose(
        st.reshape(depth, batch, st.shape[1] // HEAD_DIM, HEAD_DIM, seq), (0, 1, 4, 2, 3))
    kf_p, vf_p, kb_p, vb_p = (per_head(st) for st in stacks)
    return (xp.reshape(batch, seq, d), xs.reshape(dec_batch, dec_seq, d),
            kf_p, vf_p, jnp.stack(logf_p, axis=0), kb_p, vb_p,
            stack(st_s, 0), stack(st_s, 1), stack(st_s, 2), stack(st_s, 3), stack(st_s, 4),
            stack(st_s, 5))
```

```python
import functools
from typing import NamedTuple

import numpy as np
import jax
import jax.numpy as jnp
from jax import lax
from jax.experimental import pallas as pl
from jax.experimental.pallas import tpu as pltpu

F32 = jnp.float32
BF16 = jnp.bfloat16

EPS = 1e-6
HEAD_DIM = 64
LANES = 128
SUBLANES = 8
CHUNK = 64
SGU_LEN = 128
G_SGU = 4
NEG_BIG = -1e30
LOG2E = 1.4426950408889634

PRUNE_LOG = 30.0

VMEM_LIMIT = 56 * 1024 * 1024
PASS_ROWS = 256


def _cparams(sem):
    return pltpu.CompilerParams(dimension_semantics=sem, vmem_limit_bytes=VMEM_LIMIT)


def _const_spec(shape):
    nd = len(shape)
    return pl.BlockSpec(shape, lambda *_: (0,) * nd, pipeline_mode=pl.Buffered(1))


def _layer_spec(arr, layer):
    nd = arr.ndim
    return pl.BlockSpec((1,) + arr.shape[1:], lambda *_: (layer,) + (0,) * (nd - 1),
                        pipeline_mode=pl.Buffered(1))


def _mod_spec(mod, layer, k, tm):
    d = mod.shape[2] // 6
    if mod.shape[1] == 1:
        return pl.BlockSpec((1, 1, d), lambda i: (layer, 0, k))
    return pl.BlockSpec((1, tm, d), lambda i: (layer, i, k))


def _row_passes(tm):
    sub = min(tm, PASS_ROWS)
    return [slice(r0, r0 + sub) for r0 in range(0, tm, sub)]


def _mod_rows(ref, rows):
    return ref[0] if ref.shape[1] == 1 else ref[0, rows]


def _dot(a, b):
    return jnp.dot(a, b, preferred_element_type=F32)


def _dot_nt(a, b):
    return lax.dot_general(a, b, (((1,), (1,)), ((), ())), preferred_element_type=F32)


def _split3(x):
    h = x.astype(BF16)
    r = x - h.astype(F32)
    m = r.astype(BF16)
    l = (r - m.astype(F32)).astype(BF16)
    return h, m, l


def _split2(x):
    h = x.astype(BF16)
    l = (x - h.astype(F32)).astype(BF16)
    return h, l


def _mod_kernel(c_ref, w_ref, b_ref, o_ref):
    c = c_ref[...]
    s = c * (1.0 / (1.0 + jnp.exp(-c)))
    o_ref[0] = _dot(s.astype(BF16), w_ref[0].astype(BF16)) + b_ref[0]


def _modulation(c_all, w_ada, b_ada):
    depth, d, n6 = w_ada.shape
    rows = c_all.shape[0]
    tn = 1024
    return pl.pallas_call(
        _mod_kernel,
        grid=(depth, n6 // tn),
        in_specs=[
            pl.BlockSpec((rows, d), lambda l, j: (0, 0)),
            pl.BlockSpec((1, d, tn), lambda l, j: (l, 0, j)),
            pl.BlockSpec((1, 1, tn), lambda l, j: (l, 0, j)),
        ],
        out_specs=pl.BlockSpec((1, rows, tn), lambda l, j: (l, 0, j)),
        out_shape=jax.ShapeDtypeStruct((depth, rows, n6), F32),
        compiler_params=_cparams(("arbitrary", "arbitrary")),
        name="adaln_mod",
    )(c_all, w_ada, b_ada.reshape(depth, 1, n6))


def _modulated_norm(x, g, sc, sh):
    ms = jnp.mean(x * x, axis=-1, keepdims=True)
    return (x * lax.rsqrt(ms + EPS)) * g * (1.0 + sc) + sh


def _group_rms(t, ind, inv_size, g):
    sq = (t * t).astype(BF16)
    wb = ind.shape[0]
    ss = [_dot(sq[:, c:c + wb], ind) for c in range(0, t.shape[1], wb)]
    ss = jnp.concatenate(ss, axis=1) if len(ss) > 1 else ss[0]
    return t * lax.rsqrt(ss * inv_size + EPS) * g


def _log_sigmoid(x):
    return jnp.minimum(x, 0.0) - jnp.log(1.0 + jnp.exp(-jnp.abs(x)))


def _augmented_keys(kfn, f_cum, heads):
    lane = lax.broadcasted_iota(jnp.int32, f_cum.shape, 1)
    hi, mid, lo = (t.astype(F32) for t in _split3(f_cum * -LOG2E))
    aug_even = (pltpu.roll(hi, HEAD_DIM, 1) + pltpu.roll(mid, HEAD_DIM + SUBLANES, 1)
                + pltpu.roll(lo, HEAD_DIM + 2 * SUBLANES, 1))
    aug_odd = hi + pltpu.roll(mid, SUBLANES, 1) + pltpu.roll(lo, 2 * SUBLANES, 1)
    blocks = []
    for h in range(heads):
        kp = kfn[:, (h // 2) * LANES:(h // 2 + 1) * LANES]
        if h % 2 == 0:
            blocks.append(jnp.where(lane < HEAD_DIM, kp, aug_even))
        else:
            blocks.append(jnp.where(lane >= HEAD_DIM, kp, aug_odd))
    return jnp.concatenate(blocks, axis=1).astype(BF16)


def _inproj_kernel(x_ref, sc_ref, sh_ref, gmix_ref, w_ref, *refs, sweep, **static):
    run = None
    if sweep:
        carry_ref = refs[-1]
        @pl.when(pl.program_id(0) == 0)
        def _():
            carry_ref[...] = jnp.zeros(carry_ref.shape, F32)
        run = carry_ref[0:1, :]
    for rows in _row_passes(x_ref.shape[0]):
        h = _modulated_norm(x_ref[rows], gmix_ref[0], _mod_rows(sc_ref, rows), _mod_rows(sh_ref, rows))
        p = _dot(h.astype(BF16), w_ref[0])
        run = _inproj_rows(p, run, rows, *refs, sweep=sweep, **static)
    if sweep:
        carry_ref[...] = jnp.broadcast_to(run, carry_ref.shape)


def _inproj_rows(p, run, rows, gsgu_ref, gq_ref, gk_ref, bf_ref, ind96_ref, ind64_ref, msgu_ref,
                 bsgu_ref, *rest, w_sgu, w_fox, w_sb, period, sweep, layer):
    sub = p.shape[0]
    if sweep:
        prev = rest[1:1 + 4 * bool(layer)]
        (ysgu_ref, kaug_ref, qft_ref, vft_ref, qbt_ref, kb16_ref, vbt_ref,
         kft_ref, vft32_ref, kbt_ref, vbt32_ref, logft_ref, fcumt_ref, _) = rest[1 + len(prev):]
        tril_ref = rest[0]
    else:
        (ysgu_ref, qf_ref, kf16_ref, vf16_ref, qb_ref, kb16_ref, vb16_ref,
         kf32_ref, vf32_ref, kb32_ref, vb32_ref, logft_ref, sguv_ref) = rest
    o = 0
    u = p[:, o:o + w_sgu]; o += w_sgu
    vs = p[:, o:o + w_sgu]; o += w_sgu
    qf = p[:, o:o + w_fox]; o += w_fox
    kf = p[:, o:o + w_fox]; o += w_fox
    vf = p[:, o:o + w_fox]; o += w_fox
    qb = p[:, o:o + w_sb]; o += w_sb
    kb = p[:, o:o + w_sb]; o += w_sb
    vb = p[:, o:o + w_sb]; o += w_sb
    fl = p[:, o:o + LANES]

    scale = HEAD_DIM ** -0.5
    ind64 = ind64_ref[...]
    qfn = _group_rms(qf, ind64, 1.0 / HEAD_DIM, gq_ref[0])
    kfn = _group_rms(kf, ind64, 1.0 / HEAD_DIM, gk_ref[0])
    kb16_ref[rows] = kb.astype(BF16)
    lf = _log_sigmoid(fl + bf_ref[0])
    logft_ref[:, rows] = lf.T[0:SUBLANES, :]
    if sweep:
        qft_ref[:, rows] = (qfn * (scale * LOG2E)).T.astype(BF16)
        qbt_ref[:, rows] = (qb * (scale * LOG2E)).T.astype(BF16)
        vf_t, vb_t = vf.T, vb.T
        vft_ref[:, rows] = vf_t.astype(BF16)
        vbt_ref[:, rows] = vb_t.astype(BF16)
        for dst, own, earlier in zip((kft_ref, vft32_ref, kbt_ref, vbt32_ref),
                                     (kfn.T, vf_t, kb.T, vb_t), prev or (None,) * 4):
            if earlier is not None:
                dst[0:layer, :, rows] = earlier[:, :, rows]
            dst[layer, :, rows] = own
        lane = lax.broadcasted_iota(jnp.int32, lf.shape, 1)
        lfh, lfm, lfl = _split3(jnp.where(lane < SUBLANES, lf, 0.0))
        tril = tril_ref[...]
        chunks = []
        for c in range(0, sub, LANES):
            part = (_dot(tril, lfh[c:c + LANES]) + _dot(tril, lfm[c:c + LANES])
                    + _dot(tril, lfl[c:c + LANES]) + run)
            run = part[LANES - 1:LANES, :]
            chunks.append(part)
        f_cum = jnp.concatenate(chunks, axis=0) if len(chunks) > 1 else chunks[0]
        fcumt_ref[:, rows] = f_cum.T[0:SUBLANES, :]
        kaug_ref[rows] = _augmented_keys(kfn, f_cum, w_fox // HEAD_DIM)
    else:
        kf32_ref[rows] = kfn
        vf32_ref[rows] = vf
        kb32_ref[rows] = kb
        vb32_ref[rows] = vb
        qf_ref[rows] = (qfn * scale).astype(BF16)
        kf16_ref[rows] = kfn.astype(BF16)
        vf16_ref[rows] = vf.astype(BF16)
        qb_ref[rows] = (qb * scale).astype(BF16)
        vb16_ref[rows] = vb.astype(BF16)

    cg = w_sgu // G_SGU
    vsn = _group_rms(vs, ind96_ref[...], 1.0 / cg, gsgu_ref[0])
    if not sweep:
        sguv_ref[rows] = vsn
    r = lax.broadcasted_iota(jnp.int32, (SGU_LEN, G_SGU * SGU_LEN), 0)
    c = lax.broadcasted_iota(jnp.int32, (SGU_LEN, G_SGU * SGU_LEN), 1) % SGU_LEN
    keep = (r // period == c // period) & ((c % period) // CHUNK <= (r % period) // CHUNK)
    mix = jnp.where(keep, msgu_ref[0], 0.0).astype(BF16)
    lane_group = lax.broadcasted_iota(jnp.int32, (SGU_LEN, w_sgu), 1) // cg
    vsb = vsn.astype(BF16)
    spat = []
    for ci in range(sub // SGU_LEN):
        vc = vsb[ci * SGU_LEN:(ci + 1) * SGU_LEN]
        stacked = jnp.concatenate(
            [jnp.where(lane_group == g, vc, jnp.zeros_like(vc)) for g in range(G_SGU)], axis=0)
        spat.append(_dot(mix, stacked) + bsgu_ref[0])
    spat = jnp.concatenate(spat, axis=0) if len(spat) > 1 else spat[0]
    ysgu_ref[rows] = (u * spat).astype(BF16)
    return run


def _inproj(x, mod, layer, gmix, w, gsgu, gq, gk, bf, ind96, ind64, msgu, bsgu, *, tm, period,
            sweep, prev_states=()):
    n, d = x.shape
    w_sgu, w_fox = gsgu.shape[2], gq.shape[2]
    w_sb = (w.shape[2] - LANES - 2 * w_sgu - 3 * w_fox) // 3
    row = lambda width: pl.BlockSpec((tm, width), lambda i: (i, 0))
    col = lambda height: pl.BlockSpec((height, tm), lambda i: (0, i))
    sds = jax.ShapeDtypeStruct
    if sweep:
        stack = lambda width: pl.BlockSpec((layer + 1, width, tm), lambda i: (0, 0, i))
        states_specs = [stack(w_fox), stack(w_fox), stack(w_sb), stack(w_sb), col(SUBLANES)]
        states_shape = [sds((layer + 1, wd, n), F32) for wd in (w_fox, w_fox, w_sb, w_sb)] \
            + [sds((SUBLANES, n), F32)]
    else:
        states_specs = [row(w_fox), row(w_fox), row(w_sb), row(w_sb), col(SUBLANES)]
        states_shape = [sds((n, w_fox), F32), sds((n, w_fox), F32), sds((n, w_sb), F32),
                        sds((n, w_sb), F32), sds((SUBLANES, n), F32)]
    operands = [x, mod, mod, gmix, w, gsgu, gq, gk, bf, ind96, ind64, msgu, bsgu]
    lay = lambda a: _layer_spec(a, layer)
    in_specs = [row(d), _mod_spec(mod, layer, 1, tm), _mod_spec(mod, layer, 0, tm), lay(gmix), lay(w),
                lay(gsgu), lay(gq), lay(gk), lay(bf), _const_spec(ind96.shape),
                _const_spec(ind64.shape), lay(msgu), lay(bsgu)]
    scratch = []
    if sweep:
        heads = w_fox // HEAD_DIM
        a = np.arange(LANES)
        operands.append(jnp.asarray(a[None, :] <= a[:, None], dtype=BF16))
        in_specs.append(_const_spec((LANES, LANES)))
        for st in prev_states:
            operands.append(st)
            in_specs.append(pl.BlockSpec((layer, st.shape[1], tm), lambda i: (0, 0, i)))
        out_specs = [row(w_sgu), row(heads * LANES), col(w_fox), col(w_fox), col(w_sb), row(w_sb),
                     col(w_sb)] + states_specs + [col(SUBLANES)]
        out_shape = [sds((n, w_sgu), BF16), sds((n, heads * LANES), BF16), sds((w_fox, n), BF16),
                     sds((w_fox, n), BF16), sds((w_sb, n), BF16), sds((n, w_sb), BF16),
                     sds((w_sb, n), BF16)] + states_shape + [sds((SUBLANES, n), F32)]
        scratch = [pltpu.VMEM((SUBLANES, LANES), F32)]
    else:
        out_specs = [row(w_sgu), row(w_fox), row(w_fox), row(w_fox), row(w_sb), row(w_sb),
                     row(w_sb)] + states_specs + [row(w_sgu)]
        out_shape = [sds((n, w_sgu), BF16), sds((n, w_fox), BF16), sds((n, w_fox), BF16),
                     sds((n, w_fox), BF16), sds((n, w_sb), BF16), sds((n, w_sb), BF16),
                     sds((n, w_sb), BF16)] + states_shape + [sds((n, w_sgu), F32)]
    kern = functools.partial(_inproj_kernel, w_sgu=w_sgu, w_fox=w_fox, w_sb=w_sb,
                             period=period, sweep=sweep, layer=layer)
    return pl.pallas_call(
        kern,
        grid=(n // tm,),
        in_specs=in_specs,
        out_specs=out_specs,
        out_shape=out_shape,
        scratch_shapes=scratch,
        compiler_params=_cparams(("arbitrary",)),
        name="inproj",
    )(*operands)


def _seq_cumsum(x, nc, reverse_exclusive):
    rows = x.shape[0]
    a = lax.broadcasted_iota(jnp.int32, (LANES, LANES), 0)
    b = lax.broadcasted_iota(jnp.int32, (LANES, LANES), 1)
    tri = (a > b) if reverse_exclusive else (a <= b)
    tri = jnp.where(tri, 1.0, 0.0).astype(BF16)
    ones = jnp.ones((LANES, LANES), BF16)
    xh, xm, xl = _split3(x)
    within = _dot(xh, tri) + _dot(xm, tri) + _dot(xl, tri)
    tot = _dot(xh, ones) + _dot(xm, ones) + _dot(xl, ones)
    ra = lax.broadcasted_iota(jnp.int32, (rows, rows), 0)
    rb = lax.broadcasted_iota(jnp.int32, (rows, rows), 1)
    other = (rb > ra) if reverse_exclusive else (rb < ra)
    blk = jnp.where((ra // nc == rb // nc) & other, 1.0, 0.0).astype(BF16)
    th, tm_, tl = _split3(tot)
    return within + _dot(blk, th) + _dot(blk, tm_) + _dot(blk, tl)


def _sample_cumsum_kernel(clf_ref, lf_ref, suf_ref, cum_ref, *, nc, dec_seq):
    suf_ref[...] = _seq_cumsum(clf_ref[...], nc, reverse_exclusive=True)
    a = lax.broadcasted_iota(jnp.int32, (LANES, LANES), 0)
    b = lax.broadcasted_iota(jnp.int32, (LANES, LANES), 1)
    tri = jnp.where((a // dec_seq == b // dec_seq) & (a <= b), 1.0, 0.0).astype(BF16)
    xh, xm, xl = _split3(lf_ref[...])
    cum_ref[...] = _dot(xh, tri) + _dot(xm, tri) + _dot(xl, tri)


def _sample_cumsum(clogf_t, logft, dec_seq):
    rows, past = clogf_t.shape
    nc = past // LANES
    suf, cum = pl.pallas_call(
        functools.partial(_sample_cumsum_kernel, nc=nc, dec_seq=dec_seq),
        out_shape=[jax.ShapeDtypeStruct((rows * nc, LANES), F32),
                   jax.ShapeDtypeStruct(logft.shape, F32)],
        compiler_params=pltpu.CompilerParams(vmem_limit_bytes=VMEM_LIMIT),
        name="sample_logf_cumsum",
    )(clogf_t.reshape(rows * nc, LANES), logft)
    return suf.reshape(rows, past), cum


def _head_query_blocks(qt_ref, heads, tq, ones_rows, lane0=0):
    rowi = lax.broadcasted_iota(jnp.int32, (LANES, tq), 0)
    out = []
    for h in range(heads):
        qp = qt_ref[(h // 2) * LANES:(h // 2 + 1) * LANES, lane0:lane0 + tq]
        own = (rowi < HEAD_DIM) if h % 2 == 0 else (rowi >= HEAD_DIM)
        fill = jnp.zeros((LANES, tq), F32)
        for r in ones_rows(h):
            fill = jnp.where(rowi == r, 1.0, fill)
        out.append(jnp.where(own, qp, fill.astype(BF16)))
    return out


def _bias_rows(h):
    base = HEAD_DIM if h % 2 == 0 else 0
    return [base + h, base + SUBLANES + h, base + 2 * SUBLANES + h]


def _fox_kernel(fs_ref, fe_ref, cut_ref, qt_ref, ka_ref, vt_ref, *rest, tq, heads):
    n_cast = len(rest) // 2
    o_ref = rest[n_cast]
    for src_ref, dst_ref in zip(rest[:n_cast], rest[n_cast + 1:]):
        dst_ref[...] = src_ref[...].astype(BF16)
    i = pl.program_id(0)
    tk = tq
    cpb = tk // LANES
    qa = _head_query_blocks(qt_ref, heads, tq, _bias_rows)
    krow = lax.broadcasted_iota(jnp.int32, (tk, tq), 0)
    qcol = lax.broadcasted_iota(jnp.int32, (tk, tq), 1)

    def logits_of(j):
        k0 = pl.multiple_of(j * tk, tk)
        return tuple(_dot(ka_ref[pl.ds(k0, tk), h * LANES:(h + 1) * LANES], qa[h])
                     for h in range(heads))

    def absorb(j, logits, carry, mask):
        k0 = pl.multiple_of(j * tk, tk)
        out = []
        for h in range(heads):
            vt = vt_ref[h * HEAD_DIM:(h + 1) * HEAD_DIM, pl.ds(k0, tk)]
            m, l, acc = carry[3 * h:3 * h + 3]
            s = logits[h]
            if mask is not None:
                s = jnp.where(mask, s, NEG_BIG)
            mn = jnp.maximum(m, jnp.max(s, axis=0, keepdims=True))
            alpha = jnp.exp2(m - mn)
            pe = jnp.exp2(s - mn)
            l = alpha * l + jnp.sum(pe, axis=0, keepdims=True)
            acc = alpha * acc + _dot(vt, pe.astype(BF16))
            out += [mn, l, acc]
        return tuple(out)

    def run(blocks, carry):
        staged = [logits_of(j) for j, _ in blocks]
        for (j, mask), logits in zip(blocks, staged):
            carry = absorb(j, logits, carry, mask)
        return carry

    def live(j):
        jc = j * cpb + (cpb - 1)
        ok = fs_ref[0, i * cpb] - fe_ref[0, jc] >= cut_ref[0]
        for h in range(1, heads):
            ok = ok | (fs_ref[h, i * cpb] - fe_ref[h, jc] >= cut_ref[0])
        return ok

    init = []
    for _ in range(heads):
        init += [jnp.full((1, tq), NEG_BIG, F32), jnp.zeros((1, tq), F32),
                 jnp.zeros((HEAD_DIM, tq), F32)]
    carry = run([(i, krow <= qcol), (jnp.maximum(i - 1, 0), jnp.broadcast_to(i > 0, (tk, tq)))],
                tuple(init))

    def cond(state):
        return (state[0] >= 1) & live(jnp.maximum(state[0] - 1, 0))

    def body(state):
        j = state[0]
        return (j - 2,) + run([(j, None), (j - 1, None)], state[1:])

    state = lax.while_loop(cond, body, (i - 2,) + carry)
    j = jnp.maximum(state[0], 0)
    last = (state[0] >= 0) & live(j)
    res = lax.cond(last, lambda c: run([(j, None)], c), lambda c: c, state[1:])
    for p in range(heads // 2):
        _, l0, a0, _, l1, a1 = res[6 * p:6 * p + 6]
        ot = jnp.concatenate([a0 / l0, a1 / l1], axis=0)
        o_ref[:, p * LANES:(p + 1) * LANES] = ot.T.astype(BF16)


def _cast_spec(rows, cols, steps):
    rb = next(r for r in range(2 * SUBLANES, rows + 1, 2 * SUBLANES)
              if rows % r == 0 and rows // r <= steps)
    last = rows // rb - 1
    return pl.BlockSpec((rb, cols), lambda i, *_: (jnp.minimum(i, last), 0))


def _fox_prompt(fs, fe, cut, qt, kaug, vt, *, tq, cast=()):
    w, n = qt.shape
    heads = w // HEAD_DIM
    steps = n // tq
    cast_specs = [_cast_spec(a.shape[0], a.shape[1], steps) for a in cast]
    grid_spec = pltpu.PrefetchScalarGridSpec(
        num_scalar_prefetch=3,
        grid=(steps,),
        in_specs=[
            pl.BlockSpec((w, tq), lambda i, *_: (0, i)),
            pl.BlockSpec(kaug.shape, lambda i, *_: (0, 0), pipeline_mode=pl.Buffered(1)),
            pl.BlockSpec((w, n), lambda i, *_: (0, 0), pipeline_mode=pl.Buffered(1)),
        ] + cast_specs,
        out_specs=[pl.BlockSpec((tq, w), lambda i, *_: (i, 0))] + cast_specs,
    )
    return pl.pallas_call(
        functools.partial(_fox_kernel, tq=tq, heads=heads),
        grid_spec=grid_spec,
        out_shape=[jax.ShapeDtypeStruct((n, w), BF16)]
        + [jax.ShapeDtypeStruct(a.shape, BF16) for a in cast],
        compiler_params=_cparams(("arbitrary",)),
        name="fox_prompt",
    )(fs, fe, cut, qt, kaug, vt, *cast)


def _sb_kernel(qt_ref, k_ref, vt_ref, o_ref, *, tq, tb, heads):
    i = pl.program_id(0)
    nsub = tq // tb
    qm = [_head_query_blocks(qt_ref, heads, tb, lambda h: [], lane0=s * tb) for s in range(nsub)]
    krow = lax.broadcasted_iota(jnp.int32, (tb, tb), 0)
    qcol = lax.broadcasted_iota(jnp.int32, (tb, tb), 1)
    a = lax.broadcasted_iota(jnp.int32, (tb, 2 * tb), 0)
    b = lax.broadcasted_iota(jnp.int32, (tb, 2 * tb), 1) % tb
    tri2 = jnp.where(b > a, 1.0, 0.0).astype(BF16)

    def sweep(tasks, carry):
        k0s = [pl.multiple_of(t[1] * tb, tb) for t in tasks]
        z = [[_dot(k_ref[pl.ds(k0, tb), (h // 2) * LANES:(h // 2 + 1) * LANES], qm[t[0]][h])
              for h in range(heads)] for t, k0 in zip(tasks, k0s)]
        logsig, later, mass = {}, {}, {}
        for ti, (_, _, mask, valid) in enumerate(tasks):
            for h in range(heads):
                zz = z[ti][h]
                sp = jnp.maximum(zz, 0.0) + jnp.log2(1.0 + jnp.exp2(-jnp.abs(zz)))
                logsig[ti, h] = zz - sp
                if mask is not None:
                    sp = jnp.where(mask, sp, 0.0)
                hi, lo = _split2(sp)
                later[ti, h] = _dot(tri2, jnp.concatenate([hi, lo], axis=0))
                mass[ti, h] = jnp.sum(sp, axis=0, keepdims=True)
                if valid is not None:
                    mass[ti, h] = mass[ti, h] * valid
        carry = dict(carry)
        for ti, (s, _, mask, valid) in enumerate(tasks):
            for h in range(heads):
                cr, acc = carry[s, h]
                w = jnp.exp2(logsig[ti, h] - later[ti, h] - cr)
                if mask is not None:
                    w = jnp.where(mask, w, 0.0)
                if valid is not None:
                    w = w * valid
                vt = vt_ref[h * HEAD_DIM:(h + 1) * HEAD_DIM, pl.ds(k0s[ti], tb)]
                carry[s, h] = (cr + mass[ti, h], acc + _dot(vt, w.astype(BF16)))
        return carry

    tasks = []
    for s in range(nsub):
        qb = i * nsub + s
        tasks.append((s, qb, krow < qcol, None))
        tasks.append((s, jnp.maximum(qb - 1, 0), None, jnp.where(qb > 0, 1.0, 0.0) if s == 0 else None))
    zero = (jnp.zeros((1, tb), F32), jnp.zeros((HEAD_DIM, tb), F32))
    carry = sweep(tasks, {(s, h): zero for s in range(nsub) for h in range(heads)})

    keys = [(s, h) for s in range(nsub) for h in range(heads)]
    flat = lambda c: tuple(v for key in keys for v in c[key])

    def cond(state):
        left = state[1]
        for idx in range(1, len(keys)):
            left = jnp.minimum(left, state[1 + 2 * idx])
        return (i * nsub + nsub - 3 - state[0] >= 0) & (jnp.min(left) < PRUNE_LOG * LOG2E)

    def body(state):
        t = state[0]
        more = []
        for s in range(nsub):
            j = i * nsub + s - 2 - t
            more.append((s, jnp.maximum(j, 0), None, jnp.where(j >= 0, 1.0, 0.0)))
        c = {key: (state[1 + 2 * idx], state[2 + 2 * idx]) for idx, key in enumerate(keys)}
        return (t + 1,) + flat(sweep(more, c))

    res = lax.while_loop(cond, body, (0,) + flat(carry))[1:]
    for s in range(nsub):
        for p in range(heads // 2):
            e, o = keys.index((s, 2 * p)), keys.index((s, 2 * p + 1))
            ot = jnp.concatenate([res[2 * e + 1], res[2 * o + 1]], axis=0)
            o_ref[s * tb:(s + 1) * tb, p * LANES:(p + 1) * LANES] = ot.T.astype(BF16)


def _sb_prompt(qt, k, vt, *, tq, tb):
    w, n = qt.shape
    return pl.pallas_call(
        functools.partial(_sb_kernel, tq=tq, tb=tb, heads=w // HEAD_DIM),
        grid=(n // tq,),
        in_specs=[
            pl.BlockSpec((w, tq), lambda i: (0, i)),
            pl.BlockSpec((n, w), lambda i: (0, 0), pipeline_mode=pl.Buffered(1)),
            pl.BlockSpec((w, n), lambda i: (0, 0), pipeline_mode=pl.Buffered(1)),
        ],
        out_specs=pl.BlockSpec((tq, w), lambda i: (i, 0)),
        out_shape=jax.ShapeDtypeStruct((n, w), BF16),
        compiler_params=_cparams(("arbitrary",)),
        name="sb_prompt",
    )(qt, k, vt)


def _sb_block(z, carry, v, tri, mask, v_feature_major=False):
    lg = jnp.log(1.0 + jnp.exp(-jnp.abs(z)))
    sp = jnp.maximum(z, 0.0) + lg
    if mask is not None:
        sp = jnp.where(mask, sp, 0.0)
    hi, lo = _split2(sp)
    later = _dot(hi, tri) + _dot(lo, tri)
    a = jnp.exp((jnp.minimum(z, 0.0) - lg) - later - carry)
    if mask is not None:
        a = jnp.where(mask, a, 0.0)
    pv = _dot_nt(a.astype(BF16), v) if v_feature_major else _dot(a.astype(BF16), v)
    return pv, carry + jnp.sum(sp, axis=1, keepdims=True)


def _suffix_matrix(tk):
    a = lax.broadcasted_iota(jnp.int32, (tk, tk), 0)
    b = lax.broadcasted_iota(jnp.int32, (tk, tk), 1)
    return jnp.where(a > b, 1.0, 0.0).astype(BF16)


def _head_rows(x, heads):
    lane_head = lax.broadcasted_iota(jnp.int32, x.shape, 1) // HEAD_DIM
    return jnp.concatenate(
        [jnp.where(lane_head == h, x, jnp.zeros_like(x)) for h in range(heads)], axis=0)


def _fold_heads(o, heads, s):
    lane_head = lax.broadcasted_iota(jnp.int32, (s, o.shape[1]), 1) // HEAD_DIM
    out = jnp.zeros((s, o.shape[1]), F32)
    for h in range(heads):
        out = jnp.where(lane_head == h, o[h * s:(h + 1) * s], out)
    return out


def _sample_attn_kernel(qf_ref, kf_ref, vf_ref, ck_ref, cv_ref, suf_ref, cum_ref,
                        qb_ref, kb_ref, vb_ref, cbk_ref, cbv_ref, of_ref, ob_ref,
                        *, s, h_fox, h_sb, past):
    qa = _head_rows(qf_ref[...], h_fox)
    rows = h_fox * s
    ck = ck_ref[0, 0].astype(BF16)
    cv = cv_ref[0, 0].astype(BF16)
    suf = suf_ref[0]
    cum = cum_ref[0]
    bias_c = jnp.concatenate(
        [jnp.broadcast_to(suf[h:h + 1, :], (s, past)) for h in range(h_fox)], axis=0)
    bias_n = jnp.concatenate(
        [jnp.broadcast_to(-cum[h:h + 1, :], (s, s)) for h in range(h_fox)], axis=0)
    lc = _dot(qa, ck) + bias_c
    ln = _dot_nt(qa, kf_ref[...]) + bias_n
    r_pos = lax.broadcasted_iota(jnp.int32, (rows, s), 0) % s
    k_pos = lax.broadcasted_iota(jnp.int32, (rows, s), 1)
    ln = jnp.where(k_pos <= r_pos, ln, NEG_BIG)
    m = jnp.maximum(jnp.max(lc, axis=1, keepdims=True), jnp.max(ln, axis=1, keepdims=True))
    pc = jnp.exp(lc - m)
    pn = jnp.exp(ln - m)
    den = jnp.sum(pc, axis=1, keepdims=True) + jnp.sum(pn, axis=1, keepdims=True)
    o = (_dot_nt(pc.astype(BF16), cv) + _dot(pn.astype(BF16), vf_ref[...])) / den
    of_ref[...] = _fold_heads(o, h_fox, s).astype(BF16)

    qb = _head_rows(qb_ref[...], h_sb)
    rows_b = h_sb * s
    rb = lax.broadcasted_iota(jnp.int32, (rows_b, s), 0) % s
    cb = lax.broadcasted_iota(jnp.int32, (rows_b, s), 1)
    acc, carry = _sb_block(_dot_nt(qb, kb_ref[...]), jnp.zeros((rows_b, 1), F32), vb_ref[...],
                           _suffix_matrix(s), cb < rb)
    tri = _suffix_matrix(LANES)

    def cond(state):
        return (state[0] < past // LANES) & (jnp.min(state[2]) < PRUNE_LOG)

    def body(state):
        t, acc, carry = state
        k0 = pl.multiple_of(past - (t + 1) * LANES, LANES)
        k = cbk_ref[0, 0, :, pl.ds(k0, LANES)].astype(BF16)
        v = cbv_ref[0, 0, :, pl.ds(k0, LANES)].astype(BF16)
        pv, carry = _sb_block(_dot(qb, k), carry, v, tri, None, v_feature_major=True)
        return t + 1, acc + pv, carry

    _, acc, _ = lax.while_loop(cond, body, (0, acc, carry))
    ob_ref[...] = _fold_heads(acc, h_sb, s).astype(BF16)


def _sample_attn(qf, kf, vf, ck, cv, suf, cum, qb, kb, vb, cbk, cbv, *, layer, batch, s):
    w_fox, w_sb = qf.shape[1], qb.shape[1]
    past = ck.shape[3]
    h_fox, h_sb = w_fox // HEAD_DIM, w_sb // HEAD_DIM
    new = lambda w: pl.BlockSpec((s, w), lambda b: (b, 0))
    cache = lambda w: pl.BlockSpec((1, 1, w, past), lambda b: (layer, b, 0, 0))
    kern = functools.partial(_sample_attn_kernel, s=s, h_fox=h_fox, h_sb=h_sb, past=past)
    return pl.pallas_call(
        kern,
        grid=(batch,),
        in_specs=[new(w_fox), new(w_fox), new(w_fox), cache(w_fox), cache(w_fox),
                  pl.BlockSpec((1, SUBLANES, past), lambda b: (b, 0, 0)),
                  pl.BlockSpec((1, SUBLANES, s), lambda b: (b, 0, 0)),
                  new(w_sb), new(w_sb), new(w_sb), cache(w_sb), cache(w_sb)],
        out_specs=[new(w_fox), new(w_sb)],
        out_shape=[jax.ShapeDtypeStruct((batch * s, w_fox), BF16),
                   jax.ShapeDtypeStruct((batch * s, w_sb), BF16)],
        compiler_params=_cparams(("arbitrary",)),
        name="sample_attn",
    )(qf, kf, vf, ck, cv, suf, cum, qb, kb, vb, cbk, cbv)


def _merge_kernel(x_ref, sc_ref, sh_ref, gt_ref, gmix_ref, ysgu_ref, ofox_ref, osb_ref,
                  wg_ref, bg_ref, wbs_ref, wbf_ref, wbb_ref, wo_ref, o_ref):
    x = x_ref[...]
    d = x.shape[1]
    h = _modulated_norm(x, gmix_ref[0], sc_ref[0], sh_ref[0])
    gates = _dot(h.astype(BF16), wg_ref[0]) + bg_ref[0]
    gates = 1.0 / (1.0 + jnp.exp(-gates))
    merged = gates[:, 0:d] * _dot(ysgu_ref[...], wbs_ref[0]) \
        + gates[:, d:2 * d] * _dot(ofox_ref[...], wbf_ref[0]) \
        + gates[:, 2 * d:3 * d] * _dot(osb_ref[...], wbb_ref[0])
    o_ref[...] = x + gt_ref[0] * _dot(merged.astype(BF16), wo_ref[0])


def _merge(x, mod, layer, gmix, ysgu, ofox, osb, wg, bg, wbs, wbf, wbb, wo, *, tm):
    n, d = x.shape
    row = lambda width: pl.BlockSpec((tm, width), lambda i: (i, 0))
    lay = lambda a: _layer_spec(a, layer)
    return pl.pallas_call(
        _merge_kernel,
        grid=(n // tm,),
        in_specs=[row(d), _mod_spec(mod, layer, 1, tm), _mod_spec(mod, layer, 0, tm),
                  _mod_spec(mod, layer, 2, tm), lay(gmix),
                  row(ysgu.shape[1]), row(ofox.shape[1]), row(osb.shape[1]),
                  lay(wg), lay(bg), lay(wbs), lay(wbf), lay(wbb), lay(wo)],
        out_specs=row(d),
        out_shape=jax.ShapeDtypeStruct((n, d), F32),
        compiler_params=_cparams(("arbitrary",)),
        name="merge",
    )(x, mod, mod, mod, gmix, ysgu, ofox, osb, wg, bg, wbs, wbf, wbb, wo)


def _ffn_kernel(x_ref, sc_ref, sh_ref, gt_ref, g_ref, wi_ref, wo_ref, o_ref, *, d_ff):
    for rows in _row_passes(x_ref.shape[0]):
        x = x_ref[rows]
        h = _modulated_norm(x, g_ref[0], _mod_rows(sc_ref, rows), _mod_rows(sh_ref, rows))
        ag = _dot(h.astype(BF16), wi_ref[0])
        a = ag[:, 0:d_ff]
        act = a * (1.0 / (1.0 + jnp.exp(-a))) * ag[:, d_ff:2 * d_ff]
        o_ref[rows] = x + _mod_rows(gt_ref, rows) * _dot(act.astype(BF16), wo_ref[0])


def _ffn(x, mod, layer, g, wi, wo, *, tm):
    n, d = x.shape
    d_ff = wo.shape[1]
    row = pl.BlockSpec((tm, d), lambda i: (i, 0))
    return pl.pallas_call(
        functools.partial(_ffn_kernel, d_ff=d_ff),
        grid=(n // tm,),
        in_specs=[row, _mod_spec(mod, layer, 4, tm), _mod_spec(mod, layer, 3, tm),
                  _mod_spec(mod, layer, 5, tm), _layer_spec(g, layer), _layer_spec(wi, layer),
                  _layer_spec(wo, layer)],
        out_specs=row,
        out_shape=jax.ShapeDtypeStruct((n, d), F32),
        compiler_params=_cparams(("arbitrary",)),
        name="ffn",
    )(x, mod, mod, mod, g, wi, wo)


class _Tiles(NamedTuple):
    dense: int
    ffn: int
    fox: int
    sb: int
    sb_sub: int


def _prompt_tiles(seq):
    return _Tiles(dense=min(512, seq), ffn=min(512, seq), fox=min(256, seq), sb=min(512, seq),
                  sb_sub=min(128, seq))


def _indicator(width, group):
    idx = np.arange(width) // group
    return jnp.asarray(idx[:, None] == idx[None, :], dtype=BF16)


def kernel(x_prompt, x_sample, c_prompt, c_sample, cache_fox_k, cache_fox_v, cache_fox_logf,
           cache_sb_k, cache_sb_v, w_ada, b_ada, g_mix, g_ffn, w_in, g_sgu_v, w_sgu, b_sgu, b_fgt,
           g_q, g_k, w_br_sgu, w_br_fox, w_br_sb, w_gate, b_gate, w_out, w_ffn_in, w_ffn_out):
    batch, seq, d = x_prompt.shape
    dec_batch, dec_seq, _ = x_sample.shape
    depth = w_ada.shape[0]
    past = cache_fox_k.shape[2]
    h_fox, h_sb = cache_fox_k.shape[3], cache_sb_k.shape[3]
    w_fox, w_sb = h_fox * HEAD_DIM, h_sb * HEAD_DIM
    g_sgu, cg = g_sgu_v.shape[1], g_sgu_v.shape[2]
    w_sgu_ = g_sgu * cg
    assert batch == 1 and g_sgu == G_SGU and w_sgu.shape[2] == SGU_LEN
    n_dec = dec_batch * dec_seq

    n_c = batch + dec_batch
    c_rows = -(-n_c // 8) * 8
    c_all = jnp.zeros((c_rows, d), F32).at[:n_c].set(jnp.concatenate([c_prompt, c_sample], axis=0))
    mod = _modulation(c_all, w_ada, b_ada)

    offs = np.cumsum([0, w_sgu_, w_sgu_, w_fox, w_fox, w_fox, h_fox, w_sb, w_sb, w_sb]).tolist()
    f_cols = jnp.zeros((depth, d, LANES), F32).at[:, :, :h_fox].set(w_in[:, :, offs[5]:offs[6]])
    w_main = jnp.concatenate([w_in[:, :, :offs[5]], w_in[:, :, offs[6]:], f_cols],
                             axis=2).astype(BF16)
    bf_pad = jnp.zeros((depth, 1, LANES), F32).at[:, 0, :h_fox].set(b_fgt)
    gmix3, gffn3, bg3 = g_mix.reshape(depth, 1, d), g_ffn.reshape(depth, 1, d), b_gate.reshape(depth, 1, 3 * d)
    mod_p = mod[:, 0:batch]
    mod_s = jnp.repeat(mod[:, batch:batch + dec_batch], dec_seq, axis=1)
    ind96, ind64 = _indicator(w_sgu_, cg), _indicator(LANES, HEAD_DIM)
    gq_t = jnp.tile(g_q, (1, h_fox)).reshape(depth, 1, w_fox)
    gk_t = jnp.tile(g_k, (1, h_fox)).reshape(depth, 1, w_fox)
    gsgu = g_sgu_v.reshape(depth, 1, w_sgu_)
    msgu_p = jnp.transpose(w_sgu, (0, 2, 1, 3)).reshape(depth, SGU_LEN, g_sgu * SGU_LEN)
    reps = SGU_LEN // dec_seq
    w_small = jnp.tile(w_sgu[:, :, :dec_seq, :dec_seq], (1, 1, reps, reps))
    msgu_s = jnp.transpose(w_small, (0, 2, 1, 3)).reshape(depth, SGU_LEN, g_sgu * SGU_LEN)
    bsgu_p = jnp.repeat(jnp.transpose(b_sgu, (0, 2, 1)), cg, axis=2)
    bsgu_s = jnp.tile(bsgu_p[:, :dec_seq], (1, reps, 1))
    dense_w = (w_gate, w_br_sgu, w_br_fox, w_br_sb, w_out, w_ffn_in, w_ffn_out)

    to_fm = lambda c: jnp.transpose(c, (0, 1, 3, 4, 2)).reshape(
        depth, dec_batch, c.shape[3] * HEAD_DIM, past)
    ck_t, cv_t, cbk_t, cbv_t = (to_fm(c) for c in (cache_fox_k, cache_fox_v, cache_sb_k, cache_sb_v))

    xp = x_prompt.reshape(seq, d)
    xs = x_sample.reshape(n_dec, d)
    tiles = _prompt_tiles(seq)
    stacks, logf_p, st_s = (), [], []
    for l in range(depth):
        shared = (l, gmix3, w_main, gsgu, gq_t, gk_t, bf_pad, ind96, ind64)

        (ysgu, kaug, qft, vft, qbt, kb16, vbt, *stacks, logft, fcumt) = _inproj(
            xp, mod_p, *shared, msgu_p, bsgu_p, tm=tiles.dense, period=SGU_LEN, sweep=True,
            prev_states=stacks)
        bound = 1.01 * HEAD_DIM ** 0.5 * jnp.max(jnp.abs(g_q[l])) * jnp.max(jnp.abs(g_k[l]))
        cut = (-(2.0 * bound + PRUNE_LOG)).reshape(1)
        ofox, *cast = _fox_prompt(
            fcumt[:, 0::LANES], fcumt[:, LANES - 1::LANES], cut, qft, kaug, vft, tq=tiles.fox,
            cast=[a.reshape(-1, a.shape[2]) for a in dense_w] if l == 0 else ())
        if l == 0:
            wg, wbs, wbf, wbb, wo, wfi, wfo = (c.reshape(a.shape) for c, a in zip(cast, dense_w))
            branch_w = (wg, bg3, wbs, wbf, wbb, wo)
        osb = _sb_prompt(qbt, kb16, vbt, tq=tiles.sb, tb=tiles.sb_sub)
        x1 = _merge(xp, mod_p, l, gmix3, ysgu, ofox, osb, *branch_w, tm=tiles.dense)
        xp = _ffn(x1, mod_p, l, gffn3, wfi, wfo, tm=tiles.ffn)
        logf_p.append(logft[:h_fox].T.reshape(batch, seq, h_fox))

        (ysgu, qf, kf16, vf16, qb, kb16, vb16, kf32, vf32, kb32, vb32, logft, sguv) = _inproj(
            xs, mod_s, *shared, msgu_s, bsgu_s, tm=n_dec, period=dec_seq, sweep=False)
        clf = jnp.zeros((dec_batch, SUBLANES, past), F32).at[:, :h_fox].set(
            jnp.transpose(cache_fox_logf[l], (0, 2, 1)))
        suf, cum = _sample_cumsum(clf.reshape(dec_batch * SUBLANES, past), logft, dec_seq)
        cum_b = jnp.transpose(cum.reshape(SUBLANES, dec_batch, dec_seq), (1, 0, 2))
        ofox, osb = _sample_attn(
            qf, kf16, vf16, ck_t, cv_t, suf.reshape(dec_batch, SUBLANES, past), cum_b,
            qb, kb16, vb16, cbk_t, cbv_t, layer=l, batch=dec_batch, s=dec_seq)
        x1 = _merge(xs, mod_s, l, gmix3, ysgu, ofox, osb, *branch_w, tm=n_dec)
        xs = _ffn(x1, mod_s, l, gffn3, wfi, wfo, tm=n_dec)
        st_s.append((kf32.reshape(dec_batch, dec_seq, h_fox, HEAD_DIM),
                     vf32.reshape(dec_batch, dec_seq, h_fox, HEAD_DIM),
                     logft[:h_fox].T.reshape(dec_batch, dec_seq, h_fox),
                     kb32.reshape(dec_batch, dec_seq, h_sb, HEAD_DIM),
                     vb32.reshape(dec_batch, dec_seq, h_sb, HEAD_DIM),
                     sguv.reshape(dec_batch, dec_seq, w_sgu_)))

    def stack(states, idx):
        return jnp.stack([s[idx] for s in states], axis=0)

    per_head = lambda st: jnp.transpose(
        st.reshape(depth, batch, st.shape[1] // HEAD_DIM, HEAD_DIM, seq), (0, 1, 4, 2, 3))
    kf_p, vf_p, kb_p, vb_p = (per_head(st) for st in stacks)
    return (xp.reshape(batch, seq, d), xs.reshape(dec_batch, dec_seq, d),
            kf_p, vf_p, jnp.stack(logf_p, axis=0), kb_p, vb_p,
            stack(st_s, 0), stack(st_s, 1), stack(st_s, 2), stack(st_s, 3), stack(st_s, 4),
            stack(st_s, 5))
```

```python
import functools
from typing import NamedTuple

import numpy as np
import jax
import jax.numpy as jnp
from jax import lax
from jax.experimental import pallas as pl
from jax.experimental.pallas import tpu as pltpu

F32 = jnp.float32
BF16 = jnp.bfloat16

EPS = 1e-6
HEAD_DIM = 64
LANES = 128
SUBLANES = 8
CHUNK = 64
SGU_LEN = 128
G_SGU = 4
NEG_BIG = -1e30
LOG2E = 1.4426950408889634

PRUNE_LOG = 30.0

VMEM_LIMIT = 56 * 1024 * 1024
PASS_ROWS = 256


def _cparams(sem):
    return pltpu.CompilerParams(dimension_semantics=sem, vmem_limit_bytes=VMEM_LIMIT)


def _const_spec(shape):
    nd = len(shape)
    return pl.BlockSpec(shape, lambda *_: (0,) * nd, pipeline_mode=pl.Buffered(1))


def _layer_spec(arr, layer):
    nd = arr.ndim
    return pl.BlockSpec((1,) + arr.shape[1:], lambda *_: (layer,) + (0,) * (nd - 1),
                        pipeline_mode=pl.Buffered(1))


def _mod_spec(mod, layer, k, tm):
    d = mod.shape[2] // 6
    if mod.shape[1] == 1:
        return pl.BlockSpec((1, 1, d), lambda i: (layer, 0, k))
    return pl.BlockSpec((1, tm, d), lambda i: (layer, i, k))


def _row_passes(tm):
    sub = min(tm, PASS_ROWS)
    return [slice(r0, r0 + sub) for r0 in range(0, tm, sub)]


def _mod_rows(ref, rows):
    return ref[0] if ref.shape[1] == 1 else ref[0, rows]


def _dot(a, b):
    return jnp.dot(a, b, preferred_element_type=F32)


def _dot_nt(a, b):
    return lax.dot_general(a, b, (((1,), (1,)), ((), ())), preferred_element_type=F32)


def _split3(x):
    h = x.astype(BF16)
    r = x - h.astype(F32)
    m = r.astype(BF16)
    l = (r - m.astype(F32)).astype(BF16)
    return h, m, l


def _split2(x):
    h = x.astype(BF16)
    l = (x - h.astype(F32)).astype(BF16)
    return h, l


def _mod_kernel(c_ref, w_ref, b_ref, o_ref):
    c = c_ref[...]
    s = c * (1.0 / (1.0 + jnp.exp(-c)))
    o_ref[0] = _dot(s.astype(BF16), w_ref[0].astype(BF16)) + b_ref[0]


def _modulation(c_all, w_ada, b_ada):
    depth, d, n6 = w_ada.shape
    rows = c_all.shape[0]
    tn = 1024
    return pl.pallas_call(
        _mod_kernel,
        grid=(depth, n6 // tn),
        in_specs=[
            pl.BlockSpec((rows, d), lambda l, j: (0, 0)),
            pl.BlockSpec((1, d, tn), lambda l, j: (l, 0, j)),
            pl.BlockSpec((1, 1, tn), lambda l, j: (l, 0, j)),
        ],
        out_specs=pl.BlockSpec((1, rows, tn), lambda l, j: (l, 0, j)),
        out_shape=jax.ShapeDtypeStruct((depth, rows, n6), F32),
        compiler_params=_cparams(("arbitrary", "arbitrary")),
        name="adaln_mod",
    )(c_all, w_ada, b_ada.reshape(depth, 1, n6))


def _modulated_norm(x, g, sc, sh):
    ms = jnp.mean(x * x, axis=-1, keepdims=True)
    return (x * lax.rsqrt(ms + EPS)) * g * (1.0 + sc) + sh


def _group_rms(t, ind, inv_size, g):
    sq = (t * t).astype(BF16)
    wb = ind.shape[0]
    ss = [_dot(sq[:, c:c + wb], ind) for c in range(0, t.shape[1], wb)]
    ss = jnp.concatenate(ss, axis=1) if len(ss) > 1 else ss[0]
    return t * lax.rsqrt(ss * inv_size + EPS) * g


def _log_sigmoid(x):
    return jnp.minimum(x, 0.0) - jnp.log(1.0 + jnp.exp(-jnp.abs(x)))


def _augmented_keys(kfn, f_cum, heads):
    lane = lax.broadcasted_iota(jnp.int32, f_cum.shape, 1)
    hi, mid, lo = (t.astype(F32) for t in _split3(f_cum * -LOG2E))
    aug_even = (pltpu.roll(hi, HEAD_DIM, 1) + pltpu.roll(mid, HEAD_DIM + SUBLANES, 1)
                + pltpu.roll(lo, HEAD_DIM + 2 * SUBLANES, 1))
    aug_odd = hi + pltpu.roll(mid, SUBLANES, 1) + pltpu.roll(lo, 2 * SUBLANES, 1)
    blocks = []
    for h in range(heads):
        kp = kfn[:, (h // 2) * LANES:(h // 2 + 1) * LANES]
        if h % 2 == 0:
            blocks.append(jnp.where(lane < HEAD_DIM, kp, aug_even))
        else:
            blocks.append(jnp.where(lane >= HEAD_DIM, kp, aug_odd))
    return jnp.concatenate(blocks, axis=1).astype(BF16)


def _inproj_kernel(x_ref, sc_ref, sh_ref, gmix_ref, w_ref, *refs, sweep, **static):
    run = None
    if sweep:
        carry_ref = refs[-1]
        @pl.when(pl.program_id(0) == 0)
        def _():
            carry_ref[...] = jnp.zeros(carry_ref.shape, F32)
        run = carry_ref[0:1, :]
    for rows in _row_passes(x_ref.shape[0]):
        h = _modulated_norm(x_ref[rows], gmix_ref[0], _mod_rows(sc_ref, rows), _mod_rows(sh_ref, rows))
        p = _dot(h.astype(BF16), w_ref[0])
        run = _inproj_rows(p, run, rows, *refs, sweep=sweep, **static)
    if sweep:
        carry_ref[...] = jnp.broadcast_to(run, carry_ref.shape)


def _inproj_rows(p, run, rows, gsgu_ref, gq_ref, gk_ref, bf_ref, ind96_ref, ind64_ref, msgu_ref,
                 bsgu_ref, *rest, w_sgu, w_fox, w_sb, period, sweep, layer):
    sub = p.shape[0]
    if sweep:
        prev = rest[1:1 + 4 * bool(layer)]
        (ysgu_ref, kaug_ref, qft_ref, vft_ref, qbt_ref, kb16_ref, vbt_ref,
         kft_ref, vft32_ref, kbt_ref, vbt32_ref, logft_ref, fcumt_ref, _) = rest[1 + len(prev):]
        tril_ref = rest[0]
    else:
        (ysgu_ref, qf_ref, kf16_ref, vf16_ref, qb_ref, kb16_ref, vb16_ref,
         kf32_ref, vf32_ref, kb32_ref, vb32_ref, logft_ref, sguv_ref) = rest
    o = 0
    u = p[:, o:o + w_sgu]; o += w_sgu
    vs = p[:, o:o + w_sgu]; o += w_sgu
    qf = p[:, o:o + w_fox]; o += w_fox
    kf = p[:, o:o + w_fox]; o += w_fox
    vf = p[:, o:o + w_fox]; o += w_fox
    qb = p[:, o:o + w_sb]; o += w_sb
    kb = p[:, o:o + w_sb]; o += w_sb
    vb = p[:, o:o + w_sb]; o += w_sb
    fl = p[:, o:o + LANES]

    scale = HEAD_DIM ** -0.5
    ind64 = ind64_ref[...]
    qfn = _group_rms(qf, ind64, 1.0 / HEAD_DIM, gq_ref[0])
    kfn = _group_rms(kf, ind64, 1.0 / HEAD_DIM, gk_ref[0])
    kb16_ref[rows] = kb.astype(BF16)
    lf = _log_sigmoid(fl + bf_ref[0])
    logft_ref[:, rows] = lf.T[0:SUBLANES, :]
    if sweep:
        qft_ref[:, rows] = (qfn * (scale * LOG2E)).T.astype(BF16)
        qbt_ref[:, rows] = (qb * (scale * LOG2E)).T.astype(BF16)
        vf_t, vb_t = vf.T, vb.T
        vft_ref[:, rows] = vf_t.astype(BF16)
        vbt_ref[:, rows] = vb_t.astype(BF16)
        for dst, own, earlier in zip((kft_ref, vft32_ref, kbt_ref, vbt32_ref),
                                     (kfn.T, vf_t, kb.T, vb_t), prev or (None,) * 4):
            if earlier is not None:
                dst[0:layer, :, rows] = earlier[:, :, rows]
            dst[layer, :, rows] = own
        lane = lax.broadcasted_iota(jnp.int32, lf.shape, 1)
        lfh, lfm, lfl = _split3(jnp.where(lane < SUBLANES, lf, 0.0))
        tril = tril_ref[...]
        chunks = []
        for c in range(0, sub, LANES):
            part = (_dot(tril, lfh[c:c + LANES]) + _dot(tril, lfm[c:c + LANES])
                    + _dot(tril, lfl[c:c + LANES]) + run)
            run = part[LANES - 1:LANES, :]
            chunks.append(part)
        f_cum = jnp.concatenate(chunks, axis=0) if len(chunks) > 1 else chunks[0]
        fcumt_ref[:, rows] = f_cum.T[0:SUBLANES, :]
        kaug_ref[rows] = _augmented_keys(kfn, f_cum, w_fox // HEAD_DIM)
    else:
        kf32_ref[rows] = kfn
        vf32_ref[rows] = vf
        kb32_ref[rows] = kb
        vb32_ref[rows] = vb
        qf_ref[rows] = (qfn * scale).astype(BF16)
        kf16_ref[rows] = kfn.astype(BF16)
        vf16_ref[rows] = vf.astype(BF16)
        qb_ref[rows] = (qb * scale).astype(BF16)
        vb16_ref[rows] = vb.astype(BF16)

    cg = w_sgu // G_SGU
    vsn = _group_rms(vs, ind96_ref[...], 1.0 / cg, gsgu_ref[0])
    if not sweep:
        sguv_ref[rows] = vsn
    r = lax.broadcasted_iota(jnp.int32, (SGU_LEN, G_SGU * SGU_LEN), 0)
    c = lax.broadcasted_iota(jnp.int32, (SGU_LEN, G_SGU * SGU_LEN), 1) % SGU_LEN
    keep = (r // period == c // period) & ((c % period) // CHUNK <= (r % period) // CHUNK)
    mix = jnp.where(keep, msgu_ref[0], 0.0).astype(BF16)
    lane_group = lax.broadcasted_iota(jnp.int32, (SGU_LEN, w_sgu), 1) // cg
    vsb = vsn.astype(BF16)
    spat = []
    for ci in range(sub // SGU_LEN):
        vc = vsb[ci * SGU_LEN:(ci + 1) * SGU_LEN]
        stacked = jnp.concatenate(
            [jnp.where(lane_group == g, vc, jnp.zeros_like(vc)) for g in range(G_SGU)], axis=0)
        spat.append(_dot(mix, stacked) + bsgu_ref[0])
    spat = jnp.concatenate(spat, axis=0) if len(spat) > 1 else spat[0]
    ysgu_ref[rows] = (u * spat).astype(BF16)
    return run


def _inproj(x, mod, layer, gmix, w, gsgu, gq, gk, bf, ind96, ind64, msgu, bsgu, *, tm, period,
            sweep, prev_states=()):
    n, d = x.shape
    w_sgu, w_fox = gsgu.shape[2], gq.shape[2]
    w_sb = (w.shape[2] - LANES - 2 * w_sgu - 3 * w_fox) // 3
    row = lambda width: pl.BlockSpec((tm, width), lambda i: (i, 0))
    col = lambda height: pl.BlockSpec((height, tm), lambda i: (0, i))
    sds = jax.ShapeDtypeStruct
    if sweep:
        stack = lambda width: pl.BlockSpec((layer + 1, width, tm), lambda i: (0, 0, i))
        states_specs = [stack(w_fox), stack(w_fox), stack(w_sb), stack(w_sb), col(SUBLANES)]
        states_shape = [sds((layer + 1, wd, n), F32) for wd in (w_fox, w_fox, w_sb, w_sb)] \
            + [sds((SUBLANES, n), F32)]
    else:
        states_specs = [row(w_fox), row(w_fox), row(w_sb), row(w_sb), col(SUBLANES)]
        states_shape = [sds((n, w_fox), F32), sds((n, w_fox), F32), sds((n, w_sb), F32),
                        sds((n, w_sb), F32), sds((SUBLANES, n), F32)]
    operands = [x, mod, mod, gmix, w, gsgu, gq, gk, bf, ind96, ind64, msgu, bsgu]
    lay = lambda a: _layer_spec(a, layer)
    in_specs = [row(d), _mod_spec(mod, layer, 1, tm), _mod_spec(mod, layer, 0, tm), lay(gmix), lay(w),
                lay(gsgu), lay(gq), lay(gk), lay(bf), _const_spec(ind96.shape),
                _const_spec(ind64.shape), lay(msgu), lay(bsgu)]
    scratch = []
    if sweep:
        heads = w_fox // HEAD_DIM
        a = np.arange(LANES)
        operands.append(jnp.asarray(a[None, :] <= a[:, None], dtype=BF16))
        in_specs.append(_const_spec((LANES, LANES)))
        for st in prev_states:
            operands.append(st)
            in_specs.append(pl.BlockSpec((layer, st.shape[1], tm), lambda i: (0, 0, i)))
        out_specs = [row(w_sgu), row(heads * LANES), col(w_fox), col(w_fox), col(w_sb), row(w_sb),
                     col(w_sb)] + states_specs + [col(SUBLANES)]
        out_shape = [sds((n, w_sgu), BF16), sds((n, heads * LANES), BF16), sds((w_fox, n), BF16),
                     sds((w_fox, n), BF16), sds((w_sb, n), BF16), sds((n, w_sb), BF16),
                     sds((w_sb, n), BF16)] + states_shape + [sds((SUBLANES, n), F32)]
        scratch = [pltpu.VMEM((SUBLANES, LANES), F32)]
    else:
        out_specs = [row(w_sgu), row(w_fox), row(w_fox), row(w_fox), row(w_sb), row(w_sb),
                     row(w_sb)] + states_specs + [row(w_sgu)]
        out_shape = [sds((n, w_sgu), BF16), sds((n, w_fox), BF16), sds((n, w_fox), BF16),
                     sds((n, w_fox), BF16), sds((n, w_sb), BF16), sds((n, w_sb), BF16),
                     sds((n, w_sb), BF16)] + states_shape + [sds((n, w_sgu), F32)]
    kern = functools.partial(_inproj_kernel, w_sgu=w_sgu, w_fox=w_fox, w_sb=w_sb,
                             period=period, sweep=sweep, layer=layer)
    return pl.pallas_call(
        kern,
        grid=(n // tm,),
        in_specs=in_specs,
        out_specs=out_specs,
        out_shape=out_shape,
        scratch_shapes=scratch,
        compiler_params=_cparams(("arbitrary",)),
        name="inproj",
    )(*operands)


def _seq_cumsum(x, nc, reverse_exclusive):
    rows = x.shape[0]
    a = lax.broadcasted_iota(jnp.int32, (LANES, LANES), 0)
    b = lax.broadcasted_iota(jnp.int32, (LANES, LANES), 1)
    tri = (a > b) if reverse_exclusive else (a <= b)
    tri = jnp.where(tri, 1.0, 0.0).astype(BF16)
    ones = jnp.ones((LANES, LANES), BF16)
    xh, xm, xl = _split3(x)
    within = _dot(xh, tri) + _dot(xm, tri) + _dot(xl, tri)
    tot = _dot(xh, ones) + _dot(xm, ones) + _dot(xl, ones)
    ra = lax.broadcasted_iota(jnp.int32, (rows, rows), 0)
    rb = lax.broadcasted_iota(jnp.int32, (rows, rows), 1)
    other = (rb > ra) if reverse_exclusive else (rb < ra)
    blk = jnp.where((ra // nc == rb // nc) & other, 1.0, 0.0).astype(BF16)
    th, tm_, tl = _split3(tot)
    return within + _dot(blk, th) + _dot(blk, tm_) + _dot(blk, tl)


def _sample_cumsum_kernel(clf_ref, lf_ref, suf_ref, cum_ref, *, nc, dec_seq):
    suf_ref[...] = _seq_cumsum(clf_ref[...], nc, reverse_exclusive=True)
    a = lax.broadcasted_iota(jnp.int32, (LANES, LANES), 0)
    b = lax.broadcasted_iota(jnp.int32, (LANES, LANES), 1)
    tri = jnp.where((a // dec_seq == b // dec_seq) & (a <= b), 1.0, 0.0).astype(BF16)
    xh, xm, xl = _split3(lf_ref[...])
    cum_ref[...] = _dot(xh, tri) + _dot(xm, tri) + _dot(xl, tri)


def _sample_cumsum(clogf_t, logft, dec_seq):
    rows, past = clogf_t.shape
    nc = past // LANES
    suf, cum = pl.pallas_call(
        functools.partial(_sample_cumsum_kernel, nc=nc, dec_seq=dec_seq),
        out_shape=[jax.ShapeDtypeStruct((rows * nc, LANES), F32),
                   jax.ShapeDtypeStruct(logft.shape, F32)],
        compiler_params=pltpu.CompilerParams(vmem_limit_bytes=VMEM_LIMIT),
        name="sample_logf_cumsum",
    )(clogf_t.reshape(rows * nc, LANES), logft)
    return suf.reshape(rows, past), cum


def _head_query_blocks(qt_ref, heads, tq, ones_rows, lane0=0):
    rowi = lax.broadcasted_iota(jnp.int32, (LANES, tq), 0)
    out = []
    for h in range(heads):
        qp = qt_ref[(h // 2) * LANES:(h // 2 + 1) * LANES, lane0:lane0 + tq]
        own = (rowi < HEAD_DIM) if h % 2 == 0 else (rowi >= HEAD_DIM)
        fill = jnp.zeros((LANES, tq), F32)
        for r in ones_rows(h):
            fill = jnp.where(rowi == r, 1.0, fill)
        out.append(jnp.where(own, qp, fill.astype(BF16)))
    return out


def _bias_rows(h):
    base = HEAD_DIM if h % 2 == 0 else 0
    return [base + h, base + SUBLANES + h, base + 2 * SUBLANES + h]


def _fox_kernel(fs_ref, fe_ref, cut_ref, qt_ref, ka_ref, vt_ref, *rest, tq, heads):
    n_cast = len(rest) // 2
    o_ref = rest[n_cast]
    for src_ref, dst_ref in zip(rest[:n_cast], rest[n_cast + 1:]):
        dst_ref[...] = src_ref[...].astype(BF16)
    i = pl.program_id(0)
    tk = tq
    cpb = tk // LANES
    qa = _head_query_blocks(qt_ref, heads, tq, _bias_rows)
    krow = lax.broadcasted_iota(jnp.int32, (tk, tq), 0)
    qcol = lax.broadcasted_iota(jnp.int32, (tk, tq), 1)

    def logits_of(j):
        k0 = pl.multiple_of(j * tk, tk)
        return tuple(_dot(ka_ref[pl.ds(k0, tk), h * LANES:(h + 1) * LANES], qa[h])
                     for h in range(heads))

    def absorb(j, logits, carry, mask):
        k0 = pl.multiple_of(j * tk, tk)
        out = []
        for h in range(heads):
            vt = vt_ref[h * HEAD_DIM:(h + 1) * HEAD_DIM, pl.ds(k0, tk)]
            m, l, acc = carry[3 * h:3 * h + 3]
            s = logits[h]
            if mask is not None:
                s = jnp.where(mask, s, NEG_BIG)
            mn = jnp.maximum(m, jnp.max(s, axis=0, keepdims=True))
            alpha = jnp.exp2(m - mn)
            pe = jnp.exp2(s - mn)
            l = alpha * l + jnp.sum(pe, axis=0, keepdims=True)
            acc = alpha * acc + _dot(vt, pe.astype(BF16))
            out += [mn, l, acc]
        return tuple(out)

    def run(blocks, carry):
        staged = [logits_of(j) for j, _ in blocks]
        for (j, mask), logits in zip(blocks, staged):
            carry = absorb(j, logits, carry, mask)
        return carry

    def live(j):
        jc = j * cpb + (cpb - 1)
        ok = fs_ref[0, i * cpb] - fe_ref[0, jc] >= cut_ref[0]
        for h in range(1, heads):
            ok = ok | (fs_ref[h, i * cpb] - fe_ref[h, jc] >= cut_ref[0])
        return ok

    init = []
    for _ in range(heads):
        init += [jnp.full((1, tq), NEG_BIG, F32), jnp.zeros((1, tq), F32),
                 jnp.zeros((HEAD_DIM, tq), F32)]
    carry = run([(i, krow <= qcol), (jnp.maximum(i - 1, 0), jnp.broadcast_to(i > 0, (tk, tq)))],
                tuple(init))

    def cond(state):
        return (state[0] >= 1) & live(jnp.maximum(state[0] - 1, 0))

    def body(state):
        j = state[0]
        return (j - 2,) + run([(j, None), (j - 1, None)], state[1:])

    state = lax.while_loop(cond, body, (i - 2,) + carry)
    j = jnp.maximum(state[0], 0)
    last = (state[0] >= 0) & live(j)
    res = lax.cond(last, lambda c: run([(j, None)], c), lambda c: c, state[1:])
    for p in range(heads // 2):
        _, l0, a0, _, l1, a1 = res[6 * p:6 * p + 6]
        ot = jnp.concatenate([a0 / l0, a1 / l1], axis=0)
        o_ref[:, p * LANES:(p + 1) * LANES] = ot.T.astype(BF16)


def _cast_spec(rows, cols, steps):
    rb = next(r for r in range(2 * SUBLANES, rows + 1, 2 * SUBLANES)
              if rows % r == 0 and rows // r <= steps)
    last = rows // rb - 1
    return pl.BlockSpec((rb, cols), lambda i, *_: (jnp.minimum(i, last), 0))


def _fox_prompt(fs, fe, cut, qt, kaug, vt, *, tq, cast=()):
    w, n = qt.shape
    heads = w // HEAD_DIM
    steps = n // tq
    cast_specs = [_cast_spec(a.shape[0], a.shape[1], steps) for a in cast]
    grid_spec = pltpu.PrefetchScalarGridSpec(
        num_scalar_prefetch=3,
        grid=(steps,),
        in_specs=[
            pl.BlockSpec((w, tq), lambda i, *_: (0, i)),
            pl.BlockSpec(kaug.shape, lambda i, *_: (0, 0), pipeline_mode=pl.Buffered(1)),
            pl.BlockSpec((w, n), lambda i, *_: (0, 0), pipeline_mode=pl.Buffered(1)),
        ] + cast_specs,
        out_specs=[pl.BlockSpec((tq, w), lambda i, *_: (i, 0))] + cast_specs,
    )
    return pl.pallas_call(
        functools.partial(_fox_kernel, tq=tq, heads=heads),
        grid_spec=grid_spec,
        out_shape=[jax.ShapeDtypeStruct((n, w), BF16)]
        + [jax.ShapeDtypeStruct(a.shape, BF16) for a in cast],
        compiler_params=_cparams(("arbitrary",)),
        name="fox_prompt",
    )(fs, fe, cut, qt, kaug, vt, *cast)


def _sb_kernel(qt_ref, k_ref, vt_ref, o_ref, *, tq, tb, heads):
    i = pl.program_id(0)
    nsub = tq // tb
    qm = [_head_query_blocks(qt_ref, heads, tb, lambda h: [], lane0=s * tb) for s in range(nsub)]
    krow = lax.broadcasted_iota(jnp.int32, (tb, tb), 0)
    qcol = lax.broadcasted_iota(jnp.int32, (tb, tb), 1)
    a = lax.broadcasted_iota(jnp.int32, (tb, 2 * tb), 0)
    b = lax.broadcasted_iota(jnp.int32, (tb, 2 * tb), 1) % tb
    tri2 = jnp.where(b > a, 1.0, 0.0).astype(BF16)

    def sweep(tasks, carry):
        k0s = [pl.multiple_of(t[1] * tb, tb) for t in tasks]
        z = [[_dot(k_ref[pl.ds(k0, tb), (h // 2) * LANES:(h // 2 + 1) * LANES], qm[t[0]][h])
              for h in range(heads)] for t, k0 in zip(tasks, k0s)]
        logsig, later, mass = {}, {}, {}
        for ti, (_, _, mask, valid) in enumerate(tasks):
            for h in range(heads):
                zz = z[ti][h]
                sp = jnp.maximum(zz, 0.0) + jnp.log2(1.0 + jnp.exp2(-jnp.abs(zz)))
                logsig[ti, h] = zz - sp
                if mask is not None:
                    sp = jnp.where(mask, sp, 0.0)
                hi, lo = _split2(sp)
                later[ti, h] = _dot(tri2, jnp.concatenate([hi, lo], axis=0))
                mass[ti, h] = jnp.sum(sp, axis=0, keepdims=True)
                if valid is not None:
                    mass[ti, h] = mass[ti, h] * valid
        carry = dict(carry)
        for ti, (s, _, mask, valid) in enumerate(tasks):
            for h in range(heads):
                cr, acc = carry[s, h]
                w = jnp.exp2(logsig[ti, h] - later[ti, h] - cr)
                if mask is not None:
                    w = jnp.where(mask, w, 0.0)
                if valid is not None:
                    w = w * valid
                vt = vt_ref[h * HEAD_DIM:(h + 1) * HEAD_DIM, pl.ds(k0s[ti], tb)]
                carry[s, h] = (cr + mass[ti, h], acc + _dot(vt, w.astype(BF16)))
        return carry

    tasks = []
    for s in range(nsub):
        qb = i * nsub + s
        tasks.append((s, qb, krow < qcol, None))
        tasks.append((s, jnp.maximum(qb - 1, 0), None, jnp.where(qb > 0, 1.0, 0.0) if s == 0 else None))
    zero = (jnp.zeros((1, tb), F32), jnp.zeros((HEAD_DIM, tb), F32))
    carry = sweep(tasks, {(s, h): zero for s in range(nsub) for h in range(heads)})

    keys = [(s, h) for s in range(nsub) for h in range(heads)]
    flat = lambda c: tuple(v for key in keys for v in c[key])

    def cond(state):
        left = state[1]
        for idx in range(1, len(keys)):
            left = jnp.minimum(left, state[1 + 2 * idx])
        return (i * nsub + nsub - 3 - state[0] >= 0) & (jnp.min(left) < PRUNE_LOG * LOG2E)

    def body(state):
        t = state[0]
        more = []
        for s in range(nsub):
            j = i * nsub + s - 2 - t
            more.append((s, jnp.maximum(j, 0), None, jnp.where(j >= 0, 1.0, 0.0)))
        c = {key: (state[1 + 2 * idx], state[2 + 2 * idx]) for idx, key in enumerate(keys)}
        return (t + 1,) + flat(sweep(more, c))

    res = lax.while_loop(cond, body, (0,) + flat(carry))[1:]
    for s in range(nsub):
        for p in range(heads // 2):
            e, o = keys.index((s, 2 * p)), keys.index((s, 2 * p + 1))
            ot = jnp.concatenate([res[2 * e + 1], res[2 * o + 1]], axis=0)
            o_ref[s * tb:(s + 1) * tb, p * LANES:(p + 1) * LANES] = ot.T.astype(BF16)


def _sb_prompt(qt, k, vt, *, tq, tb):
    w, n = qt.shape
    return pl.pallas_call(
        functools.partial(_sb_kernel, tq=tq, tb=tb, heads=w // HEAD_DIM),
        grid=(n // tq,),
        in_specs=[
            pl.BlockSpec((w, tq), lambda i: (0, i)),
            pl.BlockSpec((n, w), lambda i: (0, 0), pipeline_mode=pl.Buffered(1)),
            pl.BlockSpec((w, n), lambda i: (0, 0), pipeline_mode=pl.Buffered(1)),
        ],
        out_specs=pl.BlockSpec((tq, w), lambda i: (i, 0)),
        out_shape=jax.ShapeDtypeStruct((n, w), BF16),
        compiler_params=_cparams(("arbitrary",)),
        name="sb_prompt",
    )(qt, k, vt)


def _sb_block(z, carry, v, tri, mask, v_feature_major=False):
    lg = jnp.log(1.0 + jnp.exp(-jnp.abs(z)))
    sp = jnp.maximum(z, 0.0) + lg
    if mask is not None:
        sp = jnp.where(mask, sp, 0.0)
    hi, lo = _split2(sp)
    later = _dot(hi, tri) + _dot(lo, tri)
    a = jnp.exp((jnp.minimum(z, 0.0) - lg) - later - carry)
    if mask is not None:
        a = jnp.where(mask, a, 0.0)
    pv = _dot_nt(a.astype(BF16), v) if v_feature_major else _dot(a.astype(BF16), v)
    return pv, carry + jnp.sum(sp, axis=1, keepdims=True)


def _suffix_matrix(tk):
    a = lax.broadcasted_iota(jnp.int32, (tk, tk), 0)
    b = lax.broadcasted_iota(jnp.int32, (tk, tk), 1)
    return jnp.where(a > b, 1.0, 0.0).astype(BF16)


def _head_rows(x, heads):
    lane_head = lax.broadcasted_iota(jnp.int32, x.shape, 1) // HEAD_DIM
    return jnp.concatenate(
        [jnp.where(lane_head == h, x, jnp.zeros_like(x)) for h in range(heads)], axis=0)


def _fold_heads(o, heads, s):
    lane_head = lax.broadcasted_iota(jnp.int32, (s, o.shape[1]), 1) // HEAD_DIM
    out = jnp.zeros((s, o.shape[1]), F32)
    for h in range(heads):
        out = jnp.where(lane_head == h, o[h * s:(h + 1) * s], out)
    return out


def _sample_attn_kernel(qf_ref, kf_ref, vf_ref, ck_ref, cv_ref, suf_ref, cum_ref,
                        qb_ref, kb_ref, vb_ref, cbk_ref, cbv_ref, of_ref, ob_ref,
                        *, s, h_fox, h_sb, past):
    qa = _head_rows(qf_ref[...], h_fox)
    rows = h_fox * s
    ck = ck_ref[0, 0].astype(BF16)
    cv = cv_ref[0, 0].astype(BF16)
    suf = suf_ref[0]
    cum = cum_ref[0]
    bias_c = jnp.concatenate(
        [jnp.broadcast_to(suf[h:h + 1, :], (s, past)) for h in range(h_fox)], axis=0)
    bias_n = jnp.concatenate(
        [jnp.broadcast_to(-cum[h:h + 1, :], (s, s)) for h in range(h_fox)], axis=0)
    lc = _dot(qa, ck) + bias_c
    ln = _dot_nt(qa, kf_ref[...]) + bias_n
    r_pos = lax.broadcasted_iota(jnp.int32, (rows, s), 0) % s
    k_pos = lax.broadcasted_iota(jnp.int32, (rows, s), 1)
    ln = jnp.where(k_pos <= r_pos, ln, NEG_BIG)
    m = jnp.maximum(jnp.max(lc, axis=1, keepdims=True), jnp.max(ln, axis=1, keepdims=True))
    pc = jnp.exp(lc - m)
    pn = jnp.exp(ln - m)
    den = jnp.sum(pc, axis=1, keepdims=True) + jnp.sum(pn, axis=1, keepdims=True)
    o = (_dot_nt(pc.astype(BF16), cv) + _dot(pn.astype(BF16), vf_ref[...])) / den
    of_ref[...] = _fold_heads(o, h_fox, s).astype(BF16)

    qb = _head_rows(qb_ref[...], h_sb)
    rows_b = h_sb * s
    rb = lax.broadcasted_iota(jnp.int32, (rows_b, s), 0) % s
    cb = lax.broadcasted_iota(jnp.int32, (rows_b, s), 1)
    acc, carry = _sb_block(_dot_nt(qb, kb_ref[...]), jnp.zeros((rows_b, 1), F32), vb_ref[...],
                           _suffix_matrix(s), cb < rb)
    tri = _suffix_matrix(LANES)

    def cond(state):
        return (state[0] < past // LANES) & (jnp.min(state[2]) < PRUNE_LOG)

    def body(state):
        t, acc, carry = state
        k0 = pl.multiple_of(past - (t + 1) * LANES, LANES)
        k = cbk_ref[0, 0, :, pl.ds(k0, LANES)].astype(BF16)
        v = cbv_ref[0, 0, :, pl.ds(k0, LANES)].astype(BF16)
        pv, carry = _sb_block(_dot(qb, k), carry, v, tri, None, v_feature_major=True)
        return t + 1, acc + pv, carry

    _, acc, _ = lax.while_loop(cond, body, (0, acc, carry))
    ob_ref[...] = _fold_heads(acc, h_sb, s).astype(BF16)


def _sample_attn(qf, kf, vf, ck, cv, suf, cum, qb, kb, vb, cbk, cbv, *, layer, batch, s):
    w_fox, w_sb = qf.shape[1], qb.shape[1]
    past = ck.shape[3]
    h_fox, h_sb = w_fox // HEAD_DIM, w_sb // HEAD_DIM
    new = lambda w: pl.BlockSpec((s, w), lambda b: (b, 0))
    cache = lambda w: pl.BlockSpec((1, 1, w, past), lambda b: (layer, b, 0, 0))
    kern = functools.partial(_sample_attn_kernel, s=s, h_fox=h_fox, h_sb=h_sb, past=past)
    return pl.pallas_call(
        kern,
        grid=(batch,),
        in_specs=[new(w_fox), new(w_fox), new(w_fox), cache(w_fox), cache(w_fox),
                  pl.BlockSpec((1, SUBLANES, past), lambda b: (b, 0, 0)),
                  pl.BlockSpec((1, SUBLANES, s), lambda b: (b, 0, 0)),
                  new(w_sb), new(w_sb), new(w_sb), cache(w_sb), cache(w_sb)],
        out_specs=[new(w_fox), new(w_sb)],
        out_shape=[jax.ShapeDtypeStruct((batch * s, w_fox), BF16),
                   jax.ShapeDtypeStruct((batch * s, w_sb), BF16)],
        compiler_params=_cparams(("arbitrary",)),
        name="sample_attn",
    )(qf, kf, vf, ck, cv, suf, cum, qb, kb, vb, cbk, cbv)


def _merge_kernel(*refs):
    prompt_refs, sample_refs = refs[0:7], refs[7:14]
    gmix_ref, wg_ref, bg_ref, wbs_ref, wbf_ref, wbb_ref, wo_ref, o_ref, os_ref = refs[14:]

    def tile(x_ref, sc_ref, sh_ref, gt_ref, ysgu_ref, ofox_ref, osb_ref, o_ref):
        x = x_ref[...]
        d = x.shape[1]
        h = _modulated_norm(x, gmix_ref[0], sc_ref[0], sh_ref[0])
        gates = _dot(h.astype(BF16), wg_ref[0]) + bg_ref[0]
        gates = 1.0 / (1.0 + jnp.exp(-gates))
        merged = gates[:, 0:d] * _dot(ysgu_ref[...], wbs_ref[0]) \
            + gates[:, d:2 * d] * _dot(ofox_ref[...], wbf_ref[0]) \
            + gates[:, 2 * d:3 * d] * _dot(osb_ref[...], wbb_ref[0])
        o_ref[...] = x + gt_ref[0] * _dot(merged.astype(BF16), wo_ref[0])

    last = pl.num_programs(0) - 1

    @pl.when(pl.program_id(0) < last)
    def _():
        tile(*prompt_refs, o_ref)

    @pl.when(pl.program_id(0) == last)
    def _():
        tile(*sample_refs, os_ref)


def _merge(prompt, sample, layer, gmix, wg, bg, wbs, wbf, wbb, wo, *, tm):
    x, mod, *branches = prompt
    xs, mod_s, *branches_s = sample
    n, d = x.shape
    ns = xs.shape[0]
    steps = n // tm
    assert mod.shape[1] == 1 and mod_s.shape[1] == ns
    row = lambda width: pl.BlockSpec((tm, width), lambda i: (jnp.minimum(i, steps - 1), 0))
    row_s = lambda width: pl.BlockSpec((ns, width), lambda i: (0, 0))
    mod_spec_s = lambda k: pl.BlockSpec((1, ns, d), lambda i: (layer, 0, k))
    lay = lambda a: _layer_spec(a, layer)
    return pl.pallas_call(
        _merge_kernel,
        grid=(steps + 1,),
        in_specs=[row(d), _mod_spec(mod, layer, 1, tm), _mod_spec(mod, layer, 0, tm),
                  _mod_spec(mod, layer, 2, tm), *(row(b.shape[1]) for b in branches),
                  row_s(d), mod_spec_s(1), mod_spec_s(0), mod_spec_s(2),
                  *(row_s(b.shape[1]) for b in branches_s),
                  lay(gmix), lay(wg), lay(bg), lay(wbs), lay(wbf), lay(wbb), lay(wo)],
        out_specs=[row(d), row_s(d)],
        out_shape=[jax.ShapeDtypeStruct((n, d), F32), jax.ShapeDtypeStruct((ns, d), F32)],
        compiler_params=_cparams(("arbitrary",)),
        name="merge",
    )(x, mod, mod, mod, *branches, xs, mod_s, mod_s, mod_s, *branches_s,
      gmix, wg, bg, wbs, wbf, wbb, wo)


def _ffn_kernel(x_ref, sc_ref, sh_ref, gt_ref, xs_ref, scs_ref, shs_ref, gts_ref, g_ref, wi_ref,
                wo_ref, o_ref, os_ref, *, d_ff):
    def tile(x_ref, sc_ref, sh_ref, gt_ref, o_ref):
        for rows in _row_passes(x_ref.shape[0]):
            x = x_ref[rows]
            h = _modulated_norm(x, g_ref[0], _mod_rows(sc_ref, rows), _mod_rows(sh_ref, rows))
            ag = _dot(h.astype(BF16), wi_ref[0])
            a = ag[:, 0:d_ff]
            act = a * (1.0 / (1.0 + jnp.exp(-a))) * ag[:, d_ff:2 * d_ff]
            o_ref[rows] = x + _mod_rows(gt_ref, rows) * _dot(act.astype(BF16), wo_ref[0])

    last = pl.num_programs(0) - 1

    @pl.when(pl.program_id(0) < last)
    def _():
        tile(x_ref, sc_ref, sh_ref, gt_ref, o_ref)

    @pl.when(pl.program_id(0) == last)
    def _():
        tile(xs_ref, scs_ref, shs_ref, gts_ref, os_ref)


def _ffn(x, mod, xs, mod_s, layer, g, wi, wo, *, tm):
    n, d = x.shape
    ns = xs.shape[0]
    d_ff = wo.shape[1]
    steps = n // tm
    assert mod.shape[1] == 1 and mod_s.shape[1] == ns
    row = pl.BlockSpec((tm, d), lambda i: (jnp.minimum(i, steps - 1), 0))
    row_s = pl.BlockSpec((ns, d), lambda i: (0, 0))
    mod_spec_s = lambda k: pl.BlockSpec((1, ns, d), lambda i: (layer, 0, k))
    return pl.pallas_call(
        functools.partial(_ffn_kernel, d_ff=d_ff),
        grid=(steps + 1,),
        in_specs=[row, _mod_spec(mod, layer, 4, tm), _mod_spec(mod, layer, 3, tm),
                  _mod_spec(mod, layer, 5, tm), row_s, mod_spec_s(4), mod_spec_s(3), mod_spec_s(5),
                  _layer_spec(g, layer), _layer_spec(wi, layer), _layer_spec(wo, layer)],
        out_specs=[row, row_s],
        out_shape=[jax.ShapeDtypeStruct((n, d), F32), jax.ShapeDtypeStruct((ns, d), F32)],
        compiler_params=_cparams(("arbitrary",)),
        name="ffn",
    )(x, mod, mod, mod, xs, mod_s, mod_s, mod_s, g, wi, wo)


class _Tiles(NamedTuple):
    dense: int
    ffn: int
    fox: int
    sb: int
    sb_sub: int


def _prompt_tiles(seq):
    return _Tiles(dense=min(512, seq), ffn=min(512, seq), fox=min(256, seq), sb=min(512, seq),
                  sb_sub=min(128, seq))


def _indicator(width, group):
    idx = np.arange(width) // group
    return jnp.asarray(idx[:, None] == idx[None, :], dtype=BF16)


def kernel(x_prompt, x_sample, c_prompt, c_sample, cache_fox_k, cache_fox_v, cache_fox_logf,
           cache_sb_k, cache_sb_v, w_ada, b_ada, g_mix, g_ffn, w_in, g_sgu_v, w_sgu, b_sgu, b_fgt,
           g_q, g_k, w_br_sgu, w_br_fox, w_br_sb, w_gate, b_gate, w_out, w_ffn_in, w_ffn_out):
    batch, seq, d = x_prompt.shape
    dec_batch, dec_seq, _ = x_sample.shape
    depth = w_ada.shape[0]
    past = cache_fox_k.shape[2]
    h_fox, h_sb = cache_fox_k.shape[3], cache_sb_k.shape[3]
    w_fox, w_sb = h_fox * HEAD_DIM, h_sb * HEAD_DIM
    g_sgu, cg = g_sgu_v.shape[1], g_sgu_v.shape[2]
    w_sgu_ = g_sgu * cg
    assert batch == 1 and g_sgu == G_SGU and w_sgu.shape[2] == SGU_LEN
    n_dec = dec_batch * dec_seq

    n_c = batch + dec_batch
    c_rows = -(-n_c // 8) * 8
    c_all = jnp.zeros((c_rows, d), F32).at[:n_c].set(jnp.concatenate([c_prompt, c_sample], axis=0))
    mod = _modulation(c_all, w_ada, b_ada)

    offs = np.cumsum([0, w_sgu_, w_sgu_, w_fox, w_fox, w_fox, h_fox, w_sb, w_sb, w_sb]).tolist()
    f_cols = jnp.zeros((depth, d, LANES), F32).at[:, :, :h_fox].set(w_in[:, :, offs[5]:offs[6]])
    w_main = jnp.concatenate([w_in[:, :, :offs[5]], w_in[:, :, offs[6]:], f_cols],
                             axis=2).astype(BF16)
    bf_pad = jnp.zeros((depth, 1, LANES), F32).at[:, 0, :h_fox].set(b_fgt)
    gmix3, gffn3, bg3 = g_mix.reshape(depth, 1, d), g_ffn.reshape(depth, 1, d), b_gate.reshape(depth, 1, 3 * d)
    mod_p = mod[:, 0:batch]
    mod_s = jnp.repeat(mod[:, batch:batch + dec_batch], dec_seq, axis=1)
    ind96, ind64 = _indicator(w_sgu_, cg), _indicator(LANES, HEAD_DIM)
    gq_t = jnp.tile(g_q, (1, h_fox)).reshape(depth, 1, w_fox)
    gk_t = jnp.tile(g_k, (1, h_fox)).reshape(depth, 1, w_fox)
    gsgu = g_sgu_v.reshape(depth, 1, w_sgu_)
    msgu_p = jnp.transpose(w_sgu, (0, 2, 1, 3)).reshape(depth, SGU_LEN, g_sgu * SGU_LEN)
    reps = SGU_LEN // dec_seq
    w_small = jnp.tile(w_sgu[:, :, :dec_seq, :dec_seq], (1, 1, reps, reps))
    msgu_s = jnp.transpose(w_small, (0, 2, 1, 3)).reshape(depth, SGU_LEN, g_sgu * SGU_LEN)
    bsgu_p = jnp.repeat(jnp.transpose(b_sgu, (0, 2, 1)), cg, axis=2)
    bsgu_s = jnp.tile(bsgu_p[:, :dec_seq], (1, reps, 1))
    dense_w = (w_gate, w_br_sgu, w_br_fox, w_br_sb, w_out, w_ffn_in, w_ffn_out)

    to_fm = lambda c: jnp.transpose(c, (0, 1, 3, 4, 2)).reshape(
        depth, dec_batch, c.shape[3] * HEAD_DIM, past)
    ck_t, cv_t, cbk_t, cbv_t = (to_fm(c) for c in (cache_fox_k, cache_fox_v, cache_sb_k, cache_sb_v))

    xp = x_prompt.reshape(seq, d)
    xs = x_sample.reshape(n_dec, d)
    tiles = _prompt_tiles(seq)
    stacks, logf_p, st_s = (), [], []
    for l in range(depth):
        shared = (l, gmix3, w_main, gsgu, gq_t, gk_t, bf_pad, ind96, ind64)

        (ysgu, kaug, qft, vft, qbt, kb16, vbt, *stacks, logft, fcumt) = _inproj(
            xp, mod_p, *shared, msgu_p, bsgu_p, tm=tiles.dense, period=SGU_LEN, sweep=True,
            prev_states=stacks)
        bound = 1.01 * HEAD_DIM ** 0.5 * jnp.max(jnp.abs(g_q[l])) * jnp.max(jnp.abs(g_k[l]))
        cut = (-(2.0 * bound + PRUNE_LOG)).reshape(1)
        ofox, *cast = _fox_prompt(
            fcumt[:, 0::LANES], fcumt[:, LANES - 1::LANES], cut, qft, kaug, vft, tq=tiles.fox,
            cast=[a.reshape(-1, a.shape[2]) for a in dense_w] if l == 0 else ())
        if l == 0:
            wg, wbs, wbf, wbb, wo, wfi, wfo = (c.reshape(a.shape) for c, a in zip(cast, dense_w))
            branch_w = (wg, bg3, wbs, wbf, wbb, wo)
        osb = _sb_prompt(qbt, kb16, vbt, tq=tiles.sb, tb=tiles.sb_sub)
        prompt = (xp, mod_p, ysgu, ofox, osb)
        logf_p.append(logft[:h_fox].T.reshape(batch, seq, h_fox))

        (ysgu, qf, kf16, vf16, qb, kb16, vb16, kf32, vf32, kb32, vb32, logft, sguv) = _inproj(
            xs, mod_s, *shared, msgu_s, bsgu_s, tm=n_dec, period=dec_seq, sweep=False)
        clf = jnp.zeros((dec_batch, SUBLANES, past), F32).at[:, :h_fox].set(
            jnp.transpose(cache_fox_logf[l], (0, 2, 1)))
        suf, cum = _sample_cumsum(clf.reshape(dec_batch * SUBLANES, past), logft, dec_seq)
        cum_b = jnp.transpose(cum.reshape(SUBLANES, dec_batch, dec_seq), (1, 0, 2))
        ofox, osb = _sample_attn(
            qf, kf16, vf16, ck_t, cv_t, suf.reshape(dec_batch, SUBLANES, past), cum_b,
            qb, kb16, vb16, cbk_t, cbv_t, layer=l, batch=dec_batch, s=dec_seq)
        x1p, x1s = _merge(prompt, (xs, mod_s, ysgu, ofox, osb), l, gmix3, *branch_w, tm=tiles.dense)
        xp, xs = _ffn(x1p, mod_p, x1s, mod_s, l, gffn3, wfi, wfo, tm=tiles.ffn)
        st_s.append((kf32.reshape(dec_batch, dec_seq, h_fox, HEAD_DIM),
                     vf32.reshape(dec_batch, dec_seq, h_fox, HEAD_DIM),
                     logft[:h_fox].T.reshape(dec_batch, dec_seq, h_fox),
                     kb32.reshape(dec_batch, dec_seq, h_sb, HEAD_DIM),
                     vb32.reshape(dec_batch, dec_seq, h_sb, HEAD_DIM),
                     sguv.reshape(dec_batch, dec_seq, w_sgu_)))

    def stack(states, idx):
        return jnp.stack([s[idx] for s in states], axis=0)

    per_head = lambda st: jnp.transpose(
        st.reshape(depth, batch, st.shape[1] // HEAD_DIM, HEAD_DIM, seq), (0, 1, 4, 2, 3))
    kf_p, vf_p, kb_p, vb_p = (per_head(st) for st in stacks)
    return (xp.reshape(batch, seq, d), xs.reshape(dec_batch, dec_seq, d),
            kf_p, vf_p, jnp.stack(logf_p, axis=0), kb_p, vb_p,
            stack(st_s, 0), stack(st_s, 1), stack(st_s, 2), stack(st_s, 3), stack(st_s, 4),
            stack(st_s, 5))
```

```python
import functools
from typing import NamedTuple

import numpy as np
import jax
import jax.numpy as jnp
from jax import lax
from jax.experimental import pallas as pl
from jax.experimental.pallas import tpu as pltpu

F32 = jnp.float32
BF16 = jnp.bfloat16

EPS = 1e-6
HEAD_DIM = 64
LANES = 128
SUBLANES = 8
CHUNK = 64
SGU_LEN = 128
G_SGU = 4
NEG_BIG = -1e30
LOG2E = 1.4426950408889634

PRUNE_LOG = 30.0

VMEM_LIMIT = 60 * 1024 * 1024
PASS_ROWS = 256


def _cparams(sem):
    return pltpu.CompilerParams(dimension_semantics=sem, vmem_limit_bytes=VMEM_LIMIT)


def _const_spec(shape):
    nd = len(shape)
    return pl.BlockSpec(shape, lambda *_: (0,) * nd, pipeline_mode=pl.Buffered(1))


def _layer_spec(arr, layer):
    nd = arr.ndim
    return pl.BlockSpec((1,) + arr.shape[1:], lambda *_: (layer,) + (0,) * (nd - 1),
                        pipeline_mode=pl.Buffered(1))


def _mod_spec(mod, layer, k, tm):
    d = mod.shape[2] // 6
    if mod.shape[1] == 1:
        return pl.BlockSpec((1, 1, d), lambda i: (layer, 0, k))
    return pl.BlockSpec((1, tm, d), lambda i: (layer, i, k))


def _row_passes(tm):
    sub = min(tm, PASS_ROWS)
    return [slice(r0, r0 + sub) for r0 in range(0, tm, sub)]


def _mod_rows(ref, rows):
    return ref[0] if ref.shape[1] == 1 else ref[0, rows]


def _dot(a, b):
    return jnp.dot(a, b, preferred_element_type=F32)


def _dot_nt(a, b):
    return lax.dot_general(a, b, (((1,), (1,)), ((), ())), preferred_element_type=F32)


def _split3(x):
    h = x.astype(BF16)
    r = x - h.astype(F32)
    m = r.astype(BF16)
    l = (r - m.astype(F32)).astype(BF16)
    return h, m, l


def _split2(x):
    h = x.astype(BF16)
    l = (x - h.astype(F32)).astype(BF16)
    return h, l


def _mod_kernel(c_ref, w_ref, b_ref, o_ref):
    c = c_ref[...]
    s = c * (1.0 / (1.0 + jnp.exp(-c)))
    o_ref[0] = _dot(s.astype(BF16), w_ref[0].astype(BF16)) + b_ref[0]


def _modulation(c_all, w_ada, b_ada):
    depth, d, n6 = w_ada.shape
    rows = c_all.shape[0]
    tn = 1024
    return pl.pallas_call(
        _mod_kernel,
        grid=(depth, n6 // tn),
        in_specs=[
            pl.BlockSpec((rows, d), lambda l, j: (0, 0)),
            pl.BlockSpec((1, d, tn), lambda l, j: (l, 0, j)),
            pl.BlockSpec((1, 1, tn), lambda l, j: (l, 0, j)),
        ],
        out_specs=pl.BlockSpec((1, rows, tn), lambda l, j: (l, 0, j)),
        out_shape=jax.ShapeDtypeStruct((depth, rows, n6), F32),
        compiler_params=_cparams(("arbitrary", "arbitrary")),
        name="adaln_mod",
    )(c_all, w_ada, b_ada.reshape(depth, 1, n6))


def _modulated_norm(x, g, sc, sh):
    ms = jnp.mean(x * x, axis=-1, keepdims=True)
    return (x * lax.rsqrt(ms + EPS)) * g * (1.0 + sc) + sh


def _group_rms(t, ind, inv_size, g):
    sq = (t * t).astype(BF16)
    wb = ind.shape[0]
    ss = [_dot(sq[:, c:c + wb], ind) for c in range(0, t.shape[1], wb)]
    ss = jnp.concatenate(ss, axis=1) if len(ss) > 1 else ss[0]
    return t * lax.rsqrt(ss * inv_size + EPS) * g


def _log_sigmoid(x):
    return jnp.minimum(x, 0.0) - jnp.log(1.0 + jnp.exp(-jnp.abs(x)))


def _augmented_keys(kfn, f_cum, heads):
    lane = lax.broadcasted_iota(jnp.int32, f_cum.shape, 1)
    hi, mid, lo = (t.astype(F32) for t in _split3(f_cum * -LOG2E))
    aug_even = (pltpu.roll(hi, HEAD_DIM, 1) + pltpu.roll(mid, HEAD_DIM + SUBLANES, 1)
                + pltpu.roll(lo, HEAD_DIM + 2 * SUBLANES, 1))
    aug_odd = hi + pltpu.roll(mid, SUBLANES, 1) + pltpu.roll(lo, 2 * SUBLANES, 1)
    blocks = []
    for h in range(heads):
        kp = kfn[:, (h // 2) * LANES:(h // 2 + 1) * LANES]
        if h % 2 == 0:
            blocks.append(jnp.where(lane < HEAD_DIM, kp, aug_even))
        else:
            blocks.append(jnp.where(lane >= HEAD_DIM, kp, aug_odd))
    return jnp.concatenate(blocks, axis=1).astype(BF16)


def _inproj_kernel(x_ref, sc_ref, sh_ref, gmix_ref, w_ref, *refs, sweep, **static):
    run = None
    if sweep:
        carry_ref = refs[-1]
        @pl.when(pl.program_id(0) == 0)
        def _():
            carry_ref[...] = jnp.zeros(carry_ref.shape, F32)
        run = carry_ref[0:1, :]
    for rows in _row_passes(x_ref.shape[0]):
        h = _modulated_norm(x_ref[rows], gmix_ref[0], _mod_rows(sc_ref, rows), _mod_rows(sh_ref, rows))
        p = _dot(h.astype(BF16), w_ref[0])
        run = _inproj_rows(p, run, rows, *refs, sweep=sweep, **static)
    if sweep:
        carry_ref[...] = jnp.broadcast_to(run, carry_ref.shape)


def _inproj_rows(p, run, rows, gsgu_ref, gq_ref, gk_ref, bf_ref, ind96_ref, ind64_ref, msgu_ref,
                 bsgu_ref, *rest, w_sgu, w_fox, w_sb, period, sweep, layer):
    sub = p.shape[0]
    if sweep:
        prev = rest[1:1 + 4 * bool(layer)]
        (ysgu_ref, kaug_ref, qft_ref, vft_ref, qbt_ref, kb16_ref, vbt_ref,
         kft_ref, vft32_ref, kbt_ref, vbt32_ref, logft_ref, fcumt_ref, _) = rest[1 + len(prev):]
        tril_ref = rest[0]
    else:
        (ysgu_ref, qf_ref, kf16_ref, vf16_ref, qb_ref, kb16_ref, vb16_ref,
         kf32_ref, vf32_ref, kb32_ref, vb32_ref, logft_ref, sguv_ref) = rest
    o = 0
    u = p[:, o:o + w_sgu]; o += w_sgu
    vs = p[:, o:o + w_sgu]; o += w_sgu
    qf = p[:, o:o + w_fox]; o += w_fox
    kf = p[:, o:o + w_fox]; o += w_fox
    vf = p[:, o:o + w_fox]; o += w_fox
    qb = p[:, o:o + w_sb]; o += w_sb
    kb = p[:, o:o + w_sb]; o += w_sb
    vb = p[:, o:o + w_sb]; o += w_sb
    fl = p[:, o:o + LANES]

    scale = HEAD_DIM ** -0.5
    ind64 = ind64_ref[...]
    qfn = _group_rms(qf, ind64, 1.0 / HEAD_DIM, gq_ref[0])
    kfn = _group_rms(kf, ind64, 1.0 / HEAD_DIM, gk_ref[0])
    kb16_ref[rows] = kb.astype(BF16)
    lf = _log_sigmoid(fl + bf_ref[0])
    logft_ref[:, rows] = lf.T[0:SUBLANES, :]
    if sweep:
        qft_ref[:, rows] = (qfn * (scale * LOG2E)).T.astype(BF16)
        qbt_ref[:, rows] = (qb * (scale * LOG2E)).T.astype(BF16)
        vf_t, vb_t = vf.T, vb.T
        vft_ref[:, rows] = vf_t.astype(BF16)
        vbt_ref[:, rows] = vb_t.astype(BF16)
        for dst, own, earlier in zip((kft_ref, vft32_ref, kbt_ref, vbt32_ref),
                                     (kfn.T, vf_t, kb.T, vb_t), prev or (None,) * 4):
            if earlier is not None:
                dst[0:layer, :, rows] = earlier[:, :, rows]
            dst[layer, :, rows] = own
        lane = lax.broadcasted_iota(jnp.int32, lf.shape, 1)
        lfh, lfm, lfl = _split3(jnp.where(lane < SUBLANES, lf, 0.0))
        tril = tril_ref[...]
        chunks = []
        for c in range(0, sub, LANES):
            part = (_dot(tril, lfh[c:c + LANES]) + _dot(tril, lfm[c:c + LANES])
                    + _dot(tril, lfl[c:c + LANES]) + run)
            run = part[LANES - 1:LANES, :]
            chunks.append(part)
        f_cum = jnp.concatenate(chunks, axis=0) if len(chunks) > 1 else chunks[0]
        fcumt_ref[:, rows] = f_cum.T[0:SUBLANES, :]
        kaug_ref[rows] = _augmented_keys(kfn, f_cum, w_fox // HEAD_DIM)
    else:
        kf32_ref[rows] = kfn
        vf32_ref[rows] = vf
        kb32_ref[rows] = kb
        vb32_ref[rows] = vb
        qf_ref[rows] = (qfn * scale).astype(BF16)
        kf16_ref[rows] = kfn.astype(BF16)
        vf16_ref[rows] = vf.astype(BF16)
        qb_ref[rows] = (qb * scale).astype(BF16)
        vb16_ref[rows] = vb.astype(BF16)

    cg = w_sgu // G_SGU
    vsn = _group_rms(vs, ind96_ref[...], 1.0 / cg, gsgu_ref[0])
    if not sweep:
        sguv_ref[rows] = vsn
    r = lax.broadcasted_iota(jnp.int32, (SGU_LEN, G_SGU * SGU_LEN), 0)
    c = lax.broadcasted_iota(jnp.int32, (SGU_LEN, G_SGU * SGU_LEN), 1) % SGU_LEN
    keep = (r // period == c // period) & ((c % period) // CHUNK <= (r % period) // CHUNK)
    mix = jnp.where(keep, msgu_ref[0], 0.0).astype(BF16)
    lane_group = lax.broadcasted_iota(jnp.int32, (SGU_LEN, w_sgu), 1) // cg
    vsb = vsn.astype(BF16)
    spat = []
    for ci in range(sub // SGU_LEN):
        vc = vsb[ci * SGU_LEN:(ci + 1) * SGU_LEN]
        stacked = jnp.concatenate(
            [jnp.where(lane_group == g, vc, jnp.zeros_like(vc)) for g in range(G_SGU)], axis=0)
        spat.append(_dot(mix, stacked) + bsgu_ref[0])
    spat = jnp.concatenate(spat, axis=0) if len(spat) > 1 else spat[0]
    ysgu_ref[rows] = (u * spat).astype(BF16)
    return run


def _inproj(x, mod, layer, gmix, w, gsgu, gq, gk, bf, ind96, ind64, msgu, bsgu, *, tm, period,
            sweep, prev_states=()):
    n, d = x.shape
    w_sgu, w_fox = gsgu.shape[2], gq.shape[2]
    w_sb = (w.shape[2] - LANES - 2 * w_sgu - 3 * w_fox) // 3
    row = lambda width: pl.BlockSpec((tm, width), lambda i: (i, 0))
    col = lambda height: pl.BlockSpec((height, tm), lambda i: (0, i))
    sds = jax.ShapeDtypeStruct
    if sweep:
        stack = lambda width: pl.BlockSpec((layer + 1, width, tm), lambda i: (0, 0, i))
        states_specs = [stack(w_fox), stack(w_fox), stack(w_sb), stack(w_sb), col(SUBLANES)]
        states_shape = [sds((layer + 1, wd, n), F32) for wd in (w_fox, w_fox, w_sb, w_sb)] \
            + [sds((SUBLANES, n), F32)]
    else:
        states_specs = [row(w_fox), row(w_fox), row(w_sb), row(w_sb), col(SUBLANES)]
        states_shape = [sds((n, w_fox), F32), sds((n, w_fox), F32), sds((n, w_sb), F32),
                        sds((n, w_sb), F32), sds((SUBLANES, n), F32)]
    operands = [x, mod, mod, gmix, w, gsgu, gq, gk, bf, ind96, ind64, msgu, bsgu]
    lay = lambda a: _layer_spec(a, layer)
    in_specs = [row(d), _mod_spec(mod, layer, 1, tm), _mod_spec(mod, layer, 0, tm), lay(gmix), lay(w),
                lay(gsgu), lay(gq), lay(gk), lay(bf), _const_spec(ind96.shape),
                _const_spec(ind64.shape), lay(msgu), lay(bsgu)]
    scratch = []
    if sweep:
        heads = w_fox // HEAD_DIM
        a = np.arange(LANES)
        operands.append(jnp.asarray(a[None, :] <= a[:, None], dtype=BF16))
        in_specs.append(_const_spec((LANES, LANES)))
        for st in prev_states:
            operands.append(st)
            in_specs.append(pl.BlockSpec((layer, st.shape[1], tm), lambda i: (0, 0, i)))
        out_specs = [row(w_sgu), row(heads * LANES), col(w_fox), col(w_fox), col(w_sb), row(w_sb),
                     col(w_sb)] + states_specs + [col(SUBLANES)]
        out_shape = [sds((n, w_sgu), BF16), sds((n, heads * LANES), BF16), sds((w_fox, n), BF16),
                     sds((w_fox, n), BF16), sds((w_sb, n), BF16), sds((n, w_sb), BF16),
                     sds((w_sb, n), BF16)] + states_shape + [sds((SUBLANES, n), F32)]
        scratch = [pltpu.VMEM((SUBLANES, LANES), F32)]
    else:
        out_specs = [row(w_sgu), row(w_fox), row(w_fox), row(w_fox), row(w_sb), row(w_sb),
                     row(w_sb)] + states_specs + [row(w_sgu)]
        out_shape = [sds((n, w_sgu), BF16), sds((n, w_fox), BF16), sds((n, w_fox), BF16),
                     sds((n, w_fox), BF16), sds((n, w_sb), BF16), sds((n, w_sb), BF16),
                     sds((n, w_sb), BF16)] + states_shape + [sds((n, w_sgu), F32)]
    kern = functools.partial(_inproj_kernel, w_sgu=w_sgu, w_fox=w_fox, w_sb=w_sb,
                             period=period, sweep=sweep, layer=layer)
    return pl.pallas_call(
        kern,
        grid=(n // tm,),
        in_specs=in_specs,
        out_specs=out_specs,
        out_shape=out_shape,
        scratch_shapes=scratch,
        compiler_params=_cparams(("arbitrary",)),
        name="inproj",
    )(*operands)


def _seq_cumsum(x, nc, reverse_exclusive):
    rows = x.shape[0]
    a = lax.broadcasted_iota(jnp.int32, (LANES, LANES), 0)
    b = lax.broadcasted_iota(jnp.int32, (LANES, LANES), 1)
    tri = (a > b) if reverse_exclusive else (a <= b)
    tri = jnp.where(tri, 1.0, 0.0).astype(BF16)
    ones = jnp.ones((LANES, LANES), BF16)
    xh, xm, xl = _split3(x)
    within = _dot(xh, tri) + _dot(xm, tri) + _dot(xl, tri)
    tot = _dot(xh, ones) + _dot(xm, ones) + _dot(xl, ones)
    ra = lax.broadcasted_iota(jnp.int32, (rows, rows), 0)
    rb = lax.broadcasted_iota(jnp.int32, (rows, rows), 1)
    other = (rb > ra) if reverse_exclusive else (rb < ra)
    blk = jnp.where((ra // nc == rb // nc) & other, 1.0, 0.0).astype(BF16)
    th, tm_, tl = _split3(tot)
    return within + _dot(blk, th) + _dot(blk, tm_) + _dot(blk, tl)


def _sample_cumsum_kernel(clf_ref, lf_ref, suf_ref, cum_ref, *, nc, dec_seq):
    suf_ref[...] = _seq_cumsum(clf_ref[...], nc, reverse_exclusive=True)
    a = lax.broadcasted_iota(jnp.int32, (LANES, LANES), 0)
    b = lax.broadcasted_iota(jnp.int32, (LANES, LANES), 1)
    tri = jnp.where((a // dec_seq == b // dec_seq) & (a <= b), 1.0, 0.0).astype(BF16)
    xh, xm, xl = _split3(lf_ref[...])
    cum_ref[...] = _dot(xh, tri) + _dot(xm, tri) + _dot(xl, tri)


def _sample_cumsum(clogf_t, logft, dec_seq):
    rows, past = clogf_t.shape
    nc = past // LANES
    suf, cum = pl.pallas_call(
        functools.partial(_sample_cumsum_kernel, nc=nc, dec_seq=dec_seq),
        out_shape=[jax.ShapeDtypeStruct((rows * nc, LANES), F32),
                   jax.ShapeDtypeStruct(logft.shape, F32)],
        compiler_params=pltpu.CompilerParams(vmem_limit_bytes=VMEM_LIMIT),
        name="sample_logf_cumsum",
    )(clogf_t.reshape(rows * nc, LANES), logft)
    return suf.reshape(rows, past), cum


def _head_query_blocks(qt_ref, heads, tq, ones_rows, lane0=0):
    rowi = lax.broadcasted_iota(jnp.int32, (LANES, tq), 0)
    out = []
    for h in range(heads):
        qp = qt_ref[(h // 2) * LANES:(h // 2 + 1) * LANES, lane0:lane0 + tq]
        own = (rowi < HEAD_DIM) if h % 2 == 0 else (rowi >= HEAD_DIM)
        fill = jnp.zeros((LANES, tq), F32)
        for r in ones_rows(h):
            fill = jnp.where(rowi == r, 1.0, fill)
        out.append(jnp.where(own, qp, fill.astype(BF16)))
    return out


def _bias_rows(h):
    base = HEAD_DIM if h % 2 == 0 else 0
    return [base + h, base + SUBLANES + h, base + 2 * SUBLANES + h]


def _fox_kernel(fs_ref, fe_ref, cut_ref, qt_ref, ka_ref, vt_ref, *rest, tq, heads):
    n_cast = len(rest) // 2
    o_ref = rest[n_cast]
    for src_ref, dst_ref in zip(rest[:n_cast], rest[n_cast + 1:]):
        dst_ref[...] = src_ref[...].astype(BF16)
    i = pl.program_id(0)
    tk = tq
    cpb = tk // LANES
    qa = _head_query_blocks(qt_ref, heads, tq, _bias_rows)
    krow = lax.broadcasted_iota(jnp.int32, (tk, tq), 0)
    qcol = lax.broadcasted_iota(jnp.int32, (tk, tq), 1)

    def logits_of(j):
        k0 = pl.multiple_of(j * tk, tk)
        return tuple(_dot(ka_ref[pl.ds(k0, tk), h * LANES:(h + 1) * LANES], qa[h])
                     for h in range(heads))

    def absorb(j, logits, carry, mask):
        k0 = pl.multiple_of(j * tk, tk)
        out = []
        for h in range(heads):
            vt = vt_ref[h * HEAD_DIM:(h + 1) * HEAD_DIM, pl.ds(k0, tk)]
            m, l, acc = carry[3 * h:3 * h + 3]
            s = logits[h]
            if mask is not None:
                s = jnp.where(mask, s, NEG_BIG)
            mn = jnp.maximum(m, jnp.max(s, axis=0, keepdims=True))
            alpha = jnp.exp2(m - mn)
            pe = jnp.exp2(s - mn)
            l = alpha * l + jnp.sum(pe, axis=0, keepdims=True)
            acc = alpha * acc + _dot(vt, pe.astype(BF16))
            out += [mn, l, acc]
        return tuple(out)

    def run(blocks, carry):
        staged = [logits_of(j) for j, _ in blocks]
        for (j, mask), logits in zip(blocks, staged):
            carry = absorb(j, logits, carry, mask)
        return carry

    def live(j):
        jc = j * cpb + (cpb - 1)
        ok = fs_ref[0, i * cpb] - fe_ref[0, jc] >= cut_ref[0]
        for h in range(1, heads):
            ok = ok | (fs_ref[h, i * cpb] - fe_ref[h, jc] >= cut_ref[0])
        return ok

    init = []
    for _ in range(heads):
        init += [jnp.full((1, tq), NEG_BIG, F32), jnp.zeros((1, tq), F32),
                 jnp.zeros((HEAD_DIM, tq), F32)]
    carry = run([(i, krow <= qcol), (jnp.maximum(i - 1, 0), jnp.broadcast_to(i > 0, (tk, tq)))],
                tuple(init))

    def cond(state):
        return (state[0] >= 1) & live(jnp.maximum(state[0] - 1, 0))

    def body(state):
        j = state[0]
        return (j - 2,) + run([(j, None), (j - 1, None)], state[1:])

    state = lax.while_loop(cond, body, (i - 2,) + carry)
    j = jnp.maximum(state[0], 0)
    last = (state[0] >= 0) & live(j)
    res = lax.cond(last, lambda c: run([(j, None)], c), lambda c: c, state[1:])
    for p in range(heads // 2):
        _, l0, a0, _, l1, a1 = res[6 * p:6 * p + 6]
        ot = jnp.concatenate([a0 / l0, a1 / l1], axis=0)
        o_ref[:, p * LANES:(p + 1) * LANES] = ot.T.astype(BF16)


def _cast_spec(rows, cols, steps):
    rb = next(r for r in range(2 * SUBLANES, rows + 1, 2 * SUBLANES)
              if rows % r == 0 and rows // r <= steps)
    last = rows // rb - 1
    return pl.BlockSpec((rb, cols), lambda i, *_: (jnp.minimum(i, last), 0))


def _fox_prompt(fs, fe, cut, qt, kaug, vt, *, tq, cast=()):
    w, n = qt.shape
    heads = w // HEAD_DIM
    steps = n // tq
    cast_specs = [_cast_spec(a.shape[0], a.shape[1], steps) for a in cast]
    grid_spec = pltpu.PrefetchScalarGridSpec(
        num_scalar_prefetch=3,
        grid=(steps,),
        in_specs=[
            pl.BlockSpec((w, tq), lambda i, *_: (0, i)),
            pl.BlockSpec(kaug.shape, lambda i, *_: (0, 0), pipeline_mode=pl.Buffered(1)),
            pl.BlockSpec((w, n), lambda i, *_: (0, 0), pipeline_mode=pl.Buffered(1)),
        ] + cast_specs,
        out_specs=[pl.BlockSpec((tq, w), lambda i, *_: (i, 0))] + cast_specs,
    )
    return pl.pallas_call(
        functools.partial(_fox_kernel, tq=tq, heads=heads),
        grid_spec=grid_spec,
        out_shape=[jax.ShapeDtypeStruct((n, w), BF16)]
        + [jax.ShapeDtypeStruct(a.shape, BF16) for a in cast],
        compiler_params=_cparams(("arbitrary",)),
        name="fox_prompt",
    )(fs, fe, cut, qt, kaug, vt, *cast)


def _sb_kernel(qt_ref, k_ref, vt_ref, o_ref, *, tq, tb, heads):
    i = pl.program_id(0)
    nsub = tq // tb
    qm = [_head_query_blocks(qt_ref, heads, tb, lambda h: [], lane0=s * tb) for s in range(nsub)]
    krow = lax.broadcasted_iota(jnp.int32, (tb, tb), 0)
    qcol = lax.broadcasted_iota(jnp.int32, (tb, tb), 1)
    a = lax.broadcasted_iota(jnp.int32, (tb, 2 * tb), 0)
    b = lax.broadcasted_iota(jnp.int32, (tb, 2 * tb), 1) % tb
    tri2 = jnp.where(b > a, 1.0, 0.0).astype(BF16)

    def sweep(tasks, carry):
        k0s = [pl.multiple_of(t[1] * tb, tb) for t in tasks]
        z = [[_dot(k_ref[pl.ds(k0, tb), (h // 2) * LANES:(h // 2 + 1) * LANES], qm[t[0]][h])
              for h in range(heads)] for t, k0 in zip(tasks, k0s)]
        logsig, later, mass = {}, {}, {}
        for ti, (_, _, mask, valid) in enumerate(tasks):
            for h in range(heads):
                zz = z[ti][h]
                sp = jnp.maximum(zz, 0.0) + jnp.log2(1.0 + jnp.exp2(-jnp.abs(zz)))
                logsig[ti, h] = zz - sp
                if mask is not None:
                    sp = jnp.where(mask, sp, 0.0)
                hi, lo = _split2(sp)
                later[ti, h] = _dot(tri2, jnp.concatenate([hi, lo], axis=0))
                mass[ti, h] = jnp.sum(sp, axis=0, keepdims=True)
                if valid is not None:
                    mass[ti, h] = mass[ti, h] * valid
        carry = dict(carry)
        for ti, (s, _, mask, valid) in enumerate(tasks):
            for h in range(heads):
                cr, acc = carry[s, h]
                w = jnp.exp2(logsig[ti, h] - later[ti, h] - cr)
                if mask is not None:
                    w = jnp.where(mask, w, 0.0)
                if valid is not None:
                    w = w * valid
                vt = vt_ref[h * HEAD_DIM:(h + 1) * HEAD_DIM, pl.ds(k0s[ti], tb)]
                carry[s, h] = (cr + mass[ti, h], acc + _dot(vt, w.astype(BF16)))
        return carry

    tasks = []
    for s in range(nsub):
        qb = i * nsub + s
        tasks.append((s, qb, krow < qcol, None))
        tasks.append((s, jnp.maximum(qb - 1, 0), None, jnp.where(qb > 0, 1.0, 0.0) if s == 0 else None))
    zero = (jnp.zeros((1, tb), F32), jnp.zeros((HEAD_DIM, tb), F32))
    carry = sweep(tasks, {(s, h): zero for s in range(nsub) for h in range(heads)})

    keys = [(s, h) for s in range(nsub) for h in range(heads)]
    flat = lambda c: tuple(v for key in keys for v in c[key])

    def cond(state):
        left = state[1]
        for idx in range(1, len(keys)):
            left = jnp.minimum(left, state[1 + 2 * idx])
        return (i * nsub + nsub - 3 - state[0] >= 0) & (jnp.min(left) < PRUNE_LOG * LOG2E)

    def body(state):
        t = state[0]
        more = []
        for s in range(nsub):
            j = i * nsub + s - 2 - t
            more.append((s, jnp.maximum(j, 0), None, jnp.where(j >= 0, 1.0, 0.0)))
        c = {key: (state[1 + 2 * idx], state[2 + 2 * idx]) for idx, key in enumerate(keys)}
        return (t + 1,) + flat(sweep(more, c))

    res = lax.while_loop(cond, body, (0,) + flat(carry))[1:]
    for s in range(nsub):
        for p in range(heads // 2):
            e, o = keys.index((s, 2 * p)), keys.index((s, 2 * p + 1))
            ot = jnp.concatenate([res[2 * e + 1], res[2 * o + 1]], axis=0)
            o_ref[s * tb:(s + 1) * tb, p * LANES:(p + 1) * LANES] = ot.T.astype(BF16)


def _sb_prompt(qt, k, vt, *, tq, tb):
    w, n = qt.shape
    return pl.pallas_call(
        functools.partial(_sb_kernel, tq=tq, tb=tb, heads=w // HEAD_DIM),
        grid=(n // tq,),
        in_specs=[
            pl.BlockSpec((w, tq), lambda i: (0, i)),
            pl.BlockSpec((n, w), lambda i: (0, 0), pipeline_mode=pl.Buffered(1)),
            pl.BlockSpec((w, n), lambda i: (0, 0), pipeline_mode=pl.Buffered(1)),
        ],
        out_specs=pl.BlockSpec((tq, w), lambda i: (i, 0)),
        out_shape=jax.ShapeDtypeStruct((n, w), BF16),
        compiler_params=_cparams(("arbitrary",)),
        name="sb_prompt",
    )(qt, k, vt)


def _sb_block(z, carry, v, tri, mask, v_feature_major=False):
    lg = jnp.log(1.0 + jnp.exp(-jnp.abs(z)))
    sp = jnp.maximum(z, 0.0) + lg
    if mask is not None:
        sp = jnp.where(mask, sp, 0.0)
    hi, lo = _split2(sp)
    later = _dot(hi, tri) + _dot(lo, tri)
    a = jnp.exp((jnp.minimum(z, 0.0) - lg) - later - carry)
    if mask is not None:
        a = jnp.where(mask, a, 0.0)
    pv = _dot_nt(a.astype(BF16), v) if v_feature_major else _dot(a.astype(BF16), v)
    return pv, carry + jnp.sum(sp, axis=1, keepdims=True)


def _suffix_matrix(tk):
    a = lax.broadcasted_iota(jnp.int32, (tk, tk), 0)
    b = lax.broadcasted_iota(jnp.int32, (tk, tk), 1)
    return jnp.where(a > b, 1.0, 0.0).astype(BF16)


def _head_rows(x, heads):
    lane_head = lax.broadcasted_iota(jnp.int32, x.shape, 1) // HEAD_DIM
    return jnp.concatenate(
        [jnp.where(lane_head == h, x, jnp.zeros_like(x)) for h in range(heads)], axis=0)


def _fold_heads(o, heads, s):
    lane_head = lax.broadcasted_iota(jnp.int32, (s, o.shape[1]), 1) // HEAD_DIM
    out = jnp.zeros((s, o.shape[1]), F32)
    for h in range(heads):
        out = jnp.where(lane_head == h, o[h * s:(h + 1) * s], out)
    return out


def _sample_attn_kernel(qf_ref, kf_ref, vf_ref, ck_ref, cv_ref, suf_ref, cum_ref,
                        qb_ref, kb_ref, vb_ref, cbk_ref, cbv_ref, of_ref, ob_ref,
                        *, s, h_fox, h_sb, past):
    qa = _head_rows(qf_ref[...], h_fox)
    rows = h_fox * s
    ck = ck_ref[0, 0].astype(BF16)
    cv = cv_ref[0, 0].astype(BF16)
    suf = suf_ref[0]
    cum = cum_ref[0]
    bias_c = jnp.concatenate(
        [jnp.broadcast_to(suf[h:h + 1, :], (s, past)) for h in range(h_fox)], axis=0)
    bias_n = jnp.concatenate(
        [jnp.broadcast_to(-cum[h:h + 1, :], (s, s)) for h in range(h_fox)], axis=0)
    lc = _dot(qa, ck) + bias_c
    ln = _dot_nt(qa, kf_ref[...]) + bias_n
    r_pos = lax.broadcasted_iota(jnp.int32, (rows, s), 0) % s
    k_pos = lax.broadcasted_iota(jnp.int32, (rows, s), 1)
    ln = jnp.where(k_pos <= r_pos, ln, NEG_BIG)
    m = jnp.maximum(jnp.max(lc, axis=1, keepdims=True), jnp.max(ln, axis=1, keepdims=True))
    pc = jnp.exp(lc - m)
    pn = jnp.exp(ln - m)
    den = jnp.sum(pc, axis=1, keepdims=True) + jnp.sum(pn, axis=1, keepdims=True)
    o = (_dot_nt(pc.astype(BF16), cv) + _dot(pn.astype(BF16), vf_ref[...])) / den
    of_ref[...] = _fold_heads(o, h_fox, s).astype(BF16)

    qb = _head_rows(qb_ref[...], h_sb)
    rows_b = h_sb * s
    rb = lax.broadcasted_iota(jnp.int32, (rows_b, s), 0) % s
    cb = lax.broadcasted_iota(jnp.int32, (rows_b, s), 1)
    acc, carry = _sb_block(_dot_nt(qb, kb_ref[...]), jnp.zeros((rows_b, 1), F32), vb_ref[...],
                           _suffix_matrix(s), cb < rb)
    tri = _suffix_matrix(LANES)

    def cond(state):
        return (state[0] < past // LANES) & (jnp.min(state[2]) < PRUNE_LOG)

    def body(state):
        t, acc, carry = state
        k0 = pl.multiple_of(past - (t + 1) * LANES, LANES)
        k = cbk_ref[0, 0, :, pl.ds(k0, LANES)].astype(BF16)
        v = cbv_ref[0, 0, :, pl.ds(k0, LANES)].astype(BF16)
        pv, carry = _sb_block(_dot(qb, k), carry, v, tri, None, v_feature_major=True)
        return t + 1, acc + pv, carry

    _, acc, _ = lax.while_loop(cond, body, (0, acc, carry))
    ob_ref[...] = _fold_heads(acc, h_sb, s).astype(BF16)


def _sample_attn(qf, kf, vf, ck, cv, suf, cum, qb, kb, vb, cbk, cbv, *, layer, batch, s):
    w_fox, w_sb = qf.shape[1], qb.shape[1]
    past = ck.shape[3]
    h_fox, h_sb = w_fox // HEAD_DIM, w_sb // HEAD_DIM
    new = lambda w: pl.BlockSpec((s, w), lambda b: (b, 0))
    cache = lambda w: pl.BlockSpec((1, 1, w, past), lambda b: (layer, b, 0, 0))
    kern = functools.partial(_sample_attn_kernel, s=s, h_fox=h_fox, h_sb=h_sb, past=past)
    return pl.pallas_call(
        kern,
        grid=(batch,),
        in_specs=[new(w_fox), new(w_fox), new(w_fox), cache(w_fox), cache(w_fox),
                  pl.BlockSpec((1, SUBLANES, past), lambda b: (b, 0, 0)),
                  pl.BlockSpec((1, SUBLANES, s), lambda b: (b, 0, 0)),
                  new(w_sb), new(w_sb), new(w_sb), cache(w_sb), cache(w_sb)],
        out_specs=[new(w_fox), new(w_sb)],
        out_shape=[jax.ShapeDtypeStruct((batch * s, w_fox), BF16),
                   jax.ShapeDtypeStruct((batch * s, w_sb), BF16)],
        compiler_params=_cparams(("arbitrary",)),
        name="sample_attn",
    )(qf, kf, vf, ck, cv, suf, cum, qb, kb, vb, cbk, cbv)


def _merge_kernel(*refs):
    prompt_refs, sample_refs = refs[0:7], refs[7:14]
    gmix_ref, wg_ref, bg_ref, wbs_ref, wbf_ref, wbb_ref, wo_ref, o_ref, os_ref = refs[14:]

    def tile(x_ref, sc_ref, sh_ref, gt_ref, ysgu_ref, ofox_ref, osb_ref, o_ref):
        x = x_ref[...]
        d = x.shape[1]
        h = _modulated_norm(x, gmix_ref[0], sc_ref[0], sh_ref[0])
        gates = _dot(h.astype(BF16), wg_ref[0]) + bg_ref[0]
        gates = 1.0 / (1.0 + jnp.exp(-gates))
        merged = gates[:, 0:d] * _dot(ysgu_ref[...], wbs_ref[0]) \
            + gates[:, d:2 * d] * _dot(ofox_ref[...], wbf_ref[0]) \
            + gates[:, 2 * d:3 * d] * _dot(osb_ref[...], wbb_ref[0])
        o_ref[...] = x + gt_ref[0] * _dot(merged.astype(BF16), wo_ref[0])

    last = pl.num_programs(0) - 1

    @pl.when(pl.program_id(0) < last)
    def _():
        tile(*prompt_refs, o_ref)

    @pl.when(pl.program_id(0) == last)
    def _():
        tile(*sample_refs, os_ref)


def _merge(prompt, sample, layer, gmix, wg, bg, wbs, wbf, wbb, wo, *, tm):
    x, mod, *branches = prompt
    xs, mod_s, *branches_s = sample
    n, d = x.shape
    ns = xs.shape[0]
    steps = n // tm
    assert mod.shape[1] == 1 and mod_s.shape[1] == ns
    row = lambda width: pl.BlockSpec((tm, width), lambda i: (jnp.minimum(i, steps - 1), 0))
    row_s = lambda width: pl.BlockSpec((ns, width), lambda i: (0, 0))
    mod_spec_s = lambda k: pl.BlockSpec((1, ns, d), lambda i: (layer, 0, k))
    lay = lambda a: _layer_spec(a, layer)
    return pl.pallas_call(
        _merge_kernel,
        grid=(steps + 1,),
        in_specs=[row(d), _mod_spec(mod, layer, 1, tm), _mod_spec(mod, layer, 0, tm),
                  _mod_spec(mod, layer, 2, tm), *(row(b.shape[1]) for b in branches),
                  row_s(d), mod_spec_s(1), mod_spec_s(0), mod_spec_s(2),
                  *(row_s(b.shape[1]) for b in branches_s),
                  lay(gmix), lay(wg), lay(bg), lay(wbs), lay(wbf), lay(wbb), lay(wo)],
        out_specs=[row(d), row_s(d)],
        out_shape=[jax.ShapeDtypeStruct((n, d), F32), jax.ShapeDtypeStruct((ns, d), F32)],
        compiler_params=_cparams(("arbitrary",)),
        name="merge",
    )(x, mod, mod, mod, *branches, xs, mod_s, mod_s, mod_s, *branches_s,
      gmix, wg, bg, wbs, wbf, wbb, wo)


def _ffn_kernel(x_ref, sc_ref, sh_ref, gt_ref, xs_ref, scs_ref, shs_ref, gts_ref, g_ref, wi_ref,
                wo_ref, o_ref, os_ref, *, d_ff):
    def tile(x_ref, sc_ref, sh_ref, gt_ref, o_ref):
        for rows in _row_passes(x_ref.shape[0]):
            x = x_ref[rows]
            h = _modulated_norm(x, g_ref[0], _mod_rows(sc_ref, rows), _mod_rows(sh_ref, rows))
            ag = _dot(h.astype(BF16), wi_ref[0])
            a = ag[:, 0:d_ff]
            act = a * (1.0 / (1.0 + jnp.exp(-a))) * ag[:, d_ff:2 * d_ff]
            o_ref[rows] = x + _mod_rows(gt_ref, rows) * _dot(act.astype(BF16), wo_ref[0])

    last = pl.num_programs(0) - 1

    @pl.when(pl.program_id(0) < last)
    def _():
        tile(x_ref, sc_ref, sh_ref, gt_ref, o_ref)

    @pl.when(pl.program_id(0) == last)
    def _():
        tile(xs_ref, scs_ref, shs_ref, gts_ref, os_ref)


def _ffn(x, mod, xs, mod_s, layer, g, wi, wo, *, tm):
    n, d = x.shape
    ns = xs.shape[0]
    d_ff = wo.shape[1]
    steps = n // tm
    assert mod.shape[1] == 1 and mod_s.shape[1] == ns
    row = pl.BlockSpec((tm, d), lambda i: (jnp.minimum(i, steps - 1), 0))
    row_s = pl.BlockSpec((ns, d), lambda i: (0, 0))
    mod_spec_s = lambda k: pl.BlockSpec((1, ns, d), lambda i: (layer, 0, k))
    return pl.pallas_call(
        functools.partial(_ffn_kernel, d_ff=d_ff),
        grid=(steps + 1,),
        in_specs=[row, _mod_spec(mod, layer, 4, tm), _mod_spec(mod, layer, 3, tm),
                  _mod_spec(mod, layer, 5, tm), row_s, mod_spec_s(4), mod_spec_s(3), mod_spec_s(5),
                  _layer_spec(g, layer), _layer_spec(wi, layer), _layer_spec(wo, layer)],
        out_specs=[row, row_s],
        out_shape=[jax.ShapeDtypeStruct((n, d), F32), jax.ShapeDtypeStruct((ns, d), F32)],
        compiler_params=_cparams(("arbitrary",)),
        name="ffn",
    )(x, mod, mod, mod, xs, mod_s, mod_s, mod_s, g, wi, wo)


class _Tiles(NamedTuple):
    inproj: int
    dense: int
    ffn: int
    fox: int
    sb: int
    sb_sub: int


def _prompt_tiles(seq):
    return _Tiles(inproj=min(1024, seq), dense=min(512, seq), ffn=min(512, seq), fox=min(256, seq), sb=min(512, seq),
                  sb_sub=min(128, seq))


def _indicator(width, group):
    idx = np.arange(width) // group
    return jnp.asarray(idx[:, None] == idx[None, :], dtype=BF16)


def kernel(x_prompt, x_sample, c_prompt, c_sample, cache_fox_k, cache_fox_v, cache_fox_logf,
           cache_sb_k, cache_sb_v, w_ada, b_ada, g_mix, g_ffn, w_in, g_sgu_v, w_sgu, b_sgu, b_fgt,
           g_q, g_k, w_br_sgu, w_br_fox, w_br_sb, w_gate, b_gate, w_out, w_ffn_in, w_ffn_out):
    batch, seq, d = x_prompt.shape
    dec_batch, dec_seq, _ = x_sample.shape
    depth = w_ada.shape[0]
    past = cache_fox_k.shape[2]
    h_fox, h_sb = cache_fox_k.shape[3], cache_sb_k.shape[3]
    w_fox, w_sb = h_fox * HEAD_DIM, h_sb * HEAD_DIM
    g_sgu, cg = g_sgu_v.shape[1], g_sgu_v.shape[2]
    w_sgu_ = g_sgu * cg
    assert batch == 1 and g_sgu == G_SGU and w_sgu.shape[2] == SGU_LEN
    n_dec = dec_batch * dec_seq

    n_c = batch + dec_batch
    c_rows = -(-n_c // 8) * 8
    c_all = jnp.zeros((c_rows, d), F32).at[:n_c].set(jnp.concatenate([c_prompt, c_sample], axis=0))
    mod = _modulation(c_all, w_ada, b_ada)

    offs = np.cumsum([0, w_sgu_, w_sgu_, w_fox, w_fox, w_fox, h_fox, w_sb, w_sb, w_sb]).tolist()
    f_cols = jnp.zeros((depth, d, LANES), F32).at[:, :, :h_fox].set(w_in[:, :, offs[5]:offs[6]])
    w_main = jnp.concatenate([w_in[:, :, :offs[5]], w_in[:, :, offs[6]:], f_cols],
                             axis=2).astype(BF16)
    bf_pad = jnp.zeros((depth, 1, LANES), F32).at[:, 0, :h_fox].set(b_fgt)
    gmix3, gffn3, bg3 = g_mix.reshape(depth, 1, d), g_ffn.reshape(depth, 1, d), b_gate.reshape(depth, 1, 3 * d)
    mod_p = mod[:, 0:batch]
    mod_s = jnp.repeat(mod[:, batch:batch + dec_batch], dec_seq, axis=1)
    ind96, ind64 = _indicator(w_sgu_, cg), _indicator(LANES, HEAD_DIM)
    gq_t = jnp.tile(g_q, (1, h_fox)).reshape(depth, 1, w_fox)
    gk_t = jnp.tile(g_k, (1, h_fox)).reshape(depth, 1, w_fox)
    gsgu = g_sgu_v.reshape(depth, 1, w_sgu_)
    msgu_p = jnp.transpose(w_sgu, (0, 2, 1, 3)).reshape(depth, SGU_LEN, g_sgu * SGU_LEN)
    reps = SGU_LEN // dec_seq
    w_small = jnp.tile(w_sgu[:, :, :dec_seq, :dec_seq], (1, 1, reps, reps))
    msgu_s = jnp.transpose(w_small, (0, 2, 1, 3)).reshape(depth, SGU_LEN, g_sgu * SGU_LEN)
    bsgu_p = jnp.repeat(jnp.transpose(b_sgu, (0, 2, 1)), cg, axis=2)
    bsgu_s = jnp.tile(bsgu_p[:, :dec_seq], (1, reps, 1))
    dense_w = (w_gate, w_br_sgu, w_br_fox, w_br_sb, w_out, w_ffn_in, w_ffn_out)

    to_fm = lambda c: jnp.transpose(c, (0, 1, 3, 4, 2)).reshape(
        depth, dec_batch, c.shape[3] * HEAD_DIM, past)
    ck_t, cv_t, cbk_t, cbv_t = (to_fm(c) for c in (cache_fox_k, cache_fox_v, cache_sb_k, cache_sb_v))

    xp = x_prompt.reshape(seq, d)
    xs = x_sample.reshape(n_dec, d)
    tiles = _prompt_tiles(seq)
    stacks, logf_p, st_s = (), [], []
    for l in range(depth):
        shared = (l, gmix3, w_main, gsgu, gq_t, gk_t, bf_pad, ind96, ind64)

        (ysgu, kaug, qft, vft, qbt, kb16, vbt, *stacks, logft, fcumt) = _inproj(
            xp, mod_p, *shared, msgu_p, bsgu_p, tm=tiles.inproj, period=SGU_LEN, sweep=True,
            prev_states=stacks)
        bound = 1.01 * HEAD_DIM ** 0.5 * jnp.max(jnp.abs(g_q[l])) * jnp.max(jnp.abs(g_k[l]))
        cut = (-(2.0 * bound + PRUNE_LOG)).reshape(1)
        ofox, *cast = _fox_prompt(
            fcumt[:, 0::LANES], fcumt[:, LANES - 1::LANES], cut, qft, kaug, vft, tq=tiles.fox,
            cast=[a.reshape(-1, a.shape[2]) for a in dense_w] if l == 0 else ())
        if l == 0:
            wg, wbs, wbf, wbb, wo, wfi, wfo = (c.reshape(a.shape) for c, a in zip(cast, dense_w))
            branch_w = (wg, bg3, wbs, wbf, wbb, wo)
        osb = _sb_prompt(qbt, kb16, vbt, tq=tiles.sb, tb=tiles.sb_sub)
        prompt = (xp, mod_p, ysgu, ofox, osb)
        logf_p.append(logft[:h_fox].T.reshape(batch, seq, h_fox))

        (ysgu, qf, kf16, vf16, qb, kb16, vb16, kf32, vf32, kb32, vb32, logft, sguv) = _inproj(
            xs, mod_s, *shared, msgu_s, bsgu_s, tm=n_dec, period=dec_seq, sweep=False)
        clf = jnp.zeros((dec_batch, SUBLANES, past), F32).at[:, :h_fox].set(
            jnp.transpose(cache_fox_logf[l], (0, 2, 1)))
        suf, cum = _sample_cumsum(clf.reshape(dec_batch * SUBLANES, past), logft, dec_seq)
        cum_b = jnp.transpose(cum.reshape(SUBLANES, dec_batch, dec_seq), (1, 0, 2))
        ofox, osb = _sample_attn(
            qf, kf16, vf16, ck_t, cv_t, suf.reshape(dec_batch, SUBLANES, past), cum_b,
            qb, kb16, vb16, cbk_t, cbv_t, layer=l, batch=dec_batch, s=dec_seq)
        x1p, x1s = _merge(prompt, (xs, mod_s, ysgu, ofox, osb), l, gmix3, *branch_w, tm=tiles.dense)
        xp, xs = _ffn(x1p, mod_p, x1s, mod_s, l, gffn3, wfi, wfo, tm=tiles.ffn)
        st_s.append((kf32.reshape(dec_batch, dec_seq, h_fox, HEAD_DIM),
                     vf32.reshape(dec_batch, dec_seq, h_fox, HEAD_DIM),
                     logft[:h_fox].T.reshape(dec_batch, dec_seq, h_fox),
                     kb32.reshape(dec_batch, dec_seq, h_sb, HEAD_DIM),
                     vb32.reshape(dec_batch, dec_seq, h_sb, HEAD_DIM),
                     sguv.reshape(dec_batch, dec_seq, w_sgu_)))

    def stack(states, idx):
        return jnp.stack([s[idx] for s in states], axis=0)

    per_head = lambda st: jnp.transpose(
        st.reshape(depth, batch, st.shape[1] // HEAD_DIM, HEAD_DIM, seq), (0, 1, 4, 2, 3))
    kf_p, vf_p, kb_p, vb_p = (per_head(st) for st in stacks)
    return (xp.reshape(batch, seq, d), xs.reshape(dec_batch, dec_seq, d),
            kf_p, vf_p, jnp.stack(logf_p, axis=0), kb_p, vb_p,
            stack(st_s, 0), stack(st_s, 1), stack(st_s, 2), stack(st_s, 3), stack(st_s, 4),
            stack(st_s, 5))
```

```python
import functools
from typing import NamedTuple

import numpy as np
import jax
import jax.numpy as jnp
from jax import lax
from jax.experimental import pallas as pl
from jax.experimental.pallas import tpu as pltpu

F32 = jnp.float32
BF16 = jnp.bfloat16

EPS = 1e-6
HEAD_DIM = 64
LANES = 128
SUBLANES = 8
CHUNK = 64
SGU_LEN = 128
G_SGU = 4
NEG_BIG = -1e30
LOG2E = 1.4426950408889634

PRUNE_LOG = 30.0

VMEM_LIMIT = 56 * 1024 * 1024
MXU_TILE = 256
PASS_ROWS = 256


def _cparams(sem):
    return pltpu.CompilerParams(dimension_semantics=sem, vmem_limit_bytes=VMEM_LIMIT)


def _const_spec(shape):
    nd = len(shape)
    return pl.BlockSpec(shape, lambda *_: (0,) * nd, pipeline_mode=pl.Buffered(1))


def _layer_spec(arr, layer):
    nd = arr.ndim
    return pl.BlockSpec((1,) + arr.shape[1:], lambda *_: (layer,) + (0,) * (nd - 1),
                        pipeline_mode=pl.Buffered(1))


def _mod_spec(mod, layer, k, tm):
    d = mod.shape[2] // 6
    if mod.shape[1] == 1:
        return pl.BlockSpec((1, 1, d), lambda i: (layer, 0, k))
    return pl.BlockSpec((1, tm, d), lambda i: (layer, i, k))


def _row_passes(tm):
    sub = min(tm, PASS_ROWS)
    return [slice(r0, r0 + sub) for r0 in range(0, tm, sub)]


def _mod_rows(ref, rows):
    return ref[0] if ref.shape[1] == 1 else ref[0, rows]


def _dot(a, b):
    return jnp.dot(a, b, preferred_element_type=F32)


def _dot_nt(a, b):
    return lax.dot_general(a, b, (((1,), (1,)), ((), ())), preferred_element_type=F32)


def _split3(x):
    h = x.astype(BF16)
    r = x - h.astype(F32)
    m = r.astype(BF16)
    l = (r - m.astype(F32)).astype(BF16)
    return h, m, l


def _split2(x):
    h = x.astype(BF16)
    l = (x - h.astype(F32)).astype(BF16)
    return h, l


def _mod_kernel(c_ref, w_ref, b_ref, o_ref):
    c = c_ref[...]
    s = c * (1.0 / (1.0 + jnp.exp(-c)))
    o_ref[0] = _dot(s.astype(BF16), w_ref[0].astype(BF16)) + b_ref[0]


def _modulation(c_all, w_ada, b_ada):
    depth, d, n6 = w_ada.shape
    rows = c_all.shape[0]
    tn = 1024
    return pl.pallas_call(
        _mod_kernel,
        grid=(depth, n6 // tn),
        in_specs=[
            pl.BlockSpec((rows, d), lambda l, j: (0, 0)),
            pl.BlockSpec((1, d, tn), lambda l, j: (l, 0, j)),
            pl.BlockSpec((1, 1, tn), lambda l, j: (l, 0, j)),
        ],
        out_specs=pl.BlockSpec((1, rows, tn), lambda l, j: (l, 0, j)),
        out_shape=jax.ShapeDtypeStruct((depth, rows, n6), F32),
        compiler_params=_cparams(("arbitrary", "arbitrary")),
        name="adaln_mod",
    )(c_all, w_ada, b_ada.reshape(depth, 1, n6))


def _modulated_norm(x, g, sc, sh):
    ms = jnp.mean(x * x, axis=-1, keepdims=True)
    return (x * lax.rsqrt(ms + EPS)) * g * (1.0 + sc) + sh


def _group_rms(t, ind, inv_size, g):
    sq = (t * t).astype(BF16)
    wb = ind.shape[0]
    ss = [_dot(sq[:, c:c + wb], ind) for c in range(0, t.shape[1], wb)]
    ss = jnp.concatenate(ss, axis=1) if len(ss) > 1 else ss[0]
    return t * lax.rsqrt(ss * inv_size + EPS) * g


def _log_sigmoid(x):
    return jnp.minimum(x, 0.0) - jnp.log(1.0 + jnp.exp(-jnp.abs(x)))


def _augmented_keys(kfn, f_cum, heads):
    lane = lax.broadcasted_iota(jnp.int32, f_cum.shape, 1)
    hi, mid, lo = (t.astype(F32) for t in _split3(f_cum * -LOG2E))
    aug_even = (pltpu.roll(hi, HEAD_DIM, 1) + pltpu.roll(mid, HEAD_DIM + SUBLANES, 1)
                + pltpu.roll(lo, HEAD_DIM + 2 * SUBLANES, 1))
    aug_odd = hi + pltpu.roll(mid, SUBLANES, 1) + pltpu.roll(lo, 2 * SUBLANES, 1)
    blocks = []
    for h in range(heads):
        kp = kfn[:, (h // 2) * LANES:(h // 2 + 1) * LANES]
        if h % 2 == 0:
            blocks.append(jnp.where(lane < HEAD_DIM, kp, aug_even))
        else:
            blocks.append(jnp.where(lane >= HEAD_DIM, kp, aug_odd))
    return jnp.concatenate(blocks, axis=1).astype(BF16)


def _inproj_kernel(x_ref, sc_ref, sh_ref, gmix_ref, w_ref, *refs, sweep, **static):
    run = None
    if sweep:
        carry_ref = refs[-1]
        @pl.when(pl.program_id(0) == 0)
        def _():
            carry_ref[...] = jnp.zeros(carry_ref.shape, F32)
        run = carry_ref[0:1, :]
    for rows in _row_passes(x_ref.shape[0]):
        h = _modulated_norm(x_ref[rows], gmix_ref[0], _mod_rows(sc_ref, rows), _mod_rows(sh_ref, rows))
        p = _dot(h.astype(BF16), w_ref[0])
        run = _inproj_rows(p, run, rows, *refs, sweep=sweep, **static)
    if sweep:
        carry_ref[...] = jnp.broadcast_to(run, carry_ref.shape)


def _inproj_rows(p, run, rows, gsgu_ref, gq_ref, gk_ref, bf_ref, ind96_ref, ind64_ref, msgu_ref,
                 bsgu_ref, *rest, w_sgu, w_fox, w_sb, period, sweep, layer):
    sub = p.shape[0]
    if sweep:
        prev = rest[1:1 + 4 * bool(layer)]
        (ysgu_ref, kaug_ref, qft_ref, vft_ref, qbt_ref, kb16_ref, vbt_ref,
         kft_ref, vft32_ref, kbt_ref, vbt32_ref, logft_ref, fcumt_ref, _) = rest[1 + len(prev):]
        tril_ref = rest[0]
    else:
        (ysgu_ref, qf_ref, kf16_ref, vf16_ref, qb_ref, kb16_ref, vb16_ref,
         kf32_ref, vf32_ref, kb32_ref, vb32_ref, logft_ref, sguv_ref) = rest
    o = 0
    u = p[:, o:o + w_sgu]; o += w_sgu
    vs = p[:, o:o + w_sgu]; o += w_sgu
    qf = p[:, o:o + w_fox]; o += w_fox
    kf = p[:, o:o + w_fox]; o += w_fox
    vf = p[:, o:o + w_fox]; o += w_fox
    qb = p[:, o:o + w_sb]; o += w_sb
    kb = p[:, o:o + w_sb]; o += w_sb
    vb = p[:, o:o + w_sb]; o += w_sb
    fl = p[:, o:o + LANES]

    scale = HEAD_DIM ** -0.5
    ind64 = ind64_ref[...]
    qfn = _group_rms(qf, ind64, 1.0 / HEAD_DIM, gq_ref[0])
    kfn = _group_rms(kf, ind64, 1.0 / HEAD_DIM, gk_ref[0])
    kb16_ref[rows] = kb.astype(BF16)
    lf = _log_sigmoid(fl + bf_ref[0])
    logft_ref[:, rows] = lf.T[0:SUBLANES, :]
    if sweep:
        qft_ref[:, rows] = (qfn * (scale * LOG2E)).T.astype(BF16)
        qbt_ref[:, rows] = (qb * (scale * LOG2E)).T.astype(BF16)
        vf_t, vb_t = vf.T, vb.T
        vft_ref[:, rows] = vf_t.astype(BF16)
        vbt_ref[:, rows] = vb_t.astype(BF16)
        for dst, own, earlier in zip((kft_ref, vft32_ref, kbt_ref, vbt32_ref),
                                     (kfn.T, vf_t, kb.T, vb_t), prev or (None,) * 4):
            if earlier is not None:
                dst[0:layer, :, rows] = earlier[:, :, rows]
            dst[layer, :, rows] = own
        lane = lax.broadcasted_iota(jnp.int32, lf.shape, 1)
        lfh, lfm, lfl = _split3(jnp.where(lane < SUBLANES, lf, 0.0))
        tril = tril_ref[...]
        chunks = []
        for c in range(0, sub, LANES):
            part = (_dot(tril, lfh[c:c + LANES]) + _dot(tril, lfm[c:c + LANES])
                    + _dot(tril, lfl[c:c + LANES]) + run)
            run = part[LANES - 1:LANES, :]
            chunks.append(part)
        f_cum = jnp.concatenate(chunks, axis=0) if len(chunks) > 1 else chunks[0]
        fcumt_ref[:, rows] = f_cum.T[0:SUBLANES, :]
        kaug_ref[rows] = _augmented_keys(kfn, f_cum, w_fox // HEAD_DIM)
    else:
        kf32_ref[rows] = kfn
        vf32_ref[rows] = vf
        kb32_ref[rows] = kb
        vb32_ref[rows] = vb
        qf_ref[rows] = (qfn * scale).astype(BF16)
        kf16_ref[rows] = kfn.astype(BF16)
        vf16_ref[rows] = vf.astype(BF16)
        qb_ref[rows] = (qb * scale).astype(BF16)
        vb16_ref[rows] = vb.astype(BF16)

    cg = w_sgu // G_SGU
    vsn = _group_rms(vs, ind96_ref[...], 1.0 / cg, gsgu_ref[0])
    if not sweep:
        sguv_ref[rows] = vsn
    r = lax.broadcasted_iota(jnp.int32, (SGU_LEN, G_SGU * SGU_LEN), 0)
    c = lax.broadcasted_iota(jnp.int32, (SGU_LEN, G_SGU * SGU_LEN), 1) % SGU_LEN
    keep = (r // period == c // period) & ((c % period) // CHUNK <= (r % period) // CHUNK)
    mix = jnp.where(keep, msgu_ref[0], 0.0).astype(BF16)
    lane_group = lax.broadcasted_iota(jnp.int32, (SGU_LEN, w_sgu), 1) // cg
    vsb = vsn.astype(BF16)
    spat = []
    for ci in range(sub // SGU_LEN):
        vc = vsb[ci * SGU_LEN:(ci + 1) * SGU_LEN]
        stacked = jnp.concatenate(
            [jnp.where(lane_group == g, vc, jnp.zeros_like(vc)) for g in range(G_SGU)], axis=0)
        spat.append(_dot(mix, stacked) + bsgu_ref[0])
    spat = jnp.concatenate(spat, axis=0) if len(spat) > 1 else spat[0]
    ysgu_ref[rows] = (u * spat).astype(BF16)
    return run


def _inproj(x, mod, layer, gmix, w, gsgu, gq, gk, bf, ind96, ind64, msgu, bsgu, *, tm, period,
            sweep, prev_states=()):
    n, d = x.shape
    w_sgu, w_fox = gsgu.shape[2], gq.shape[2]
    w_sb = (w.shape[2] - LANES - 2 * w_sgu - 3 * w_fox) // 3
    row = lambda width: pl.BlockSpec((tm, width), lambda i: (i, 0))
    col = lambda height: pl.BlockSpec((height, tm), lambda i: (0, i))
    sds = jax.ShapeDtypeStruct
    if sweep:
        stack = lambda width: pl.BlockSpec((layer + 1, width, tm), lambda i: (0, 0, i))
        states_specs = [stack(w_fox), stack(w_fox), stack(w_sb), stack(w_sb), col(SUBLANES)]
        states_shape = [sds((layer + 1, wd, n), F32) for wd in (w_fox, w_fox, w_sb, w_sb)] \
            + [sds((SUBLANES, n), F32)]
    else:
        states_specs = [row(w_fox), row(w_fox), row(w_sb), row(w_sb), col(SUBLANES)]
        states_shape = [sds((n, w_fox), F32), sds((n, w_fox), F32), sds((n, w_sb), F32),
                        sds((n, w_sb), F32), sds((SUBLANES, n), F32)]
    operands = [x, mod, mod, gmix, w, gsgu, gq, gk, bf, ind96, ind64, msgu, bsgu]
    lay = lambda a: _layer_spec(a, layer)
    in_specs = [row(d), _mod_spec(mod, layer, 1, tm), _mod_spec(mod, layer, 0, tm), lay(gmix), lay(w),
                lay(gsgu), lay(gq), lay(gk), lay(bf), _const_spec(ind96.shape),
                _const_spec(ind64.shape), lay(msgu), lay(bsgu)]
    scratch = []
    if sweep:
        heads = w_fox // HEAD_DIM
        a = np.arange(LANES)
        operands.append(jnp.asarray(a[None, :] <= a[:, None], dtype=BF16))
        in_specs.append(_const_spec((LANES, LANES)))
        for st in prev_states:
            operands.append(st)
            in_specs.append(pl.BlockSpec((layer, st.shape[1], tm), lambda i: (0, 0, i)))
        out_specs = [row(w_sgu), row(heads * LANES), col(w_fox), col(w_fox), col(w_sb), row(w_sb),
                     col(w_sb)] + states_specs + [col(SUBLANES)]
        out_shape = [sds((n, w_sgu), BF16), sds((n, heads * LANES), BF16), sds((w_fox, n), BF16),
                     sds((w_fox, n), BF16), sds((w_sb, n), BF16), sds((n, w_sb), BF16),
                     sds((w_sb, n), BF16)] + states_shape + [sds((SUBLANES, n), F32)]
        scratch = [pltpu.VMEM((SUBLANES, LANES), F32)]
    else:
        out_specs = [row(w_sgu), row(w_fox), row(w_fox), row(w_fox), row(w_sb), row(w_sb),
                     row(w_sb)] + states_specs + [row(w_sgu)]
        out_shape = [sds((n, w_sgu), BF16), sds((n, w_fox), BF16), sds((n, w_fox), BF16),
                     sds((n, w_fox), BF16), sds((n, w_sb), BF16), sds((n, w_sb), BF16),
                     sds((n, w_sb), BF16)] + states_shape + [sds((n, w_sgu), F32)]
    kern = functools.partial(_inproj_kernel, w_sgu=w_sgu, w_fox=w_fox, w_sb=w_sb,
                             period=period, sweep=sweep, layer=layer)
    return pl.pallas_call(
        kern,
        grid=(n // tm,),
        in_specs=in_specs,
        out_specs=out_specs,
        out_shape=out_shape,
        scratch_shapes=scratch,
        compiler_params=_cparams(("arbitrary",)),
        name="inproj",
    )(*operands)


def _seq_cumsum(x, nc, reverse_exclusive):
    rows = x.shape[0]
    a = lax.broadcasted_iota(jnp.int32, (LANES, LANES), 0)
    b = lax.broadcasted_iota(jnp.int32, (LANES, LANES), 1)
    tri = (a > b) if reverse_exclusive else (a <= b)
    tri = jnp.where(tri, 1.0, 0.0).astype(BF16)
    ones = jnp.ones((LANES, LANES), BF16)
    xh, xm, xl = _split3(x)
    within = _dot(xh, tri) + _dot(xm, tri) + _dot(xl, tri)
    tot = _dot(xh, ones) + _dot(xm, ones) + _dot(xl, ones)
    ra = lax.broadcasted_iota(jnp.int32, (rows, rows), 0)
    rb = lax.broadcasted_iota(jnp.int32, (rows, rows), 1)
    other = (rb > ra) if reverse_exclusive else (rb < ra)
    blk = jnp.where((ra // nc == rb // nc) & other, 1.0, 0.0).astype(BF16)
    th, tm_, tl = _split3(tot)
    return within + _dot(blk, th) + _dot(blk, tm_) + _dot(blk, tl)


def _sample_cumsum_kernel(clf_ref, lf_ref, suf_ref, cum_ref, *, nc, dec_seq):
    suf_ref[...] = _seq_cumsum(clf_ref[...], nc, reverse_exclusive=True)
    a = lax.broadcasted_iota(jnp.int32, (LANES, LANES), 0)
    b = lax.broadcasted_iota(jnp.int32, (LANES, LANES), 1)
    tri = jnp.where((a // dec_seq == b // dec_seq) & (a <= b), 1.0, 0.0).astype(BF16)
    xh, xm, xl = _split3(lf_ref[...])
    cum_ref[...] = _dot(xh, tri) + _dot(xm, tri) + _dot(xl, tri)


def _sample_cumsum(clogf_t, logft, dec_seq):
    rows, past = clogf_t.shape
    nc = past // LANES
    suf, cum = pl.pallas_call(
        functools.partial(_sample_cumsum_kernel, nc=nc, dec_seq=dec_seq),
        out_shape=[jax.ShapeDtypeStruct((rows * nc, LANES), F32),
                   jax.ShapeDtypeStruct(logft.shape, F32)],
        compiler_params=pltpu.CompilerParams(vmem_limit_bytes=VMEM_LIMIT),
        name="sample_logf_cumsum",
    )(clogf_t.reshape(rows * nc, LANES), logft)
    return suf.reshape(rows, past), cum


def _head_query_blocks(qt_ref, heads, tq, ones_rows, lane0=0):
    rowi = lax.broadcasted_iota(jnp.int32, (LANES, tq), 0)
    out = []
    for h in range(heads):
        qp = qt_ref[(h // 2) * LANES:(h // 2 + 1) * LANES, lane0:lane0 + tq]
        own = (rowi < HEAD_DIM) if h % 2 == 0 else (rowi >= HEAD_DIM)
        fill = jnp.zeros((LANES, tq), F32)
        for r in ones_rows(h):
            fill = jnp.where(rowi == r, 1.0, fill)
        out.append(jnp.where(own, qp, fill.astype(BF16)))
    return out


def _bias_rows(h):
    base = HEAD_DIM if h % 2 == 0 else 0
    return [base + h, base + SUBLANES + h, base + 2 * SUBLANES + h]


def _fox_kernel(fs_ref, fe_ref, cut_ref, qt_ref, ka_ref, vt_ref, *rest, tq, heads):
    n_cast = len(rest) // 2
    o_ref = rest[n_cast]
    for src_ref, dst_ref in zip(rest[:n_cast], rest[n_cast + 1:]):
        dst_ref[...] = src_ref[...].astype(BF16)
    i = pl.program_id(0)
    tk = tq
    cpb = tk // LANES
    qa = _head_query_blocks(qt_ref, heads, tq, _bias_rows)
    krow = lax.broadcasted_iota(jnp.int32, (tk, tq), 0)
    qcol = lax.broadcasted_iota(jnp.int32, (tk, tq), 1)

    def logits_of(j):
        k0 = pl.multiple_of(j * tk, tk)
        return tuple(_dot(ka_ref[pl.ds(k0, tk), h * LANES:(h + 1) * LANES], qa[h])
                     for h in range(heads))

    def absorb(j, logits, carry, mask):
        k0 = pl.multiple_of(j * tk, tk)
        out = []
        for h in range(heads):
            vt = vt_ref[h * HEAD_DIM:(h + 1) * HEAD_DIM, pl.ds(k0, tk)]
            m, l, acc = carry[3 * h:3 * h + 3]
            s = logits[h]
            if mask is not None:
                s = jnp.where(mask, s, NEG_BIG)
            mn = jnp.maximum(m, jnp.max(s, axis=0, keepdims=True))
            alpha = jnp.exp2(m - mn)
            pe = jnp.exp2(s - mn)
            l = alpha * l + jnp.sum(pe, axis=0, keepdims=True)
            acc = alpha * acc + _dot(vt, pe.astype(BF16))
            out += [mn, l, acc]
        return tuple(out)

    def run(blocks, carry):
        staged = [logits_of(j) for j, _ in blocks]
        for (j, mask), logits in zip(blocks, staged):
            carry = absorb(j, logits, carry, mask)
        return carry

    def live(j):
        jc = j * cpb + (cpb - 1)
        ok = fs_ref[0, i * cpb] - fe_ref[0, jc] >= cut_ref[0]
        for h in range(1, heads):
            ok = ok | (fs_ref[h, i * cpb] - fe_ref[h, jc] >= cut_ref[0])
        return ok

    init = []
    for _ in range(heads):
        init += [jnp.full((1, tq), NEG_BIG, F32), jnp.zeros((1, tq), F32),
                 jnp.zeros((HEAD_DIM, tq), F32)]
    carry = run([(i, krow <= qcol), (jnp.maximum(i - 1, 0), jnp.broadcast_to(i > 0, (tk, tq)))],
                tuple(init))

    def cond(state):
        return (state[0] >= 1) & live(jnp.maximum(state[0] - 1, 0))

    def body(state):
        j = state[0]
        return (j - 2,) + run([(j, None), (j - 1, None)], state[1:])

    state = lax.while_loop(cond, body, (i - 2,) + carry)
    j = jnp.maximum(state[0], 0)
    last = (state[0] >= 0) & live(j)
    res = lax.cond(last, lambda c: run([(j, None)], c), lambda c: c, state[1:])
    for p in range(heads // 2):
        _, l0, a0, _, l1, a1 = res[6 * p:6 * p + 6]
        ot = jnp.concatenate([a0 / l0, a1 / l1], axis=0)
        o_ref[:, p * LANES:(p + 1) * LANES] = ot.T.astype(BF16)


def _cast_spec(rows, cols, steps):
    rb = next(r for r in range(2 * SUBLANES, rows + 1, 2 * SUBLANES)
              if rows % r == 0 and rows // r <= steps)
    last = rows // rb - 1
    return pl.BlockSpec((rb, cols), lambda i, *_: (jnp.minimum(i, last), 0))


def _fox_prompt(fs, fe, cut, qt, kaug, vt, *, tq, cast=()):
    w, n = qt.shape
    heads = w // HEAD_DIM
    steps = n // tq
    cast_specs = [_cast_spec(a.shape[0], a.shape[1], steps) for a in cast]
    grid_spec = pltpu.PrefetchScalarGridSpec(
        num_scalar_prefetch=3,
        grid=(steps,),
        in_specs=[
            pl.BlockSpec((w, tq), lambda i, *_: (0, i)),
            pl.BlockSpec(kaug.shape, lambda i, *_: (0, 0), pipeline_mode=pl.Buffered(1)),
            pl.BlockSpec((w, n), lambda i, *_: (0, 0), pipeline_mode=pl.Buffered(1)),
        ] + cast_specs,
        out_specs=[pl.BlockSpec((tq, w), lambda i, *_: (i, 0))] + cast_specs,
    )
    return pl.pallas_call(
        functools.partial(_fox_kernel, tq=tq, heads=heads),
        grid_spec=grid_spec,
        out_shape=[jax.ShapeDtypeStruct((n, w), BF16)]
        + [jax.ShapeDtypeStruct(a.shape, BF16) for a in cast],
        compiler_params=_cparams(("arbitrary",)),
        name="fox_prompt",
    )(fs, fe, cut, qt, kaug, vt, *cast)


def _sb_kernel(qt_ref, k_ref, vt_ref, o_ref, *, tq, tb, heads):
    i = pl.program_id(0)
    nsub = tq // tb
    qm = [_head_query_blocks(qt_ref, heads, tb, lambda h: [], lane0=s * tb) for s in range(nsub)]
    krow = lax.broadcasted_iota(jnp.int32, (tb, tb), 0)
    qcol = lax.broadcasted_iota(jnp.int32, (tb, tb), 1)
    a = lax.broadcasted_iota(jnp.int32, (tb, 2 * tb), 0)
    b = lax.broadcasted_iota(jnp.int32, (tb, 2 * tb), 1) % tb
    tri2 = jnp.where(b > a, 1.0, 0.0).astype(BF16)

    def sweep(tasks, carry):
        k0s = [pl.multiple_of(t[1] * tb, tb) for t in tasks]
        z = [[_dot(k_ref[pl.ds(k0, tb), (h // 2) * LANES:(h // 2 + 1) * LANES], qm[t[0]][h])
              for h in range(heads)] for t, k0 in zip(tasks, k0s)]
        logsig, later, mass = {}, {}, {}
        for ti, (_, _, mask, valid) in enumerate(tasks):
            for h in range(heads):
                zz = z[ti][h]
                sp = jnp.maximum(zz, 0.0) + jnp.log2(1.0 + jnp.exp2(-jnp.abs(zz)))
                logsig[ti, h] = zz - sp
                if mask is not None:
                    sp = jnp.where(mask, sp, 0.0)
                hi, lo = _split2(sp)
                later[ti, h] = _dot(tri2, jnp.concatenate([hi, lo], axis=0))
                mass[ti, h] = jnp.sum(sp, axis=0, keepdims=True)
                if valid is not None:
                    mass[ti, h] = mass[ti, h] * valid
        carry = dict(carry)
        for ti, (s, _, mask, valid) in enumerate(tasks):
            for h in range(heads):
                cr, acc = carry[s, h]
                w = jnp.exp2(logsig[ti, h] - later[ti, h] - cr)
                if mask is not None:
                    w = jnp.where(mask, w, 0.0)
                if valid is not None:
                    w = w * valid
                vt = vt_ref[h * HEAD_DIM:(h + 1) * HEAD_DIM, pl.ds(k0s[ti], tb)]
                carry[s, h] = (cr + mass[ti, h], acc + _dot(vt, w.astype(BF16)))
        return carry

    tasks = []
    for s in range(nsub):
        qb = i * nsub + s
        tasks.append((s, qb, krow < qcol, None))
        tasks.append((s, jnp.maximum(qb - 1, 0), None, jnp.where(qb > 0, 1.0, 0.0) if s == 0 else None))
    zero = (jnp.zeros((1, tb), F32), jnp.zeros((HEAD_DIM, tb), F32))
    carry = sweep(tasks, {(s, h): zero for s in range(nsub) for h in range(heads)})

    keys = [(s, h) for s in range(nsub) for h in range(heads)]
    flat = lambda c: tuple(v for key in keys for v in c[key])

    def cond(state):
        left = state[1]
        for idx in range(1, len(keys)):
            left = jnp.minimum(left, state[1 + 2 * idx])
        return (i * nsub + nsub - 3 - state[0] >= 0) & (jnp.min(left) < PRUNE_LOG * LOG2E)

    def body(state):
        t = state[0]
        more = []
        for s in range(nsub):
            j = i * nsub + s - 2 - t
            more.append((s, jnp.maximum(j, 0), None, jnp.where(j >= 0, 1.0, 0.0)))
        c = {key: (state[1 + 2 * idx], state[2 + 2 * idx]) for idx, key in enumerate(keys)}
        return (t + 1,) + flat(sweep(more, c))

    res = lax.while_loop(cond, body, (0,) + flat(carry))[1:]
    for s in range(nsub):
        for p in range(heads // 2):
            e, o = keys.index((s, 2 * p)), keys.index((s, 2 * p + 1))
            ot = jnp.concatenate([res[2 * e + 1], res[2 * o + 1]], axis=0)
            o_ref[s * tb:(s + 1) * tb, p * LANES:(p + 1) * LANES] = ot.T.astype(BF16)


def _sb_prompt(qt, k, vt, *, tq, tb):
    w, n = qt.shape
    return pl.pallas_call(
        functools.partial(_sb_kernel, tq=tq, tb=tb, heads=w // HEAD_DIM),
        grid=(n // tq,),
        in_specs=[
            pl.BlockSpec((w, tq), lambda i: (0, i)),
            pl.BlockSpec((n, w), lambda i: (0, 0), pipeline_mode=pl.Buffered(1)),
            pl.BlockSpec((w, n), lambda i: (0, 0), pipeline_mode=pl.Buffered(1)),
        ],
        out_specs=pl.BlockSpec((tq, w), lambda i: (i, 0)),
        out_shape=jax.ShapeDtypeStruct((n, w), BF16),
        compiler_params=_cparams(("arbitrary",)),
        name="sb_prompt",
    )(qt, k, vt)


def _sb_block(z, carry, v, tri, mask, v_feature_major=False):
    lg = jnp.log(1.0 + jnp.exp(-jnp.abs(z)))
    sp = jnp.maximum(z, 0.0) + lg
    if mask is not None:
        sp = jnp.where(mask, sp, 0.0)
    hi, lo = _split2(sp)
    later = _dot(hi, tri) + _dot(lo, tri)
    a = jnp.exp((jnp.minimum(z, 0.0) - lg) - later - carry)
    if mask is not None:
        a = jnp.where(mask, a, 0.0)
    pv = _dot_nt(a.astype(BF16), v) if v_feature_major else _dot(a.astype(BF16), v)
    return pv, carry + jnp.sum(sp, axis=1, keepdims=True)


def _suffix_matrix(tk):
    a = lax.broadcasted_iota(jnp.int32, (tk, tk), 0)
    b = lax.broadcasted_iota(jnp.int32, (tk, tk), 1)
    return jnp.where(a > b, 1.0, 0.0).astype(BF16)


def _head_rows(x, heads):
    lane_head = lax.broadcasted_iota(jnp.int32, x.shape, 1) // HEAD_DIM
    return jnp.concatenate(
        [jnp.where(lane_head == h, x, jnp.zeros_like(x)) for h in range(heads)], axis=0)


def _fold_heads(o, heads, s):
    lane_head = lax.broadcasted_iota(jnp.int32, (s, o.shape[1]), 1) // HEAD_DIM
    out = jnp.zeros((s, o.shape[1]), F32)
    for h in range(heads):
        out = jnp.where(lane_head == h, o[h * s:(h + 1) * s], out)
    return out


def _sample_attn_kernel(qf_ref, kf_ref, vf_ref, ck_ref, cv_ref, suf_ref, cum_ref,
                        qb_ref, kb_ref, vb_ref, cbk_ref, cbv_ref, of_ref, ob_ref,
                        *, s, h_fox, h_sb, past):
    qa = _head_rows(qf_ref[...], h_fox)
    rows = h_fox * s
    ck = ck_ref[0, 0].astype(BF16)
    cv = cv_ref[0, 0].astype(BF16)
    suf = suf_ref[0]
    cum = cum_ref[0]
    bias_c = jnp.concatenate(
        [jnp.broadcast_to(suf[h:h + 1, :], (s, past)) for h in range(h_fox)], axis=0)
    bias_n = jnp.concatenate(
        [jnp.broadcast_to(-cum[h:h + 1, :], (s, s)) for h in range(h_fox)], axis=0)
    lc = _dot(qa, ck) + bias_c
    ln = _dot_nt(qa, kf_ref[...]) + bias_n
    r_pos = lax.broadcasted_iota(jnp.int32, (rows, s), 0) % s
    k_pos = lax.broadcasted_iota(jnp.int32, (rows, s), 1)
    ln = jnp.where(k_pos <= r_pos, ln, NEG_BIG)
    m = jnp.maximum(jnp.max(lc, axis=1, keepdims=True), jnp.max(ln, axis=1, keepdims=True))
    pc = jnp.exp(lc - m)
    pn = jnp.exp(ln - m)
    den = jnp.sum(pc, axis=1, keepdims=True) + jnp.sum(pn, axis=1, keepdims=True)
    o = (_dot_nt(pc.astype(BF16), cv) + _dot(pn.astype(BF16), vf_ref[...])) / den
    of_ref[...] = _fold_heads(o, h_fox, s).astype(BF16)

    qb = _head_rows(qb_ref[...], h_sb)
    rows_b = h_sb * s
    rb = lax.broadcasted_iota(jnp.int32, (rows_b, s), 0) % s
    cb = lax.broadcasted_iota(jnp.int32, (rows_b, s), 1)
    acc, carry = _sb_block(_dot_nt(qb, kb_ref[...]), jnp.zeros((rows_b, 1), F32), vb_ref[...],
                           _suffix_matrix(s), cb < rb)
    tri = _suffix_matrix(LANES)

    def cond(state):
        return (state[0] < past // LANES) & (jnp.min(state[2]) < PRUNE_LOG)

    def body(state):
        t, acc, carry = state
        k0 = pl.multiple_of(past - (t + 1) * LANES, LANES)
        k = cbk_ref[0, 0, :, pl.ds(k0, LANES)].astype(BF16)
        v = cbv_ref[0, 0, :, pl.ds(k0, LANES)].astype(BF16)
        pv, carry = _sb_block(_dot(qb, k), carry, v, tri, None, v_feature_major=True)
        return t + 1, acc + pv, carry

    _, acc, _ = lax.while_loop(cond, body, (0, acc, carry))
    ob_ref[...] = _fold_heads(acc, h_sb, s).astype(BF16)


def _sample_attn(qf, kf, vf, ck, cv, suf, cum, qb, kb, vb, cbk, cbv, *, layer, batch, s):
    w_fox, w_sb = qf.shape[1], qb.shape[1]
    past = ck.shape[3]
    h_fox, h_sb = w_fox // HEAD_DIM, w_sb // HEAD_DIM
    new = lambda w: pl.BlockSpec((s, w), lambda b: (b, 0))
    cache = lambda w: pl.BlockSpec((1, 1, w, past), lambda b: (layer, b, 0, 0))
    kern = functools.partial(_sample_attn_kernel, s=s, h_fox=h_fox, h_sb=h_sb, past=past)
    return pl.pallas_call(
        kern,
        grid=(batch,),
        in_specs=[new(w_fox), new(w_fox), new(w_fox), cache(w_fox), cache(w_fox),
                  pl.BlockSpec((1, SUBLANES, past), lambda b: (b, 0, 0)),
                  pl.BlockSpec((1, SUBLANES, s), lambda b: (b, 0, 0)),
                  new(w_sb), new(w_sb), new(w_sb), cache(w_sb), cache(w_sb)],
        out_specs=[new(w_fox), new(w_sb)],
        out_shape=[jax.ShapeDtypeStruct((batch * s, w_fox), BF16),
                   jax.ShapeDtypeStruct((batch * s, w_sb), BF16)],
        compiler_params=_cparams(("arbitrary",)),
        name="sample_attn",
    )(qf, kf, vf, ck, cv, suf, cum, qb, kb, vb, cbk, cbv)


def _merge_kernel(*refs):
    prompt_refs, sample_refs = refs[0:7], refs[7:14]
    gmix_ref, wg_ref, bg_ref, wbs_ref, wbf_ref, wbb_ref, wo_ref, o_ref, os_ref = refs[14:]

    def tile(x_ref, sc_ref, sh_ref, gt_ref, ysgu_ref, ofox_ref, osb_ref, o_ref):
        x = x_ref[...]
        d = x.shape[1]
        h = _modulated_norm(x, gmix_ref[0], sc_ref[0], sh_ref[0])
        gates = _dot(h.astype(BF16), wg_ref[0]) + bg_ref[0]
        gates = 1.0 / (1.0 + jnp.exp(-gates))
        merged = gates[:, 0:d] * _dot(ysgu_ref[...], wbs_ref[0]) \
            + gates[:, d:2 * d] * _dot(ofox_ref[...], wbf_ref[0]) \
            + gates[:, 2 * d:3 * d] * _dot(osb_ref[...], wbb_ref[0])
        o_ref[...] = x + gt_ref[0] * _dot(merged.astype(BF16), wo_ref[0])

    last = pl.num_programs(0) - 1

    @pl.when(pl.program_id(0) < last)
    def _():
        tile(*prompt_refs, o_ref)

    @pl.when(pl.program_id(0) == last)
    def _():
        tile(*sample_refs, os_ref)


def _merge(prompt, sample, layer, gmix, wg, bg, wbs, wbf, wbb, wo, *, tm):
    x, mod, *branches = prompt
    xs, mod_s, *branches_s = sample
    n, d = x.shape
    ns = xs.shape[0]
    steps = n // tm
    assert mod.shape[1] == 1 and mod_s.shape[1] == ns
    row = lambda width: pl.BlockSpec((tm, width), lambda i: (jnp.minimum(i, steps - 1), 0))
    row_s = lambda width: pl.BlockSpec((ns, width), lambda i: (0, 0))
    mod_spec_s = lambda k: pl.BlockSpec((1, ns, d), lambda i: (layer, 0, k))
    lay = lambda a: _layer_spec(a, layer)
    return pl.pallas_call(
        _merge_kernel,
        grid=(steps + 1,),
        in_specs=[row(d), _mod_spec(mod, layer, 1, tm), _mod_spec(mod, layer, 0, tm),
                  _mod_spec(mod, layer, 2, tm), *(row(b.shape[1]) for b in branches),
                  row_s(d), mod_spec_s(1), mod_spec_s(0), mod_spec_s(2),
                  *(row_s(b.shape[1]) for b in branches_s),
                  lay(gmix), lay(wg), lay(bg), lay(wbs), lay(wbf), lay(wbb), lay(wo)],
        out_specs=[row(d), row_s(d)],
        out_shape=[jax.ShapeDtypeStruct((n, d), F32), jax.ShapeDtypeStruct((ns, d), F32)],
        compiler_params=_cparams(("arbitrary",)),
        name="merge",
    )(x, mod, mod, mod, *branches, xs, mod_s, mod_s, mod_s, *branches_s,
      gmix, wg, bg, wbs, wbf, wbb, wo)


def _ffn_chunks(d_ff):
    mid = -(-d_ff // (2 * MXU_TILE)) * MXU_TILE
    return [(0, mid), (mid, d_ff)] if 0 < mid < d_ff else [(0, d_ff)]


def _ffn_kernel(x_ref, sc_ref, sh_ref, gt_ref, xs_ref, scs_ref, shs_ref, gts_ref, g_ref, wi_ref,
                wo_ref, o_ref, os_ref, *, d_ff):
    def tile(x_ref, sc_ref, sh_ref, gt_ref, o_ref):
        for rows in _row_passes(x_ref.shape[0]):
            x = x_ref[rows]
            h = _modulated_norm(x, g_ref[0], _mod_rows(sc_ref, rows), _mod_rows(sh_ref, rows))
            hb = h.astype(BF16)
            y = None
            for c0, c1 in _ffn_chunks(d_ff):
                a = _dot(hb, wi_ref[0, :, c0:c1])
                gl = _dot(hb, wi_ref[0, :, d_ff + c0:d_ff + c1])
                act = a * (1.0 / (1.0 + jnp.exp(-a))) * gl
                part = _dot(act.astype(BF16), wo_ref[0, c0:c1, :])
                y = part if y is None else y + part
            o_ref[rows] = x + _mod_rows(gt_ref, rows) * y

    last = pl.num_programs(0) - 1

    @pl.when(pl.program_id(0) < last)
    def _():
        tile(x_ref, sc_ref, sh_ref, gt_ref, o_ref)

    @pl.when(pl.program_id(0) == last)
    def _():
        tile(xs_ref, scs_ref, shs_ref, gts_ref, os_ref)


def _ffn(x, mod, xs, mod_s, layer, g, wi, wo, *, tm):
    n, d = x.shape
    ns = xs.shape[0]
    d_ff = wo.shape[1]
    steps = n // tm
    assert mod.shape[1] == 1 and mod_s.shape[1] == ns
    row = pl.BlockSpec((tm, d), lambda i: (jnp.minimum(i, steps - 1), 0))
    row_s = pl.BlockSpec((ns, d), lambda i: (0, 0))
    mod_spec_s = lambda k: pl.BlockSpec((1, ns, d), lambda i: (layer, 0, k))
    return pl.pallas_call(
        functools.partial(_ffn_kernel, d_ff=d_ff),
        grid=(steps + 1,),
        in_specs=[row, _mod_spec(mod, layer, 4, tm), _mod_spec(mod, layer, 3, tm),
                  _mod_spec(mod, layer, 5, tm), row_s, mod_spec_s(4), mod_spec_s(3), mod_spec_s(5),
                  _layer_spec(g, layer), _layer_spec(wi, layer), _layer_spec(wo, layer)],
        out_specs=[row, row_s],
        out_shape=[jax.ShapeDtypeStruct((n, d), F32), jax.ShapeDtypeStruct((ns, d), F32)],
        compiler_params=_cparams(("arbitrary",)),
        name="ffn",
    )(x, mod, mod, mod, xs, mod_s, mod_s, mod_s, g, wi, wo)


class _Tiles(NamedTuple):
    dense: int
    ffn: int
    fox: int
    sb: int
    sb_sub: int


def _prompt_tiles(seq):
    return _Tiles(dense=min(512, seq), ffn=min(512, seq), fox=min(256, seq), sb=min(512, seq),
                  sb_sub=min(128, seq))


def _indicator(width, group):
    idx = np.arange(width) // group
    return jnp.asarray(idx[:, None] == idx[None, :], dtype=BF16)


def kernel(x_prompt, x_sample, c_prompt, c_sample, cache_fox_k, cache_fox_v, cache_fox_logf,
           cache_sb_k, cache_sb_v, w_ada, b_ada, g_mix, g_ffn, w_in, g_sgu_v, w_sgu, b_sgu, b_fgt,
           g_q, g_k, w_br_sgu, w_br_fox, w_br_sb, w_gate, b_gate, w_out, w_ffn_in, w_ffn_out):
    batch, seq, d = x_prompt.shape
    dec_batch, dec_seq, _ = x_sample.shape
    depth = w_ada.shape[0]
    past = cache_fox_k.shape[2]
    h_fox, h_sb = cache_fox_k.shape[3], cache_sb_k.shape[3]
    w_fox, w_sb = h_fox * HEAD_DIM, h_sb * HEAD_DIM
    g_sgu, cg = g_sgu_v.shape[1], g_sgu_v.shape[2]
    w_sgu_ = g_sgu * cg
    assert batch == 1 and g_sgu == G_SGU and w_sgu.shape[2] == SGU_LEN
    n_dec = dec_batch * dec_seq

    n_c = batch + dec_batch
    c_rows = -(-n_c // 8) * 8
    c_all = jnp.zeros((c_rows, d), F32).at[:n_c].set(jnp.concatenate([c_prompt, c_sample], axis=0))
    mod = _modulation(c_all, w_ada, b_ada)

    offs = np.cumsum([0, w_sgu_, w_sgu_, w_fox, w_fox, w_fox, h_fox, w_sb, w_sb, w_sb]).tolist()
    f_cols = jnp.zeros((depth, d, LANES), F32).at[:, :, :h_fox].set(w_in[:, :, offs[5]:offs[6]])
    w_main = jnp.concatenate([w_in[:, :, :offs[5]], w_in[:, :, offs[6]:], f_cols],
                             axis=2).astype(BF16)
    bf_pad = jnp.zeros((depth, 1, LANES), F32).at[:, 0, :h_fox].set(b_fgt)
    gmix3, gffn3, bg3 = g_mix.reshape(depth, 1, d), g_ffn.reshape(depth, 1, d), b_gate.reshape(depth, 1, 3 * d)
    mod_p = mod[:, 0:batch]
    mod_s = jnp.repeat(mod[:, batch:batch + dec_batch], dec_seq, axis=1)
    ind96, ind64 = _indicator(w_sgu_, cg), _indicator(LANES, HEAD_DIM)
    gq_t = jnp.tile(g_q, (1, h_fox)).reshape(depth, 1, w_fox)
    gk_t = jnp.tile(g_k, (1, h_fox)).reshape(depth, 1, w_fox)
    gsgu = g_sgu_v.reshape(depth, 1, w_sgu_)
    msgu_p = jnp.transpose(w_sgu, (0, 2, 1, 3)).reshape(depth, SGU_LEN, g_sgu * SGU_LEN)
    reps = SGU_LEN // dec_seq
    w_small = jnp.tile(w_sgu[:, :, :dec_seq, :dec_seq], (1, 1, reps, reps))
    msgu_s = jnp.transpose(w_small, (0, 2, 1, 3)).reshape(depth, SGU_LEN, g_sgu * SGU_LEN)
    bsgu_p = jnp.repeat(jnp.transpose(b_sgu, (0, 2, 1)), cg, axis=2)
    bsgu_s = jnp.tile(bsgu_p[:, :dec_seq], (1, reps, 1))
    dense_w = (w_gate, w_br_sgu, w_br_fox, w_br_sb, w_out, w_ffn_in, w_ffn_out)

    to_fm = lambda c: jnp.transpose(c, (0, 1, 3, 4, 2)).reshape(
        depth, dec_batch, c.shape[3] * HEAD_DIM, past)
    ck_t, cv_t, cbk_t, cbv_t = (to_fm(c) for c in (cache_fox_k, cache_fox_v, cache_sb_k, cache_sb_v))

    xp = x_prompt.reshape(seq, d)
    xs = x_sample.reshape(n_dec, d)
    tiles = _prompt_tiles(seq)
    stacks, logf_p, st_s = (), [], []
    for l in range(depth):
        shared = (l, gmix3, w_main, gsgu, gq_t, gk_t, bf_pad, ind96, ind64)

        (ysgu, kaug, qft, vft, qbt, kb16, vbt, *stacks, logft, fcumt) = _inproj(
            xp, mod_p, *shared, msgu_p, bsgu_p, tm=tiles.dense, period=SGU_LEN, sweep=True,
            prev_states=stacks)
        bound = 1.01 * HEAD_DIM ** 0.5 * jnp.max(jnp.abs(g_q[l])) * jnp.max(jnp.abs(g_k[l]))
        cut = (-(2.0 * bound + PRUNE_LOG)).reshape(1)
        ofox, *cast = _fox_prompt(
            fcumt[:, 0::LANES], fcumt[:, LANES - 1::LANES], cut, qft, kaug, vft, tq=tiles.fox,
            cast=[a.reshape(-1, a.shape[2]) for a in dense_w] if l == 0 else ())
        if l == 0:
            wg, wbs, wbf, wbb, wo, wfi, wfo = (c.reshape(a.shape) for c, a in zip(cast, dense_w))
            branch_w = (wg, bg3, wbs, wbf, wbb, wo)
        osb = _sb_prompt(qbt, kb16, vbt, tq=tiles.sb, tb=tiles.sb_sub)
        prompt = (xp, mod_p, ysgu, ofox, osb)
        logf_p.append(logft[:h_fox].T.reshape(batch, seq, h_fox))

        (ysgu, qf, kf16, vf16, qb, kb16, vb16, kf32, vf32, kb32, vb32, logft, sguv) = _inproj(
            xs, mod_s, *shared, msgu_s, bsgu_s, tm=n_dec, period=dec_seq, sweep=False)
        clf = jnp.zeros((dec_batch, SUBLANES, past), F32).at[:, :h_fox].set(
            jnp.transpose(cache_fox_logf[l], (0, 2, 1)))
        suf, cum = _sample_cumsum(clf.reshape(dec_batch * SUBLANES, past), logft, dec_seq)
        cum_b = jnp.transpose(cum.reshape(SUBLANES, dec_batch, dec_seq), (1, 0, 2))
        ofox, osb = _sample_attn(
            qf, kf16, vf16, ck_t, cv_t, suf.reshape(dec_batch, SUBLANES, past), cum_b,
            qb, kb16, vb16, cbk_t, cbv_t, layer=l, batch=dec_batch, s=dec_seq)
        x1p, x1s = _merge(prompt, (xs, mod_s, ysgu, ofox, osb), l, gmix3, *branch_w, tm=tiles.dense)
        xp, xs = _ffn(x1p, mod_p, x1s, mod_s, l, gffn3, wfi, wfo, tm=tiles.ffn)
        st_s.append((kf32.reshape(dec_batch, dec_seq, h_fox, HEAD_DIM),
                     vf32.reshape(dec_batch, dec_seq, h_fox, HEAD_DIM),
                     logft[:h_fox].T.reshape(dec_batch, dec_seq, h_fox),
                     kb32.reshape(dec_batch, dec_seq, h_sb, HEAD_DIM),
                     vb32.reshape(dec_batch, dec_seq, h_sb, HEAD_DIM),
                     sguv.reshape(dec_batch, dec_seq, w_sgu_)))

    def stack(states, idx):
        return jnp.stack([s[idx] for s in states], axis=0)

    per_head = lambda st: jnp.transpose(
        st.reshape(depth, batch, st.shape[1] // HEAD_DIM, HEAD_DIM, seq), (0, 1, 4, 2, 3))
    kf_p, vf_p, kb_p, vb_p = (per_head(st) for st in stacks)
    return (xp.reshape(batch, seq, d), xs.reshape(dec_batch, dec_seq, d),
            kf_p, vf_p, jnp.stack(logf_p, axis=0), kb_p, vb_p,
            stack(st_s, 0), stack(st_s, 1), stack(st_s, 2), stack(st_s, 3), stack(st_s, 4),
            stack(st_s, 5))
```
